```python
import math
import jax
import jax.numpy as jnp
from jax import lax
import numpy as np

D_MODEL = 1024
BATCH = 4
SEQ = 4096
DEPTH = 2
DEC_BATCH = 128
DEC_SEQ = 8
PAST_LEN = 2048
PAGE_SIZE = 128

HEAD_DIM = 64
CONV_K = 4
RMS_EPS = 1e-6
MLP_HIDDEN = 4 * D_MODEL
QUERY_BLOCK = 128
N_AB = (DEPTH + 1) // 2
N_CD = DEPTH // 2

SSM_HEADS = D_MODEL // 64
SSM_HEAD_DIM = 64
SSM_INNER = SSM_HEADS * SSM_HEAD_DIM
SSM_GROUPS = 2
SSM_STATE = 128
SSM_CONV_DIM = SSM_INNER + 2 * SSM_GROUPS * SSM_STATE
SSM_CHUNK = 128

NSA_HEADS = D_MODEL // 64
NSA_KV_HEADS = 2
NSA_REP = NSA_HEADS // NSA_KV_HEADS
NSA_BLOCK = 64
NSA_TOPK = 16
NSA_WINDOW = 512
NSA_CMP_HIDDEN = 256

GDN_HEADS = D_MODEL // 128
GDN_HEAD_DIM = 128
GDN_WIDTH = GDN_HEADS * GDN_HEAD_DIM
GDN_CHUNK = 64

DIL_PATTERNS = ((128, 1), (512, 4), (2048, 16))
DIL_GROUPS = len(DIL_PATTERNS)
DIL_HEADS_PER_GROUP = 8

AB_IN = SSM_INNER + SSM_CONV_DIM + SSM_HEADS + NSA_HEADS * HEAD_DIM + 6 * NSA_KV_HEADS * HEAD_DIM + 3 * NSA_HEADS
AB_MIX = SSM_INNER + NSA_HEADS * HEAD_DIM
DIL_IN = 3 * DIL_GROUPS * DIL_HEADS_PER_GROUP * HEAD_DIM
CD_IN = 4 * GDN_WIDTH + 2 * GDN_HEADS + DIL_IN
CD_MIX = GDN_WIDTH + DIL_HEADS_PER_GROUP * HEAD_DIM
NEG = -1e30

kernel_name = 'hybrid_ssd_nsa_gdn_dilated_decoder_step'


def split_cols(x, sizes):
    cuts, acc = [], 0
    for s in sizes[:-1]:
        acc += s
        cuts.append(acc)
    return jnp.split(x, cuts, axis=-1)


def rmsnorm(x, w):
    xf = x.astype(jnp.float32)
    y = xf * lax.rsqrt(jnp.mean(xf * xf, axis=-1, keepdims=True) + RMS_EPS)
    return (y * w.astype(jnp.float32)).astype(x.dtype)


def l2norm(x):
    xf = x.astype(jnp.float32)
    return xf * lax.rsqrt(jnp.sum(xf * xf, axis=-1, keepdims=True) + 1e-6)


def causal_conv(x, buf, w, b):
    t = x.shape[1]
    xp = jnp.concatenate([buf.astype(x.dtype), x], axis=1)
    y = xp[:, 0:t] * w[0]
    for i in range(1, CONV_K):
        y = y + xp[:, i:i + t] * w[i]
    if b is not None:
        y = y + b
    return y, xp[:, t:]


def attend_dense(q, k, v, mask):
    s = jnp.einsum('bqgrd,bkgd->bqgrk', q, k).astype(jnp.float32) * (q.shape[-1] ** -0.5)
    s = jnp.where(mask[:, :, None, None, :], s, NEG)
    m = jnp.max(s, axis=-1, keepdims=True)
    p = jnp.exp(s - m)
    l = jnp.sum(p, axis=-1, keepdims=True)
    o = jnp.einsum('bqgrk,bkgd->bqgrd', (p / l).astype(v.dtype), v)
    return o, (m + jnp.log(l))[..., 0]


def attend_gathered(q, k, v, mask):
    s = jnp.einsum('bqgrd,bqgkd->bqgrk', q, k).astype(jnp.float32) * (q.shape[-1] ** -0.5)
    s = jnp.where(mask[:, :, :, None, :], s, NEG)
    m = jnp.max(s, axis=-1, keepdims=True)
    p = jnp.exp(s - m)
    l = jnp.sum(p, axis=-1, keepdims=True)
    o = jnp.einsum('bqgrk,bqgkd->bqgrd', (p / l).astype(v.dtype), v)
    return o, (m + jnp.log(l))[..., 0]


def banded_attention(q, k, v, window, qblk):
    bsz, t, g, r, hd = q.shape
    nq = t // qblk
    span = qblk + window
    kp = jnp.pad(k, ((0, 0), (window, 0), (0, 0), (0, 0)))
    vp = jnp.pad(v, ((0, 0), (window, 0), (0, 0), (0, 0)))
    diff = jnp.arange(qblk)[:, None] - jnp.arange(span)[None, :] + window
    band = (diff >= 0) & (diff <= window)

    def one(args):
        i, qi = args
        start = i * qblk
        ks = lax.dynamic_slice_in_dim(kp, start, span, axis=1)
        vs = lax.dynamic_slice_in_dim(vp, start, span, axis=1)
        kpos = start + jnp.arange(span) - window
        return attend_dense(qi, ks, vs, (band & (kpos >= 0)[None, :])[None])

    qs = q.reshape(bsz, nq, qblk, g, r, hd).swapaxes(0, 1)
    o, lse = lax.map(one, (jnp.arange(nq), qs))
    return o.swapaxes(0, 1).reshape(bsz, t, g, r, hd), lse.swapaxes(0, 1).reshape(bsz, t, g, r)


def ssd_scan(x, dt, a, bm, cm, h0, chunk):
    bsz, t, nh, hp = x.shape
    ng, ns = bm.shape[2], bm.shape[3]
    rep = nh // ng
    nc = t // chunk
    f32 = jnp.float32
    la = (dt * a).reshape(bsz, nc, chunk, ng, rep)
    xd = (x.astype(f32) * dt[..., None]).reshape(bsz, nc, chunk, ng, rep, hp)
    bc = bm.astype(f32).reshape(bsz, nc, chunk, ng, ns)
    cc = cm.astype(f32).reshape(bsz, nc, chunk, ng, ns)
    acs = jnp.cumsum(la, axis=2)
    ii = jnp.arange(chunk)
    incl = ii[:, None] >= ii[None, :]
    lmat = jnp.exp(jnp.where(incl[:, :, None, None], acs[:, :, :, None] - acs[:, :, None, :], -jnp.inf))
    cb = jnp.einsum('bcign,bcjgn->bcijg', cc, bc)
    y_diag = jnp.einsum('bcijgr,bcjgrp->bcigrp', cb[..., None] * lmat, xd)
    states = jnp.einsum('bcjgn,bcjgrp->bcgrpn', bc, xd * jnp.exp(acs[:, :, -1:] - acs)[..., None])
    cdec = jnp.exp(acs[:, :, -1])

    def step(h, inp):
        s, d = inp
        return d[..., None, None] * h + s, h

    h_fin, h_prev = lax.scan(step, h0.astype(f32).reshape(bsz, ng, rep, hp, ns),
                             (jnp.moveaxis(states, 1, 0), jnp.moveaxis(cdec, 1, 0)))
    h_prev = jnp.moveaxis(h_prev, 0, 1)
    y_off = jnp.einsum('bcign,bcgrpn->bcigrp', cc, h_prev) * jnp.exp(acs)[..., None]
    return (y_diag + y_off).reshape(bsz, t, nh, hp), h_fin.reshape(bsz, nh, hp, ns)


def gated_delta(q, k, v, g, beta, s0, chunk):
    bsz, t, nh, dk = q.shape
    dv = v.shape[-1]
    nc = t // chunk
    f32 = jnp.float32

    def rs(arr):
        return arr.astype(f32).reshape((bsz, nc, chunk) + arr.shape[2:])

    q, k, v, beta = rs(q), rs(k), rs(v), rs(beta)
    gc = jnp.cumsum(rs(g), axis=2)
    kb = k * beta[..., None]
    vb = v * beta[..., None]
    ii = jnp.arange(chunk)
    incl = ii[:, None] >= ii[None, :]
    strict = ii[:, None] > ii[None, :]
    decay = jnp.exp(jnp.where(incl[:, :, None], gc[:, :, :, None, :] - gc[:, :, None, :, :], -jnp.inf))
    amat = jnp.where(strict[:, :, None], jnp.einsum('bcihk,bcjhk->bcijh', kb, k) * decay, 0.0)
    amat = jnp.moveaxis(amat, -1, 2)
    eye = jnp.eye(chunk, dtype=f32)
    tmat = lax.linalg.triangular_solve(amat + eye, jnp.broadcast_to(eye, amat.shape),
                                       left_side=True, lower=True, unit_diagonal=True)
    u = jnp.einsum('bchij,bcjhv->bcihv', tmat, vb)
    w = jnp.einsum('bchij,bcjhk->bcihk', tmat, kb * jnp.exp(gc)[..., None])
    qk = jnp.where(incl[:, :, None], jnp.einsum('bcihk,bcjhk->bcijh', q, k) * decay, 0.0)
    qg = q * jnp.exp(gc)[..., None]
    kdec = k * jnp.exp(gc[:, :, -1:] - gc)[..., None]
    glast = jnp.exp(gc[:, :, -1])

    def step(s, inp):
        qg_c, kdec_c, u_c, w_c, qk_c, gl = inp
        v_new = u_c - jnp.einsum('bihk,bhkv->bihv', w_c, s)
        o = jnp.einsum('bihk,bhkv->bihv', qg_c, s) + jnp.einsum('bijh,bjhv->bihv', qk_c, v_new)
        s = s * gl[:, :, None, None] + jnp.einsum('bjhk,bjhv->bhkv', kdec_c, v_new)
        return s, o

    xs = tuple(jnp.moveaxis(arr, 1, 0) for arr in (qg, kdec, u, w, qk, glast))
    s_fin, o = lax.scan(step, s0.astype(f32), xs)
    return jnp.moveaxis(o, 0, 1).reshape(bsz, t, nh, dv), s_fin


def compress_blocks(k, pe, w1, w2):
    bsz, length, g, hd = k.shape
    nb = length // NSA_BLOCK
    kb = k.reshape(bsz, nb, NSA_BLOCK, g, hd) + pe[:, None, :]
    flat = kb.transpose(0, 1, 3, 2, 4).reshape(bsz, nb, g, NSA_BLOCK * hd)
    return jax.nn.silu(flat @ w1) @ w2


def nsa_global(q, k_cmp, v_cmp, k_sel, v_sel, q_pos, lp, qblk):
    bsz, tq, g, r, hd = q.shape
    length = k_cmp.shape[1]
    nb = -(-length // NSA_BLOCK)
    pad = ((0, 0), (0, nb * NSA_BLOCK - length), (0, 0), (0, 0))
    kc = compress_blocks(jnp.pad(k_cmp, pad), lp['nsa_pe_k'], lp['nsa_ck_w1'], lp['nsa_ck_w2'])
    vc = compress_blocks(jnp.pad(v_cmp, pad), lp['nsa_pe_v'], lp['nsa_cv_w1'], lp['nsa_cv_w2'])
    kb = jnp.pad(k_sel, pad).reshape(bsz, nb, NSA_BLOCK, g, hd).transpose(0, 3, 1, 2, 4)
    vb = jnp.pad(v_sel, pad).reshape(bsz, nb, NSA_BLOCK, g, hd).transpose(0, 3, 1, 2, 4)
    n_sel = min(NSA_TOPK, nb)
    blk = jnp.arange(nb)
    b_idx = jnp.arange(bsz)[:, None, None, None]
    g_idx = jnp.arange(g)[None, None, :, None]
    scale = hd ** -0.5

    def one(args):
        qi, pos = args
        cur = pos // NSA_BLOCK
        complete = (blk[None, :] + 1) * NSA_BLOCK <= pos[:, None] + 1
        s = jnp.einsum('bqgrd,bngd->bqgrn', qi, kc).astype(jnp.float32) * scale
        s = jnp.where(complete[None, :, None, None, :], s, NEG)
        p = jax.nn.softmax(s, axis=-1)
        p = jnp.where(jnp.any(complete, axis=-1)[None, :, None, None, None], p, 0.0)
        o_cmp = jnp.einsum('bqgrn,bngd->bqgrd', p.astype(vc.dtype), vc)
        imp = jnp.sum(p, axis=3)
        forced = (blk[None, :] == 0) | (blk[None, :] == cur[:, None]) | (blk[None, :] == cur[:, None] - 1)
        future = blk[None, :] > cur[:, None]
        score = jnp.where(forced[None, :, None, :], r + 1.0, imp)
        score = jnp.where(future[None, :, None, :], -1.0, score)
        _, sel = lax.top_k(score, n_sel)
        qb = qi.shape[1]
        ks = kb[b_idx, g_idx, sel].reshape(bsz, qb, g, n_sel * NSA_BLOCK, hd)
        vs = vb[b_idx, g_idx, sel].reshape(bsz, qb, g, n_sel * NSA_BLOCK, hd)
        kpos = sel[..., None] * NSA_BLOCK + jnp.arange(NSA_BLOCK)
        valid = (kpos <= pos[None, :, None, None, None]).reshape(bsz, qb, g, n_sel * NSA_BLOCK)
        o_sel, _ = attend_gathered(qi, ks, vs, valid)
        return o_cmp, o_sel

    nq = tq // qblk
    qs = q.reshape(bsz, nq, qblk, g, r, hd).swapaxes(0, 1)
    o_cmp, o_sel = lax.map(one, (qs, q_pos.reshape(nq, qblk)))
    return (o_cmp.swapaxes(0, 1).reshape(bsz, tq, g, r, hd),
            o_sel.swapaxes(0, 1).reshape(bsz, tq, g, r, hd))


def dilated_prompt(q, k, v, win, dil):
    bsz, t, nh, hd = q.shape
    n = t // dil

    def to_cls(arr):
        return arr.reshape(bsz, n, dil, nh, hd).transpose(0, 2, 1, 3, 4).reshape(bsz * dil, n, nh, hd)

    o, lse = banded_attention(to_cls(q)[:, :, :, None], to_cls(k), to_cls(v), win // dil, math.gcd(n, QUERY_BLOCK))
    o = o[:, :, :, 0].reshape(bsz, dil, n, nh, hd).transpose(0, 2, 1, 3, 4).reshape(bsz, t, nh, hd)
    lse = lse[..., 0].reshape(bsz, dil, n, nh).transpose(0, 2, 1, 3).reshape(bsz, t, nh)
    return o, lse


def mixer_ab(h, lp, past, page_table, is_prompt):
    bsz, t, _ = h.shape
    z, xbc, dt_raw, q, kv, gate = split_cols(h @ lp['w_in'], [SSM_INNER, SSM_CONV_DIM, SSM_HEADS, NSA_HEADS * HEAD_DIM,
                                                          6 * NSA_KV_HEADS * HEAD_DIM, 3 * NSA_HEADS])
    conv_buf = jnp.zeros((bsz, CONV_K - 1, SSM_CONV_DIM), h.dtype) if is_prompt else past['ssm_conv']
    xbc, new_conv = causal_conv(xbc, conv_buf, lp['ssm_conv_w'], lp['ssm_conv_b'])
    xs, bm, cm = split_cols(jax.nn.silu(xbc), [SSM_INNER, SSM_GROUPS * SSM_STATE, SSM_GROUPS * SSM_STATE])
    xs = xs.reshape(bsz, t, SSM_HEADS, SSM_HEAD_DIM)
    bm = bm.reshape(bsz, t, SSM_GROUPS, SSM_STATE)
    cm = cm.reshape(bsz, t, SSM_GROUPS, SSM_STATE)
    dt = jax.nn.softplus(dt_raw.astype(jnp.float32) + lp['ssm_dt_bias'])
    a = -jnp.exp(lp['ssm_a_log'].astype(jnp.float32))
    h0 = jnp.zeros((bsz, SSM_HEADS, SSM_HEAD_DIM, SSM_STATE), jnp.float32) if is_prompt else past['ssm']
    chunk = math.gcd(t, SSM_CHUNK) if is_prompt else t
    y, new_ssm = ssd_scan(xs, dt, a, bm, cm, h0, chunk)
    y = y + lp['ssm_d'].astype(jnp.float32)[:, None] * xs.astype(jnp.float32)
    y_ssm = rmsnorm(y.reshape(bsz, t, SSM_INNER) * jax.nn.silu(z.astype(jnp.float32)), lp['ssm_norm_w']).astype(h.dtype)
    q = q.reshape(bsz, t, NSA_KV_HEADS, NSA_REP, HEAD_DIM)
    kv = kv.reshape(bsz, t, 6, NSA_KV_HEADS, HEAD_DIM)
    rows_new = kv[:, :, 0:4]
    win_new = kv[:, :, 4:6]
    gate = jax.nn.sigmoid(gate.astype(jnp.float32)).reshape(bsz, t, 3, NSA_KV_HEADS, NSA_REP)
    if is_prompt:
        pos = jnp.arange(t)
        rows_all = rows_new
        qblk = math.gcd(t, QUERY_BLOCK)
        o_win, _ = banded_attention(q, win_new[:, :, 0], win_new[:, :, 1], NSA_WINDOW, qblk)
        new_win = win_new[:, t - min(NSA_WINDOW, t):]
    else:
        n_pages = page_table.shape[1]
        past_len = n_pages * PAGE_SIZE
        past_rows = past['nsa_kv'][page_table].reshape(bsz, past_len, 4, NSA_KV_HEADS, HEAD_DIM)
        rows_all = jnp.concatenate([past_rows.astype(rows_new.dtype), rows_new], axis=1)
        pos = past_len + jnp.arange(t)
        qblk = 1
        lw = past['nsa_win'].shape[1]
        win_all = jnp.concatenate([past['nsa_win'].astype(win_new.dtype), win_new], axis=1)
        kpos = past_len - lw + jnp.arange(lw + t)
        diff = pos[:, None] - kpos[None, :]
        o_win, _ = attend_dense(q, win_all[:, :, 0], win_all[:, :, 1], ((diff >= 0) & (diff <= NSA_WINDOW))[None])
        new_win = win_all[:, t:]
    o_cmp, o_sel = nsa_global(q, rows_all[:, :, 0], rows_all[:, :, 1], rows_all[:, :, 2], rows_all[:, :, 3], pos, lp, qblk)
    o_nsa = (gate[:, :, 0, :, :, None] * o_cmp + gate[:, :, 1, :, :, None] * o_sel
             + gate[:, :, 2, :, :, None] * o_win)
    y_nsa = o_nsa.reshape(bsz, t, NSA_HEADS * HEAD_DIM).astype(h.dtype)
    out = jnp.concatenate([y_ssm, y_nsa], axis=-1) @ lp['w_out']
    return out, (new_conv, new_ssm, rows_new, new_win)


def mixer_cd(h, lp, past, is_prompt):
    bsz, t, _ = h.shape
    qkv, z, b_raw, a_raw, dil = split_cols(h @ lp['w_in'], [3 * GDN_WIDTH, GDN_WIDTH, GDN_HEADS, GDN_HEADS, DIL_IN])
    conv_buf = jnp.zeros((bsz, CONV_K - 1, 3 * GDN_WIDTH), h.dtype) if is_prompt else past['gdn_conv']
    qkv, new_conv = causal_conv(qkv, conv_buf, lp['gdn_conv_w'], None)
    qkv = jax.nn.silu(qkv).reshape(bsz, t, 3, GDN_HEADS, GDN_HEAD_DIM)
    q = l2norm(qkv[:, :, 0]) * (GDN_HEAD_DIM ** -0.5)
    k = l2norm(qkv[:, :, 1])
    v = qkv[:, :, 2]
    beta = jax.nn.sigmoid(b_raw.astype(jnp.float32))
    g = -jnp.exp(lp['gdn_a_log'].astype(jnp.float32)) * jax.nn.softplus(a_raw.astype(jnp.float32) + lp['gdn_dt_bias'])
    s0 = jnp.zeros((bsz, GDN_HEADS, GDN_HEAD_DIM, GDN_HEAD_DIM), jnp.float32) if is_prompt else past['gdn']
    chunk = math.gcd(t, GDN_CHUNK) if is_prompt else t
    o, new_gdn = gated_delta(q, k, v, g, beta, s0, chunk)
    o = rmsnorm(o, lp['gdn_norm_w']) * jax.nn.silu(z.astype(jnp.float32)).reshape(bsz, t, GDN_HEADS, GDN_HEAD_DIM)
    y_gdn = o.reshape(bsz, t, GDN_WIDTH).astype(h.dtype)
    dil = dil.reshape(bsz, t, 3, DIL_GROUPS, DIL_HEADS_PER_GROUP, HEAD_DIM)
    outs, lses, bufs = [], [], []
    for gi, (win, dstep) in enumerate(DIL_PATTERNS):
        qg, kg, vg = dil[:, :, 0, gi], dil[:, :, 1, gi], dil[:, :, 2, gi]
        kv_new = jnp.stack([kg, vg], axis=2)
        if is_prompt:
            o_g, lse_g = dilated_prompt(qg, kg, vg, win, dstep)
            bufs.append(kv_new[:, t - min(win, t):])
        else:
            buf = past['dil'][gi]
            lg = buf.shape[1]
            kv_all = jnp.concatenate([buf.astype(kv_new.dtype), kv_new], axis=1)
            n_keys = win // dstep + 1
            idx = lg + jnp.arange(t)[:, None] - dstep * jnp.arange(n_keys)[None, :]
            kv_g = kv_all[:, jnp.maximum(idx, 0)]
            kk = kv_g[:, :, :, 0].transpose(0, 1, 3, 2, 4)
            vv = kv_g[:, :, :, 1].transpose(0, 1, 3, 2, 4)
            o_g, lse_g = attend_gathered(qg[:, :, :, None], kk, vv, (idx >= 0)[None, :, None, :])
            o_g, lse_g = o_g[:, :, :, 0], lse_g[..., 0]
            bufs.append(kv_all[:, t:])
        outs.append(o_g)
        lses.append(lse_g)
    wts = jax.nn.softmax(jnp.stack(lses), axis=0)
    o_dil = jnp.sum(wts[..., None] * jnp.stack(outs).astype(jnp.float32), axis=0)
    y_dil = o_dil.reshape(bsz, t, DIL_HEADS_PER_GROUP * HEAD_DIM).astype(h.dtype)
    out = jnp.concatenate([y_gdn, y_dil], axis=-1) @ lp['w_out']
    return out, (new_conv, new_gdn, bufs[0], bufs[1], bufs[2])


def sq_relu_mlp(h, w1, w2):
    a = jax.nn.relu(h @ w1)
    return (a * a) @ w2


def stack_states(states, i):
    return jnp.stack([s[i] for s in states])


def setup_inputs(seed: int = 0) -> dict:
    key = jax.random.key(seed)
    keys = iter(jax.random.split(key, 64))
    f32 = jnp.float32

    def normal(shape, scale):
        return jax.random.normal(next(keys), shape, f32) * scale

    def gain(shape):
        return 1.0 + normal(shape, 0.02)

    def dt_bias(shape):
        dt = jnp.exp(jax.random.uniform(next(keys), shape, f32, minval=math.log(1e-3), maxval=math.log(1e-1)))
        return dt + jnp.log(-jnp.expm1(-dt))

    def a_log(shape):
        return jnp.log(jax.random.uniform(next(keys), shape, f32, minval=1.0, maxval=16.0))

    n_pages = PAST_LEN // PAGE_SIZE
    n_used = DEC_BATCH * n_pages
    n_phys = n_used + (n_used + 3) // 4
    perm = jax.random.permutation(next(keys), n_phys)
    page_table = perm[:n_used].reshape(DEC_BATCH, n_pages).astype(jnp.int32)
    lw = min(NSA_WINDOW, PAST_LEN)
    dil_kv = [normal((N_CD, DEC_BATCH, min(w, PAST_LEN), 2, DIL_HEADS_PER_GROUP, HEAD_DIM), 1.0) for w, _ in DIL_PATTERNS]
    blk_fan = NSA_BLOCK * HEAD_DIM
    return {
        'x_prompt': normal((BATCH, SEQ, D_MODEL), 1.0),
        'x_sample': normal((DEC_BATCH, DEC_SEQ, D_MODEL), 1.0),
        'cache_ssm_conv': normal((N_AB, DEC_BATCH, CONV_K - 1, SSM_CONV_DIM), 1.0),
        'state_ssm': normal((N_AB, DEC_BATCH, SSM_HEADS, SSM_HEAD_DIM, SSM_STATE), 0.1),
        'cache_nsa_kv': normal((N_AB, n_phys, PAGE_SIZE, 4, NSA_KV_HEADS, HEAD_DIM), 1.0),
        'cache_nsa_win_kv': normal((N_AB, DEC_BATCH, lw, 2, NSA_KV_HEADS, HEAD_DIM), 1.0),
        'cache_gdn_conv': normal((N_CD, DEC_BATCH, CONV_K - 1, 3 * GDN_WIDTH), 1.0),
        'state_gdn': normal((N_CD, DEC_BATCH, GDN_HEADS, GDN_HEAD_DIM, GDN_HEAD_DIM), 0.1),
        'cache_dil0_kv': dil_kv[0],
        'cache_dil1_kv': dil_kv[1],
        'cache_dil2_kv': dil_kv[2],
        'page_table': page_table,
        'norm_mix_pre': gain((DEPTH, D_MODEL)),
        'norm_mix_post': gain((DEPTH, D_MODEL)),
        'norm_mlp_pre': gain((DEPTH, D_MODEL)),
        'norm_mlp_post': gain((DEPTH, D_MODEL)),
        'mlp_w1': normal((DEPTH, D_MODEL, MLP_HIDDEN), D_MODEL ** -0.5),
        'mlp_w2': normal((DEPTH, MLP_HIDDEN, D_MODEL), MLP_HIDDEN ** -0.5),
        'ab_w_in': normal((N_AB, D_MODEL, AB_IN), D_MODEL ** -0.5),
        'ab_w_out': normal((N_AB, AB_MIX, D_MODEL), AB_MIX ** -0.5),
        'ssm_conv_w': normal((N_AB, CONV_K, SSM_CONV_DIM), CONV_K ** -0.5),
        'ssm_conv_b': normal((N_AB, SSM_CONV_DIM), 0.02),
        'ssm_dt_bias': dt_bias((N_AB, SSM_HEADS)),
        'ssm_a_log': a_log((N_AB, SSM_HEADS)),
        'ssm_d': gain((N_AB, SSM_HEADS)),
        'ssm_norm_w': gain((N_AB, SSM_INNER)),
        'nsa_pe_k': normal((N_AB, NSA_BLOCK, HEAD_DIM), 0.1),
        'nsa_pe_v': normal((N_AB, NSA_BLOCK, HEAD_DIM), 0.1),
        'nsa_ck_w1': normal((N_AB, blk_fan, NSA_CMP_HIDDEN), blk_fan ** -0.5),
        'nsa_ck_w2': normal((N_AB, NSA_CMP_HIDDEN, HEAD_DIM), NSA_CMP_HIDDEN ** -0.5),
        'nsa_cv_w1': normal((N_AB, blk_fan, NSA_CMP_HIDDEN), blk_fan ** -0.5),
        'nsa_cv_w2': normal((N_AB, NSA_CMP_HIDDEN, HEAD_DIM), NSA_CMP_HIDDEN ** -0.5),
        'cd_w_in': normal((N_CD, D_MODEL, CD_IN), D_MODEL ** -0.5),
        'cd_w_out': normal((N_CD, CD_MIX, D_MODEL), CD_MIX ** -0.5),
        'gdn_conv_w': normal((N_CD, CONV_K, 3 * GDN_WIDTH), CONV_K ** -0.5),
        'gdn_dt_bias': dt_bias((N_CD, GDN_HEADS)),
        'gdn_a_log': a_log((N_CD, GDN_HEADS)),
        'gdn_norm_w': gain((N_CD, GDN_HEAD_DIM)),
    }


def reference(x_prompt, x_sample, cache_ssm_conv, state_ssm, cache_nsa_kv, cache_nsa_win_kv, cache_gdn_conv, state_gdn,
              cache_dil0_kv, cache_dil1_kv, cache_dil2_kv, page_table, norm_mix_pre, norm_mix_post, norm_mlp_pre,
              norm_mlp_post, mlp_w1, mlp_w2, ab_w_in, ab_w_out, ssm_conv_w, ssm_conv_b, ssm_dt_bias, ssm_a_log, ssm_d,
              ssm_norm_w, nsa_pe_k, nsa_pe_v, nsa_ck_w1, nsa_ck_w2, nsa_cv_w1, nsa_cv_w2, cd_w_in, cd_w_out, gdn_conv_w,
              gdn_dt_bias, gdn_a_log, gdn_norm_w):
    yp, ys = x_prompt, x_sample
    ab_p, ab_s, cd_p, cd_s = [], [], [], []
    for l in range(DEPTH):
        j = l // 2
        hp = rmsnorm(yp, norm_mix_pre[l])
        hs = rmsnorm(ys, norm_mix_pre[l])
        if l % 2 == 0:
            lp = {'w_in': ab_w_in[j], 'w_out': ab_w_out[j], 'ssm_conv_w': ssm_conv_w[j], 'ssm_conv_b': ssm_conv_b[j],
                  'ssm_dt_bias': ssm_dt_bias[j], 'ssm_a_log': ssm_a_log[j], 'ssm_d': ssm_d[j], 'ssm_norm_w': ssm_norm_w[j],
                  'nsa_pe_k': nsa_pe_k[j], 'nsa_pe_v': nsa_pe_v[j], 'nsa_ck_w1': nsa_ck_w1[j], 'nsa_ck_w2': nsa_ck_w2[j],
                  'nsa_cv_w1': nsa_cv_w1[j], 'nsa_cv_w2': nsa_cv_w2[j]}
            past = {'ssm_conv': cache_ssm_conv[j], 'ssm': state_ssm[j], 'nsa_kv': cache_nsa_kv[j], 'nsa_win': cache_nsa_win_kv[j]}
            mp, stp = mixer_ab(hp, lp, None, None, True)
            ms, sts = mixer_ab(hs, lp, past, page_table, False)
            ab_p.append(stp)
            ab_s.append(sts)
        else:
            lp = {'w_in': cd_w_in[j], 'w_out': cd_w_out[j], 'gdn_conv_w': gdn_conv_w[j], 'gdn_dt_bias': gdn_dt_bias[j],
                  'gdn_a_log': gdn_a_log[j], 'gdn_norm_w': gdn_norm_w[j]}
            past = {'gdn_conv': cache_gdn_conv[j], 'gdn': state_gdn[j],
                    'dil': (cache_dil0_kv[j], cache_dil1_kv[j], cache_dil2_kv[j])}
            mp, stp = mixer_cd(hp, lp, None, True)
            ms, sts = mixer_cd(hs, lp, past, False)
            cd_p.append(stp)
            cd_s.append(sts)
        yp = yp + rmsnorm(mp, norm_mix_post[l])
        ys = ys + rmsnorm(ms, norm_mix_post[l])
        yp = yp + rmsnorm(sq_relu_mlp(rmsnorm(yp, norm_mlp_pre[l]), mlp_w1[l], mlp_w2[l]), norm_mlp_post[l])
        ys = ys + rmsnorm(sq_relu_mlp(rmsnorm(ys, norm_mlp_pre[l]), mlp_w1[l], mlp_w2[l]), norm_mlp_post[l])
    return (yp, ys,
            stack_states(ab_p, 0), stack_states(ab_s, 0), stack_states(ab_p, 1), stack_states(ab_s, 1),
            stack_states(ab_p, 2), stack_states(ab_s, 2), stack_states(ab_p, 3), stack_states(ab_s, 3),
            stack_states(cd_p, 0), stack_states(cd_s, 0), stack_states(cd_p, 1), stack_states(cd_s, 1),
            stack_states(cd_p, 2), stack_states(cd_s, 2), stack_states(cd_p, 3), stack_states(cd_s, 3),
            stack_states(cd_p, 4), stack_states(cd_s, 4))
```

```python
import functools
import math

import jax
import jax.numpy as jnp
from jax import lax
from jax.experimental import pallas as pl
from jax.experimental.pallas import tpu as pltpu

F32 = jnp.float32
BF16 = jnp.bfloat16

D_MODEL = 1024
HEAD_DIM = 64
CONV_K = 4
RMS_EPS = 1e-6
MLP_HIDDEN = 4 * D_MODEL
QUERY_BLOCK = 128
PAGE_SIZE = 128

SSM_HEADS = 16
SSM_HEAD_DIM = 64
SSM_INNER = SSM_HEADS * SSM_HEAD_DIM
SSM_GROUPS = 2
SSM_STATE = 128
SSM_CONV_DIM = SSM_INNER + 2 * SSM_GROUPS * SSM_STATE
SSM_CHUNK = 128

NSA_HEADS = 16
NSA_KV_HEADS = 2
NSA_REP = NSA_HEADS // NSA_KV_HEADS
NSA_BLOCK = 64
NSA_TOPK = 16
NSA_WINDOW = 512
NSA_CMP_HIDDEN = 256

GDN_HEADS = 8
GDN_HEAD_DIM = 128
GDN_WIDTH = GDN_HEADS * GDN_HEAD_DIM
GDN_CHUNK = 64

DIL_PATTERNS = ((128, 1), (512, 4), (2048, 16))
DIL_GROUPS = len(DIL_PATTERNS)
DIL_HEADS_PER_GROUP = 8
DIL_WIDTH = DIL_HEADS_PER_GROUP * HEAD_DIM
DIL_IN = 3 * DIL_GROUPS * DIL_WIDTH
NEG = -1e30

VMEM_LIMIT_BYTES = 56 * 1024 * 1024
LANE = 128


def _cparams(*sem):
    return pltpu.CompilerParams(dimension_semantics=sem, vmem_limit_bytes=VMEM_LIMIT_BYTES)


def _const_spec(shape):
    nd = len(shape)
    return pl.BlockSpec(shape, lambda *_: (0,) * nd, pipeline_mode=pl.Buffered(1))


def _rms(x, w):
    return x * lax.rsqrt(jnp.mean(x * x, axis=-1, keepdims=True) + RMS_EPS) * w


def _proj_body(x_ref, nw_ref, *refs):
    n = len(refs) // 2
    xn = _rms(x_ref[...], nw_ref[...]).astype(BF16)
    for w_ref, o_ref in zip(refs[:n], refs[n:]):
        o_ref[...] = jnp.dot(xn, w_ref[...], preferred_element_type=F32)


def _norm_proj(x, norm_w, w_segs, tm):
    n_tok, d = x.shape
    widths = [w.shape[1] for w in w_segs]
    return pl.pallas_call(
        _proj_body,
        grid=(n_tok // tm,),
        in_specs=[pl.BlockSpec((tm, d), lambda i: (i, 0)), _const_spec((1, d))]
        + [_const_spec((d, n)) for n in widths],
        out_specs=[pl.BlockSpec((tm, n), lambda i: (i, 0)) for n in widths],
        out_shape=[jax.ShapeDtypeStruct((n_tok, n), F32) for n in widths],
        compiler_params=_cparams("arbitrary"),
        name="norm_proj",
    )(x, norm_w.reshape(1, d), *w_segs)


def _post_body(a_ref, b_ref, y_ref, woa_ref, wob_ref, w1_ref, w2_ref, nmix_ref, npre_ref, npost_ref, o_ref):
    m = jnp.dot(a_ref[...].astype(BF16), woa_ref[...], preferred_element_type=F32)
    m = m + jnp.dot(b_ref[...].astype(BF16), wob_ref[...], preferred_element_type=F32)
    y1 = y_ref[...] + _rms(m, nmix_ref[...])
    h = _rms(y1, npre_ref[...]).astype(BF16)
    a = jnp.maximum(jnp.dot(h, w1_ref[...], preferred_element_type=F32), 0.0)
    m2 = jnp.dot((a * a).astype(BF16), w2_ref[...], preferred_element_type=F32)
    o_ref[...] = y1 + _rms(m2, npost_ref[...])


def _post_block(mix_a, mix_b, y, wo_a, wo_b, w1, w2, n_mix, n_pre, n_post, tm):
    n_tok, d = y.shape
    ka, kb = mix_a.shape[1], mix_b.shape[1]
    row = lambda n: pl.BlockSpec((tm, n), lambda i: (i, 0))
    return pl.pallas_call(
        _post_body,
        grid=(n_tok // tm,),
        in_specs=[row(ka), row(kb), row(d), _const_spec(wo_a.shape), _const_spec(wo_b.shape),
                  _const_spec(w1.shape), _const_spec(w2.shape), _const_spec((1, d)), _const_spec((1, d)),
                  _const_spec((1, d))],
        out_specs=row(d),
        out_shape=jax.ShapeDtypeStruct((n_tok, d), F32),
        compiler_params=_cparams("arbitrary"),
        name="post_block",
    )(mix_a, mix_b, y, wo_a, wo_b, w1, w2, n_mix.reshape(1, d), n_pre.reshape(1, d), n_post.reshape(1, d))


def _rmsnorm(x, w):
    xf = x.astype(F32)
    y = xf * lax.rsqrt(jnp.mean(xf * xf, axis=-1, keepdims=True) + RMS_EPS)
    return (y * w.astype(F32)).astype(x.dtype)


def _l2norm(x):
    xf = x.astype(F32)
    return xf * lax.rsqrt(jnp.sum(xf * xf, axis=-1, keepdims=True) + 1e-6)


def _causal_conv(x, buf, w, b):
    t = x.shape[1]
    xp = jnp.concatenate([buf.astype(x.dtype), x], axis=1)
    y = xp[:, 0:t] * w[0]
    for i in range(1, CONV_K):
        y = y + xp[:, i:i + t] * w[i]
    if b is not None:
        y = y + b
    return y, xp[:, t:]


def _attend_dense(q, k, v, mask):
    s = jnp.einsum('bqgrd,bkgd->bqgrk', q, k).astype(F32) * (q.shape[-1] ** -0.5)
    s = jnp.where(mask[:, :, None, None, :], s, NEG)
    m = jnp.max(s, axis=-1, keepdims=True)
    p = jnp.exp(s - m)
    l = jnp.sum(p, axis=-1, keepdims=True)
    o = jnp.einsum('bqgrk,bkgd->bqgrd', (p / l).astype(v.dtype), v)
    return o, (m + jnp.log(l))[..., 0]


def _attend_gathered(q, k, v, mask):
    s = jnp.einsum('bqgrd,bqgkd->bqgrk', q, k).astype(F32) * (q.shape[-1] ** -0.5)
    s = jnp.where(mask[:, :, :, None, :], s, NEG)
    m = jnp.max(s, axis=-1, keepdims=True)
    p = jnp.exp(s - m)
    l = jnp.sum(p, axis=-1, keepdims=True)
    o = jnp.einsum('bqgrk,bqgkd->bqgrd', (p / l).astype(v.dtype), v)
    return o, (m + jnp.log(l))[..., 0]


def _banded_attention(q, k, v, window, qblk):
    bsz, t, g, r, hd = q.shape
    nq = t // qblk
    span = qblk + window
    kp = jnp.pad(k, ((0, 0), (window, 0), (0, 0), (0, 0)))
    vp = jnp.pad(v, ((0, 0), (window, 0), (0, 0), (0, 0)))
    diff = jnp.arange(qblk)[:, None] - jnp.arange(span)[None, :] + window
    band = (diff >= 0) & (diff <= window)

    def one(args):
        i, qi = args
        start = i * qblk
        ks = lax.dynamic_slice_in_dim(kp, start, span, axis=1)
        vs = lax.dynamic_slice_in_dim(vp, start, span, axis=1)
        kpos = start + jnp.arange(span) - window
        return _attend_dense(qi, ks, vs, (band & (kpos >= 0)[None, :])[None])

    qs = q.reshape(bsz, nq, qblk, g, r, hd).swapaxes(0, 1)
    o, lse = lax.map(one, (jnp.arange(nq), qs))
    return o.swapaxes(0, 1).reshape(bsz, t, g, r, hd), lse.swapaxes(0, 1).reshape(bsz, t, g, r)


def _ssd_scan(x, dt, a, bm, cm, h0, chunk):
    bsz, t, nh, hp = x.shape
    ng, ns = bm.shape[2], bm.shape[3]
    rep = nh // ng
    nc = t // chunk
    la = (dt * a).reshape(bsz, nc, chunk, ng, rep)
    xd = (x.astype(F32) * dt[..., None]).reshape(bsz, nc, chunk, ng, rep, hp)
    bc = bm.astype(F32).reshape(bsz, nc, chunk, ng, ns)
    cc = cm.astype(F32).reshape(bsz, nc, chunk, ng, ns)
    acs = jnp.cumsum(la, axis=2)
    ii = jnp.arange(chunk)
    incl = ii[:, None] >= ii[None, :]
    lmat = jnp.exp(jnp.where(incl[:, :, None, None], acs[:, :, :, None] - acs[:, :, None, :], -jnp.inf))
    cb = jnp.einsum('bcign,bcjgn->bcijg', cc, bc)
    y_diag = jnp.einsum('bcijgr,bcjgrp->bcigrp', cb[..., None] * lmat, xd)
    states = jnp.einsum('bcjgn,bcjgrp->bcgrpn', bc, xd * jnp.exp(acs[:, :, -1:] - acs)[..., None])
    cdec = jnp.exp(acs[:, :, -1])

    def step(h, inp):
        s, d = inp
        return d[..., None, None] * h + s, h

    h_fin, h_prev = lax.scan(step, h0.astype(F32).reshape(bsz, ng, rep, hp, ns),
                             (jnp.moveaxis(states, 1, 0), jnp.moveaxis(cdec, 1, 0)))
    h_prev = jnp.moveaxis(h_prev, 0, 1)
    y_off = jnp.einsum('bcign,bcgrpn->bcigrp', cc, h_prev) * jnp.exp(acs)[..., None]
    return (y_diag + y_off).reshape(bsz, t, nh, hp), h_fin.reshape(bsz, nh, hp, ns)


def _gated_delta(q, k, v, g, beta, s0, chunk):
    bsz, t, nh, dk = q.shape
    dv = v.shape[-1]
    nc = t // chunk

    def rs(arr):
        return arr.astype(F32).reshape((bsz, nc, chunk) + arr.shape[2:])

    q, k, v, beta = rs(q), rs(k), rs(v), rs(beta)
    gc = jnp.cumsum(rs(g), axis=2)
    kb = k * beta[..., None]
    vb = v * beta[..., None]
    ii = jnp.arange(chunk)
    incl = ii[:, None] >= ii[None, :]
    strict = ii[:, None] > ii[None, :]
    decay = jnp.exp(jnp.where(incl[:, :, None], gc[:, :, :, None, :] - gc[:, :, None, :, :], -jnp.inf))
    amat = jnp.where(strict[:, :, None], jnp.einsum('bcihk,bcjhk->bcijh', kb, k) * decay, 0.0)
    amat = jnp.moveaxis(amat, -1, 2)
    eye = jnp.eye(chunk, dtype=F32)
    tmat = lax.linalg.triangular_solve(amat + eye, jnp.broadcast_to(eye, amat.shape),
                                       left_side=True, lower=True, unit_diagonal=True)
    u = jnp.einsum('bchij,bcjhv->bcihv', tmat, vb)
    w = jnp.einsum('bchij,bcjhk->bcihk', tmat, kb * jnp.exp(gc)[..., None])
    qk = jnp.where(incl[:, :, None], jnp.einsum('bcihk,bcjhk->bcijh', q, k) * decay, 0.0)
    qg = q * jnp.exp(gc)[..., None]
    kdec = k * jnp.exp(gc[:, :, -1:] - gc)[..., None]
    glast = jnp.exp(gc[:, :, -1])

    def step(s, inp):
        qg_c, kdec_c, u_c, w_c, qk_c, gl = inp
        v_new = u_c - jnp.einsum('bihk,bhkv->bihv', w_c, s)
        o = jnp.einsum('bihk,bhkv->bihv', qg_c, s) + jnp.einsum('bijh,bjhv->bihv', qk_c, v_new)
        s = s * gl[:, :, None, None] + jnp.einsum('bjhk,bjhv->bhkv', kdec_c, v_new)
        return s, o

    xs = tuple(jnp.moveaxis(arr, 1, 0) for arr in (qg, kdec, u, w, qk, glast))
    s_fin, o = lax.scan(step, s0.astype(F32), xs)
    return jnp.moveaxis(o, 0, 1).reshape(bsz, t, nh, dv), s_fin


def _compress_blocks(k, pe, w1, w2):
    bsz, length, g, hd = k.shape
    nb = length // NSA_BLOCK
    kb = k.reshape(bsz, nb, NSA_BLOCK, g, hd) + pe[:, None, :]
    flat = kb.transpose(0, 1, 3, 2, 4).reshape(bsz, nb, g, NSA_BLOCK * hd)
    return jax.nn.silu(flat @ w1) @ w2


def _nsa_global(q, k_cmp, v_cmp, k_sel, v_sel, q_pos, lp, qblk):
    bsz, tq, g, r, hd = q.shape
    length = k_cmp.shape[1]
    nb = -(-length // NSA_BLOCK)
    pad = ((0, 0), (0, nb * NSA_BLOCK - length), (0, 0), (0, 0))
    kc = _compress_blocks(jnp.pad(k_cmp, pad), lp['nsa_pe_k'], lp['nsa_ck_w1'], lp['nsa_ck_w2'])
    vc = _compress_blocks(jnp.pad(v_cmp, pad), lp['nsa_pe_v'], lp['nsa_cv_w1'], lp['nsa_cv_w2'])
    kb = jnp.pad(k_sel, pad).reshape(bsz, nb, NSA_BLOCK, g, hd).transpose(0, 3, 1, 2, 4)
    vb = jnp.pad(v_sel, pad).reshape(bsz, nb, NSA_BLOCK, g, hd).transpose(0, 3, 1, 2, 4)
    n_sel = min(NSA_TOPK, nb)
    blk = jnp.arange(nb)
    b_idx = jnp.arange(bsz)[:, None, None, None]
    g_idx = jnp.arange(g)[None, None, :, None]
    scale = hd ** -0.5

    def one(args):
        qi, pos = args
        cur = pos // NSA_BLOCK
        complete = (blk[None, :] + 1) * NSA_BLOCK <= pos[:, None] + 1
        s = jnp.einsum('bqgrd,bngd->bqgrn', qi, kc).astype(F32) * scale
        s = jnp.where(complete[None, :, None, None, :], s, NEG)
        p = jax.nn.softmax(s, axis=-1)
        p = jnp.where(jnp.any(complete, axis=-1)[None, :, None, None, None], p, 0.0)
        o_cmp = jnp.einsum('bqgrn,bngd->bqgrd', p.astype(vc.dtype), vc)
        imp = jnp.sum(p, axis=3)
        forced = (blk[None, :] == 0) | (blk[None, :] == cur[:, None]) | (blk[None, :] == cur[:, None] - 1)
        future = blk[None, :] > cur[:, None]
        score = jnp.where(forced[None, :, None, :], r + 1.0, imp)
        score = jnp.where(future[None, :, None, :], -1.0, score)
        _, sel = lax.top_k(score, n_sel)
        qb = qi.shape[1]
        ks = kb[b_idx, g_idx, sel].reshape(bsz, qb, g, n_sel * NSA_BLOCK, hd)
        vs = vb[b_idx, g_idx, sel].reshape(bsz, qb, g, n_sel * NSA_BLOCK, hd)
        kpos = sel[..., None] * NSA_BLOCK + jnp.arange(NSA_BLOCK)
        valid = (kpos <= pos[None, :, None, None, None]).reshape(bsz, qb, g, n_sel * NSA_BLOCK)
        o_sel, _ = _attend_gathered(qi, ks, vs, valid)
        return o_cmp, o_sel

    nq = tq // qblk
    qs = q.reshape(bsz, nq, qblk, g, r, hd).swapaxes(0, 1)
    o_cmp, o_sel = lax.map(one, (qs, q_pos.reshape(nq, qblk)))
    return (o_cmp.swapaxes(0, 1).reshape(bsz, tq, g, r, hd),
            o_sel.swapaxes(0, 1).reshape(bsz, tq, g, r, hd))


def _dilated_prompt(q, k, v, win, dil):
    bsz, t, nh, hd = q.shape
    n = t // dil

    def to_cls(arr):
        return arr.reshape(bsz, n, dil, nh, hd).transpose(0, 2, 1, 3, 4).reshape(bsz * dil, n, nh, hd)

    o, lse = _banded_attention(to_cls(q)[:, :, :, None], to_cls(k), to_cls(v), win // dil, math.gcd(n, QUERY_BLOCK))
    o = o[:, :, :, 0].reshape(bsz, dil, n, nh, hd).transpose(0, 2, 1, 3, 4).reshape(bsz, t, nh, hd)
    lse = lse[..., 0].reshape(bsz, dil, n, nh).transpose(0, 2, 1, 3).reshape(bsz, t, nh)
    return o, lse


def _mixer_ab(z, xbc, dt_raw, q, rows_new, win_new, gate, lp, past, page_table, is_prompt):
    bsz, t, _ = z.shape
    dtype = z.dtype
    conv_buf = jnp.zeros((bsz, CONV_K - 1, SSM_CONV_DIM), dtype) if is_prompt else past['ssm_conv']
    xbc, new_conv = _causal_conv(xbc, conv_buf, lp['ssm_conv_w'], lp['ssm_conv_b'])
    act = jax.nn.silu(xbc)
    xs = act[..., :SSM_INNER].reshape(bsz, t, SSM_HEADS, SSM_HEAD_DIM)
    bm = act[..., SSM_INNER:SSM_INNER + SSM_GROUPS * SSM_STATE].reshape(bsz, t, SSM_GROUPS, SSM_STATE)
    cm = act[..., SSM_INNER + SSM_GROUPS * SSM_STATE:].reshape(bsz, t, SSM_GROUPS, SSM_STATE)
    dt = jax.nn.softplus(dt_raw.astype(F32) + lp['ssm_dt_bias'])
    a = -jnp.exp(lp['ssm_a_log'].astype(F32))
    h0 = jnp.zeros((bsz, SSM_HEADS, SSM_HEAD_DIM, SSM_STATE), F32) if is_prompt else past['ssm']
    chunk = math.gcd(t, SSM_CHUNK) if is_prompt else t
    y, new_ssm = _ssd_scan(xs, dt, a, bm, cm, h0, chunk)
    y = y + lp['ssm_d'].astype(F32)[:, None] * xs.astype(F32)
    y_ssm = _rmsnorm(y.reshape(bsz, t, SSM_INNER) * jax.nn.silu(z.astype(F32)), lp['ssm_norm_w']).astype(dtype)
    q = q.reshape(bsz, t, NSA_KV_HEADS, NSA_REP, HEAD_DIM)
    rows_new = rows_new.reshape(bsz, t, 4, NSA_KV_HEADS, HEAD_DIM)
    win_new = win_new.reshape(bsz, t, 2, NSA_KV_HEADS, HEAD_DIM)
    gate = jax.nn.sigmoid(gate.astype(F32)).reshape(bsz, t, 3, NSA_KV_HEADS, NSA_REP)
    if is_prompt:
        pos = jnp.arange(t)
        rows_all = rows_new
        qblk = math.gcd(t, QUERY_BLOCK)
        o_win, _ = _banded_attention(q, win_new[:, :, 0], win_new[:, :, 1], NSA_WINDOW, qblk)
        new_win = win_new[:, t - min(NSA_WINDOW, t):]
    else:
        n_pages = page_table.shape[1]
        past_len = n_pages * PAGE_SIZE
        past_rows = past['nsa_kv'][page_table].reshape(bsz, past_len, 4, NSA_KV_HEADS, HEAD_DIM)
        rows_all = jnp.concatenate([past_rows.astype(rows_new.dtype), rows_new], axis=1)
        pos = past_len + jnp.arange(t)
        qblk = 1
        lw = past['nsa_win'].shape[1]
        win_all = jnp.concatenate([past['nsa_win'].astype(win_new.dtype), win_new], axis=1)
        kpos = past_len - lw + jnp.arange(lw + t)
        diff = pos[:, None] - kpos[None, :]
        o_win, _ = _attend_dense(q, win_all[:, :, 0], win_all[:, :, 1], ((diff >= 0) & (diff <= NSA_WINDOW))[None])
        new_win = win_all[:, t:]
    o_cmp, o_sel = _nsa_global(q, rows_all[:, :, 0], rows_all[:, :, 1], rows_all[:, :, 2], rows_all[:, :, 3], pos, lp, qblk)
    o_nsa = (gate[:, :, 0, :, :, None] * o_cmp + gate[:, :, 1, :, :, None] * o_sel
             + gate[:, :, 2, :, :, None] * o_win)
    y_nsa = o_nsa.reshape(bsz, t, NSA_HEADS * HEAD_DIM).astype(dtype)
    return y_ssm, y_nsa, (new_conv, new_ssm, rows_new, new_win)


def _mixer_cd(qkv, z, b_raw, a_raw, dil, lp, past, is_prompt):
    bsz, t, _ = z.shape
    dtype = z.dtype
    conv_buf = jnp.zeros((bsz, CONV_K - 1, 3 * GDN_WIDTH), dtype) if is_prompt else past['gdn_conv']
    qkv, new_conv = _causal_conv(qkv, conv_buf, lp['gdn_conv_w'], None)
    qkv = jax.nn.silu(qkv).reshape(bsz, t, 3, GDN_HEADS, GDN_HEAD_DIM)
    q = _l2norm(qkv[:, :, 0]) * (GDN_HEAD_DIM ** -0.5)
    k = _l2norm(qkv[:, :, 1])
    v = qkv[:, :, 2]
    beta = jax.nn.sigmoid(b_raw.astype(F32))
    g = -jnp.exp(lp['gdn_a_log'].astype(F32)) * jax.nn.softplus(a_raw.astype(F32) + lp['gdn_dt_bias'])
    s0 = jnp.zeros((bsz, GDN_HEADS, GDN_HEAD_DIM, GDN_HEAD_DIM), F32) if is_prompt else past['gdn']
    chunk = math.gcd(t, GDN_CHUNK) if is_prompt else t
    o, new_gdn = _gated_delta(q, k, v, g, beta, s0, chunk)
    o = _rmsnorm(o, lp['gdn_norm_w']) * jax.nn.silu(z.astype(F32)).reshape(bsz, t, GDN_HEADS, GDN_HEAD_DIM)
    y_gdn = o.reshape(bsz, t, GDN_WIDTH).astype(dtype)
    dil = dil.reshape(bsz, t, 3, DIL_GROUPS, DIL_HEADS_PER_GROUP, HEAD_DIM)
    outs, lses, bufs = [], [], []
    for gi, (win, dstep) in enumerate(DIL_PATTERNS):
        qg, kg, vg = dil[:, :, 0, gi], dil[:, :, 1, gi], dil[:, :, 2, gi]
        kv_new = jnp.stack([kg, vg], axis=2)
        if is_prompt:
            o_g, lse_g = _dilated_prompt(qg, kg, vg, win, dstep)
            bufs.append(kv_new[:, t - min(win, t):])
        else:
            buf = past['dil'][gi]
            lg = buf.shape[1]
            kv_all = jnp.concatenate([buf.astype(kv_new.dtype), kv_new], axis=1)
            n_keys = win // dstep + 1
            idx = lg + jnp.arange(t)[:, None] - dstep * jnp.arange(n_keys)[None, :]
            kv_g = kv_all[:, jnp.maximum(idx, 0)]
            kk = kv_g[:, :, :, 0].transpose(0, 1, 3, 2, 4)
            vv = kv_g[:, :, :, 1].transpose(0, 1, 3, 2, 4)
            o_g, lse_g = _attend_gathered(qg[:, :, :, None], kk, vv, (idx >= 0)[None, :, None, :])
            o_g, lse_g = o_g[:, :, :, 0], lse_g[..., 0]
            bufs.append(kv_all[:, t:])
        outs.append(o_g)
        lses.append(lse_g)
    wts = jax.nn.softmax(jnp.stack(lses), axis=0)
    o_dil = jnp.sum(wts[..., None] * jnp.stack(outs).astype(F32), axis=0)
    y_dil = o_dil.reshape(bsz, t, DIL_HEADS_PER_GROUP * HEAD_DIM).astype(dtype)
    return y_gdn, y_dil, (new_conv, new_gdn, bufs[0], bufs[1], bufs[2])


_AB_CUTS = (0, SSM_INNER, SSM_INNER + SSM_CONV_DIM, SSM_INNER + SSM_CONV_DIM + SSM_HEADS)
_AB_Q0 = _AB_CUTS[3]
_AB_KV0 = _AB_Q0 + NSA_HEADS * HEAD_DIM
_AB_WIN0 = _AB_KV0 + 4 * NSA_KV_HEADS * HEAD_DIM
_AB_GATE0 = _AB_WIN0 + 2 * NSA_KV_HEADS * HEAD_DIM
_AB_END = _AB_GATE0 + 3 * NSA_HEADS


def _pad_cols(w, n):
    return jnp.pad(w, ((0, 0), (0, n - w.shape[1])))


def _ab_weight_segs(w_in):
    small = jnp.concatenate([w_in[:, _AB_CUTS[2]:_AB_CUTS[3]], w_in[:, _AB_GATE0:_AB_END]], axis=1)
    segs = [w_in[:, _AB_CUTS[0]:_AB_CUTS[1]], w_in[:, _AB_CUTS[1]:_AB_CUTS[2]], w_in[:, _AB_Q0:_AB_KV0],
            w_in[:, _AB_KV0:_AB_WIN0], w_in[:, _AB_WIN0:_AB_GATE0], _pad_cols(small, LANE)]
    return [s.astype(BF16) for s in segs]


_CD_Z0 = 3 * GDN_WIDTH
_CD_B0 = _CD_Z0 + GDN_WIDTH
_CD_DIL0 = _CD_B0 + 2 * GDN_HEADS
_CD_END = _CD_DIL0 + DIL_IN


def _cd_weight_segs(w_in):
    segs = [w_in[:, :_CD_Z0], w_in[:, _CD_Z0:_CD_B0], w_in[:, _CD_DIL0:_CD_END],
            _pad_cols(w_in[:, _CD_B0:_CD_DIL0], LANE)]
    return [s.astype(BF16) for s in segs]


def _layer_ab(y, norm_pre, segs, lp, past, page_table, is_prompt, tm):
    bsz, t, d = y.shape
    z, xbc, q, rows, win, small = _norm_proj(y.reshape(bsz * t, d), norm_pre, segs, tm)
    r3 = lambda a: a.reshape(bsz, t, a.shape[-1])
    y_ssm, y_nsa, st = _mixer_ab(r3(z), r3(xbc), r3(small)[..., :SSM_HEADS], r3(q), r3(rows), r3(win),
                                 r3(small)[..., SSM_HEADS:SSM_HEADS + 3 * NSA_HEADS], lp, past, page_table, is_prompt)
    return y_ssm.reshape(bsz * t, -1), y_nsa.reshape(bsz * t, -1), st


def _layer_cd(y, norm_pre, segs, lp, past, is_prompt, tm):
    bsz, t, d = y.shape
    qkv, z, dil, small = _norm_proj(y.reshape(bsz * t, d), norm_pre, segs, tm)
    r3 = lambda a: a.reshape(bsz, t, a.shape[-1])
    y_gdn, y_dil, st = _mixer_cd(r3(qkv), r3(z), r3(small)[..., :GDN_HEADS], r3(small)[..., GDN_HEADS:2 * GDN_HEADS],
                                 r3(dil), lp, past, is_prompt)
    return y_gdn.reshape(bsz * t, -1), y_dil.reshape(bsz * t, -1), st


def kernel(x_prompt, x_sample, cache_ssm_conv, state_ssm, cache_nsa_kv, cache_nsa_win_kv, cache_gdn_conv, state_gdn,
           cache_dil0_kv, cache_dil1_kv, cache_dil2_kv, page_table, norm_mix_pre, norm_mix_post, norm_mlp_pre,
           norm_mlp_post, mlp_w1, mlp_w2, ab_w_in, ab_w_out, ssm_conv_w, ssm_conv_b, ssm_dt_bias, ssm_a_log, ssm_d,
           ssm_norm_w, nsa_pe_k, nsa_pe_v, nsa_ck_w1, nsa_ck_w2, nsa_cv_w1, nsa_cv_w2, cd_w_in, cd_w_out, gdn_conv_w,
           gdn_dt_bias, gdn_a_log, gdn_norm_w):
    depth = norm_mix_pre.shape[0]
    yp, ys = x_prompt, x_sample
    ab_p, ab_s, cd_p, cd_s = [], [], [], []
    tm = 256
    for l in range(depth):
        j = l // 2
        if l % 2 == 0:
            lp = {'ssm_conv_w': ssm_conv_w[j], 'ssm_conv_b': ssm_conv_b[j], 'ssm_dt_bias': ssm_dt_bias[j],
                  'ssm_a_log': ssm_a_log[j], 'ssm_d': ssm_d[j], 'ssm_norm_w': ssm_norm_w[j],
                  'nsa_pe_k': nsa_pe_k[j], 'nsa_pe_v': nsa_pe_v[j], 'nsa_ck_w1': nsa_ck_w1[j],
                  'nsa_ck_w2': nsa_ck_w2[j], 'nsa_cv_w1': nsa_cv_w1[j], 'nsa_cv_w2': nsa_cv_w2[j]}
            past = {'ssm_conv': cache_ssm_conv[j], 'ssm': state_ssm[j], 'nsa_kv': cache_nsa_kv[j],
                    'nsa_win': cache_nsa_win_kv[j]}
            segs = _ab_weight_segs(ab_w_in[j])
            w_out = ab_w_out[j].astype(BF16)
            ka = SSM_INNER
            ap, bp, stp = _layer_ab(yp, norm_mix_pre[l], segs, lp, None, None, True, tm)
            as_, bs, sts = _layer_ab(ys, norm_mix_pre[l], segs, lp, past, page_table, False, tm)
            ab_p.append(stp)
            ab_s.append(sts)
        else:
            lp = {'gdn_conv_w': gdn_conv_w[j], 'gdn_dt_bias': gdn_dt_bias[j], 'gdn_a_log': gdn_a_log[j],
                  'gdn_norm_w': gdn_norm_w[j]}
            past = {'gdn_conv': cache_gdn_conv[j], 'gdn': state_gdn[j],
                    'dil': (cache_dil0_kv[j], cache_dil1_kv[j], cache_dil2_kv[j])}
            segs = _cd_weight_segs(cd_w_in[j])
            w_out = cd_w_out[j].astype(BF16)
            ka = GDN_WIDTH
            ap, bp, stp = _layer_cd(yp, norm_mix_pre[l], segs, lp, None, True, tm)
            as_, bs, sts = _layer_cd(ys, norm_mix_pre[l], segs, lp, past, False, tm)
            cd_p.append(stp)
            cd_s.append(sts)
        w1 = mlp_w1[l].astype(BF16)
        w2 = mlp_w2[l].astype(BF16)
        post = functools.partial(_post_block, wo_a=w_out[:ka], wo_b=w_out[ka:], w1=w1, w2=w2, n_mix=norm_mix_post[l],
                                 n_pre=norm_mlp_pre[l], n_post=norm_mlp_post[l], tm=tm)
        yp = post(ap, bp, yp.reshape(-1, D_MODEL)).reshape(yp.shape)
        ys = post(as_, bs, ys.reshape(-1, D_MODEL)).reshape(ys.shape)
    stack = lambda states, i: jnp.stack([s[i] for s in states])
    return (yp, ys,
            stack(ab_p, 0), stack(ab_s, 0), stack(ab_p, 1), stack(ab_s, 1),
            stack(ab_p, 2), stack(ab_s, 2), stack(ab_p, 3), stack(ab_s, 3),
            stack(cd_p, 0), stack(cd_s, 0), stack(cd_p, 1), stack(cd_s, 1),
            stack(cd_p, 2), stack(cd_s, 2), stack(cd_p, 3), stack(cd_s, 3),
            stack(cd_p, 4), stack(cd_s, 4))
```

```python
import functools
import math

import jax
import jax.numpy as jnp
from jax import lax
from jax.experimental import pallas as pl
from jax.experimental.pallas import tpu as pltpu

F32 = jnp.float32
BF16 = jnp.bfloat16

D_MODEL = 1024
HEAD_DIM = 64
CONV_K = 4
RMS_EPS = 1e-6
MLP_HIDDEN = 4 * D_MODEL
QUERY_BLOCK = 128
PAGE_SIZE = 128

SSM_HEADS = 16
SSM_HEAD_DIM = 64
SSM_INNER = SSM_HEADS * SSM_HEAD_DIM
SSM_GROUPS = 2
SSM_STATE = 128
SSM_CONV_DIM = SSM_INNER + 2 * SSM_GROUPS * SSM_STATE
SSM_CHUNK = 128

NSA_HEADS = 16
NSA_KV_HEADS = 2
NSA_REP = NSA_HEADS // NSA_KV_HEADS
NSA_BLOCK = 64
NSA_TOPK = 16
NSA_WINDOW = 512
NSA_CMP_HIDDEN = 256

GDN_HEADS = 8
GDN_HEAD_DIM = 128
GDN_WIDTH = GDN_HEADS * GDN_HEAD_DIM
GDN_CHUNK = 64

DIL_PATTERNS = ((128, 1), (512, 4), (2048, 16))
DIL_GROUPS = len(DIL_PATTERNS)
DIL_HEADS_PER_GROUP = 8
DIL_WIDTH = DIL_HEADS_PER_GROUP * HEAD_DIM
DIL_IN = 3 * DIL_GROUPS * DIL_WIDTH
NEG = -1e30

VMEM_LIMIT_BYTES = 56 * 1024 * 1024
LANE = 128


def _cparams(*sem):
    return pltpu.CompilerParams(dimension_semantics=sem, vmem_limit_bytes=VMEM_LIMIT_BYTES)


def _const_spec(shape):
    nd = len(shape)
    return pl.BlockSpec(shape, lambda *_: (0,) * nd, pipeline_mode=pl.Buffered(1))


def _rms(x, w):
    return x * lax.rsqrt(jnp.mean(x * x, axis=-1, keepdims=True) + RMS_EPS) * w


def _proj_body(x_ref, nw_ref, *refs):
    n = len(refs) // 2
    xn = _rms(x_ref[...], nw_ref[...]).astype(BF16)
    for w_ref, o_ref in zip(refs[:n], refs[n:]):
        o_ref[...] = jnp.dot(xn, w_ref[...], preferred_element_type=F32)


def _norm_proj(x, norm_w, w_segs, tm):
    n_tok, d = x.shape
    tm = min(tm, n_tok)
    widths = [w.shape[1] for w in w_segs]
    return pl.pallas_call(
        _proj_body,
        grid=(n_tok // tm,),
        in_specs=[pl.BlockSpec((tm, d), lambda i: (i, 0)), _const_spec((1, d))]
        + [_const_spec((d, n)) for n in widths],
        out_specs=[pl.BlockSpec((tm, n), lambda i: (i, 0)) for n in widths],
        out_shape=[jax.ShapeDtypeStruct((n_tok, n), F32) for n in widths],
        compiler_params=_cparams("arbitrary"),
        name="norm_proj",
    )(x, norm_w.reshape(1, d), *w_segs)


def _post_body(a_ref, b_ref, y_ref, woa_ref, wob_ref, w1_ref, w2_ref, nmix_ref, npre_ref, npost_ref, o_ref):
    m = jnp.dot(a_ref[...].astype(BF16), woa_ref[...], preferred_element_type=F32)
    m = m + jnp.dot(b_ref[...].astype(BF16), wob_ref[...], preferred_element_type=F32)
    y1 = y_ref[...] + _rms(m, nmix_ref[...])
    h = _rms(y1, npre_ref[...]).astype(BF16)
    a = jnp.maximum(jnp.dot(h, w1_ref[...], preferred_element_type=F32), 0.0)
    m2 = jnp.dot((a * a).astype(BF16), w2_ref[...], preferred_element_type=F32)
    o_ref[...] = y1 + _rms(m2, npost_ref[...])


def _post_block(mix_a, mix_b, y, wo_a, wo_b, w1, w2, n_mix, n_pre, n_post, tm):
    n_tok, d = y.shape
    tm = min(tm, n_tok)
    ka, kb = mix_a.shape[1], mix_b.shape[1]
    row = lambda n: pl.BlockSpec((tm, n), lambda i: (i, 0))
    return pl.pallas_call(
        _post_body,
        grid=(n_tok // tm,),
        in_specs=[row(ka), row(kb), row(d), _const_spec(wo_a.shape), _const_spec(wo_b.shape),
                  _const_spec(w1.shape), _const_spec(w2.shape), _const_spec((1, d)), _const_spec((1, d)),
                  _const_spec((1, d))],
        out_specs=row(d),
        out_shape=jax.ShapeDtypeStruct((n_tok, d), F32),
        compiler_params=_cparams("arbitrary"),
        name="post_block",
    )(mix_a, mix_b, y, wo_a, wo_b, w1, w2, n_mix.reshape(1, d), n_pre.reshape(1, d), n_post.reshape(1, d))


def _compress_rows(src_ref, pe_ref, w1_ref, w2_ref, nb):
    hd = HEAD_DIM

    def body(r, acc):
        x = src_ref[pl.ds(r, nb, stride=NSA_BLOCK), :] + pe_ref[pl.ds(r, 1), :]
        xg = jnp.concatenate([x[:, :hd], x[:, hd:]], axis=0).astype(BF16)
        return acc + jnp.dot(xg, w1_ref[r], preferred_element_type=F32)

    hid = lax.fori_loop(0, NSA_BLOCK, body, jnp.zeros((2 * nb, NSA_CMP_HIDDEN), F32))
    hid = hid * jax.nn.sigmoid(hid)
    out = jnp.dot(hid.astype(BF16), w2_ref[...], preferred_element_type=F32)
    return jnp.concatenate([out[:nb], out[nb:]], axis=1)


def _compress_body(kc_ref, vc_ref, pek_ref, pev_ref, wk1_ref, wk2_ref, wv1_ref, wv2_ref, o_ref, *, nb):
    o_ref[:, :LANE] = _compress_rows(kc_ref, pek_ref, wk1_ref, wk2_ref, nb)
    o_ref[:, LANE:] = _compress_rows(vc_ref, pev_ref, wv1_ref, wv2_ref, nb)


def _pe2(pe):
    return jnp.concatenate([pe, pe], axis=1)


def _nsa_compress_prompt(rows, lp):
    bsz, t, _ = rows.shape
    nb = t // NSA_BLOCK
    w1 = lambda w: w.reshape(NSA_BLOCK, HEAD_DIM, NSA_CMP_HIDDEN).astype(BF16)
    col = lambda c: pl.BlockSpec((None, t, LANE), lambda b: (b, 0, c))
    return pl.pallas_call(
        functools.partial(_compress_body, nb=nb),
        grid=(bsz,),
        in_specs=[col(0), col(1), _const_spec((NSA_BLOCK, LANE)), _const_spec((NSA_BLOCK, LANE)),
                  _const_spec((NSA_BLOCK, HEAD_DIM, NSA_CMP_HIDDEN)), _const_spec((NSA_CMP_HIDDEN, HEAD_DIM)),
                  _const_spec((NSA_BLOCK, HEAD_DIM, NSA_CMP_HIDDEN)), _const_spec((NSA_CMP_HIDDEN, HEAD_DIM))],
        out_specs=pl.BlockSpec((None, nb, 2 * LANE), lambda b: (b, 0, 0)),
        out_shape=jax.ShapeDtypeStruct((bsz, nb, 2 * LANE), F32),
        compiler_params=_cparams("arbitrary"),
        name="nsa_compress",
    )(rows, rows, _pe2(lp['nsa_pe_k']), _pe2(lp['nsa_pe_v']), w1(lp['nsa_ck_w1']), lp['nsa_ck_w2'].astype(BF16),
      w1(lp['nsa_cv_w1']), lp['nsa_cv_w2'].astype(BF16))


_NT = (((1,), (1,)), ((), ()))


def _half_mask(shape, g):
    lane = lax.broadcasted_iota(jnp.int32, shape, len(shape) - 1)
    return lane >= HEAD_DIM if g else lane < HEAD_DIM


def _stack_heads(x, heads, g):
    keep = _half_mask((x.shape[0], LANE), g)
    out = []
    for h in heads:
        blk = x[:, (h // 2) * LANE:(h // 2 + 1) * LANE]
        if h % 2 != g:
            blk = pltpu.roll(blk, HEAD_DIM, axis=1)
        out.append(jnp.where(keep, blk, 0.0))
    return jnp.concatenate(out, axis=0)


def _unstack_pair(a, b, g):
    if g == 0:
        b = pltpu.roll(b, HEAD_DIM, axis=1)
    else:
        a = pltpu.roll(a, HEAD_DIM, axis=1)
    return jnp.where(_half_mask(a.shape, 1), b, a)


def _flash_init(m_ref, l_ref, acc_ref):
    m_ref[...] = jnp.full(m_ref.shape, NEG, F32)
    l_ref[...] = jnp.zeros(l_ref.shape, F32)
    acc_ref[...] = jnp.zeros(acc_ref.shape, F32)


def _flash_tile(qg, kt, vt, valid, n_rep, m_ref, l_ref, acc_ref):
    nq, nk = valid.shape
    s = lax.dot_general(qg, kt, _NT, preferred_element_type=F32).reshape(n_rep, nq, nk)
    s = jnp.where(valid[None], s, NEG).reshape(n_rep * nq, nk)
    m_old = m_ref[...]
    m_new = jnp.maximum(m_old, jnp.max(s, axis=-1, keepdims=True))
    alpha = jnp.exp(m_old - m_new)
    p = jnp.exp(s - m_new[:, :1]).reshape(n_rep, nq, nk)
    p = jnp.where(valid[None], p, 0.0).reshape(n_rep * nq, nk)
    l_ref[...] = alpha * l_ref[...] + jnp.sum(p, axis=-1, keepdims=True)
    acc_ref[...] = alpha * acc_ref[...] + jnp.dot(p.astype(BF16), vt, preferred_element_type=F32)
    m_ref[...] = m_new


def _row_gate(sig, lanes, rows):
    return jnp.concatenate([jnp.broadcast_to(sig[:, c:c + 1], (rows, LANE)) for c in lanes], axis=0)


def _nsa_select(p3, pos, nb, n_sel):
    nq = p3.shape[1]
    imp = jnp.sum(p3, axis=0)
    n_i = lax.broadcasted_iota(jnp.int32, (nq, nb), 1)
    cur = pos // NSA_BLOCK
    score = jnp.where(n_i == 0, NSA_REP + 1.0, imp)
    score = jnp.where(n_i == cur, NSA_REP + 1.0, score)
    score = jnp.where(n_i == cur - 1, NSA_REP + 1.0, score)
    score = jnp.where(n_i > cur, -1.0, score)
    rank = jnp.zeros((nq, nb), F32)
    for m in range(nb):
        col = score[:, m:m + 1]
        tie = jnp.where(n_i > m, 1.0, 0.0)
        rank = rank + jnp.where(col > score, 1.0, jnp.where(col == score, tie, 0.0))
    return jnp.where(rank < n_sel, 1.0, 0.0)


def _nsa_prompt_body(q_ref, small_ref, cmp_ref, ks_ref, vs_ref, kw_ref, vw_ref, o_ref, m_ref, l_ref, acc_ref, comb_ref,
                     *, nb, n_sel):
    qb = q_ref.shape[0]
    i = pl.program_id(1)
    q0 = i * qb
    rep = NSA_REP
    row_i = lax.broadcasted_iota(jnp.int32, (qb, LANE), 0)
    lane_i = lax.broadcasted_iota(jnp.int32, (qb, LANE), 1)
    qpos = q0 + row_i
    pos_c = q0 + lax.broadcasted_iota(jnp.int32, (qb, 1), 0)
    sig = jax.nn.sigmoid(small_ref[...])
    q = q_ref[...] * (HEAD_DIM ** -0.5)
    for g in range(NSA_KV_HEADS):
        qg = _stack_heads(q, range(g * rep, (g + 1) * rep), g).astype(BF16)
        gate_lane = lambda j: [SSM_HEADS + j * NSA_HEADS + g * rep + r for r in range(rep)]
        kc = cmp_ref[:, :LANE].astype(BF16)
        vc = cmp_ref[:, LANE:].astype(BF16)
        s = lax.dot_general(qg, kc, _NT, preferred_element_type=F32).reshape(rep, qb, nb)
        n_i = lax.broadcasted_iota(jnp.int32, (qb, nb), 1)
        complete = (n_i + 1) * NSA_BLOCK <= pos_c + 1
        s = jnp.where(complete[None], s, NEG)
        p = jnp.exp(s - jnp.max(s, axis=-1, keepdims=True))
        p = p / jnp.sum(p, axis=-1, keepdims=True)
        p = jnp.where((pos_c >= NSA_BLOCK - 1)[None], p, 0.0)
        o_cmp = jnp.dot(p.reshape(rep * qb, nb).astype(BF16), vc, preferred_element_type=F32)
        comb_ref[...] = _row_gate(sig, gate_lane(0), qb) * o_cmp
        sel = _nsa_select(p, pos_c, nb, n_sel).astype(BF16)
        _flash_init(m_ref, l_ref, acc_ref)

        def sel_step(j, carry):
            k0 = pl.multiple_of(j * qb, qb)
            kt = ks_ref[pl.ds(k0, qb), :].astype(BF16)
            vt = vs_ref[pl.ds(k0, qb), :].astype(BF16)
            blk = lax.broadcasted_iota(jnp.int32, (nb, qb), 0)
            key = lax.broadcasted_iota(jnp.int32, (nb, qb), 1)
            expand = jnp.where(blk == (k0 + key) // NSA_BLOCK, 1.0, 0.0).astype(BF16)
            chosen = jnp.dot(sel, expand, preferred_element_type=F32)
            valid = jnp.where(k0 + lane_i <= qpos, chosen, 0.0) > 0.5
            _flash_tile(qg, kt, vt, valid, rep, m_ref, l_ref, acc_ref)
            return carry

        lax.fori_loop(0, i + 1, sel_step, 0)
        comb_ref[...] += _row_gate(sig, gate_lane(1), qb) * (acc_ref[...] / l_ref[...])
        _flash_init(m_ref, l_ref, acc_ref)

        def win_step(j, carry):
            k0 = pl.multiple_of(j * qb, qb)
            kt = kw_ref[pl.ds(k0, qb), :].astype(BF16)
            vt = vw_ref[pl.ds(k0, qb), :].astype(BF16)
            diff = qpos - (k0 + lane_i)
            valid = jnp.where(diff >= 0, diff, NSA_WINDOW + 1) <= NSA_WINDOW
            _flash_tile(qg, kt, vt, valid, rep, m_ref, l_ref, acc_ref)
            return carry

        lax.fori_loop(jnp.maximum(i - NSA_WINDOW // qb, 0), i + 1, win_step, 0)
        comb_ref[...] += _row_gate(sig, gate_lane(2), qb) * (acc_ref[...] / l_ref[...])
        for c in range(rep // 2):
            a = comb_ref[(2 * c) * qb:(2 * c + 1) * qb, :]
            b = comb_ref[(2 * c + 1) * qb:(2 * c + 2) * qb, :]
            col = (g * rep // 2 + c) * LANE
            o_ref[:, col:col + LANE] = _unstack_pair(a, b, g)


def _nsa_prompt_attn(q, small, cmp, rows, win, bsz, t):
    qb = math.gcd(t, QUERY_BLOCK)
    nq = t // qb
    nb = cmp.shape[1]
    n_sel = min(NSA_TOPK, nb)
    tok = lambda n: pl.BlockSpec((qb, n), lambda b, i: (b * nq + i, 0))
    seq = lambda c: pl.BlockSpec((None, t, LANE), lambda b, i: (b, 0, c))
    stacked = pltpu.VMEM((NSA_REP * qb, LANE), F32)
    return pl.pallas_call(
        functools.partial(_nsa_prompt_body, nb=nb, n_sel=n_sel),
        grid=(bsz, nq),
        in_specs=[tok(NSA_HEADS * HEAD_DIM), tok(LANE), pl.BlockSpec((None, nb, 2 * LANE), lambda b, i: (b, 0, 0)),
                  seq(2), seq(3), seq(0), seq(1)],
        out_specs=tok(NSA_HEADS * HEAD_DIM),
        out_shape=jax.ShapeDtypeStruct((bsz * t, NSA_HEADS * HEAD_DIM), F32),
        scratch_shapes=[stacked, stacked, stacked, stacked],
        compiler_params=_cparams("arbitrary", "arbitrary"),
        name="nsa_prompt_attn",
    )(q, small, cmp, rows, rows, win, win)


_HI = lax.Precision.HIGHEST
_TN = (((0,), (0,)), ((), ()))


def _silu(x):
    return x * jax.nn.sigmoid(x)


def _softplus(x):
    return jnp.maximum(x, 0.0) + jnp.log(1.0 + jnp.exp(-jnp.abs(x)))


def _conv_silu(x_ref, buf_ref, w_ref, b_ref, xp_ref, tail_ref, first):
    n = x_ref.shape[0]

    @pl.when(first)
    def _():
        xp_ref[8 - (CONV_K - 1):8, :] = buf_ref[...]

    xp_ref[8:8 + n, :] = x_ref[...]
    y = xp_ref[8:8 + n, :] * w_ref[CONV_K - 1:CONV_K, :]
    for k in range(CONV_K - 1):
        y = y + xp_ref[5 + k:5 + k + n, :] * w_ref[k:k + 1, :]
    if b_ref is not None:
        y = y + b_ref[...]
    tail = xp_ref[8 + n - (CONV_K - 1):8 + n, :]
    tail_ref[...] = tail
    xp_ref[8 - (CONV_K - 1):8, :] = tail
    return _silu(y)


def _cumsum_rows(x):
    n = x.shape[0]
    tri = jnp.where(lax.broadcasted_iota(jnp.int32, (n, n), 0) >= lax.broadcasted_iota(jnp.int32, (n, n), 1), 1.0, 0.0)
    return jnp.dot(tri, x, preferred_element_type=F32, precision=_HI)


def _expand_heads(x, width, lane0=0, n_out=D_MODEL):
    h_i = lax.broadcasted_iota(jnp.int32, (LANE, n_out), 0)
    c_i = lax.broadcasted_iota(jnp.int32, (LANE, n_out), 1)
    sel = jnp.where(c_i // width + lane0 == h_i, 1.0, 0.0)
    return jnp.dot(x, sel, preferred_element_type=F32, precision=_HI)


def _decay_matrix(col, row, strict=False):
    n = col.shape[0]
    i = lax.broadcasted_iota(jnp.int32, (n, n), 0)
    j = lax.broadcasted_iota(jnp.int32, (n, n), 1)
    keep = (i > j) if strict else (i >= j)
    return jnp.exp(jnp.where(keep, col - row, NEG))


def _ssd_body(xbc_ref, z_ref, small_ref, buf_ref, h0_ref, cw_ref, cb_ref, dtb_ref, alog_ref, dx_ref, nw_ref,
              y_ref, conv_ref, h_ref, xp_ref):
    c = pl.program_id(1)
    n = xbc_ref.shape[0]
    hp = SSM_HEAD_DIM
    rep = SSM_HEADS // SSM_GROUPS

    @pl.when(c == 0)
    def _():
        h_ref[...] = h0_ref[...]

    act = _conv_silu(xbc_ref, buf_ref, cw_ref, cb_ref, xp_ref, conv_ref, c == 0)
    xs = act[:, :SSM_INNER]
    head_lane = lax.broadcasted_iota(jnp.int32, (n, LANE), 1) < SSM_HEADS
    dt = jnp.where(head_lane, _softplus(small_ref[...] + dtb_ref[...]), 0.0)
    la = dt * (-jnp.exp(alog_ref[...]))
    acs = _cumsum_rows(la)
    acs_t = acs.T
    xd = xs * _expand_heads(dt, hp)
    e_acs = _expand_heads(jnp.exp(acs), hp)
    xdd = (xd * _expand_heads(jnp.exp(acs[n - 1:n, :] - acs), hp)).astype(BF16)
    xd = xd.astype(BF16)
    lane2 = _half_mask((n, LANE), 1)
    for g in range(SSM_GROUPS):
        bc = act[:, SSM_INNER + g * SSM_STATE:SSM_INNER + (g + 1) * SSM_STATE].astype(BF16)
        cc = act[:, SSM_INNER + (SSM_GROUPS + g) * SSM_STATE:SSM_INNER + (SSM_GROUPS + g + 1) * SSM_STATE].astype(BF16)
        cb = lax.dot_general(cc, bc, _NT, preferred_element_type=F32)
        h_prev = h_ref[g * rep:(g + 1) * rep].reshape(rep * hp, SSM_STATE)
        y_off = lax.dot_general(cc, h_prev.astype(BF16), _NT, preferred_element_type=F32)
        for pair in range(rep // 2):
            halves = []
            for k in range(2):
                h = g * rep + 2 * pair + k
                lm = _decay_matrix(acs[:, h:h + 1], acs_t[h:h + 1, :])
                halves.append(jnp.dot((cb * lm).astype(BF16), xd[:, (h // 2) * LANE:(h // 2 + 1) * LANE],
                                      preferred_element_type=F32))
            col = (g * rep + 2 * pair) * hp
            y_ref[:, col:col + LANE] = jnp.where(lane2, halves[1], halves[0]) + y_off[:, 2 * pair * hp:2 * pair * hp + LANE] * e_acs[:, col:col + LANE]
        st = lax.dot_general(xdd[:, g * rep * hp:(g + 1) * rep * hp], bc, _TN, preferred_element_type=F32)
        for r in range(rep):
            h = g * rep + r
            dec = jnp.exp(acs_t[h:h + 1, n - 1:n])
            h_ref[h] = h_ref[h] * dec + st[r * hp:(r + 1) * hp, :]
    y = y_ref[...] + dx_ref[...] * xs
    y_ref[...] = _rms(y * _silu(z_ref[...]), nw_ref[...])


def _ssd_mixer(xbc, z, small, conv_buf, h0, lp, bsz, t, chunk):
    nc = t // chunk
    tok = lambda n: pl.BlockSpec((chunk, n), lambda b, c: (b * nc + c, 0))
    per_b = lambda shape: pl.BlockSpec((None,) + shape, lambda b, c: (b,) + (0,) * len(shape))
    pad = lambda v: jnp.pad(v.astype(F32), (0, LANE - v.shape[0])).reshape(1, LANE)
    cdim = SSM_CONV_DIM
    state = (SSM_HEADS, SSM_HEAD_DIM, SSM_STATE)
    return pl.pallas_call(
        _ssd_body,
        grid=(bsz, nc),
        in_specs=[tok(cdim), tok(SSM_INNER), tok(LANE), per_b((CONV_K - 1, cdim)), per_b(state),
                  _const_spec((CONV_K, cdim)), _const_spec((1, cdim)), _const_spec((1, LANE)), _const_spec((1, LANE)),
                  _const_spec((1, SSM_INNER)), _const_spec((1, SSM_INNER))],
        out_specs=[tok(SSM_INNER), per_b((CONV_K - 1, cdim)), per_b(state)],
        out_shape=[jax.ShapeDtypeStruct((bsz * t, SSM_INNER), F32),
                   jax.ShapeDtypeStruct((bsz, CONV_K - 1, cdim), F32),
                   jax.ShapeDtypeStruct((bsz,) + state, F32)],
        scratch_shapes=[pltpu.VMEM((chunk + 8, cdim), F32)],
        compiler_params=_cparams("arbitrary", "arbitrary"),
        name="ssd_mixer",
    )(xbc, z, small, conv_buf, h0, lp['ssm_conv_w'], lp['ssm_conv_b'].reshape(1, cdim), pad(lp['ssm_dt_bias']),
      pad(lp['ssm_a_log']), jnp.repeat(lp['ssm_d'].astype(F32), SSM_HEAD_DIM).reshape(1, SSM_INNER),
      lp['ssm_norm_w'].reshape(1, SSM_INNER))


def _split_bf16(x):
    hi = x.astype(BF16)
    return hi, (x - hi.astype(F32)).astype(BF16)


def _dot_split(a, b):
    a_hi, a_lo = _split_bf16(a)
    b_hi, b_lo = _split_bf16(b)
    d = functools.partial(jnp.dot, preferred_element_type=F32)
    return d(a_hi, b_hi) + d(a_hi, b_lo) + d(a_lo, b_hi)


def _unit_lower_inverse(a):
    n = a.shape[0]
    eye = jnp.where(lax.broadcasted_iota(jnp.int32, (n, n), 0) == lax.broadcasted_iota(jnp.int32, (n, n), 1), 1.0, 0.0)
    p = -a
    t = eye + p
    span = 2
    while span < n:
        p = _dot_split(p, p)
        t = t + _dot_split(t, p)
        span *= 2
    return t


def _gdn_body(qkv_ref, z_ref, small_ref, buf_ref, s0_ref, cw_ref, alog_ref, dtb_ref, nw_ref,
              y_ref, conv_ref, s_ref, xp_ref):
    c = pl.program_id(1)
    n = qkv_ref.shape[0]
    dk = GDN_HEAD_DIM
    nh = GDN_HEADS

    @pl.when(c == 0)
    def _():
        s_ref[...] = s0_ref[...]

    act = _conv_silu(qkv_ref, buf_ref, cw_ref, None, xp_ref, conv_ref, c == 0)
    lane = lax.broadcasted_iota(jnp.int32, (n, LANE), 1)
    raw = small_ref[...]
    beta = jax.nn.sigmoid(raw)
    g = jnp.where((lane >= nh) & (lane < 2 * nh), -jnp.exp(alog_ref[...]) * _softplus(raw + dtb_ref[...]), 0.0)
    gc = _cumsum_rows(g)
    gc_t = gc.T
    beta_x = _expand_heads(beta, dk)
    egc_x = _expand_heads(jnp.exp(gc), dk, nh)
    edec_x = _expand_heads(jnp.exp(gc[n - 1:n, :] - gc), dk, nh)
    ii = lax.broadcasted_iota(jnp.int32, (n, n), 0)
    jj = lax.broadcasted_iota(jnp.int32, (n, n), 1)
    for h in range(nh):
        sl = slice(h * dk, (h + 1) * dk)
        q = act[:, sl]
        k = act[:, nh * dk + h * dk:nh * dk + (h + 1) * dk]
        v = act[:, 2 * nh * dk + h * dk:2 * nh * dk + (h + 1) * dk]
        q = q * lax.rsqrt(jnp.sum(q * q, axis=-1, keepdims=True) + 1e-6) * (dk ** -0.5)
        k = k * lax.rsqrt(jnp.sum(k * k, axis=-1, keepdims=True) + 1e-6)
        kb = k * beta_x[:, sl]
        vb = v * beta_x[:, sl]
        decay = _decay_matrix(gc[:, nh + h:nh + h + 1], gc_t[nh + h:nh + h + 1, :])
        k16 = k.astype(BF16)
        amat = jnp.where(ii > jj, lax.dot_general(kb.astype(BF16), k16, _NT, preferred_element_type=F32) * decay, 0.0)
        tmat = _unit_lower_inverse(amat).astype(BF16)
        u = jnp.dot(tmat, vb.astype(BF16), preferred_element_type=F32)
        w = jnp.dot(tmat, (kb * egc_x[:, sl]).astype(BF16), preferred_element_type=F32)
        qk = lax.dot_general(q.astype(BF16), k16, _NT, preferred_element_type=F32) * decay
        s = s_ref[h]
        s16 = s.astype(BF16)
        v_new = u - jnp.dot(w.astype(BF16), s16, preferred_element_type=F32)
        v16 = v_new.astype(BF16)
        o = jnp.dot((q * egc_x[:, sl]).astype(BF16), s16, preferred_element_type=F32)
        o = o + jnp.dot(qk.astype(BF16), v16, preferred_element_type=F32)
        gl = jnp.exp(gc_t[nh + h:nh + h + 1, n - 1:n])
        s_ref[h] = s * gl + lax.dot_general((k * edec_x[:, sl]).astype(BF16), v16, _TN, preferred_element_type=F32)
        y_ref[:, sl] = _rms(o, nw_ref[...]) * _silu(z_ref[:, sl])


def _gdn_mixer(qkv, z, small, conv_buf, s0, lp, bsz, t, chunk):
    nc = t // chunk
    tok = lambda n: pl.BlockSpec((chunk, n), lambda b, c: (b * nc + c, 0))
    per_b = lambda shape: pl.BlockSpec((None,) + shape, lambda b, c: (b,) + (0,) * len(shape))
    pad8 = lambda v: jnp.pad(v.astype(F32), (GDN_HEADS, LANE - 2 * GDN_HEADS)).reshape(1, LANE)
    cdim = 3 * GDN_WIDTH
    state = (GDN_HEADS, GDN_HEAD_DIM, GDN_HEAD_DIM)
    return pl.pallas_call(
        _gdn_body,
        grid=(bsz, nc),
        in_specs=[tok(cdim), tok(GDN_WIDTH), tok(LANE), per_b((CONV_K - 1, cdim)), per_b(state),
                  _const_spec((CONV_K, cdim)), _const_spec((1, LANE)), _const_spec((1, LANE)),
                  _const_spec((1, GDN_HEAD_DIM))],
        out_specs=[tok(GDN_WIDTH), per_b((CONV_K - 1, cdim)), per_b(state)],
        out_shape=[jax.ShapeDtypeStruct((bsz * t, GDN_WIDTH), F32),
                   jax.ShapeDtypeStruct((bsz, CONV_K - 1, cdim), F32),
                   jax.ShapeDtypeStruct((bsz,) + state, F32)],
        scratch_shapes=[pltpu.VMEM((chunk + 8, cdim), F32)],
        compiler_params=_cparams("arbitrary", "arbitrary"),
        name="gdn_mixer",
    )(qkv, z, small, conv_buf, s0, lp['gdn_conv_w'], pad8(lp['gdn_a_log']), pad8(lp['gdn_dt_bias']),
      lp['gdn_norm_w'].reshape(1, GDN_HEAD_DIM))


def _softmax_pv(s, valid, v16, n_rep=1):
    rows, nk = s.shape
    s = jnp.where(valid[None], s.reshape(n_rep, rows // n_rep, nk), NEG).reshape(rows, nk)
    m = jnp.max(s, axis=-1, keepdims=True)
    p = jnp.exp(s - m)
    l = jnp.sum(p, axis=-1, keepdims=True)
    return jnp.dot((p / l).astype(BF16), v16, preferred_element_type=F32), m + jnp.log(l)


def _dil_prompt_body(q_ref, kp_ref, kc_ref, vp_ref, vc_ref, o_ref, lse_ref, *, window):
    i = pl.program_id(2)
    qb = q_ref.shape[0]
    q = q_ref[...] * (HEAD_DIM ** -0.5)
    kk = jnp.concatenate([kp_ref[...], kc_ref[...]], axis=0).astype(BF16)
    vv = jnp.concatenate([vp_ref[...], vc_ref[...]], axis=0).astype(BF16)
    qpos = i * qb + lax.broadcasted_iota(jnp.int32, (qb, 2 * qb), 0)
    kpos = (i - 1) * qb + lax.broadcasted_iota(jnp.int32, (qb, 2 * qb), 1)
    diff = jnp.where(kpos >= 0, qpos - kpos, -1)
    valid = jnp.where(diff >= 0, diff, window + 1) <= window
    upper = _half_mask((qb, LANE), 1)
    for pair in range(DIL_HEADS_PER_GROUP // 2):
        sl = slice(pair * LANE, (pair + 1) * LANE)
        o2, l2 = [], []
        for k in range(2):
            qh = jnp.where(_half_mask((qb, LANE), k), q[:, sl], 0.0).astype(BF16)
            s = lax.dot_general(qh, kk[:, sl], _NT, preferred_element_type=F32)
            o, lse = _softmax_pv(s, valid, vv[:, sl])
            o2.append(o)
            l2.append(jnp.broadcast_to(lse, (qb, LANE)))
        o_ref[:, sl] = jnp.where(upper, o2[1], o2[0])
        lse_ref[:, sl] = jnp.where(upper, l2[1], l2[0])


def _dil_prompt_attn(dil, gi, bsz, t):
    win, step = DIL_PATTERNS[gi]
    n = t // step
    qb = math.gcd(n, QUERY_BLOCK)
    nq = n // qb
    nblk = DIL_IN // DIL_WIDTH
    view = dil.reshape(bsz, n, step * DIL_IN)
    blk = lambda part, prev: pl.BlockSpec(
        (None, qb, DIL_WIDTH),
        lambda b, c, i: (b, jnp.maximum(i - 1, 0) if prev else i, c * nblk + part * DIL_GROUPS + gi))
    out_spec = pl.BlockSpec((None, qb, DIL_WIDTH), lambda b, c, i: (b, i, c))
    out_sds = jax.ShapeDtypeStruct((bsz, n, step * DIL_WIDTH), F32)
    o, lse = pl.pallas_call(
        functools.partial(_dil_prompt_body, window=win // step),
        grid=(bsz, step, nq),
        in_specs=[blk(0, False), blk(1, True), blk(1, False), blk(2, True), blk(2, False)],
        out_specs=[out_spec, out_spec],
        out_shape=[out_sds, out_sds],
        compiler_params=_cparams("arbitrary", "arbitrary", "arbitrary"),
        name=f"dil_prompt_attn_{gi}",
    )(view, view, view, view, view)
    return o.reshape(bsz * t, DIL_WIDTH), lse.reshape(bsz * t, DIL_WIDTH)


def _dil_combine_body(o0, o1, o2, l0, l1, l2, y_ref):
    m = jnp.maximum(jnp.maximum(l0[...], l1[...]), l2[...])
    e0, e1, e2 = jnp.exp(l0[...] - m), jnp.exp(l1[...] - m), jnp.exp(l2[...] - m)
    den = e0 + e1 + e2
    y_ref[...] = (e0 / den) * o0[...] + (e1 / den) * o1[...] + (e2 / den) * o2[...]


def _dil_combine(outs, lses, tm):
    n_tok = outs[0].shape[0]
    spec = pl.BlockSpec((tm, DIL_WIDTH), lambda i: (i, 0))
    return pl.pallas_call(
        _dil_combine_body,
        grid=(n_tok // tm,),
        in_specs=[spec] * 6,
        out_specs=spec,
        out_shape=jax.ShapeDtypeStruct((n_tok, DIL_WIDTH), F32),
        compiler_params=_cparams("arbitrary"),
        name="dil_combine",
    )(*outs, *lses)


def _nsa_decode_body(pt_ref, q_ref, small_ref, rows_ref, win_ref, *refs, n_pages, lw, nb, nb_pad, n_sel):
    del pt_ref
    pages = refs[:n_pages]
    (pastwin_ref, pek_ref, pev_ref, wk1_ref, wk2_ref, wv1_ref, wv2_ref, o_ref,
     kc_s, vc_s, ks_s, vs_s, kw_s, vw_s, comb_s) = refs[n_pages:]
    t = q_ref.shape[0]
    rep = NSA_REP
    past_len = n_pages * PAGE_SIZE
    for dst, c in ((kc_s, 0), (vc_s, 1), (ks_s, 2), (vs_s, 3)):
        for p in range(n_pages):
            dst[p * PAGE_SIZE:(p + 1) * PAGE_SIZE, :] = pages[p][:, c * LANE:(c + 1) * LANE]
        dst[past_len:past_len + t, :] = rows_ref[:, c * LANE:(c + 1) * LANE]
        dst[past_len + t:, :] = jnp.zeros((dst.shape[0] - past_len - t, LANE), F32)
    for dst, c in ((kw_s, 0), (vw_s, 1)):
        dst[:lw, :] = pastwin_ref[:, c * LANE:(c + 1) * LANE]
        dst[lw:lw + t, :] = win_ref[:, c * LANE:(c + 1) * LANE]
        dst[lw + t:, :] = jnp.zeros((dst.shape[0] - lw - t, LANE), F32)
    kcmp = _compress_rows(kc_s, pek_ref, wk1_ref, wk2_ref, nb_pad).astype(BF16)
    vcmp = _compress_rows(vc_s, pev_ref, wv1_ref, wv2_ref, nb_pad).astype(BF16)
    pos_c = past_len + lax.broadcasted_iota(jnp.int32, (t, 1), 0)
    sig = jax.nn.sigmoid(small_ref[...])
    q = q_ref[...] * (HEAD_DIM ** -0.5)
    nk = ks_s.shape[0]
    nw = kw_s.shape[0]
    for g in range(NSA_KV_HEADS):
        qg = _stack_heads(q, range(g * rep, (g + 1) * rep), g).astype(BF16)
        gate_lane = lambda j: [SSM_HEADS + j * NSA_HEADS + g * rep + r for r in range(rep)]
        s = lax.dot_general(qg, kcmp, _NT, preferred_element_type=F32).reshape(rep, t, nb_pad)
        n_i = lax.broadcasted_iota(jnp.int32, (t, nb_pad), 1)
        complete = (n_i + 1) * NSA_BLOCK <= pos_c + 1
        s = jnp.where(complete[None], s, NEG)
        p = jnp.exp(s - jnp.max(s, axis=-1, keepdims=True))
        p = p / jnp.sum(p, axis=-1, keepdims=True)
        p = jnp.where((pos_c >= NSA_BLOCK - 1)[None], p, 0.0)
        o_cmp = jnp.dot(p.reshape(rep * t, nb_pad).astype(BF16), vcmp, preferred_element_type=F32)
        comb = _row_gate(sig, gate_lane(0), t) * o_cmp
        sel = _nsa_select(p, pos_c, nb_pad, n_sel).astype(BF16)
        blk = lax.broadcasted_iota(jnp.int32, (nb_pad, nk), 0)
        key = lax.broadcasted_iota(jnp.int32, (nb_pad, nk), 1)
        chosen = jnp.dot(sel, jnp.where(blk == key // NSA_BLOCK, 1.0, 0.0).astype(BF16), preferred_element_type=F32)
        kpos = lax.broadcasted_iota(jnp.int32, (t, nk), 1)
        valid = jnp.where(kpos <= pos_c, chosen, 0.0) > 0.5
        s = lax.dot_general(qg, ks_s[...].astype(BF16), _NT, preferred_element_type=F32)
        o_sel, _ = _softmax_pv(s, valid, vs_s[...].astype(BF16), rep)
        comb = comb + _row_gate(sig, gate_lane(1), t) * o_sel
        diff = pos_c - (past_len - lw + lax.broadcasted_iota(jnp.int32, (t, nw), 1))
        valid = jnp.where(diff >= 0, diff, NSA_WINDOW + 1) <= NSA_WINDOW
        s = lax.dot_general(qg, kw_s[...].astype(BF16), _NT, preferred_element_type=F32)
        o_win, _ = _softmax_pv(s, valid, vw_s[...].astype(BF16), rep)
        comb_s[...] = comb + _row_gate(sig, gate_lane(2), t) * o_win
        for c in range(rep // 2):
            a = comb_s[(2 * c) * t:(2 * c + 1) * t, :]
            b = comb_s[(2 * c + 1) * t:(2 * c + 2) * t, :]
            col = (g * rep // 2 + c) * LANE
            o_ref[:, col:col + LANE] = _unstack_pair(a, b, g)


def _nsa_decode_attn(q, small, rows, win, cache_kv, cache_win, page_table, lp, bsz, t):
    n_pages = page_table.shape[1]
    past_len = n_pages * PAGE_SIZE
    lw = cache_win.shape[1]
    nb = -(-(past_len + t) // NSA_BLOCK)
    nb_pad = -(-nb // 8) * 8
    n_keys = -(-(past_len + t) // LANE) * LANE
    n_win = -(-(lw + t) // LANE) * LANE
    pages = cache_kv.reshape(cache_kv.shape[0], PAGE_SIZE, 4 * LANE)
    pastwin = cache_win.reshape(bsz, lw, 2 * LANE)
    w1 = lambda w: w.reshape(NSA_BLOCK, HEAD_DIM, NSA_CMP_HIDDEN).astype(BF16)
    tok = lambda n: pl.BlockSpec((t, n), lambda b, pt: (b, 0))
    const = lambda shape: pl.BlockSpec(shape, lambda b, pt: (0,) * len(shape), pipeline_mode=pl.Buffered(1))
    page_spec = lambda p: pl.BlockSpec((None, PAGE_SIZE, 4 * LANE), lambda b, pt: (pt[b * n_pages + p], 0, 0))
    grid_spec = pltpu.PrefetchScalarGridSpec(
        num_scalar_prefetch=1,
        grid=(bsz,),
        in_specs=[tok(NSA_HEADS * HEAD_DIM), tok(LANE), tok(4 * LANE), tok(2 * LANE)]
        + [page_spec(p) for p in range(n_pages)]
        + [pl.BlockSpec((None, lw, 2 * LANE), lambda b, pt: (b, 0, 0)),
           const((NSA_BLOCK, LANE)), const((NSA_BLOCK, LANE)),
           const((NSA_BLOCK, HEAD_DIM, NSA_CMP_HIDDEN)), const((NSA_CMP_HIDDEN, HEAD_DIM)),
           const((NSA_BLOCK, HEAD_DIM, NSA_CMP_HIDDEN)), const((NSA_CMP_HIDDEN, HEAD_DIM))],
        out_specs=tok(NSA_HEADS * HEAD_DIM),
        scratch_shapes=[pltpu.VMEM((nb_pad * NSA_BLOCK, LANE), F32), pltpu.VMEM((nb_pad * NSA_BLOCK, LANE), F32),
                        pltpu.VMEM((n_keys, LANE), F32), pltpu.VMEM((n_keys, LANE), F32),
                        pltpu.VMEM((n_win, LANE), F32), pltpu.VMEM((n_win, LANE), F32),
                        pltpu.VMEM((NSA_REP * t, LANE), F32)],
    )
    return pl.pallas_call(
        functools.partial(_nsa_decode_body, n_pages=n_pages, lw=lw, nb=nb, nb_pad=nb_pad, n_sel=min(NSA_TOPK, nb)),
        grid_spec=grid_spec,
        out_shape=jax.ShapeDtypeStruct((bsz * t, NSA_HEADS * HEAD_DIM), F32),
        compiler_params=_cparams("arbitrary"),
        name="nsa_decode_attn",
    )(page_table.reshape(-1), q, small, rows, win, *([pages] * n_pages), pastwin,
      _pe2(lp['nsa_pe_k']), _pe2(lp['nsa_pe_v']), w1(lp['nsa_ck_w1']), lp['nsa_ck_w2'].astype(BF16),
      w1(lp['nsa_cv_w1']), lp['nsa_cv_w2'].astype(BF16))


def _dil_decode_group(q, k_new, v_new, buf_ref, step, n_past):
    t = q.shape[0]
    pairs = DIL_HEADS_PER_GROUP // 2
    n_cols = t * n_past + t
    row_tok = lax.broadcasted_iota(jnp.int32, (2 * t, n_cols), 0) % t
    col = lax.broadcasted_iota(jnp.int32, (2 * t, n_cols), 1)
    back = row_tok - (col - t * n_past)
    new_bad = jnp.where(back >= 0, back % step, 1)
    past_bad = jnp.where(col // n_past == row_tok, jnp.maximum(row_tok // step - col % n_past, 0), 1)
    valid = jnp.where(col < t * n_past, past_bad, new_bad) == 0
    lower = _half_mask((t, LANE), 0)
    outs, lses = [], []
    for pr in range(pairs):
        sl = slice(pr * LANE, (pr + 1) * LANE)
        gather = lambda kv: jnp.concatenate(
            [buf_ref[pl.ds((tok % step) * 2 * pairs + kv * pairs + pr, n_past, stride=step * 2 * pairs), :]
             for tok in range(t)],
            axis=0)
        kk = jnp.concatenate([gather(0), k_new[:, sl]], axis=0).astype(BF16)
        vv = jnp.concatenate([gather(1), v_new[:, sl]], axis=0).astype(BF16)
        qp = q[:, sl]
        q2 = jnp.concatenate([jnp.where(lower, qp, 0.0), jnp.where(lower, 0.0, qp)], axis=0)
        s = lax.dot_general(q2.astype(BF16), kk, _NT, preferred_element_type=F32)
        o, lse = _softmax_pv(s, valid, vv)
        lse = jnp.broadcast_to(lse, (2 * t, LANE))
        outs.append(jnp.where(lower, o[:t], o[t:]))
        lses.append(jnp.where(lower, lse[:t], lse[t:]))
    return jnp.concatenate(outs, axis=1), jnp.concatenate(lses, axis=1)


def _dil_decode_body(x_ref, b0_ref, b1_ref, b2_ref, y_ref):
    scale = HEAD_DIM ** -0.5
    res = []
    for gi, buf_ref in enumerate((b0_ref, b1_ref, b2_ref)):
        win, step = DIL_PATTERNS[gi]
        part = lambda p: x_ref[:, (p * DIL_GROUPS + gi) * DIL_WIDTH:(p * DIL_GROUPS + gi + 1) * DIL_WIDTH]
        res.append(_dil_decode_group(part(0) * scale, part(1), part(2), buf_ref, step, win // step))
    (o0, l0), (o1, l1), (o2, l2) = res
    m = jnp.maximum(jnp.maximum(l0, l1), l2)
    e0, e1, e2 = jnp.exp(l0 - m), jnp.exp(l1 - m), jnp.exp(l2 - m)
    den = e0 + e1 + e2
    y_ref[...] = (e0 / den) * o0 + (e1 / den) * o1 + (e2 / den) * o2


def _dil_decode_attn(dil, bufs, bsz, t):
    rows_per_pos = 2 * DIL_HEADS_PER_GROUP * HEAD_DIM // LANE
    views = []
    for (win, _), buf in zip(DIL_PATTERNS, bufs):
        assert buf.shape[1] == win, "decode path needs a full window of cached rows"
        views.append(buf.reshape(bsz, win * rows_per_pos, LANE))
    return pl.pallas_call(
        _dil_decode_body,
        grid=(bsz,),
        in_specs=[pl.BlockSpec((t, DIL_IN), lambda b: (b, 0))]
        + [pl.BlockSpec((None, v.shape[1], LANE), lambda b: (b, 0, 0)) for v in views],
        out_specs=pl.BlockSpec((t, DIL_WIDTH), lambda b: (b, 0)),
        out_shape=jax.ShapeDtypeStruct((bsz * t, DIL_WIDTH), F32),
        compiler_params=_cparams("arbitrary"),
        name="dil_decode_attn",
    )(dil, *views)


_AB_CUTS = (0, SSM_INNER, SSM_INNER + SSM_CONV_DIM, SSM_INNER + SSM_CONV_DIM + SSM_HEADS)
_AB_Q0 = _AB_CUTS[3]
_AB_KV0 = _AB_Q0 + NSA_HEADS * HEAD_DIM
_AB_WIN0 = _AB_KV0 + 4 * NSA_KV_HEADS * HEAD_DIM
_AB_GATE0 = _AB_WIN0 + 2 * NSA_KV_HEADS * HEAD_DIM
_AB_END = _AB_GATE0 + 3 * NSA_HEADS


def _pad_cols(w, n):
    return jnp.pad(w, ((0, 0), (0, n - w.shape[1])))


def _ab_weight_segs(w_in):
    small = jnp.concatenate([w_in[:, _AB_CUTS[2]:_AB_CUTS[3]], w_in[:, _AB_GATE0:_AB_END]], axis=1)
    segs = [w_in[:, _AB_CUTS[0]:_AB_CUTS[1]], w_in[:, _AB_CUTS[1]:_AB_CUTS[2]], w_in[:, _AB_Q0:_AB_KV0],
            w_in[:, _AB_KV0:_AB_WIN0], w_in[:, _AB_WIN0:_AB_GATE0], _pad_cols(small, LANE)]
    return [s.astype(BF16) for s in segs]


_CD_Z0 = 3 * GDN_WIDTH
_CD_B0 = _CD_Z0 + GDN_WIDTH
_CD_DIL0 = _CD_B0 + 2 * GDN_HEADS
_CD_END = _CD_DIL0 + DIL_IN


def _cd_weight_segs(w_in):
    segs = [w_in[:, :_CD_Z0], w_in[:, _CD_Z0:_CD_B0], w_in[:, _CD_DIL0:_CD_END],
            _pad_cols(w_in[:, _CD_B0:_CD_DIL0], LANE)]
    return [s.astype(BF16) for s in segs]


def _shift_in(buf, new):
    return jnp.concatenate([buf.astype(new.dtype), new], axis=1)[:, new.shape[1]:]


def _layer_ab(y, norm_pre, segs, lp, past, page_table, is_prompt, tm):
    bsz, t, d = y.shape
    z, xbc, q, rows, win, small = _norm_proj(y.reshape(bsz * t, d), norm_pre, segs, tm)
    rows_new = rows.reshape(bsz, t, 4, NSA_KV_HEADS, HEAD_DIM)
    win_new = win.reshape(bsz, t, 2, NSA_KV_HEADS, HEAD_DIM)
    if is_prompt:
        conv_buf = jnp.zeros((bsz, CONV_K - 1, SSM_CONV_DIM), F32)
        h0 = jnp.zeros((bsz, SSM_HEADS, SSM_HEAD_DIM, SSM_STATE), F32)
        chunk = math.gcd(t, SSM_CHUNK)
        rows3 = rows.reshape(bsz, t, 4 * LANE)
        cmp = _nsa_compress_prompt(rows3, lp)
        y_nsa = _nsa_prompt_attn(q, small, cmp, rows3, win.reshape(bsz, t, 2 * LANE), bsz, t)
        new_win = win_new[:, t - min(NSA_WINDOW, t):]
    else:
        conv_buf, h0, chunk = past['ssm_conv'], past['ssm'], t
        y_nsa = _nsa_decode_attn(q, small, rows, win, past['nsa_kv'], past['nsa_win'], page_table, lp, bsz, t)
        new_win = _shift_in(past['nsa_win'], win_new)
    y_ssm, new_conv, new_ssm = _ssd_mixer(xbc, z, small, conv_buf, h0, lp, bsz, t, chunk)
    return y_ssm, y_nsa, (new_conv, new_ssm, rows_new, new_win)


def _layer_cd(y, norm_pre, segs, lp, past, is_prompt, tm):
    bsz, t, d = y.shape
    qkv, z, dil, small = _norm_proj(y.reshape(bsz * t, d), norm_pre, segs, tm)
    d6 = dil.reshape(bsz, t, 3, DIL_GROUPS, DIL_HEADS_PER_GROUP, HEAD_DIM)
    kv_new = [jnp.stack([d6[:, :, 1, gi], d6[:, :, 2, gi]], axis=2) for gi in range(DIL_GROUPS)]
    if is_prompt:
        conv_buf = jnp.zeros((bsz, CONV_K - 1, 3 * GDN_WIDTH), F32)
        s0 = jnp.zeros((bsz, GDN_HEADS, GDN_HEAD_DIM, GDN_HEAD_DIM), F32)
        chunk = math.gcd(t, GDN_CHUNK)
        parts = [_dil_prompt_attn(dil, gi, bsz, t) for gi in range(DIL_GROUPS)]
        y_dil = _dil_combine([p[0] for p in parts], [p[1] for p in parts], min(tm * 2, bsz * t))
        bufs = [kv[:, t - min(win, t):] for kv, (win, _) in zip(kv_new, DIL_PATTERNS)]
    else:
        conv_buf, s0, chunk = past['gdn_conv'], past['gdn'], t
        y_dil = _dil_decode_attn(dil, past['dil'], bsz, t)
        bufs = [_shift_in(buf, kv) for buf, kv in zip(past['dil'], kv_new)]
    y_gdn, new_conv, new_gdn = _gdn_mixer(qkv, z, small, conv_buf, s0, lp, bsz, t, chunk)
    return y_gdn, y_dil, (new_conv, new_gdn, bufs[0], bufs[1], bufs[2])


def kernel(x_prompt, x_sample, cache_ssm_conv, state_ssm, cache_nsa_kv, cache_nsa_win_kv, cache_gdn_conv, state_gdn,
           cache_dil0_kv, cache_dil1_kv, cache_dil2_kv, page_table, norm_mix_pre, norm_mix_post, norm_mlp_pre,
           norm_mlp_post, mlp_w1, mlp_w2, ab_w_in, ab_w_out, ssm_conv_w, ssm_conv_b, ssm_dt_bias, ssm_a_log, ssm_d,
           ssm_norm_w, nsa_pe_k, nsa_pe_v, nsa_ck_w1, nsa_ck_w2, nsa_cv_w1, nsa_cv_w2, cd_w_in, cd_w_out, gdn_conv_w,
           gdn_dt_bias, gdn_a_log, gdn_norm_w):
    depth = norm_mix_pre.shape[0]
    yp, ys = x_prompt, x_sample
    ab_p, ab_s, cd_p, cd_s = [], [], [], []
    tm = 256
    for l in range(depth):
        j = l // 2
        if l % 2 == 0:
            lp = {'ssm_conv_w': ssm_conv_w[j], 'ssm_conv_b': ssm_conv_b[j], 'ssm_dt_bias': ssm_dt_bias[j],
                  'ssm_a_log': ssm_a_log[j], 'ssm_d': ssm_d[j], 'ssm_norm_w': ssm_norm_w[j],
                  'nsa_pe_k': nsa_pe_k[j], 'nsa_pe_v': nsa_pe_v[j], 'nsa_ck_w1': nsa_ck_w1[j],
                  'nsa_ck_w2': nsa_ck_w2[j], 'nsa_cv_w1': nsa_cv_w1[j], 'nsa_cv_w2': nsa_cv_w2[j]}
            past = {'ssm_conv': cache_ssm_conv[j], 'ssm': state_ssm[j], 'nsa_kv': cache_nsa_kv[j],
                    'nsa_win': cache_nsa_win_kv[j]}
            segs = _ab_weight_segs(ab_w_in[j])
            w_out = ab_w_out[j].astype(BF16)
            ka = SSM_INNER
            ap, bp, stp = _layer_ab(yp, norm_mix_pre[l], segs, lp, None, None, True, tm)
            as_, bs, sts = _layer_ab(ys, norm_mix_pre[l], segs, lp, past, page_table, False, tm)
            ab_p.append(stp)
            ab_s.append(sts)
        else:
            lp = {'gdn_conv_w': gdn_conv_w[j], 'gdn_dt_bias': gdn_dt_bias[j], 'gdn_a_log': gdn_a_log[j],
                  'gdn_norm_w': gdn_norm_w[j]}
            past = {'gdn_conv': cache_gdn_conv[j], 'gdn': state_gdn[j],
                    'dil': (cache_dil0_kv[j], cache_dil1_kv[j], cache_dil2_kv[j])}
            segs = _cd_weight_segs(cd_w_in[j])
            w_out = cd_w_out[j].astype(BF16)
            ka = GDN_WIDTH
            ap, bp, stp = _layer_cd(yp, norm_mix_pre[l], segs, lp, None, True, tm)
            as_, bs, sts = _layer_cd(ys, norm_mix_pre[l], segs, lp, past, False, tm)
            cd_p.append(stp)
            cd_s.append(sts)
        w1 = mlp_w1[l].astype(BF16)
        w2 = mlp_w2[l].astype(BF16)
        post = functools.partial(_post_block, wo_a=w_out[:ka], wo_b=w_out[ka:], w1=w1, w2=w2, n_mix=norm_mix_post[l],
                                 n_pre=norm_mlp_pre[l], n_post=norm_mlp_post[l], tm=tm)
        yp = post(ap, bp, yp.reshape(-1, D_MODEL)).reshape(yp.shape)
        ys = post(as_, bs, ys.reshape(-1, D_MODEL)).reshape(ys.shape)
    stack = lambda states, i: jnp.stack([s[i] for s in states])
    return (yp, ys,
            stack(ab_p, 0), stack(ab_s, 0), stack(ab_p, 1), stack(ab_s, 1),
            stack(ab_p, 2), stack(ab_s, 2), stack(ab_p, 3), stack(ab_s, 3),
            stack(cd_p, 0), stack(cd_s, 0), stack(cd_p, 1), stack(cd_s, 1),
            stack(cd_p, 2), stack(cd_s, 2), stack(cd_p, 3), stack(cd_s, 3),
            stack(cd_p, 4), stack(cd_s, 4))
```

```python
import functools
import math

import jax
import jax.numpy as jnp
from jax import lax
from jax.experimental import pallas as pl
from jax.experimental.pallas import tpu as pltpu

F32 = jnp.float32
BF16 = jnp.bfloat16

D_MODEL = 1024
HEAD_DIM = 64
CONV_K = 4
RMS_EPS = 1e-6
MLP_HIDDEN = 4 * D_MODEL
QUERY_BLOCK = 128
PAGE_SIZE = 128

SSM_HEADS = 16
SSM_HEAD_DIM = 64
SSM_INNER = SSM_HEADS * SSM_HEAD_DIM
SSM_GROUPS = 2
SSM_STATE = 128
SSM_CONV_DIM = SSM_INNER + 2 * SSM_GROUPS * SSM_STATE
SSM_CHUNK = 128

NSA_HEADS = 16
NSA_KV_HEADS = 2
NSA_REP = NSA_HEADS // NSA_KV_HEADS
NSA_BLOCK = 64
NSA_TOPK = 16
NSA_WINDOW = 512
NSA_CMP_HIDDEN = 256

GDN_HEADS = 8
GDN_HEAD_DIM = 128
GDN_WIDTH = GDN_HEADS * GDN_HEAD_DIM
GDN_CHUNK = 64

DIL_PATTERNS = ((128, 1), (512, 4), (2048, 16))
DIL_GROUPS = len(DIL_PATTERNS)
DIL_HEADS_PER_GROUP = 8
DIL_WIDTH = DIL_HEADS_PER_GROUP * HEAD_DIM
DIL_IN = 3 * DIL_GROUPS * DIL_WIDTH
NEG = -1e30

VMEM_LIMIT_BYTES = 56 * 1024 * 1024
LANE = 128


def _cparams(*sem):
    return pltpu.CompilerParams(dimension_semantics=sem, vmem_limit_bytes=VMEM_LIMIT_BYTES)


def _const_spec(shape):
    nd = len(shape)
    return pl.BlockSpec(shape, lambda *_: (0,) * nd, pipeline_mode=pl.Buffered(1))


def _rms(x, w):
    return x * lax.rsqrt(jnp.mean(x * x, axis=-1, keepdims=True) + RMS_EPS) * w


def _proj_body(x_ref, nw_ref, *refs):
    n = len(refs) // 2
    xn = _rms(x_ref[...], nw_ref[...]).astype(BF16)
    for w_ref, o_ref in zip(refs[:n], refs[n:]):
        o_ref[...] = jnp.dot(xn, w_ref[...], preferred_element_type=F32)


def _norm_proj(x, norm_w, w_segs, tm):
    n_tok, d = x.shape
    tm = min(tm, n_tok)
    widths = [w.shape[1] for w in w_segs]
    return pl.pallas_call(
        _proj_body,
        grid=(n_tok // tm,),
        in_specs=[pl.BlockSpec((tm, d), lambda i: (i, 0)), _const_spec((1, d))]
        + [_const_spec((d, n)) for n in widths],
        out_specs=[pl.BlockSpec((tm, n), lambda i: (i, 0)) for n in widths],
        out_shape=[jax.ShapeDtypeStruct((n_tok, n), F32) for n in widths],
        compiler_params=_cparams("arbitrary"),
        name="norm_proj",
    )(x, norm_w.reshape(1, d), *w_segs)


def _post_body(a_ref, b_ref, y_ref, woa_ref, wob_ref, w1_ref, w2_ref, nmix_ref, npre_ref, npost_ref, o_ref):
    m = jnp.dot(a_ref[...].astype(BF16), woa_ref[...], preferred_element_type=F32)
    m = m + jnp.dot(b_ref[...].astype(BF16), wob_ref[...], preferred_element_type=F32)
    y1 = y_ref[...] + _rms(m, nmix_ref[...])
    h = _rms(y1, npre_ref[...]).astype(BF16)
    a = jnp.maximum(jnp.dot(h, w1_ref[...], preferred_element_type=F32), 0.0)
    m2 = jnp.dot((a * a).astype(BF16), w2_ref[...], preferred_element_type=F32)
    o_ref[...] = y1 + _rms(m2, npost_ref[...])


def _post_block(mix_a, mix_b, y, wo_a, wo_b, w1, w2, n_mix, n_pre, n_post, tm):
    n_tok, d = y.shape
    tm = min(tm, n_tok)
    ka, kb = mix_a.shape[1], mix_b.shape[1]
    row = lambda n: pl.BlockSpec((tm, n), lambda i: (i, 0))
    return pl.pallas_call(
        _post_body,
        grid=(n_tok // tm,),
        in_specs=[row(ka), row(kb), row(d), _const_spec(wo_a.shape), _const_spec(wo_b.shape),
                  _const_spec(w1.shape), _const_spec(w2.shape), _const_spec((1, d)), _const_spec((1, d)),
                  _const_spec((1, d))],
        out_specs=row(d),
        out_shape=jax.ShapeDtypeStruct((n_tok, d), F32),
        compiler_params=_cparams("arbitrary"),
        name="post_block",
    )(mix_a, mix_b, y, wo_a, wo_b, w1, w2, n_mix.reshape(1, d), n_pre.reshape(1, d), n_post.reshape(1, d))


def _compress_rows(src_ref, pe_ref, w1_ref, w2_ref, nb):
    hd = HEAD_DIM

    def body(r, acc):
        x = src_ref[pl.ds(r, nb, stride=NSA_BLOCK), :] + pe_ref[pl.ds(r, 1), :]
        xg = jnp.concatenate([x[:, :hd], x[:, hd:]], axis=0).astype(BF16)
        return acc + jnp.dot(xg, w1_ref[r], preferred_element_type=F32)

    hid = lax.fori_loop(0, NSA_BLOCK, body, jnp.zeros((2 * nb, NSA_CMP_HIDDEN), F32), unroll=8)
    hid = hid * jax.nn.sigmoid(hid)
    out = jnp.dot(hid.astype(BF16), w2_ref[...], preferred_element_type=F32)
    return jnp.concatenate([out[:nb], out[nb:]], axis=1)


def _compress_body(kc_ref, vc_ref, pek_ref, pev_ref, wk1_ref, wk2_ref, wv1_ref, wv2_ref, o_ref, *, nb):
    o_ref[:, :LANE] = _compress_rows(kc_ref, pek_ref, wk1_ref, wk2_ref, nb)
    o_ref[:, LANE:] = _compress_rows(vc_ref, pev_ref, wv1_ref, wv2_ref, nb)


def _pe2(pe):
    return jnp.concatenate([pe, pe], axis=1)


def _nsa_compress_prompt(rows, lp):
    bsz, t, _ = rows.shape
    nb = t // NSA_BLOCK
    w1 = lambda w: w.reshape(NSA_BLOCK, HEAD_DIM, NSA_CMP_HIDDEN).astype(BF16)
    col = lambda c: pl.BlockSpec((None, t, LANE), lambda b: (b, 0, c))
    return pl.pallas_call(
        functools.partial(_compress_body, nb=nb),
        grid=(bsz,),
        in_specs=[col(0), col(1), _const_spec((NSA_BLOCK, LANE)), _const_spec((NSA_BLOCK, LANE)),
                  _const_spec((NSA_BLOCK, HEAD_DIM, NSA_CMP_HIDDEN)), _const_spec((NSA_CMP_HIDDEN, HEAD_DIM)),
                  _const_spec((NSA_BLOCK, HEAD_DIM, NSA_CMP_HIDDEN)), _const_spec((NSA_CMP_HIDDEN, HEAD_DIM))],
        out_specs=pl.BlockSpec((None, nb, 2 * LANE), lambda b: (b, 0, 0)),
        out_shape=jax.ShapeDtypeStruct((bsz, nb, 2 * LANE), F32),
        compiler_params=_cparams("arbitrary"),
        name="nsa_compress",
    )(rows, rows, _pe2(lp['nsa_pe_k']), _pe2(lp['nsa_pe_v']), w1(lp['nsa_ck_w1']), lp['nsa_ck_w2'].astype(BF16),
      w1(lp['nsa_cv_w1']), lp['nsa_cv_w2'].astype(BF16))


_NT = (((1,), (1,)), ((), ()))


def _half_mask(shape, g):
    lane = lax.broadcasted_iota(jnp.int32, shape, len(shape) - 1)
    return lane >= HEAD_DIM if g else lane < HEAD_DIM


def _stack_heads(x, heads, g):
    keep = _half_mask((x.shape[0], LANE), g)
    out = []
    for h in heads:
        blk = x[:, (h // 2) * LANE:(h // 2 + 1) * LANE]
        if h % 2 != g:
            blk = pltpu.roll(blk, HEAD_DIM, axis=1)
        out.append(jnp.where(keep, blk, 0.0))
    return jnp.concatenate(out, axis=0)


def _unstack_pair(a, b, g):
    if g == 0:
        b = pltpu.roll(b, HEAD_DIM, axis=1)
    else:
        a = pltpu.roll(a, HEAD_DIM, axis=1)
    return jnp.where(_half_mask(a.shape, 1), b, a)


def _row_gate(sig, lanes, rows):
    return jnp.concatenate([jnp.broadcast_to(sig[:, c:c + 1], (rows, LANE)) for c in lanes], axis=0)


def _nsa_select(p3, pos, nb, n_sel):
    nq = p3.shape[1]
    imp = jnp.sum(p3, axis=0)
    n_i = lax.broadcasted_iota(jnp.int32, (nq, nb), 1)
    cur = pos // NSA_BLOCK
    score = jnp.where(n_i == 0, NSA_REP + 1.0, imp)
    score = jnp.where(n_i == cur, NSA_REP + 1.0, score)
    score = jnp.where(n_i == cur - 1, NSA_REP + 1.0, score)
    score = jnp.where(n_i > cur, -1.0, score)
    rank = jnp.zeros((nq, nb), F32)
    for m in range(nb):
        col = score[:, m:m + 1]
        tie = jnp.where(n_i > m, 1.0, 0.0)
        rank = rank + jnp.where(col > score, 1.0, jnp.where(col == score, tie, 0.0))
    return jnp.where(rank < n_sel, 1.0, 0.0)


def _nsa_select_t(imp, pos, n_sel):
    nb, nq = imp.shape
    n_i = lax.broadcasted_iota(jnp.int32, (nb, nq), 0)
    cur = pos // NSA_BLOCK
    score = jnp.where(n_i == 0, NSA_REP + 1.0, imp)
    score = jnp.where(n_i == cur, NSA_REP + 1.0, score)
    score = jnp.where(n_i == cur - 1, NSA_REP + 1.0, score)
    score = jnp.where(n_i > cur, -1.0, score)
    rank = jnp.zeros((nb, nq), F32)
    for m in range(nb):
        row = score[m:m + 1, :]
        tie = jnp.where(n_i > m, 1.0, 0.0)
        rank = rank + jnp.where(row > score, 1.0, jnp.where(row == score, tie, 0.0))
    return jnp.where(rank < n_sel, 1.0, 0.0)


def _lane_tile(x, n):
    return jnp.concatenate([x] * n, axis=1)


_V_ROWS = HEAD_DIM + 16


def _flash_tile_t(kt, vt_aug, q_t, valid, m_ref, acc_ref, g):
    nk, nq = valid.shape
    rep = q_t.shape[1] // nq
    masked = lambda x, fill: jnp.concatenate(
        [jnp.where(valid, x[:, r * nq:(r + 1) * nq], fill) for r in range(rep)], axis=1)
    s = masked(jnp.dot(kt, q_t, preferred_element_type=F32), NEG)
    m_old = m_ref[g, 0:1, :]
    m_new = jnp.maximum(m_old, jnp.max(s, axis=0, keepdims=True))
    alpha = jnp.exp(m_old - m_new)
    p = masked(jnp.exp(s - m_new), 0.0)
    acc_ref[g] = alpha * acc_ref[g] + jnp.dot(vt_aug, p.astype(BF16), preferred_element_type=F32)
    m_ref[g] = jnp.broadcast_to(m_new, m_ref.shape[1:])


def _nsa_prompt_body(q_ref, small_ref, cmp_ref, ks_ref, vs_ref, kw_ref, vw_ref, o_ref, vst_ref, vwt_ref, m_ref, acc_ref,
                     *, nb, n_sel):
    qb = q_ref.shape[0]
    t = ks_ref.shape[0]
    i = pl.program_id(1)
    q0 = i * qb
    rep = NSA_REP
    hd = HEAD_DIM
    groups = range(NSA_KV_HEADS)

    @pl.when(i == 0)
    def _():
        ones = jnp.ones((_V_ROWS - hd, t), BF16)
        for g in groups:
            vst_ref[g, hd:, :] = ones
            vwt_ref[g, hd:, :] = ones

        def fill(j, carry):
            k0 = pl.multiple_of(j * qb, qb)
            for src, dst in ((vs_ref, vst_ref), (vw_ref, vwt_ref)):
                v_t = src[pl.ds(k0, qb), :].T.astype(BF16)
                for g in groups:
                    dst[g, :hd, pl.ds(k0, qb)] = v_t[g * hd:(g + 1) * hd, :]
            return carry

        lax.fori_loop(0, t // qb, fill, 0)

    key_i = lax.broadcasted_iota(jnp.int32, (qb, qb), 0)
    qpos = q0 + lax.broadcasted_iota(jnp.int32, (qb, qb), 1)
    pos_r = q0 + lax.broadcasted_iota(jnp.int32, (1, qb), 1)
    sig_t = jax.nn.sigmoid(small_ref[...]).T
    q = q_ref[...] * (hd ** -0.5)
    pairs_t = [q[:, c * LANE:(c + 1) * LANE].T for c in range(NSA_HEADS // 2)]
    head_t = lambda h: pairs_t[h // 2][(h % 2) * hd:(h % 2 + 1) * hd, :]
    zeros = jnp.zeros((hd, rep * qb), F32)
    q_ts, sels = [], []
    for g in groups:
        qg_t = jnp.concatenate([head_t(g * rep + r) for r in range(rep)], axis=1)
        q_t = jnp.concatenate([qg_t, zeros] if g == 0 else [zeros, qg_t], axis=0).astype(BF16)
        q_ts.append(q_t)
        gate_row = lambda j: jnp.concatenate(
            [sig_t[SSM_HEADS + j * NSA_HEADS + g * rep + r:SSM_HEADS + j * NSA_HEADS + g * rep + r + 1, :]
             for r in range(rep)], axis=1)
        kc = cmp_ref[:, :LANE].astype(BF16)
        vc_t = cmp_ref[:, LANE:].T[g * hd:(g + 1) * hd, :].astype(BF16)
        s = jnp.dot(kc, q_t, preferred_element_type=F32)
        n_i = lax.broadcasted_iota(jnp.int32, (nb, qb), 0)
        complete = _lane_tile(jnp.where((n_i + 1) * NSA_BLOCK <= pos_r + 1, 1.0, 0.0), rep)
        s = jnp.where(complete > 0.5, s, NEG)
        p = jnp.exp(s - jnp.max(s, axis=0, keepdims=True))
        p = p / jnp.sum(p, axis=0, keepdims=True)
        p = jnp.where(_lane_tile(pos_r, rep) >= NSA_BLOCK - 1, p, 0.0)
        o_cmp = jnp.dot(vc_t, p.astype(BF16), preferred_element_type=F32)
        acc_ref[2 + g, :hd, :] = gate_row(0) * o_cmp
        imp = p[:, :qb]
        for r in range(1, rep):
            imp = imp + p[:, r * qb:(r + 1) * qb]
        sels.append(_nsa_select_t(imp, pos_r, n_sel).astype(BF16))

    def flash_init():
        m_ref[...] = jnp.full(m_ref.shape, NEG, F32)
        acc_ref[0:2] = jnp.zeros((2,) + acc_ref.shape[1:], F32)

    def flash_out(g):
        acc = acc_ref[g]
        return acc[:hd, :] / acc[hd:hd + 1, :]

    flash_init()

    def sel_step(j, carry):
        k0 = pl.multiple_of(j * qb, qb)
        kt = ks_ref[pl.ds(k0, qb), :].astype(BF16)
        blk = lax.broadcasted_iota(jnp.int32, (qb, nb), 1)
        key = lax.broadcasted_iota(jnp.int32, (qb, nb), 0)
        expand = jnp.where(blk == (k0 + key) // NSA_BLOCK, 1.0, 0.0).astype(BF16)
        causal = k0 + key_i <= qpos
        for g in groups:
            chosen = jnp.dot(expand, sels[g], preferred_element_type=F32)
            valid = jnp.where(causal, chosen, 0.0) > 0.5
            _flash_tile_t(kt, vst_ref[g, :, pl.ds(k0, qb)], q_ts[g], valid, m_ref, acc_ref, g)
        return carry

    lax.fori_loop(0, i + 1, sel_step, 0)
    for g in groups:
        gate_row = jnp.concatenate(
            [sig_t[SSM_HEADS + NSA_HEADS + g * rep + r:SSM_HEADS + NSA_HEADS + g * rep + r + 1, :] for r in range(rep)],
            axis=1)
        acc_ref[2 + g, :hd, :] += gate_row * flash_out(g)
    flash_init()

    def win_step(j, carry):
        k0 = pl.multiple_of(j * qb, qb)
        kt = kw_ref[pl.ds(k0, qb), :].astype(BF16)
        diff = qpos - (k0 + key_i)
        valid = jnp.where(diff >= 0, diff, NSA_WINDOW + 1) <= NSA_WINDOW
        for g in groups:
            _flash_tile_t(kt, vwt_ref[g, :, pl.ds(k0, qb)], q_ts[g], valid, m_ref, acc_ref, g)
        return carry

    lax.fori_loop(jnp.maximum(i - NSA_WINDOW // qb, 0), i + 1, win_step, 0)
    for g in groups:
        gate_row = jnp.concatenate(
            [sig_t[SSM_HEADS + 2 * NSA_HEADS + g * rep + r:SSM_HEADS + 2 * NSA_HEADS + g * rep + r + 1, :]
             for r in range(rep)], axis=1)
        comb = acc_ref[2 + g, :hd, :] + gate_row * flash_out(g)
        for c in range(rep // 2):
            pair = jnp.concatenate([comb[:, (2 * c) * qb:(2 * c + 1) * qb], comb[:, (2 * c + 1) * qb:(2 * c + 2) * qb]],
                                   axis=0)
            col = (g * rep // 2 + c) * LANE
            o_ref[:, col:col + LANE] = pair.T


def _nsa_prompt_attn(q, small, cmp, rows, win, bsz, t):
    qb = math.gcd(t, QUERY_BLOCK)
    nq = t // qb
    nb = cmp.shape[1]
    n_sel = min(NSA_TOPK, nb)
    tok = lambda n: pl.BlockSpec((qb, n), lambda b, i: (b * nq + i, 0))
    seq = lambda c: pl.BlockSpec((None, t, LANE), lambda b, i: (b, 0, c))
    v_t = pltpu.VMEM((NSA_KV_HEADS, _V_ROWS, t), BF16)
    return pl.pallas_call(
        functools.partial(_nsa_prompt_body, nb=nb, n_sel=n_sel),
        grid=(bsz, nq),
        in_specs=[tok(NSA_HEADS * HEAD_DIM), tok(LANE), pl.BlockSpec((None, nb, 2 * LANE), lambda b, i: (b, 0, 0)),
                  seq(2), seq(3), seq(0), seq(1)],
        out_specs=tok(NSA_HEADS * HEAD_DIM),
        out_shape=jax.ShapeDtypeStruct((bsz * t, NSA_HEADS * HEAD_DIM), F32),
        scratch_shapes=[v_t, v_t, pltpu.VMEM((NSA_KV_HEADS, 8, NSA_REP * qb), F32),
                        pltpu.VMEM((2 * NSA_KV_HEADS, _V_ROWS, NSA_REP * qb), F32)],
        compiler_params=_cparams("arbitrary", "arbitrary"),
        name="nsa_prompt_attn",
    )(q, small, cmp, rows, rows, win, win)


_HI = lax.Precision.HIGHEST
_TN = (((0,), (0,)), ((), ()))


def _silu(x):
    return x * jax.nn.sigmoid(x)


def _softplus(x):
    return jnp.maximum(x, 0.0) + jnp.log(1.0 + jnp.exp(-jnp.abs(x)))


def _conv_silu(x_ref, buf_ref, w_ref, b_ref, xp_ref, tail_ref, first):
    n = x_ref.shape[0]

    @pl.when(first)
    def _():
        xp_ref[8 - (CONV_K - 1):8, :] = buf_ref[...]

    xp_ref[8:8 + n, :] = x_ref[...]
    y = xp_ref[8:8 + n, :] * w_ref[CONV_K - 1:CONV_K, :]
    for k in range(CONV_K - 1):
        y = y + xp_ref[5 + k:5 + k + n, :] * w_ref[k:k + 1, :]
    if b_ref is not None:
        y = y + b_ref[...]
    tail = xp_ref[8 + n - (CONV_K - 1):8 + n, :]
    tail_ref[...] = tail
    xp_ref[8 - (CONV_K - 1):8, :] = tail
    return _silu(y)


def _cumsum_rows(x):
    n = x.shape[0]
    tri = jnp.where(lax.broadcasted_iota(jnp.int32, (n, n), 0) >= lax.broadcasted_iota(jnp.int32, (n, n), 1), 1.0, 0.0)
    return jnp.dot(tri, x, preferred_element_type=F32, precision=_HI)


def _expand_heads(x, width, lane0=0, n_out=D_MODEL):
    h_i = lax.broadcasted_iota(jnp.int32, (LANE, n_out), 0)
    c_i = lax.broadcasted_iota(jnp.int32, (LANE, n_out), 1)
    sel = jnp.where(c_i // width + lane0 == h_i, 1.0, 0.0)
    return jnp.dot(x, sel, preferred_element_type=F32, precision=_HI)


def _decay_matrix(col, row, strict=False):
    n = col.shape[0]
    i = lax.broadcasted_iota(jnp.int32, (n, n), 0)
    j = lax.broadcasted_iota(jnp.int32, (n, n), 1)
    keep = (i > j) if strict else (i >= j)
    return jnp.exp(jnp.where(keep, col - row, NEG))


def _ssd_body(xbc_ref, z_ref, small_ref, buf_ref, h0_ref, cw_ref, cb_ref, dtb_ref, alog_ref, dx_ref, nw_ref,
              y_ref, conv_ref, h_ref, xp_ref):
    c = pl.program_id(1)
    n = xbc_ref.shape[0]
    hp = SSM_HEAD_DIM
    rep = SSM_HEADS // SSM_GROUPS

    @pl.when(c == 0)
    def _():
        h_ref[...] = h0_ref[...]

    act = _conv_silu(xbc_ref, buf_ref, cw_ref, cb_ref, xp_ref, conv_ref, c == 0)
    xs = act[:, :SSM_INNER]
    head_lane = lax.broadcasted_iota(jnp.int32, (n, LANE), 1) < SSM_HEADS
    dt = jnp.where(head_lane, _softplus(small_ref[...] + dtb_ref[...]), 0.0)
    la = dt * (-jnp.exp(alog_ref[...]))
    acs = _cumsum_rows(la)
    acs_t = acs.T
    xd = xs * _expand_heads(dt, hp)
    e_acs = _expand_heads(jnp.exp(acs), hp)
    xdd = (xd * _expand_heads(jnp.exp(acs[n - 1:n, :] - acs), hp)).astype(BF16)
    xd = xd.astype(BF16)
    lane2 = _half_mask((n, LANE), 1)
    for g in range(SSM_GROUPS):
        bc = act[:, SSM_INNER + g * SSM_STATE:SSM_INNER + (g + 1) * SSM_STATE].astype(BF16)
        cc = act[:, SSM_INNER + (SSM_GROUPS + g) * SSM_STATE:SSM_INNER + (SSM_GROUPS + g + 1) * SSM_STATE].astype(BF16)
        cb = lax.dot_general(cc, bc, _NT, preferred_element_type=F32)
        h_prev = h_ref[g * rep:(g + 1) * rep].reshape(rep * hp, SSM_STATE)
        y_off = lax.dot_general(cc, h_prev.astype(BF16), _NT, preferred_element_type=F32)
        for pair in range(rep // 2):
            halves = []
            for k in range(2):
                h = g * rep + 2 * pair + k
                lm = _decay_matrix(acs[:, h:h + 1], acs_t[h:h + 1, :])
                halves.append(jnp.dot((cb * lm).astype(BF16), xd[:, (h // 2) * LANE:(h // 2 + 1) * LANE],
                                      preferred_element_type=F32))
            col = (g * rep + 2 * pair) * hp
            y_ref[:, col:col + LANE] = jnp.where(lane2, halves[1], halves[0]) + y_off[:, 2 * pair * hp:2 * pair * hp + LANE] * e_acs[:, col:col + LANE]
        st = lax.dot_general(xdd[:, g * rep * hp:(g + 1) * rep * hp], bc, _TN, preferred_element_type=F32)
        for r in range(rep):
            h = g * rep + r
            dec = jnp.exp(acs_t[h:h + 1, n - 1:n])
            h_ref[h] = h_ref[h] * dec + st[r * hp:(r + 1) * hp, :]
    y = y_ref[...] + dx_ref[...] * xs
    y_ref[...] = _rms(y * _silu(z_ref[...]), nw_ref[...])


def _ssd_mixer(xbc, z, small, conv_buf, h0, lp, bsz, t, chunk):
    nc = t // chunk
    tok = lambda n: pl.BlockSpec((chunk, n), lambda b, c: (b * nc + c, 0))
    per_b = lambda shape: pl.BlockSpec((None,) + shape, lambda b, c: (b,) + (0,) * len(shape))
    pad = lambda v: jnp.pad(v.astype(F32), (0, LANE - v.shape[0])).reshape(1, LANE)
    cdim = SSM_CONV_DIM
    state = (SSM_HEADS, SSM_HEAD_DIM, SSM_STATE)
    return pl.pallas_call(
        _ssd_body,
        grid=(bsz, nc),
        in_specs=[tok(cdim), tok(SSM_INNER), tok(LANE), per_b((CONV_K - 1, cdim)), per_b(state),
                  _const_spec((CONV_K, cdim)), _const_spec((1, cdim)), _const_spec((1, LANE)), _const_spec((1, LANE)),
                  _const_spec((1, SSM_INNER)), _const_spec((1, SSM_INNER))],
        out_specs=[tok(SSM_INNER), per_b((CONV_K - 1, cdim)), per_b(state)],
        out_shape=[jax.ShapeDtypeStruct((bsz * t, SSM_INNER), F32),
                   jax.ShapeDtypeStruct((bsz, CONV_K - 1, cdim), F32),
                   jax.ShapeDtypeStruct((bsz,) + state, F32)],
        scratch_shapes=[pltpu.VMEM((chunk + 8, cdim), F32)],
        compiler_params=_cparams("arbitrary", "arbitrary"),
        name="ssd_mixer",
    )(xbc, z, small, conv_buf, h0, lp['ssm_conv_w'], lp['ssm_conv_b'].reshape(1, cdim), pad(lp['ssm_dt_bias']),
      pad(lp['ssm_a_log']), jnp.repeat(lp['ssm_d'].astype(F32), SSM_HEAD_DIM).reshape(1, SSM_INNER),
      lp['ssm_norm_w'].reshape(1, SSM_INNER))


def _split_bf16(x):
    hi = x.astype(BF16)
    return hi, (x - hi.astype(F32)).astype(BF16)


def _bmm(a, b):
    return jnp.einsum('hmk,hkn->hmn', a, b, preferred_element_type=F32)


def _bmm_nt(a, b):
    return jnp.einsum('hmk,hnk->hmn', a, b, preferred_element_type=F32)


def _bmm_tn(a, b):
    return jnp.einsum('hkm,hkn->hmn', a, b, preferred_element_type=F32)


def _bmm_split(a, b):
    a_hi, a_lo = _split_bf16(a)
    b_hi, b_lo = _split_bf16(b)
    return _bmm(a_hi, b_hi) + _bmm(a_hi, b_lo) + _bmm(a_lo, b_hi)


def _unit_lower_inverse(a):
    n = a.shape[-1]
    eye = jnp.where(lax.broadcasted_iota(jnp.int32, (n, n), 0) == lax.broadcasted_iota(jnp.int32, (n, n), 1), 1.0, 0.0)
    p = -a
    t = eye + p
    span = 2
    while span < n:
        p = _bmm_split(p, p)
        t = t + _bmm_split(t, p)
        span *= 2
    return t


def _gdn_body(qkv_ref, z_ref, small_ref, buf_ref, s0_ref, cw_ref, alog_ref, dtb_ref, nw_ref,
              y_ref, conv_ref, s_ref, xp_ref):
    c = pl.program_id(1)
    n = qkv_ref.shape[0]
    dk = GDN_HEAD_DIM
    nh = GDN_HEADS

    @pl.when(c == 0)
    def _():
        s_ref[...] = s0_ref[...]

    act = _conv_silu(qkv_ref, buf_ref, cw_ref, None, xp_ref, conv_ref, c == 0)
    lane = lax.broadcasted_iota(jnp.int32, (n, LANE), 1)
    raw = small_ref[...]
    beta = jax.nn.sigmoid(raw)
    g = jnp.where((lane >= nh) & (lane < 2 * nh), -jnp.exp(alog_ref[...]) * _softplus(raw + dtb_ref[...]), 0.0)
    gc = _cumsum_rows(g)
    gc_t = gc.T
    beta_x = _expand_heads(beta, dk)
    egc_x = _expand_heads(jnp.exp(gc), dk, nh)
    edec_x = _expand_heads(jnp.exp(gc[n - 1:n, :] - gc), dk, nh)
    ii = lax.broadcasted_iota(jnp.int32, (n, n), 0)
    jj = lax.broadcasted_iota(jnp.int32, (n, n), 1)
    heads = lambda x, base=0: jnp.stack([x[:, base + h * dk:base + (h + 1) * dk] for h in range(nh)])
    q, k, v = heads(act), heads(act, nh * dk), heads(act, 2 * nh * dk)
    q = q * lax.rsqrt(jnp.sum(q * q, axis=-1, keepdims=True) + 1e-6) * (dk ** -0.5)
    k = k * lax.rsqrt(jnp.sum(k * k, axis=-1, keepdims=True) + 1e-6)
    beta3, egc, edec = heads(beta_x), heads(egc_x), heads(edec_x)
    kb = k * beta3
    decay = jnp.stack([_decay_matrix(gc[:, nh + h:nh + h + 1], gc_t[nh + h:nh + h + 1, :]) for h in range(nh)])
    k16 = k.astype(BF16)
    amat = jnp.where(ii > jj, _bmm_nt(kb.astype(BF16), k16) * decay, 0.0)
    tmat = _unit_lower_inverse(amat).astype(BF16)
    u = _bmm(tmat, (v * beta3).astype(BF16))
    w = _bmm(tmat, (kb * egc).astype(BF16))
    qk = _bmm_nt(q.astype(BF16), k16) * decay
    s = s_ref[...]
    s16 = s.astype(BF16)
    v16 = (u - _bmm(w.astype(BF16), s16)).astype(BF16)
    o = _bmm((q * egc).astype(BF16), s16) + _bmm(qk.astype(BF16), v16)
    gl = jnp.stack([jnp.exp(gc_t[nh + h:nh + h + 1, n - 1:n]) for h in range(nh)])
    s_ref[...] = s * gl + _bmm_tn((k * edec).astype(BF16), v16)
    y = _rms(o, nw_ref[...])
    for h in range(nh):
        sl = slice(h * dk, (h + 1) * dk)
        y_ref[:, sl] = y[h] * _silu(z_ref[:, sl])


def _gdn_mixer(qkv, z, small, conv_buf, s0, lp, bsz, t, chunk):
    nc = t // chunk
    tok = lambda n: pl.BlockSpec((chunk, n), lambda b, c: (b * nc + c, 0))
    per_b = lambda shape: pl.BlockSpec((None,) + shape, lambda b, c: (b,) + (0,) * len(shape))
    pad8 = lambda v: jnp.pad(v.astype(F32), (GDN_HEADS, LANE - 2 * GDN_HEADS)).reshape(1, LANE)
    cdim = 3 * GDN_WIDTH
    state = (GDN_HEADS, GDN_HEAD_DIM, GDN_HEAD_DIM)
    return pl.pallas_call(
        _gdn_body,
        grid=(bsz, nc),
        in_specs=[tok(cdim), tok(GDN_WIDTH), tok(LANE), per_b((CONV_K - 1, cdim)), per_b(state),
                  _const_spec((CONV_K, cdim)), _const_spec((1, LANE)), _const_spec((1, LANE)),
                  _const_spec((1, GDN_HEAD_DIM))],
        out_specs=[tok(GDN_WIDTH), per_b((CONV_K - 1, cdim)), per_b(state)],
        out_shape=[jax.ShapeDtypeStruct((bsz * t, GDN_WIDTH), F32),
                   jax.ShapeDtypeStruct((bsz, CONV_K - 1, cdim), F32),
                   jax.ShapeDtypeStruct((bsz,) + state, F32)],
        scratch_shapes=[pltpu.VMEM((chunk + 8, cdim), F32)],
        compiler_params=_cparams("arbitrary", "arbitrary"),
        name="gdn_mixer",
    )(qkv, z, small, conv_buf, s0, lp['gdn_conv_w'], pad8(lp['gdn_a_log']), pad8(lp['gdn_dt_bias']),
      lp['gdn_norm_w'].reshape(1, GDN_HEAD_DIM))


def _softmax_pv(s, valid, v16, n_rep=1):
    rows, nk = s.shape
    s = jnp.where(valid[None], s.reshape(n_rep, rows // n_rep, nk), NEG).reshape(rows, nk)
    m = jnp.max(s, axis=-1, keepdims=True)
    p = jnp.exp(s - m)
    l = jnp.sum(p, axis=-1, keepdims=True)
    return jnp.dot((p / l).astype(BF16), v16, preferred_element_type=F32), m + jnp.log(l)


def _dil_prompt_body(q_ref, kp_ref, kc_ref, vp_ref, vc_ref, o_ref, lse_ref, *, window):
    i = pl.program_id(2)
    qb = q_ref.shape[0]
    q = q_ref[...] * (HEAD_DIM ** -0.5)
    kk = jnp.concatenate([kp_ref[...], kc_ref[...]], axis=0).astype(BF16)
    vv = jnp.concatenate([vp_ref[...], vc_ref[...]], axis=0).astype(BF16)
    qpos = i * qb + lax.broadcasted_iota(jnp.int32, (qb, 2 * qb), 0)
    kpos = (i - 1) * qb + lax.broadcasted_iota(jnp.int32, (qb, 2 * qb), 1)
    diff = jnp.where(kpos >= 0, qpos - kpos, -1)
    valid = jnp.where(diff >= 0, diff, window + 1) <= window
    upper = _half_mask((qb, LANE), 1)
    for pair in range(DIL_HEADS_PER_GROUP // 2):
        sl = slice(pair * LANE, (pair + 1) * LANE)
        o2, l2 = [], []
        for k in range(2):
            qh = jnp.where(_half_mask((qb, LANE), k), q[:, sl], 0.0).astype(BF16)
            s = lax.dot_general(qh, kk[:, sl], _NT, preferred_element_type=F32)
            o, lse = _softmax_pv(s, valid, vv[:, sl])
            o2.append(o)
            l2.append(jnp.broadcast_to(lse, (qb, LANE)))
        o_ref[:, sl] = jnp.where(upper, o2[1], o2[0])
        lse_ref[:, sl] = jnp.where(upper, l2[1], l2[0])


def _dil_prompt_attn(dil, gi, bsz, t):
    win, step = DIL_PATTERNS[gi]
    n = t // step
    qb = math.gcd(n, QUERY_BLOCK)
    nq = n // qb
    nblk = DIL_IN // DIL_WIDTH
    view = dil.reshape(bsz, n, step * DIL_IN)
    blk = lambda part, prev: pl.BlockSpec(
        (None, qb, DIL_WIDTH),
        lambda b, c, i: (b, jnp.maximum(i - 1, 0) if prev else i, c * nblk + part * DIL_GROUPS + gi))
    out_spec = pl.BlockSpec((None, qb, DIL_WIDTH), lambda b, c, i: (b, i, c))
    out_sds = jax.ShapeDtypeStruct((bsz, n, step * DIL_WIDTH), F32)
    o, lse = pl.pallas_call(
        functools.partial(_dil_prompt_body, window=win // step),
        grid=(bsz, step, nq),
        in_specs=[blk(0, False), blk(1, True), blk(1, False), blk(2, True), blk(2, False)],
        out_specs=[out_spec, out_spec],
        out_shape=[out_sds, out_sds],
        compiler_params=_cparams("arbitrary", "arbitrary", "arbitrary"),
        name=f"dil_prompt_attn_{gi}",
    )(view, view, view, view, view)
    return o.reshape(bsz * t, DIL_WIDTH), lse.reshape(bsz * t, DIL_WIDTH)


def _dil_combine_body(o0, o1, o2, l0, l1, l2, y_ref):
    m = jnp.maximum(jnp.maximum(l0[...], l1[...]), l2[...])
    e0, e1, e2 = jnp.exp(l0[...] - m), jnp.exp(l1[...] - m), jnp.exp(l2[...] - m)
    den = e0 + e1 + e2
    y_ref[...] = (e0 / den) * o0[...] + (e1 / den) * o1[...] + (e2 / den) * o2[...]


def _dil_combine(outs, lses, tm):
    n_tok = outs[0].shape[0]
    spec = pl.BlockSpec((tm, DIL_WIDTH), lambda i: (i, 0))
    return pl.pallas_call(
        _dil_combine_body,
        grid=(n_tok // tm,),
        in_specs=[spec] * 6,
        out_specs=spec,
        out_shape=jax.ShapeDtypeStruct((n_tok, DIL_WIDTH), F32),
        compiler_params=_cparams("arbitrary"),
        name="dil_combine",
    )(*outs, *lses)


def _nsa_decode_body(pt_ref, q_ref, small_ref, rows_ref, win_ref, *refs, n_pages, lw, nb, nb_pad, n_sel):
    del pt_ref
    pages = refs[:n_pages]
    (pastwin_ref, pek_ref, pev_ref, wk1_ref, wk2_ref, wv1_ref, wv2_ref, o_ref,
     kc_s, vc_s, ks_s, vs_s, kw_s, vw_s, comb_s) = refs[n_pages:]
    t = q_ref.shape[0]
    rep = NSA_REP
    past_len = n_pages * PAGE_SIZE
    for dst, c in ((kc_s, 0), (vc_s, 1), (ks_s, 2), (vs_s, 3)):
        for p in range(n_pages):
            dst[p * PAGE_SIZE:(p + 1) * PAGE_SIZE, :] = pages[p][:, c * LANE:(c + 1) * LANE]
        dst[past_len:past_len + t, :] = rows_ref[:, c * LANE:(c + 1) * LANE]
        dst[past_len + t:, :] = jnp.zeros((dst.shape[0] - past_len - t, LANE), F32)
    for dst, c in ((kw_s, 0), (vw_s, 1)):
        dst[:lw, :] = pastwin_ref[:, c * LANE:(c + 1) * LANE]
        dst[lw:lw + t, :] = win_ref[:, c * LANE:(c + 1) * LANE]
        dst[lw + t:, :] = jnp.zeros((dst.shape[0] - lw - t, LANE), F32)
    kcmp = _compress_rows(kc_s, pek_ref, wk1_ref, wk2_ref, nb_pad).astype(BF16)
    vcmp = _compress_rows(vc_s, pev_ref, wv1_ref, wv2_ref, nb_pad).astype(BF16)
    pos_c = past_len + lax.broadcasted_iota(jnp.int32, (t, 1), 0)
    sig = jax.nn.sigmoid(small_ref[...])
    q = q_ref[...] * (HEAD_DIM ** -0.5)
    nk = ks_s.shape[0]
    nw = kw_s.shape[0]
    for g in range(NSA_KV_HEADS):
        qg = _stack_heads(q, range(g * rep, (g + 1) * rep), g).astype(BF16)
        gate_lane = lambda j: [SSM_HEADS + j * NSA_HEADS + g * rep + r for r in range(rep)]
        s = lax.dot_general(qg, kcmp, _NT, preferred_element_type=F32).reshape(rep, t, nb_pad)
        n_i = lax.broadcasted_iota(jnp.int32, (t, nb_pad), 1)
        complete = (n_i + 1) * NSA_BLOCK <= pos_c + 1
        s = jnp.where(complete[None], s, NEG)
        p = jnp.exp(s - jnp.max(s, axis=-1, keepdims=True))
        p = p / jnp.sum(p, axis=-1, keepdims=True)
        p = jnp.where((pos_c >= NSA_BLOCK - 1)[None], p, 0.0)
        o_cmp = jnp.dot(p.reshape(rep * t, nb_pad).astype(BF16), vcmp, preferred_element_type=F32)
        comb = _row_gate(sig, gate_lane(0), t) * o_cmp
        sel = _nsa_select(p, pos_c, nb_pad, n_sel).astype(BF16)
        blk = lax.broadcasted_iota(jnp.int32, (nb_pad, nk), 0)
        key = lax.broadcasted_iota(jnp.int32, (nb_pad, nk), 1)
        chosen = jnp.dot(sel, jnp.where(blk == key // NSA_BLOCK, 1.0, 0.0).astype(BF16), preferred_element_type=F32)
        kpos = lax.broadcasted_iota(jnp.int32, (t, nk), 1)
        valid = jnp.where(kpos <= pos_c, chosen, 0.0) > 0.5
        s = lax.dot_general(qg, ks_s[...].astype(BF16), _NT, preferred_element_type=F32)
        o_sel, _ = _softmax_pv(s, valid, vs_s[...].astype(BF16), rep)
        comb = comb + _row_gate(sig, gate_lane(1), t) * o_sel
        diff = pos_c - (past_len - lw + lax.broadcasted_iota(jnp.int32, (t, nw), 1))
        valid = jnp.where(diff >= 0, diff, NSA_WINDOW + 1) <= NSA_WINDOW
        s = lax.dot_general(qg, kw_s[...].astype(BF16), _NT, preferred_element_type=F32)
        o_win, _ = _softmax_pv(s, valid, vw_s[...].astype(BF16), rep)
        comb_s[...] = comb + _row_gate(sig, gate_lane(2), t) * o_win
        for c in range(rep // 2):
            a = comb_s[(2 * c) * t:(2 * c + 1) * t, :]
            b = comb_s[(2 * c + 1) * t:(2 * c + 2) * t, :]
            col = (g * rep // 2 + c) * LANE
            o_ref[:, col:col + LANE] = _unstack_pair(a, b, g)


def _nsa_decode_attn(q, small, rows, win, cache_kv, cache_win, page_table, lp, bsz, t):
    n_pages = page_table.shape[1]
    past_len = n_pages * PAGE_SIZE
    lw = cache_win.shape[1]
    nb = -(-(past_len + t) // NSA_BLOCK)
    nb_pad = -(-nb // 8) * 8
    n_keys = -(-(past_len + t) // LANE) * LANE
    n_win = -(-(lw + t) // LANE) * LANE
    pages = cache_kv.reshape(cache_kv.shape[0], PAGE_SIZE, 4 * LANE)
    pastwin = cache_win.reshape(bsz, lw, 2 * LANE)
    w1 = lambda w: w.reshape(NSA_BLOCK, HEAD_DIM, NSA_CMP_HIDDEN).astype(BF16)
    tok = lambda n: pl.BlockSpec((t, n), lambda b, pt: (b, 0))
    const = lambda shape: pl.BlockSpec(shape, lambda b, pt: (0,) * len(shape), pipeline_mode=pl.Buffered(1))
    page_spec = lambda p: pl.BlockSpec((None, PAGE_SIZE, 4 * LANE), lambda b, pt: (pt[b * n_pages + p], 0, 0))
    grid_spec = pltpu.PrefetchScalarGridSpec(
        num_scalar_prefetch=1,
        grid=(bsz,),
        in_specs=[tok(NSA_HEADS * HEAD_DIM), tok(LANE), tok(4 * LANE), tok(2 * LANE)]
        + [page_spec(p) for p in range(n_pages)]
        + [pl.BlockSpec((None, lw, 2 * LANE), lambda b, pt: (b, 0, 0)),
           const((NSA_BLOCK, LANE)), const((NSA_BLOCK, LANE)),
           const((NSA_BLOCK, HEAD_DIM, NSA_CMP_HIDDEN)), const((NSA_CMP_HIDDEN, HEAD_DIM)),
           const((NSA_BLOCK, HEAD_DIM, NSA_CMP_HIDDEN)), const((NSA_CMP_HIDDEN, HEAD_DIM))],
        out_specs=tok(NSA_HEADS * HEAD_DIM),
        scratch_shapes=[pltpu.VMEM((nb_pad * NSA_BLOCK, LANE), F32), pltpu.VMEM((nb_pad * NSA_BLOCK, LANE), F32),
                        pltpu.VMEM((n_keys, LANE), F32), pltpu.VMEM((n_keys, LANE), F32),
                        pltpu.VMEM((n_win, LANE), F32), pltpu.VMEM((n_win, LANE), F32),
                        pltpu.VMEM((NSA_REP * t, LANE), F32)],
    )
    return pl.pallas_call(
        functools.partial(_nsa_decode_body, n_pages=n_pages, lw=lw, nb=nb, nb_pad=nb_pad, n_sel=min(NSA_TOPK, nb)),
        grid_spec=grid_spec,
        out_shape=jax.ShapeDtypeStruct((bsz * t, NSA_HEADS * HEAD_DIM), F32),
        compiler_params=_cparams("arbitrary"),
        name="nsa_decode_attn",
    )(page_table.reshape(-1), q, small, rows, win, *([pages] * n_pages), pastwin,
      _pe2(lp['nsa_pe_k']), _pe2(lp['nsa_pe_v']), w1(lp['nsa_ck_w1']), lp['nsa_ck_w2'].astype(BF16),
      w1(lp['nsa_cv_w1']), lp['nsa_cv_w2'].astype(BF16))


def _dil_decode_group(q, k_new, v_new, cache_ref, step):
    t = q.shape[0]
    lg = cache_ref.shape[1]
    pairs = DIL_HEADS_PER_GROUP // 2
    tok_c = lax.broadcasted_iota(jnp.int32, (2 * t, lg), 0) % t
    ahead = lax.broadcasted_iota(jnp.int32, (2 * t, lg), 1) - tok_c
    valid_c = jnp.where(ahead >= 0, ahead % step, 1) == 0
    tok_n = lax.broadcasted_iota(jnp.int32, (2 * t, t), 0) % t
    back = tok_n - lax.broadcasted_iota(jnp.int32, (2 * t, t), 1)
    valid_n = jnp.where(back >= 0, back % step, 1) == 0
    lower = _half_mask((t, LANE), 0)
    outs, lses = [], []
    for pr in range(pairs):
        sl = slice(pr * LANE, (pr + 1) * LANE)
        k_t = cache_ref[pr * LANE:(pr + 1) * LANE, :].astype(BF16)
        v_t = cache_ref[DIL_WIDTH + pr * LANE:DIL_WIDTH + (pr + 1) * LANE, :].astype(BF16)
        kn = k_new[:, sl].astype(BF16)
        vn = v_new[:, sl].astype(BF16)
        qp = q[:, sl]
        q2 = jnp.concatenate([jnp.where(lower, qp, 0.0), jnp.where(lower, 0.0, qp)], axis=0).astype(BF16)
        s_c = jnp.where(valid_c, jnp.dot(q2, k_t, preferred_element_type=F32), NEG)
        s_n = jnp.where(valid_n, lax.dot_general(q2, kn, _NT, preferred_element_type=F32), NEG)
        m = jnp.maximum(jnp.max(s_c, axis=-1, keepdims=True), jnp.max(s_n, axis=-1, keepdims=True))
        p_c = jnp.exp(s_c - m)
        p_n = jnp.exp(s_n - m)
        l = jnp.sum(p_c, axis=-1, keepdims=True) + jnp.sum(p_n, axis=-1, keepdims=True)
        o = lax.dot_general((p_c / l).astype(BF16), v_t, _NT, preferred_element_type=F32)
        o = o + jnp.dot((p_n / l).astype(BF16), vn, preferred_element_type=F32)
        lse = jnp.broadcast_to(m + jnp.log(l), (2 * t, LANE))
        outs.append(jnp.where(lower, o[:t], o[t:]))
        lses.append(jnp.where(lower, lse[:t], lse[t:]))
    return jnp.concatenate(outs, axis=1), jnp.concatenate(lses, axis=1)


def _dil_decode_body(x_ref, c0_ref, c1_ref, c2_ref, y_ref, n0_ref, n1_ref, n2_ref):
    scale = HEAD_DIM ** -0.5
    t = x_ref.shape[0]
    res = []
    for gi, (cache_ref, new_ref) in enumerate(((c0_ref, n0_ref), (c1_ref, n1_ref), (c2_ref, n2_ref))):
        _, step = DIL_PATTERNS[gi]
        part = lambda p: x_ref[:, (p * DIL_GROUPS + gi) * DIL_WIDTH:(p * DIL_GROUPS + gi + 1) * DIL_WIDTH]
        res.append(_dil_decode_group(part(0) * scale, part(1), part(2), cache_ref, step))
        new_t = jnp.concatenate([part(1).T, part(2).T], axis=0)
        new_ref[...] = jnp.concatenate([cache_ref[:, t:], new_t], axis=1)
    (o0, l0), (o1, l1), (o2, l2) = res
    m = jnp.maximum(jnp.maximum(l0, l1), l2)
    e0, e1, e2 = jnp.exp(l0 - m), jnp.exp(l1 - m), jnp.exp(l2 - m)
    den = e0 + e1 + e2
    y_ref[...] = (e0 / den) * o0 + (e1 / den) * o1 + (e2 / den) * o2


def _dil_decode_attn(dil, bufs, bsz, t):
    n_rows = 2 * DIL_WIDTH
    views = []
    for (win, _), buf in zip(DIL_PATTERNS, bufs):
        assert buf.shape[1] == win, "decode path needs a full window of cached rows"
        views.append(jnp.transpose(buf, (0, 2, 3, 4, 1)).reshape(bsz, n_rows, win))
    cache_spec = lambda v: pl.BlockSpec((None, n_rows, v.shape[2]), lambda b: (b, 0, 0))
    tok_spec = lambda n: pl.BlockSpec((t, n), lambda b: (b, 0))
    y, *new = pl.pallas_call(
        _dil_decode_body,
        grid=(bsz,),
        in_specs=[tok_spec(DIL_IN)] + [cache_spec(v) for v in views],
        out_specs=[tok_spec(DIL_WIDTH)] + [cache_spec(v) for v in views],
        out_shape=[jax.ShapeDtypeStruct((bsz * t, DIL_WIDTH), F32)]
        + [jax.ShapeDtypeStruct(v.shape, F32) for v in views],
        compiler_params=_cparams("arbitrary"),
        name="dil_decode_attn",
    )(dil, *views)
    unview = lambda v: jnp.transpose(v.reshape(bsz, 2, DIL_HEADS_PER_GROUP, HEAD_DIM, v.shape[2]), (0, 4, 1, 2, 3))
    return y, [unview(v) for v in new]


_AB_CUTS = (0, SSM_INNER, SSM_INNER + SSM_CONV_DIM, SSM_INNER + SSM_CONV_DIM + SSM_HEADS)
_AB_Q0 = _AB_CUTS[3]
_AB_KV0 = _AB_Q0 + NSA_HEADS * HEAD_DIM
_AB_WIN0 = _AB_KV0 + 4 * NSA_KV_HEADS * HEAD_DIM
_AB_GATE0 = _AB_WIN0 + 2 * NSA_KV_HEADS * HEAD_DIM
_AB_END = _AB_GATE0 + 3 * NSA_HEADS


def _pad_cols(w, n):
    return jnp.pad(w, ((0, 0), (0, n - w.shape[1])))


def _ab_weight_segs(w_in):
    small = jnp.concatenate([w_in[:, _AB_CUTS[2]:_AB_CUTS[3]], w_in[:, _AB_GATE0:_AB_END]], axis=1)
    segs = [w_in[:, _AB_CUTS[0]:_AB_CUTS[1]], w_in[:, _AB_CUTS[1]:_AB_CUTS[2]], w_in[:, _AB_Q0:_AB_KV0],
            w_in[:, _AB_KV0:_AB_WIN0], w_in[:, _AB_WIN0:_AB_GATE0], _pad_cols(small, LANE)]
    return [s.astype(BF16) for s in segs]


_CD_Z0 = 3 * GDN_WIDTH
_CD_B0 = _CD_Z0 + GDN_WIDTH
_CD_DIL0 = _CD_B0 + 2 * GDN_HEADS
_CD_END = _CD_DIL0 + DIL_IN


def _cd_weight_segs(w_in):
    segs = [w_in[:, :_CD_Z0], w_in[:, _CD_Z0:_CD_B0], w_in[:, _CD_DIL0:_CD_END],
            _pad_cols(w_in[:, _CD_B0:_CD_DIL0], LANE)]
    return [s.astype(BF16) for s in segs]


def _shift_in(buf, new):
    return jnp.concatenate([buf.astype(new.dtype), new], axis=1)[:, new.shape[1]:]


def _layer_ab(y, norm_pre, segs, lp, past, page_table, is_prompt, tm):
    bsz, t, d = y.shape
    z, xbc, q, rows, win, small = _norm_proj(y.reshape(bsz * t, d), norm_pre, segs, tm)
    rows_new = rows.reshape(bsz, t, 4, NSA_KV_HEADS, HEAD_DIM)
    win_new = win.reshape(bsz, t, 2, NSA_KV_HEADS, HEAD_DIM)
    if is_prompt:
        conv_buf = jnp.zeros((bsz, CONV_K - 1, SSM_CONV_DIM), F32)
        h0 = jnp.zeros((bsz, SSM_HEADS, SSM_HEAD_DIM, SSM_STATE), F32)
        chunk = math.gcd(t, SSM_CHUNK)
        rows3 = rows.reshape(bsz, t, 4 * LANE)
        cmp = _nsa_compress_prompt(rows3, lp)
        y_nsa = _nsa_prompt_attn(q, small, cmp, rows3, win.reshape(bsz, t, 2 * LANE), bsz, t)
        new_win = win_new[:, t - min(NSA_WINDOW, t):]
    else:
        conv_buf, h0, chunk = past['ssm_conv'], past['ssm'], t
        y_nsa = _nsa_decode_attn(q, small, rows, win, past['nsa_kv'], past['nsa_win'], page_table, lp, bsz, t)
        new_win = _shift_in(past['nsa_win'], win_new)
    y_ssm, new_conv, new_ssm = _ssd_mixer(xbc, z, small, conv_buf, h0, lp, bsz, t, chunk)
    return y_ssm, y_nsa, (new_conv, new_ssm, rows_new, new_win)


def _layer_cd(y, norm_pre, segs, lp, past, is_prompt, tm):
    bsz, t, d = y.shape
    qkv, z, dil, small = _norm_proj(y.reshape(bsz * t, d), norm_pre, segs, tm)
    if is_prompt:
        d6 = dil.reshape(bsz, t, 3, DIL_GROUPS, DIL_HEADS_PER_GROUP, HEAD_DIM)
        kv_new = [jnp.stack([d6[:, :, 1, gi], d6[:, :, 2, gi]], axis=2) for gi in range(DIL_GROUPS)]
        conv_buf = jnp.zeros((bsz, CONV_K - 1, 3 * GDN_WIDTH), F32)
        s0 = jnp.zeros((bsz, GDN_HEADS, GDN_HEAD_DIM, GDN_HEAD_DIM), F32)
        chunk = math.gcd(t, GDN_CHUNK)
        parts = [_dil_prompt_attn(dil, gi, bsz, t) for gi in range(DIL_GROUPS)]
        y_dil = _dil_combine([p[0] for p in parts], [p[1] for p in parts], min(tm * 2, bsz * t))
        bufs = [kv[:, t - min(win, t):] for kv, (win, _) in zip(kv_new, DIL_PATTERNS)]
    else:
        conv_buf, s0, chunk = past['gdn_conv'], past['gdn'], t
        y_dil, bufs = _dil_decode_attn(dil, past['dil'], bsz, t)
    y_gdn, new_conv, new_gdn = _gdn_mixer(qkv, z, small, conv_buf, s0, lp, bsz, t, chunk)
    return y_gdn, y_dil, (new_conv, new_gdn, bufs[0], bufs[1], bufs[2])


def kernel(x_prompt, x_sample, cache_ssm_conv, state_ssm, cache_nsa_kv, cache_nsa_win_kv, cache_gdn_conv, state_gdn,
           cache_dil0_kv, cache_dil1_kv, cache_dil2_kv, page_table, norm_mix_pre, norm_mix_post, norm_mlp_pre,
           norm_mlp_post, mlp_w1, mlp_w2, ab_w_in, ab_w_out, ssm_conv_w, ssm_conv_b, ssm_dt_bias, ssm_a_log, ssm_d,
           ssm_norm_w, nsa_pe_k, nsa_pe_v, nsa_ck_w1, nsa_ck_w2, nsa_cv_w1, nsa_cv_w2, cd_w_in, cd_w_out, gdn_conv_w,
           gdn_dt_bias, gdn_a_log, gdn_norm_w):
    depth = norm_mix_pre.shape[0]
    yp, ys = x_prompt, x_sample
    ab_p, ab_s, cd_p, cd_s = [], [], [], []
    tm = 256
    for l in range(depth):
        j = l // 2
        if l % 2 == 0:
            lp = {'ssm_conv_w': ssm_conv_w[j], 'ssm_conv_b': ssm_conv_b[j], 'ssm_dt_bias': ssm_dt_bias[j],
                  'ssm_a_log': ssm_a_log[j], 'ssm_d': ssm_d[j], 'ssm_norm_w': ssm_norm_w[j],
                  'nsa_pe_k': nsa_pe_k[j], 'nsa_pe_v': nsa_pe_v[j], 'nsa_ck_w1': nsa_ck_w1[j],
                  'nsa_ck_w2': nsa_ck_w2[j], 'nsa_cv_w1': nsa_cv_w1[j], 'nsa_cv_w2': nsa_cv_w2[j]}
            past = {'ssm_conv': cache_ssm_conv[j], 'ssm': state_ssm[j], 'nsa_kv': cache_nsa_kv[j],
                    'nsa_win': cache_nsa_win_kv[j]}
            segs = _ab_weight_segs(ab_w_in[j])
            w_out = ab_w_out[j].astype(BF16)
            ka = SSM_INNER
            ap, bp, stp = _layer_ab(yp, norm_mix_pre[l], segs, lp, None, None, True, tm)
            as_, bs, sts = _layer_ab(ys, norm_mix_pre[l], segs, lp, past, page_table, False, tm)
            ab_p.append(stp)
            ab_s.append(sts)
        else:
            lp = {'gdn_conv_w': gdn_conv_w[j], 'gdn_dt_bias': gdn_dt_bias[j], 'gdn_a_log': gdn_a_log[j],
                  'gdn_norm_w': gdn_norm_w[j]}
            past = {'gdn_conv': cache_gdn_conv[j], 'gdn': state_gdn[j],
                    'dil': (cache_dil0_kv[j], cache_dil1_kv[j], cache_dil2_kv[j])}
            segs = _cd_weight_segs(cd_w_in[j])
            w_out = cd_w_out[j].astype(BF16)
            ka = GDN_WIDTH
            ap, bp, stp = _layer_cd(yp, norm_mix_pre[l], segs, lp, None, True, tm)
            as_, bs, sts = _layer_cd(ys, norm_mix_pre[l], segs, lp, past, False, tm)
            cd_p.append(stp)
            cd_s.append(sts)
        w1 = mlp_w1[l].astype(BF16)
        w2 = mlp_w2[l].astype(BF16)
        post = functools.partial(_post_block, wo_a=w_out[:ka], wo_b=w_out[ka:], w1=w1, w2=w2, n_mix=norm_mix_post[l],
                                 n_pre=norm_mlp_pre[l], n_post=norm_mlp_post[l], tm=tm)
        yp = post(ap, bp, yp.reshape(-1, D_MODEL)).reshape(yp.shape)
        ys = post(as_, bs, ys.reshape(-1, D_MODEL)).reshape(ys.shape)
    stack = lambda states, i: jnp.stack([s[i] for s in states])
    return (yp, ys,
            stack(ab_p, 0), stack(ab_s, 0), stack(ab_p, 1), stack(ab_s, 1),
            stack(ab_p, 2), stack(ab_s, 2), stack(ab_p, 3), stack(ab_s, 3),
            stack(cd_p, 0), stack(cd_s, 0), stack(cd_p, 1), stack(cd_s, 1),
            stack(cd_p, 2), stack(cd_s, 2), stack(cd_p, 3), stack(cd_s, 3),
            stack(cd_p, 4), stack(cd_s, 4))
```

```python
import functools
import math

import jax
import jax.numpy as jnp
from jax import lax
from jax.experimental import pallas as pl
from jax.experimental.pallas import tpu as pltpu

F32 = jnp.float32
BF16 = jnp.bfloat16

D_MODEL = 1024
HEAD_DIM = 64
CONV_K = 4
RMS_EPS = 1e-6
MLP_HIDDEN = 4 * D_MODEL
QUERY_BLOCK = 128
PAGE_SIZE = 128

SSM_HEADS = 16
SSM_HEAD_DIM = 64
SSM_INNER = SSM_HEADS * SSM_HEAD_DIM
SSM_GROUPS = 2
SSM_STATE = 128
SSM_CONV_DIM = SSM_INNER + 2 * SSM_GROUPS * SSM_STATE
SSM_CHUNK = 128

NSA_HEADS = 16
NSA_KV_HEADS = 2
NSA_REP = NSA_HEADS // NSA_KV_HEADS
NSA_BLOCK = 64
NSA_TOPK = 16
NSA_WINDOW = 512
NSA_CMP_HIDDEN = 256

GDN_HEADS = 8
GDN_HEAD_DIM = 128
GDN_WIDTH = GDN_HEADS * GDN_HEAD_DIM
GDN_CHUNK = 64

DIL_PATTERNS = ((128, 1), (512, 4), (2048, 16))
DIL_GROUPS = len(DIL_PATTERNS)
DIL_HEADS_PER_GROUP = 8
DIL_WIDTH = DIL_HEADS_PER_GROUP * HEAD_DIM
DIL_IN = 3 * DIL_GROUPS * DIL_WIDTH
NEG = -1e30

VMEM_LIMIT_BYTES = 56 * 1024 * 1024
LANE = 128


def _cparams(*sem):
    return pltpu.CompilerParams(dimension_semantics=sem, vmem_limit_bytes=VMEM_LIMIT_BYTES)


def _const_spec(shape):
    nd = len(shape)
    return pl.BlockSpec(shape, lambda *_: (0,) * nd, pipeline_mode=pl.Buffered(1))


def _rms(x, w):
    return x * lax.rsqrt(jnp.mean(x * x, axis=-1, keepdims=True) + RMS_EPS) * w


def _proj_body(x_ref, nw_ref, *refs):
    n = len(refs) // 2
    xn = _rms(x_ref[...], nw_ref[...]).astype(BF16)
    for w_ref, o_ref in zip(refs[:n], refs[n:]):
        y = jnp.dot(xn, w_ref[...], preferred_element_type=F32)
        if len(o_ref.shape) == 2:
            o_ref[...] = y
        else:
            for k in range(o_ref.shape[0]):
                o_ref[k] = y[:, k * LANE:(k + 1) * LANE]


def _norm_proj(x, norm_w, w_segs, tm, plane_segs=()):
    n_tok, d = x.shape
    tm = min(tm, n_tok)
    widths = [w.shape[1] for w in w_segs]
    planes = [k in plane_segs for k in range(len(w_segs))]
    out_spec = lambda n, p: (pl.BlockSpec((n // LANE, tm, LANE), lambda i: (0, i, 0)) if p
                             else pl.BlockSpec((tm, n), lambda i: (i, 0)))
    out_sds = lambda n, p: jax.ShapeDtypeStruct((n // LANE, n_tok, LANE) if p else (n_tok, n), F32)
    return pl.pallas_call(
        _proj_body,
        grid=(n_tok // tm,),
        in_specs=[pl.BlockSpec((tm, d), lambda i: (i, 0)), _const_spec((1, d))]
        + [_const_spec((d, n)) for n in widths],
        out_specs=[out_spec(n, p) for n, p in zip(widths, planes)],
        out_shape=[out_sds(n, p) for n, p in zip(widths, planes)],
        compiler_params=_cparams("arbitrary"),
        name="norm_proj",
    )(x, norm_w.reshape(1, d), *w_segs)


def _post_body(a_ref, b_ref, y_ref, woa_ref, wob_ref, w1_ref, w2_ref, nmix_ref, npre_ref, npost_ref, o_ref):
    m = jnp.dot(a_ref[...].astype(BF16), woa_ref[...], preferred_element_type=F32)
    m = m + jnp.dot(b_ref[...].astype(BF16), wob_ref[...], preferred_element_type=F32)
    y1 = y_ref[...] + _rms(m, nmix_ref[...])
    h = _rms(y1, npre_ref[...]).astype(BF16)
    a = jnp.maximum(jnp.dot(h, w1_ref[...], preferred_element_type=F32), 0.0)
    m2 = jnp.dot((a * a).astype(BF16), w2_ref[...], preferred_element_type=F32)
    o_ref[...] = y1 + _rms(m2, npost_ref[...])


def _post_block(mix_a, mix_b, y, wo_a, wo_b, w1, w2, n_mix, n_pre, n_post, tm):
    n_tok, d = y.shape
    tm = min(tm, n_tok)
    ka, kb = mix_a.shape[1], mix_b.shape[1]
    row = lambda n: pl.BlockSpec((tm, n), lambda i: (i, 0))
    return pl.pallas_call(
        _post_body,
        grid=(n_tok // tm,),
        in_specs=[row(ka), row(kb), row(d), _const_spec(wo_a.shape), _const_spec(wo_b.shape),
                  _const_spec(w1.shape), _const_spec(w2.shape), _const_spec((1, d)), _const_spec((1, d)),
                  _const_spec((1, d))],
        out_specs=row(d),
        out_shape=jax.ShapeDtypeStruct((n_tok, d), F32),
        compiler_params=_cparams("arbitrary"),
        name="post_block",
    )(mix_a, mix_b, y, wo_a, wo_b, w1, w2, n_mix.reshape(1, d), n_pre.reshape(1, d), n_post.reshape(1, d))


def _compress_rows(src_ref, pe_ref, w1_ref, w2_ref, nb):
    hd = HEAD_DIM

    def body(r, acc):
        x = src_ref[pl.ds(r, nb, stride=NSA_BLOCK), :] + pe_ref[pl.ds(r, 1), :]
        xg = jnp.concatenate([x[:, :hd], x[:, hd:]], axis=0).astype(BF16)
        return acc + jnp.dot(xg, w1_ref[r], preferred_element_type=F32)

    hid = lax.fori_loop(0, NSA_BLOCK, body, jnp.zeros((2 * nb, NSA_CMP_HIDDEN), F32), unroll=8)
    hid = hid * jax.nn.sigmoid(hid)
    out = jnp.dot(hid.astype(BF16), w2_ref[...], preferred_element_type=F32)
    return jnp.concatenate([out[:nb], out[nb:]], axis=1)


def _compress_body(kc_ref, vc_ref, pek_ref, pev_ref, wk1_ref, wk2_ref, wv1_ref, wv2_ref, o_ref, *, nb):
    o_ref[:, :LANE] = _compress_rows(kc_ref, pek_ref, wk1_ref, wk2_ref, nb)
    o_ref[:, LANE:] = _compress_rows(vc_ref, pev_ref, wv1_ref, wv2_ref, nb)


def _pe2(pe):
    return jnp.concatenate([pe, pe], axis=1)


def _nsa_compress_prompt(rows, lp):
    bsz, t, _ = rows.shape
    nb = t // NSA_BLOCK
    w1 = lambda w: w.reshape(NSA_BLOCK, HEAD_DIM, NSA_CMP_HIDDEN).astype(BF16)
    col = lambda c: pl.BlockSpec((None, t, LANE), lambda b: (b, 0, c))
    return pl.pallas_call(
        functools.partial(_compress_body, nb=nb),
        grid=(bsz,),
        in_specs=[col(0), col(1), _const_spec((NSA_BLOCK, LANE)), _const_spec((NSA_BLOCK, LANE)),
                  _const_spec((NSA_BLOCK, HEAD_DIM, NSA_CMP_HIDDEN)), _const_spec((NSA_CMP_HIDDEN, HEAD_DIM)),
                  _const_spec((NSA_BLOCK, HEAD_DIM, NSA_CMP_HIDDEN)), _const_spec((NSA_CMP_HIDDEN, HEAD_DIM))],
        out_specs=pl.BlockSpec((None, nb, 2 * LANE), lambda b: (b, 0, 0)),
        out_shape=jax.ShapeDtypeStruct((bsz, nb, 2 * LANE), F32),
        compiler_params=_cparams("arbitrary"),
        name="nsa_compress",
    )(rows, rows, _pe2(lp['nsa_pe_k']), _pe2(lp['nsa_pe_v']), w1(lp['nsa_ck_w1']), lp['nsa_ck_w2'].astype(BF16),
      w1(lp['nsa_cv_w1']), lp['nsa_cv_w2'].astype(BF16))


_NT = (((1,), (1,)), ((), ()))


def _half_mask(shape, g):
    lane = lax.broadcasted_iota(jnp.int32, shape, len(shape) - 1)
    return lane >= HEAD_DIM if g else lane < HEAD_DIM


def _stack_heads(x, heads, g):
    keep = _half_mask((x.shape[0], LANE), g)
    out = []
    for h in heads:
        blk = x[:, (h // 2) * LANE:(h // 2 + 1) * LANE]
        if h % 2 != g:
            blk = pltpu.roll(blk, HEAD_DIM, axis=1)
        out.append(jnp.where(keep, blk, 0.0))
    return jnp.concatenate(out, axis=0)


def _unstack_pair(a, b, g):
    if g == 0:
        b = pltpu.roll(b, HEAD_DIM, axis=1)
    else:
        a = pltpu.roll(a, HEAD_DIM, axis=1)
    return jnp.where(_half_mask(a.shape, 1), b, a)


def _row_gate(sig, lanes, rows):
    return jnp.concatenate([jnp.broadcast_to(sig[:, c:c + 1], (rows, LANE)) for c in lanes], axis=0)


def _nsa_select(p3, pos, nb, n_sel):
    nq = p3.shape[1]
    imp = jnp.sum(p3, axis=0)
    n_i = lax.broadcasted_iota(jnp.int32, (nq, nb), 1)
    cur = pos // NSA_BLOCK
    score = jnp.where(n_i == 0, NSA_REP + 1.0, imp)
    score = jnp.where(n_i == cur, NSA_REP + 1.0, score)
    score = jnp.where(n_i == cur - 1, NSA_REP + 1.0, score)
    score = jnp.where(n_i > cur, -1.0, score)
    rank = jnp.zeros((nq, nb), F32)
    for m in range(nb):
        col = score[:, m:m + 1]
        tie = jnp.where(n_i > m, 1.0, 0.0)
        rank = rank + jnp.where(col > score, 1.0, jnp.where(col == score, tie, 0.0))
    return jnp.where(rank < n_sel, 1.0, 0.0)


def _nsa_select_t(imp, pos, n_sel):
    nb, nq = imp.shape
    n_i = lax.broadcasted_iota(jnp.int32, (nb, nq), 0)
    cur = pos // NSA_BLOCK
    score = jnp.where(n_i == 0, NSA_REP + 1.0, imp)
    score = jnp.where(n_i == cur, NSA_REP + 1.0, score)
    score = jnp.where(n_i == cur - 1, NSA_REP + 1.0, score)
    score = jnp.where(n_i > cur, -1.0, score)
    rank = jnp.zeros((nb, nq), F32)
    for m in range(nb):
        row = score[m:m + 1, :]
        tie = jnp.where(n_i > m, 1.0, 0.0)
        rank = rank + jnp.where(row > score, 1.0, jnp.where(row == score, tie, 0.0))
    return jnp.where(rank < n_sel, 1.0, 0.0)


def _lane_tile(x, n):
    return jnp.concatenate([x] * n, axis=1)


_V_ROWS = HEAD_DIM + 16


def _flash_tile_t(kt, vt_aug, q_t, valid, m_ref, acc_ref, g):
    nk, nq = valid.shape
    rep = q_t.shape[1] // nq
    s = jnp.dot(kt, q_t, preferred_element_type=F32)
    s = jnp.concatenate([jnp.where(valid, s[:, r * nq:(r + 1) * nq], NEG) for r in range(rep)], axis=1)
    m_old = m_ref[g, 0:1, :]
    m_new = jnp.maximum(m_old, jnp.max(s, axis=0, keepdims=True))
    alpha = jnp.exp(m_old - m_new)
    p = jnp.exp(s - m_new)
    acc_ref[g] = alpha * acc_ref[g] + jnp.dot(vt_aug, p.astype(BF16), preferred_element_type=F32)
    m_ref[g] = jnp.broadcast_to(m_new, m_ref.shape[1:])


def _nsa_prompt_body(q_ref, small_ref, cmp_ref, ks_ref, vs_ref, kw_ref, vw_ref, o_ref, vst_ref, vwt_ref, m_ref, acc_ref,
                     *, nb, n_sel):
    qb = q_ref.shape[0]
    t = ks_ref.shape[0]
    i = pl.program_id(1)
    q0 = i * qb
    rep = NSA_REP
    hd = HEAD_DIM
    groups = range(NSA_KV_HEADS)

    @pl.when(i == 0)
    def _():
        ones = jnp.ones((_V_ROWS - hd, t), BF16)
        for g in groups:
            vst_ref[g, hd:, :] = ones
            vwt_ref[g, hd:, :] = ones

        def fill(j, carry):
            k0 = pl.multiple_of(j * qb, qb)
            for src, dst in ((vs_ref, vst_ref), (vw_ref, vwt_ref)):
                v_t = src[pl.ds(k0, qb), :].T.astype(BF16)
                for g in groups:
                    dst[g, :hd, pl.ds(k0, qb)] = v_t[g * hd:(g + 1) * hd, :]
            return carry

        lax.fori_loop(0, t // qb, fill, 0)

    kb = 2 * qb if t % (2 * qb) == 0 else qb
    key_i = lax.broadcasted_iota(jnp.int32, (kb, qb), 0)
    qpos = q0 + lax.broadcasted_iota(jnp.int32, (kb, qb), 1)
    pos_r = q0 + lax.broadcasted_iota(jnp.int32, (1, qb), 1)
    sig_t = jax.nn.sigmoid(small_ref[...]).T
    q = q_ref[...] * (hd ** -0.5)
    pairs_t = [q[:, c * LANE:(c + 1) * LANE].T for c in range(NSA_HEADS // 2)]
    head_t = lambda h: pairs_t[h // 2][(h % 2) * hd:(h % 2 + 1) * hd, :]
    zeros = jnp.zeros((hd, rep * qb), F32)
    q_ts, sels = [], []
    for g in groups:
        qg_t = jnp.concatenate([head_t(g * rep + r) for r in range(rep)], axis=1)
        q_t = jnp.concatenate([qg_t, zeros] if g == 0 else [zeros, qg_t], axis=0).astype(BF16)
        q_ts.append(q_t)
        gate_row = lambda j: jnp.concatenate(
            [sig_t[SSM_HEADS + j * NSA_HEADS + g * rep + r:SSM_HEADS + j * NSA_HEADS + g * rep + r + 1, :]
             for r in range(rep)], axis=1)
        kc = cmp_ref[:, :LANE].astype(BF16)
        vc_t = cmp_ref[:, LANE:].T[g * hd:(g + 1) * hd, :].astype(BF16)
        s = jnp.dot(kc, q_t, preferred_element_type=F32)
        n_i = lax.broadcasted_iota(jnp.int32, (nb, qb), 0)
        complete = _lane_tile(jnp.where((n_i + 1) * NSA_BLOCK <= pos_r + 1, 1.0, 0.0), rep)
        s = jnp.where(complete > 0.5, s, NEG)
        p = jnp.exp(s - jnp.max(s, axis=0, keepdims=True))
        p = p / jnp.sum(p, axis=0, keepdims=True)
        p = jnp.where(_lane_tile(pos_r, rep) >= NSA_BLOCK - 1, p, 0.0)
        o_cmp = jnp.dot(vc_t, p.astype(BF16), preferred_element_type=F32)
        acc_ref[2 + g, :hd, :] = gate_row(0) * o_cmp
        imp = p[:, :qb]
        for r in range(1, rep):
            imp = imp + p[:, r * qb:(r + 1) * qb]
        sels.append(_nsa_select_t(imp, pos_r, n_sel).astype(BF16))

    def flash_init():
        m_ref[...] = jnp.full(m_ref.shape, NEG, F32)
        acc_ref[0:2] = jnp.zeros((2,) + acc_ref.shape[1:], F32)

    def flash_out(g):
        acc = acc_ref[g]
        return acc[:hd, :] / acc[hd:hd + 1, :]

    flash_init()

    def sel_step(j, carry):
        k0 = pl.multiple_of(j * kb, kb)
        kt = ks_ref[pl.ds(k0, kb), :].astype(BF16)
        blk = lax.broadcasted_iota(jnp.int32, (kb, nb), 1)
        key = lax.broadcasted_iota(jnp.int32, (kb, nb), 0)
        expand = jnp.where(blk == (k0 + key) // NSA_BLOCK, 1.0, 0.0).astype(BF16)
        causal = k0 + key_i <= qpos
        for g in groups:
            chosen = jnp.dot(expand, sels[g], preferred_element_type=F32)
            valid = jnp.where(causal, chosen, 0.0) > 0.5
            _flash_tile_t(kt, vst_ref[g, :, pl.ds(k0, kb)], q_ts[g], valid, m_ref, acc_ref, g)
        return carry

    last_tile = (q0 + qb - 1) // kb
    lax.fori_loop(0, last_tile + 1, sel_step, 0)
    for g in groups:
        gate_row = jnp.concatenate(
            [sig_t[SSM_HEADS + NSA_HEADS + g * rep + r:SSM_HEADS + NSA_HEADS + g * rep + r + 1, :] for r in range(rep)],
            axis=1)
        acc_ref[2 + g, :hd, :] += gate_row * flash_out(g)
    flash_init()

    def win_step(j, carry):
        k0 = pl.multiple_of(j * kb, kb)
        kt = kw_ref[pl.ds(k0, kb), :].astype(BF16)
        diff = qpos - (k0 + key_i)
        valid = jnp.where(diff >= 0, diff, NSA_WINDOW + 1) <= NSA_WINDOW
        for g in groups:
            _flash_tile_t(kt, vwt_ref[g, :, pl.ds(k0, kb)], q_ts[g], valid, m_ref, acc_ref, g)
        return carry

    lax.fori_loop(jnp.maximum(q0 - NSA_WINDOW, 0) // kb, last_tile + 1, win_step, 0)
    for g in groups:
        gate_row = jnp.concatenate(
            [sig_t[SSM_HEADS + 2 * NSA_HEADS + g * rep + r:SSM_HEADS + 2 * NSA_HEADS + g * rep + r + 1, :]
             for r in range(rep)], axis=1)
        comb = acc_ref[2 + g, :hd, :] + gate_row * flash_out(g)
        for c in range(rep // 2):
            pair = jnp.concatenate([comb[:, (2 * c) * qb:(2 * c + 1) * qb], comb[:, (2 * c + 1) * qb:(2 * c + 2) * qb]],
                                   axis=0)
            col = (g * rep // 2 + c) * LANE
            o_ref[:, col:col + LANE] = pair.T


def _nsa_prompt_attn(q, small, cmp, rows, win, bsz, t):
    qb = math.gcd(t, QUERY_BLOCK)
    nq = t // qb
    nb = cmp.shape[1]
    n_sel = min(NSA_TOPK, nb)
    tok = lambda n: pl.BlockSpec((qb, n), lambda b, i: (b * nq + i, 0))
    seq = lambda c: pl.BlockSpec((None, t, LANE), lambda b, i: (b, 0, c))
    v_t = pltpu.VMEM((NSA_KV_HEADS, _V_ROWS, t), BF16)
    return pl.pallas_call(
        functools.partial(_nsa_prompt_body, nb=nb, n_sel=n_sel),
        grid=(bsz, nq),
        in_specs=[tok(NSA_HEADS * HEAD_DIM), tok(LANE), pl.BlockSpec((None, nb, 2 * LANE), lambda b, i: (b, 0, 0)),
                  seq(2), seq(3), seq(0), seq(1)],
        out_specs=tok(NSA_HEADS * HEAD_DIM),
        out_shape=jax.ShapeDtypeStruct((bsz * t, NSA_HEADS * HEAD_DIM), F32),
        scratch_shapes=[v_t, v_t, pltpu.VMEM((NSA_KV_HEADS, 8, NSA_REP * qb), F32),
                        pltpu.VMEM((2 * NSA_KV_HEADS, _V_ROWS, NSA_REP * qb), F32)],
        compiler_params=_cparams("arbitrary", "arbitrary"),
        name="nsa_prompt_attn",
    )(q, small, cmp, rows, rows, win, win)


_HI = lax.Precision.HIGHEST
_TN = (((0,), (0,)), ((), ()))


def _silu(x):
    return x * jax.nn.sigmoid(x)


def _softplus(x):
    return jnp.maximum(x, 0.0) + jnp.log(1.0 + jnp.exp(-jnp.abs(x)))


def _conv_silu(x_ref, buf_ref, w_ref, b_ref, xp_ref, tail_ref, first):
    n = x_ref.shape[0]

    @pl.when(first)
    def _():
        xp_ref[8 - (CONV_K - 1):8, :] = buf_ref[...]

    xp_ref[8:8 + n, :] = x_ref[...]
    y = xp_ref[8:8 + n, :] * w_ref[CONV_K - 1:CONV_K, :]
    for k in range(CONV_K - 1):
        y = y + xp_ref[5 + k:5 + k + n, :] * w_ref[k:k + 1, :]
    if b_ref is not None:
        y = y + b_ref[...]
    tail = xp_ref[8 + n - (CONV_K - 1):8 + n, :]
    tail_ref[...] = tail
    xp_ref[8 - (CONV_K - 1):8, :] = tail
    return _silu(y)


def _cumsum_rows(x):
    n = x.shape[0]
    tri = jnp.where(lax.broadcasted_iota(jnp.int32, (n, n), 0) >= lax.broadcasted_iota(jnp.int32, (n, n), 1), 1.0, 0.0)
    return jnp.dot(tri, x, preferred_element_type=F32, precision=_HI)


def _expand_heads(x, width, lane0=0, n_out=D_MODEL):
    h_i = lax.broadcasted_iota(jnp.int32, (LANE, n_out), 0)
    c_i = lax.broadcasted_iota(jnp.int32, (LANE, n_out), 1)
    sel = jnp.where(c_i // width + lane0 == h_i, 1.0, 0.0)
    return jnp.dot(x, sel, preferred_element_type=F32, precision=_HI)


def _decay_matrix(col, row, strict=False):
    n = col.shape[0]
    i = lax.broadcasted_iota(jnp.int32, (n, n), 0)
    j = lax.broadcasted_iota(jnp.int32, (n, n), 1)
    keep = (i > j) if strict else (i >= j)
    return jnp.exp(jnp.where(keep, col - row, NEG))


def _ssd_body(xbc_ref, z_ref, small_ref, buf_ref, h0_ref, cw_ref, cb_ref, dtb_ref, alog_ref, dx_ref, nw_ref,
              y_ref, conv_ref, h_ref, xp_ref):
    c = pl.program_id(1)
    n = xbc_ref.shape[0]
    hp = SSM_HEAD_DIM
    rep = SSM_HEADS // SSM_GROUPS

    @pl.when(c == 0)
    def _():
        h_ref[...] = h0_ref[...]

    act = _conv_silu(xbc_ref, buf_ref, cw_ref, cb_ref, xp_ref, conv_ref, c == 0)
    xs = act[:, :SSM_INNER]
    head_lane = lax.broadcasted_iota(jnp.int32, (n, LANE), 1) < SSM_HEADS
    dt = jnp.where(head_lane, _softplus(small_ref[...] + dtb_ref[...]), 0.0)
    la = dt * (-jnp.exp(alog_ref[...]))
    acs = _cumsum_rows(la)
    acs_t = acs.T
    xd = xs * _expand_heads(dt, hp)
    e_acs = _expand_heads(jnp.exp(acs), hp)
    xdd = (xd * _expand_heads(jnp.exp(acs[n - 1:n, :] - acs), hp)).astype(BF16)
    xd = xd.astype(BF16)
    lane2 = _half_mask((n, LANE), 1)
    for g in range(SSM_GROUPS):
        bc = act[:, SSM_INNER + g * SSM_STATE:SSM_INNER + (g + 1) * SSM_STATE].astype(BF16)
        cc = act[:, SSM_INNER + (SSM_GROUPS + g) * SSM_STATE:SSM_INNER + (SSM_GROUPS + g + 1) * SSM_STATE].astype(BF16)
        cb = lax.dot_general(cc, bc, _NT, preferred_element_type=F32)
        h_prev = h_ref[g * rep:(g + 1) * rep].reshape(rep * hp, SSM_STATE)
        y_off = lax.dot_general(cc, h_prev.astype(BF16), _NT, preferred_element_type=F32)
        for pair in range(rep // 2):
            halves = []
            for k in range(2):
                h = g * rep + 2 * pair + k
                lm = _decay_matrix(acs[:, h:h + 1], acs_t[h:h + 1, :])
                halves.append(jnp.dot((cb * lm).astype(BF16), xd[:, (h // 2) * LANE:(h // 2 + 1) * LANE],
                                      preferred_element_type=F32))
            col = (g * rep + 2 * pair) * hp
            y_ref[:, col:col + LANE] = jnp.where(lane2, halves[1], halves[0]) + y_off[:, 2 * pair * hp:2 * pair * hp + LANE] * e_acs[:, col:col + LANE]
        st = lax.dot_general(xdd[:, g * rep * hp:(g + 1) * rep * hp], bc, _TN, preferred_element_type=F32)
        for r in range(rep):
            h = g * rep + r
            dec = jnp.exp(acs_t[h:h + 1, n - 1:n])
            h_ref[h] = h_ref[h] * dec + st[r * hp:(r + 1) * hp, :]
    y = y_ref[...] + dx_ref[...] * xs
    y_ref[...] = _rms(y * _silu(z_ref[...]), nw_ref[...])


def _ssd_mixer(xbc, z, small, conv_buf, h0, lp, bsz, t, chunk):
    nc = t // chunk
    tok = lambda n: pl.BlockSpec((chunk, n), lambda b, c: (b * nc + c, 0))
    per_b = lambda shape: pl.BlockSpec((None,) + shape, lambda b, c: (b,) + (0,) * len(shape))
    pad = lambda v: jnp.pad(v.astype(F32), (0, LANE - v.shape[0])).reshape(1, LANE)
    cdim = SSM_CONV_DIM
    state = (SSM_HEADS, SSM_HEAD_DIM, SSM_STATE)
    return pl.pallas_call(
        _ssd_body,
        grid=(bsz, nc),
        in_specs=[tok(cdim), tok(SSM_INNER), tok(LANE), per_b((CONV_K - 1, cdim)), per_b(state),
                  _const_spec((CONV_K, cdim)), _const_spec((1, cdim)), _const_spec((1, LANE)), _const_spec((1, LANE)),
                  _const_spec((1, SSM_INNER)), _const_spec((1, SSM_INNER))],
        out_specs=[tok(SSM_INNER), per_b((CONV_K - 1, cdim)), per_b(state)],
        out_shape=[jax.ShapeDtypeStruct((bsz * t, SSM_INNER), F32),
                   jax.ShapeDtypeStruct((bsz, CONV_K - 1, cdim), F32),
                   jax.ShapeDtypeStruct((bsz,) + state, F32)],
        scratch_shapes=[pltpu.VMEM((chunk + 8, cdim), F32)],
        compiler_params=_cparams("arbitrary", "arbitrary"),
        name="ssd_mixer",
    )(xbc, z, small, conv_buf, h0, lp['ssm_conv_w'], lp['ssm_conv_b'].reshape(1, cdim), pad(lp['ssm_dt_bias']),
      pad(lp['ssm_a_log']), jnp.repeat(lp['ssm_d'].astype(F32), SSM_HEAD_DIM).reshape(1, SSM_INNER),
      lp['ssm_norm_w'].reshape(1, SSM_INNER))


def _split_bf16(x):
    hi = x.astype(BF16)
    return hi, (x - hi.astype(F32)).astype(BF16)


def _bmm(a, b):
    return jnp.einsum('hmk,hkn->hmn', a, b, preferred_element_type=F32)


def _bmm_nt(a, b):
    return jnp.einsum('hmk,hnk->hmn', a, b, preferred_element_type=F32)


def _bmm_tn(a, b):
    return jnp.einsum('hkm,hkn->hmn', a, b, preferred_element_type=F32)


def _bmm_split(a, b):
    a_hi, a_lo = _split_bf16(a)
    b_hi, b_lo = _split_bf16(b)
    return _bmm(a_hi, b_hi) + _bmm(a_hi, b_lo) + _bmm(a_lo, b_hi)


def _unit_lower_inverse(a):
    n = a.shape[-1]
    eye = jnp.where(lax.broadcasted_iota(jnp.int32, (n, n), 0) == lax.broadcasted_iota(jnp.int32, (n, n), 1), 1.0, 0.0)
    p = -a
    t = eye + p
    span = 2
    while span < n:
        p = _bmm_split(p, p)
        t = t + _bmm_split(t, p)
        span *= 2
    return t


def _gdn_body(qkv_ref, z_ref, small_ref, buf_ref, s0_ref, cw_ref, alog_ref, dtb_ref, nw_ref,
              y_ref, conv_ref, s_ref, xp_ref):
    c = pl.program_id(1)
    n = qkv_ref.shape[0]
    dk = GDN_HEAD_DIM
    nh = GDN_HEADS

    @pl.when(c == 0)
    def _():
        s_ref[...] = s0_ref[...]

    act = _conv_silu(qkv_ref, buf_ref, cw_ref, None, xp_ref, conv_ref, c == 0)
    lane = lax.broadcasted_iota(jnp.int32, (n, LANE), 1)
    raw = small_ref[...]
    beta = jax.nn.sigmoid(raw)
    g = jnp.where((lane >= nh) & (lane < 2 * nh), -jnp.exp(alog_ref[...]) * _softplus(raw + dtb_ref[...]), 0.0)
    gc = _cumsum_rows(g)
    gc_t = gc.T
    beta_x = _expand_heads(beta, dk)
    egc_x = _expand_heads(jnp.exp(gc), dk, nh)
    edec_x = _expand_heads(jnp.exp(gc[n - 1:n, :] - gc), dk, nh)
    ii = lax.broadcasted_iota(jnp.int32, (n, n), 0)
    jj = lax.broadcasted_iota(jnp.int32, (n, n), 1)
    heads = lambda x, base=0: jnp.stack([x[:, base + h * dk:base + (h + 1) * dk] for h in range(nh)])
    q, k, v = heads(act), heads(act, nh * dk), heads(act, 2 * nh * dk)
    q = q * lax.rsqrt(jnp.sum(q * q, axis=-1, keepdims=True) + 1e-6) * (dk ** -0.5)
    k = k * lax.rsqrt(jnp.sum(k * k, axis=-1, keepdims=True) + 1e-6)
    beta3, egc, edec = heads(beta_x), heads(egc_x), heads(edec_x)
    kb = k * beta3
    decay = jnp.stack([_decay_matrix(gc[:, nh + h:nh + h + 1], gc_t[nh + h:nh + h + 1, :]) for h in range(nh)])
    k16 = k.astype(BF16)
    amat = jnp.where(ii > jj, _bmm_nt(kb.astype(BF16), k16) * decay, 0.0)
    tmat = _unit_lower_inverse(amat).astype(BF16)
    u = _bmm(tmat, (v * beta3).astype(BF16))
    w = _bmm(tmat, (kb * egc).astype(BF16))
    qk = _bmm_nt(q.astype(BF16), k16) * decay
    s = s_ref[...]
    s16 = s.astype(BF16)
    v16 = (u - _bmm(w.astype(BF16), s16)).astype(BF16)
    o = _bmm((q * egc).astype(BF16), s16) + _bmm(qk.astype(BF16), v16)
    gl = jnp.stack([jnp.exp(gc_t[nh + h:nh + h + 1, n - 1:n]) for h in range(nh)])
    s_ref[...] = s * gl + _bmm_tn((k * edec).astype(BF16), v16)
    y = _rms(o, nw_ref[...])
    for h in range(nh):
        sl = slice(h * dk, (h + 1) * dk)
        y_ref[:, sl] = y[h] * _silu(z_ref[:, sl])


def _gdn_mixer(qkv, z, small, conv_buf, s0, lp, bsz, t, chunk):
    nc = t // chunk
    tok = lambda n: pl.BlockSpec((chunk, n), lambda b, c: (b * nc + c, 0))
    per_b = lambda shape: pl.BlockSpec((None,) + shape, lambda b, c: (b,) + (0,) * len(shape))
    pad8 = lambda v: jnp.pad(v.astype(F32), (GDN_HEADS, LANE - 2 * GDN_HEADS)).reshape(1, LANE)
    cdim = 3 * GDN_WIDTH
    state = (GDN_HEADS, GDN_HEAD_DIM, GDN_HEAD_DIM)
    return pl.pallas_call(
        _gdn_body,
        grid=(bsz, nc),
        in_specs=[tok(cdim), tok(GDN_WIDTH), tok(LANE), per_b((CONV_K - 1, cdim)), per_b(state),
                  _const_spec((CONV_K, cdim)), _const_spec((1, LANE)), _const_spec((1, LANE)),
                  _const_spec((1, GDN_HEAD_DIM))],
        out_specs=[tok(GDN_WIDTH), per_b((CONV_K - 1, cdim)), per_b(state)],
        out_shape=[jax.ShapeDtypeStruct((bsz * t, GDN_WIDTH), F32),
                   jax.ShapeDtypeStruct((bsz, CONV_K - 1, cdim), F32),
                   jax.ShapeDtypeStruct((bsz,) + state, F32)],
        scratch_shapes=[pltpu.VMEM((chunk + 8, cdim), F32)],
        compiler_params=_cparams("arbitrary", "arbitrary"),
        name="gdn_mixer",
    )(qkv, z, small, conv_buf, s0, lp['gdn_conv_w'], pad8(lp['gdn_a_log']), pad8(lp['gdn_dt_bias']),
      lp['gdn_norm_w'].reshape(1, GDN_HEAD_DIM))


def _softmax_pv(s, valid, v16, n_rep=1):
    rows, nk = s.shape
    s = jnp.where(valid[None], s.reshape(n_rep, rows // n_rep, nk), NEG).reshape(rows, nk)
    m = jnp.max(s, axis=-1, keepdims=True)
    p = jnp.exp(s - m)
    l = jnp.sum(p, axis=-1, keepdims=True)
    return jnp.dot((p / l).astype(BF16), v16, preferred_element_type=F32), m + jnp.log(l)


_DIL_PAIRS = DIL_HEADS_PER_GROUP // 2


def _dil_prompt_body(q_ref, kp_ref, kc_ref, vp_ref, vc_ref, o_ref, lse_ref, *, window, step):
    i = pl.program_id(1)
    qb = q_ref.shape[1] // step
    scale = HEAD_DIM ** -0.5
    qpos = i * qb + lax.broadcasted_iota(jnp.int32, (qb, 2 * qb), 0)
    kpos = (i - 1) * qb + lax.broadcasted_iota(jnp.int32, (qb, 2 * qb), 1)
    diff = jnp.where(kpos >= 0, qpos - kpos, -1)
    valid = jnp.where(diff >= 0, diff, window + 1) <= window
    lower = _half_mask((qb, LANE), 0)

    def one_class(c, carry):
        rows = pl.ds(c, qb, stride=step)
        for pr in range(q_ref.shape[0]):
            q = q_ref[pr, rows, :] * scale
            kk = jnp.concatenate([kp_ref[pr, rows, :], kc_ref[pr, rows, :]], axis=0).astype(BF16)
            vv = jnp.concatenate([vp_ref[pr, rows, :], vc_ref[pr, rows, :]], axis=0).astype(BF16)
            o2, l2 = [], []
            for k in range(2):
                qh = jnp.where(_half_mask((qb, LANE), k), q, 0.0).astype(BF16)
                s = lax.dot_general(qh, kk, _NT, preferred_element_type=F32)
                o, lse = _softmax_pv(s, valid, vv)
                o2.append(o)
                l2.append(jnp.broadcast_to(lse, (qb, LANE)))
            o_ref[pr, rows, :] = jnp.where(lower, o2[0], o2[1])
            lse_ref[pr, rows, :] = jnp.where(lower, l2[0], l2[1])
        return carry

    lax.fori_loop(0, step, one_class, 0)


def _dil_prompt_attn(dil, gi, bsz, t):
    win, step = DIL_PATTERNS[gi]
    assert win % step == 0 and t % win == 0
    tile = win
    nq = t // tile
    pp = _DIL_PAIRS if _DIL_PAIRS * tile * LANE * 4 <= (1 << 20) else 1
    blk = lambda part, prev: pl.BlockSpec(
        (pp, tile, LANE),
        lambda b, i, h: ((part * DIL_GROUPS + gi) * (_DIL_PAIRS // pp) + h,
                         b * nq + (jnp.maximum(i - 1, 0) if prev else i), 0))
    out_spec = pl.BlockSpec((pp, tile, LANE), lambda b, i, h: (h, b * nq + i, 0))
    out_sds = jax.ShapeDtypeStruct((_DIL_PAIRS, bsz * t, LANE), F32)
    return pl.pallas_call(
        functools.partial(_dil_prompt_body, window=win // step, step=step),
        grid=(bsz, nq, _DIL_PAIRS // pp),
        in_specs=[blk(0, False), blk(1, True), blk(1, False), blk(2, True), blk(2, False)],
        out_specs=[out_spec, out_spec],
        out_shape=[out_sds, out_sds],
        compiler_params=_cparams("arbitrary", "arbitrary", "arbitrary"),
        name=f"dil_prompt_attn_{gi}",
    )(dil, dil, dil, dil, dil)


def _dil_combine_body(o0, o1, o2, l0, l1, l2, y_ref):
    for pr in range(_DIL_PAIRS):
        m = jnp.maximum(jnp.maximum(l0[pr], l1[pr]), l2[pr])
        e0, e1, e2 = jnp.exp(l0[pr] - m), jnp.exp(l1[pr] - m), jnp.exp(l2[pr] - m)
        den = e0 + e1 + e2
        y_ref[:, pr * LANE:(pr + 1) * LANE] = (e0 / den) * o0[pr] + (e1 / den) * o1[pr] + (e2 / den) * o2[pr]


def _dil_combine(outs, lses, tm):
    n_tok = outs[0].shape[1]
    spec = pl.BlockSpec((_DIL_PAIRS, tm, LANE), lambda i: (0, i, 0))
    return pl.pallas_call(
        _dil_combine_body,
        grid=(n_tok // tm,),
        in_specs=[spec] * 6,
        out_specs=pl.BlockSpec((tm, DIL_WIDTH), lambda i: (i, 0)),
        out_shape=jax.ShapeDtypeStruct((n_tok, DIL_WIDTH), F32),
        compiler_params=_cparams("arbitrary"),
        name="dil_combine",
    )(*outs, *lses)


def _attend_cached(qg, k_t, v_t, k_n, v_n, valid_c, valid_n, n_rep=1):
    rows = qg.shape[0]
    mask = lambda s, v: jnp.where(v[None], s.reshape(n_rep, rows // n_rep, s.shape[1]), NEG).reshape(rows, s.shape[1])
    s_c = mask(jnp.dot(qg, k_t, preferred_element_type=F32), valid_c)
    s_n = mask(lax.dot_general(qg, k_n, _NT, preferred_element_type=F32), valid_n)
    m = jnp.maximum(jnp.max(s_c, axis=-1, keepdims=True), jnp.max(s_n, axis=-1, keepdims=True))
    p_c = jnp.exp(s_c - m)
    p_n = jnp.exp(s_n - m)
    l = jnp.sum(p_c, axis=-1, keepdims=True) + jnp.sum(p_n, axis=-1, keepdims=True)
    o = lax.dot_general((p_c / l).astype(BF16), v_t, _NT, preferred_element_type=F32)
    o = o + jnp.dot((p_n / l).astype(BF16), v_n, preferred_element_type=F32)
    return o, m + jnp.log(l)


def _nsa_decode_body(pt_ref, q_ref, small_ref, rows_ref, win_ref, *refs, n_pages, lw, nb, nb_pad, n_sel):
    del pt_ref, nb
    pages = refs[:n_pages]
    (pastwin_ref, pek_ref, pev_ref, wk1_ref, wk2_ref, wv1_ref, wv2_ref, o_ref, neww_ref,
     kc_s, vc_s, comb_s) = refs[n_pages:]
    t = q_ref.shape[0]
    rep = NSA_REP
    past_len = n_pages * PAGE_SIZE
    for dst, c in ((kc_s, 0), (vc_s, 1)):
        for p in range(n_pages):
            dst[p * PAGE_SIZE:(p + 1) * PAGE_SIZE, :] = pages[p][c * LANE:(c + 1) * LANE, :].T
        dst[past_len:past_len + t, :] = rows_ref[:, c * LANE:(c + 1) * LANE]
        dst[past_len + t:, :] = jnp.zeros((dst.shape[0] - past_len - t, LANE), F32)
    kcmp = _compress_rows(kc_s, pek_ref, wk1_ref, wk2_ref, nb_pad).astype(BF16)
    vcmp = _compress_rows(vc_s, pev_ref, wv1_ref, wv2_ref, nb_pad).astype(BF16)
    page_rows = lambda c: jnp.concatenate([pages[p][c * LANE:(c + 1) * LANE, :] for p in range(n_pages)],
                                          axis=1).astype(BF16)
    ks_t, vs_t = page_rows(2), page_rows(3)
    ks_n, vs_n = rows_ref[:, 2 * LANE:3 * LANE].astype(BF16), rows_ref[:, 3 * LANE:].astype(BF16)
    kw_t, vw_t = pastwin_ref[:LANE, :].astype(BF16), pastwin_ref[LANE:, :].astype(BF16)
    kw_n, vw_n = win_ref[:, :LANE].astype(BF16), win_ref[:, LANE:].astype(BF16)
    neww_ref[...] = jnp.concatenate([pastwin_ref[:, t:], win_ref[...].T], axis=1)
    pos_c = past_len + lax.broadcasted_iota(jnp.int32, (t, 1), 0)
    new_i = lax.broadcasted_iota(jnp.int32, (t, t), 1)
    back = lax.broadcasted_iota(jnp.int32, (t, t), 0) - new_i
    sig = jax.nn.sigmoid(small_ref[...])
    q = q_ref[...] * (HEAD_DIM ** -0.5)
    for g in range(NSA_KV_HEADS):
        qg = _stack_heads(q, range(g * rep, (g + 1) * rep), g).astype(BF16)
        gate_lane = lambda j: [SSM_HEADS + j * NSA_HEADS + g * rep + r for r in range(rep)]
        s = lax.dot_general(qg, kcmp, _NT, preferred_element_type=F32).reshape(rep, t, nb_pad)
        n_i = lax.broadcasted_iota(jnp.int32, (t, nb_pad), 1)
        complete = (n_i + 1) * NSA_BLOCK <= pos_c + 1
        s = jnp.where(complete[None], s, NEG)
        p = jnp.exp(s - jnp.max(s, axis=-1, keepdims=True))
        p = p / jnp.sum(p, axis=-1, keepdims=True)
        p = jnp.where((pos_c >= NSA_BLOCK - 1)[None], p, 0.0)
        o_cmp = jnp.dot(p.reshape(rep * t, nb_pad).astype(BF16), vcmp, preferred_element_type=F32)
        comb = _row_gate(sig, gate_lane(0), t) * o_cmp
        sel = _nsa_select(p, pos_c, nb_pad, n_sel).astype(BF16)
        expand = lambda n_keys, first: jnp.where(
            lax.broadcasted_iota(jnp.int32, (nb_pad, n_keys), 0)
            == (first + lax.broadcasted_iota(jnp.int32, (nb_pad, n_keys), 1)) // NSA_BLOCK, 1.0, 0.0).astype(BF16)
        chosen_c = jnp.dot(sel, expand(past_len, 0), preferred_element_type=F32)
        chosen_n = jnp.dot(sel, expand(t, past_len), preferred_element_type=F32)
        kpos = lax.broadcasted_iota(jnp.int32, (t, past_len), 1)
        valid_c = jnp.where(kpos <= pos_c, chosen_c, 0.0) > 0.5
        valid_n = jnp.where(back >= 0, chosen_n, 0.0) > 0.5
        o_sel, _ = _attend_cached(qg, ks_t, vs_t, ks_n, vs_n, valid_c, valid_n, rep)
        comb = comb + _row_gate(sig, gate_lane(1), t) * o_sel
        diff = pos_c - (past_len - lw + lax.broadcasted_iota(jnp.int32, (t, lw), 1))
        valid_c = jnp.where(diff >= 0, diff, NSA_WINDOW + 1) <= NSA_WINDOW
        valid_n = jnp.where(back >= 0, back, NSA_WINDOW + 1) <= NSA_WINDOW
        o_win, _ = _attend_cached(qg, kw_t, vw_t, kw_n, vw_n, valid_c, valid_n, rep)
        comb_s[...] = comb + _row_gate(sig, gate_lane(2), t) * o_win
        for c in range(rep // 2):
            a = comb_s[(2 * c) * t:(2 * c + 1) * t, :]
            b = comb_s[(2 * c + 1) * t:(2 * c + 2) * t, :]
            col = (g * rep // 2 + c) * LANE
            o_ref[:, col:col + LANE] = _unstack_pair(a, b, g)


def _nsa_decode_attn(q, small, rows, win, cache_kv, cache_win, page_table, lp, bsz, t):
    n_pages = page_table.shape[1]
    past_len = n_pages * PAGE_SIZE
    lw = cache_win.shape[1]
    nb = -(-(past_len + t) // NSA_BLOCK)
    nb_pad = -(-nb // 8) * 8
    pages = jnp.transpose(cache_kv, (0, 2, 3, 4, 1)).reshape(cache_kv.shape[0], 4 * LANE, PAGE_SIZE)
    pastwin = jnp.transpose(cache_win, (0, 2, 3, 4, 1)).reshape(bsz, 2 * LANE, lw)
    w1 = lambda w: w.reshape(NSA_BLOCK, HEAD_DIM, NSA_CMP_HIDDEN).astype(BF16)
    tok = lambda n: pl.BlockSpec((t, n), lambda b, pt: (b, 0))
    const = lambda shape: pl.BlockSpec(shape, lambda b, pt: (0,) * len(shape), pipeline_mode=pl.Buffered(1))
    page_spec = lambda p: pl.BlockSpec((None, 4 * LANE, PAGE_SIZE), lambda b, pt: (pt[b * n_pages + p], 0, 0))
    win_spec = pl.BlockSpec((None, 2 * LANE, lw), lambda b, pt: (b, 0, 0))
    grid_spec = pltpu.PrefetchScalarGridSpec(
        num_scalar_prefetch=1,
        grid=(bsz,),
        in_specs=[tok(NSA_HEADS * HEAD_DIM), tok(LANE), tok(4 * LANE), tok(2 * LANE)]
        + [page_spec(p) for p in range(n_pages)]
        + [win_spec, const((NSA_BLOCK, LANE)), const((NSA_BLOCK, LANE)),
           const((NSA_BLOCK, HEAD_DIM, NSA_CMP_HIDDEN)), const((NSA_CMP_HIDDEN, HEAD_DIM)),
           const((NSA_BLOCK, HEAD_DIM, NSA_CMP_HIDDEN)), const((NSA_CMP_HIDDEN, HEAD_DIM))],
        out_specs=[tok(NSA_HEADS * HEAD_DIM), win_spec],
        scratch_shapes=[pltpu.VMEM((nb_pad * NSA_BLOCK, LANE), F32), pltpu.VMEM((nb_pad * NSA_BLOCK, LANE), F32),
                        pltpu.VMEM((NSA_REP * t, LANE), F32)],
    )
    y, new_win = pl.pallas_call(
        functools.partial(_nsa_decode_body, n_pages=n_pages, lw=lw, nb=nb, nb_pad=nb_pad, n_sel=min(NSA_TOPK, nb)),
        grid_spec=grid_spec,
        out_shape=[jax.ShapeDtypeStruct((bsz * t, NSA_HEADS * HEAD_DIM), F32),
                   jax.ShapeDtypeStruct((bsz, 2 * LANE, lw), F32)],
        compiler_params=_cparams("arbitrary"),
        name="nsa_decode_attn",
    )(page_table.reshape(-1), q, small, rows, win, *([pages] * n_pages), pastwin,
      _pe2(lp['nsa_pe_k']), _pe2(lp['nsa_pe_v']), w1(lp['nsa_ck_w1']), lp['nsa_ck_w2'].astype(BF16),
      w1(lp['nsa_cv_w1']), lp['nsa_cv_w2'].astype(BF16))
    new_win = jnp.transpose(new_win.reshape(bsz, 2, NSA_KV_HEADS, HEAD_DIM, lw), (0, 4, 1, 2, 3))
    return y, new_win


def _dil_decode_group(q, k_new, v_new, cache_ref, step):
    t = q.shape[0]
    lg = cache_ref.shape[1]
    pairs = DIL_HEADS_PER_GROUP // 2
    tok_c = lax.broadcasted_iota(jnp.int32, (2 * t, lg), 0) % t
    ahead = lax.broadcasted_iota(jnp.int32, (2 * t, lg), 1) - tok_c
    valid_c = jnp.where(ahead >= 0, ahead % step, 1) == 0
    tok_n = lax.broadcasted_iota(jnp.int32, (2 * t, t), 0) % t
    back = tok_n - lax.broadcasted_iota(jnp.int32, (2 * t, t), 1)
    valid_n = jnp.where(back >= 0, back % step, 1) == 0
    lower = _half_mask((t, LANE), 0)
    outs, lses = [], []
    for pr in range(pairs):
        sl = slice(pr * LANE, (pr + 1) * LANE)
        k_t = cache_ref[pr * LANE:(pr + 1) * LANE, :].astype(BF16)
        v_t = cache_ref[DIL_WIDTH + pr * LANE:DIL_WIDTH + (pr + 1) * LANE, :].astype(BF16)
        kn = k_new[:, sl].astype(BF16)
        vn = v_new[:, sl].astype(BF16)
        qp = q[:, sl]
        q2 = jnp.concatenate([jnp.where(lower, qp, 0.0), jnp.where(lower, 0.0, qp)], axis=0).astype(BF16)
        o, lse = _attend_cached(q2, k_t, v_t, kn, vn, valid_c, valid_n)
        lse = jnp.broadcast_to(lse, (2 * t, LANE))
        outs.append(jnp.where(lower, o[:t], o[t:]))
        lses.append(jnp.where(lower, lse[:t], lse[t:]))
    return jnp.concatenate(outs, axis=1), jnp.concatenate(lses, axis=1)


def _dil_decode_body(x_ref, c0_ref, c1_ref, c2_ref, y_ref, n0_ref, n1_ref, n2_ref):
    scale = HEAD_DIM ** -0.5
    t = x_ref.shape[1]
    res = []
    for gi, (cache_ref, new_ref) in enumerate(((c0_ref, n0_ref), (c1_ref, n1_ref), (c2_ref, n2_ref))):
        _, step = DIL_PATTERNS[gi]
        part = lambda p: jnp.concatenate(
            [x_ref[(p * DIL_GROUPS + gi) * _DIL_PAIRS + pr] for pr in range(_DIL_PAIRS)], axis=1)
        res.append(_dil_decode_group(part(0) * scale, part(1), part(2), cache_ref, step))
        new_t = jnp.concatenate([part(1).T, part(2).T], axis=0)
        new_ref[...] = jnp.concatenate([cache_ref[:, t:], new_t], axis=1)
    (o0, l0), (o1, l1), (o2, l2) = res
    m = jnp.maximum(jnp.maximum(l0, l1), l2)
    e0, e1, e2 = jnp.exp(l0 - m), jnp.exp(l1 - m), jnp.exp(l2 - m)
    den = e0 + e1 + e2
    y_ref[...] = (e0 / den) * o0 + (e1 / den) * o1 + (e2 / den) * o2


def _dil_decode_attn(dil, bufs, bsz, t):
    n_rows = 2 * DIL_WIDTH
    views = []
    for (win, _), buf in zip(DIL_PATTERNS, bufs):
        assert buf.shape[1] == win, "decode path needs a full window of cached rows"
        views.append(jnp.transpose(buf, (0, 2, 3, 4, 1)).reshape(bsz, n_rows, win))
    cache_spec = lambda v: pl.BlockSpec((None, n_rows, v.shape[2]), lambda b: (b, 0, 0))
    tok_spec = lambda n: pl.BlockSpec((t, n), lambda b: (b, 0))
    y, *new = pl.pallas_call(
        _dil_decode_body,
        grid=(bsz,),
        in_specs=[pl.BlockSpec((DIL_IN // LANE, t, LANE), lambda b: (0, b, 0))] + [cache_spec(v) for v in views],
        out_specs=[tok_spec(DIL_WIDTH)] + [cache_spec(v) for v in views],
        out_shape=[jax.ShapeDtypeStruct((bsz * t, DIL_WIDTH), F32)]
        + [jax.ShapeDtypeStruct(v.shape, F32) for v in views],
        compiler_params=_cparams("arbitrary"),
        name="dil_decode_attn",
    )(dil, *views)
    unview = lambda v: jnp.transpose(v.reshape(bsz, 2, DIL_HEADS_PER_GROUP, HEAD_DIM, v.shape[2]), (0, 4, 1, 2, 3))
    return y, [unview(v) for v in new]


_AB_CUTS = (0, SSM_INNER, SSM_INNER + SSM_CONV_DIM, SSM_INNER + SSM_CONV_DIM + SSM_HEADS)
_AB_Q0 = _AB_CUTS[3]
_AB_KV0 = _AB_Q0 + NSA_HEADS * HEAD_DIM
_AB_WIN0 = _AB_KV0 + 4 * NSA_KV_HEADS * HEAD_DIM
_AB_GATE0 = _AB_WIN0 + 2 * NSA_KV_HEADS * HEAD_DIM
_AB_END = _AB_GATE0 + 3 * NSA_HEADS


def _pad_cols(w, n):
    return jnp.pad(w, ((0, 0), (0, n - w.shape[1])))


def _ab_weight_segs(w_in):
    small = jnp.concatenate([w_in[:, _AB_CUTS[2]:_AB_CUTS[3]], w_in[:, _AB_GATE0:_AB_END]], axis=1)
    segs = [w_in[:, _AB_CUTS[0]:_AB_CUTS[1]], w_in[:, _AB_CUTS[1]:_AB_CUTS[2]], w_in[:, _AB_Q0:_AB_KV0],
            w_in[:, _AB_KV0:_AB_WIN0], w_in[:, _AB_WIN0:_AB_GATE0], _pad_cols(small, LANE)]
    return [s.astype(BF16) for s in segs]


_CD_Z0 = 3 * GDN_WIDTH
_CD_B0 = _CD_Z0 + GDN_WIDTH
_CD_DIL0 = _CD_B0 + 2 * GDN_HEADS
_CD_END = _CD_DIL0 + DIL_IN


def _cd_weight_segs(w_in):
    segs = [w_in[:, :_CD_Z0], w_in[:, _CD_Z0:_CD_B0], w_in[:, _CD_DIL0:_CD_END],
            _pad_cols(w_in[:, _CD_B0:_CD_DIL0], LANE)]
    return [s.astype(BF16) for s in segs]


def _layer_ab(y, norm_pre, segs, lp, past, page_table, is_prompt, tm):
    bsz, t, d = y.shape
    z, xbc, q, rows, win, small = _norm_proj(y.reshape(bsz * t, d), norm_pre, segs, tm)
    rows_new = rows.reshape(bsz, t, 4, NSA_KV_HEADS, HEAD_DIM)
    win_new = win.reshape(bsz, t, 2, NSA_KV_HEADS, HEAD_DIM)
    if is_prompt:
        conv_buf = jnp.zeros((bsz, CONV_K - 1, SSM_CONV_DIM), F32)
        h0 = jnp.zeros((bsz, SSM_HEADS, SSM_HEAD_DIM, SSM_STATE), F32)
        chunk = math.gcd(t, SSM_CHUNK)
        rows3 = rows.reshape(bsz, t, 4 * LANE)
        cmp = _nsa_compress_prompt(rows3, lp)
        y_nsa = _nsa_prompt_attn(q, small, cmp, rows3, win.reshape(bsz, t, 2 * LANE), bsz, t)
        new_win = win_new[:, t - min(NSA_WINDOW, t):]
    else:
        conv_buf, h0, chunk = past['ssm_conv'], past['ssm'], t
        y_nsa, new_win = _nsa_decode_attn(q, small, rows, win, past['nsa_kv'], past['nsa_win'], page_table, lp,
                                          bsz, t)
    y_ssm, new_conv, new_ssm = _ssd_mixer(xbc, z, small, conv_buf, h0, lp, bsz, t, chunk)
    return y_ssm, y_nsa, (new_conv, new_ssm, rows_new, new_win)


def _layer_cd(y, norm_pre, segs, lp, past, is_prompt, tm):
    bsz, t, d = y.shape
    qkv, z, dil, small = _norm_proj(y.reshape(bsz * t, d), norm_pre, segs, tm, plane_segs=(2,))
    if is_prompt:
        conv_buf = jnp.zeros((bsz, CONV_K - 1, 3 * GDN_WIDTH), F32)
        s0 = jnp.zeros((bsz, GDN_HEADS, GDN_HEAD_DIM, GDN_HEAD_DIM), F32)
        chunk = math.gcd(t, GDN_CHUNK)
        parts = [_dil_prompt_attn(dil, gi, bsz, t) for gi in range(DIL_GROUPS)]
        y_dil = _dil_combine([p[0] for p in parts], [p[1] for p in parts], min(tm * 2, bsz * t))
        bufs = []
        for gi, (win, _) in enumerate(DIL_PATTERNS):
            w = min(win, t)
            tail = lambda part: dil[(part * DIL_GROUPS + gi) * _DIL_PAIRS:(part * DIL_GROUPS + gi + 1) * _DIL_PAIRS] \
                .reshape(_DIL_PAIRS, bsz, t, 2, HEAD_DIM)[:, :, t - w:]
            kv = jnp.stack([tail(1), tail(2)])
            bufs.append(jnp.transpose(kv, (2, 3, 0, 1, 4, 5)).reshape(bsz, w, 2, DIL_HEADS_PER_GROUP, HEAD_DIM))
    else:
        conv_buf, s0, chunk = past['gdn_conv'], past['gdn'], t
        y_dil, bufs = _dil_decode_attn(dil, past['dil'], bsz, t)
    y_gdn, new_conv, new_gdn = _gdn_mixer(qkv, z, small, conv_buf, s0, lp, bsz, t, chunk)
    return y_gdn, y_dil, (new_conv, new_gdn, bufs[0], bufs[1], bufs[2])


def kernel(x_prompt, x_sample, cache_ssm_conv, state_ssm, cache_nsa_kv, cache_nsa_win_kv, cache_gdn_conv, state_gdn,
           cache_dil0_kv, cache_dil1_kv, cache_dil2_kv, page_table, norm_mix_pre, norm_mix_post, norm_mlp_pre,
           norm_mlp_post, mlp_w1, mlp_w2, ab_w_in, ab_w_out, ssm_conv_w, ssm_conv_b, ssm_dt_bias, ssm_a_log, ssm_d,
           ssm_norm_w, nsa_pe_k, nsa_pe_v, nsa_ck_w1, nsa_ck_w2, nsa_cv_w1, nsa_cv_w2, cd_w_in, cd_w_out, gdn_conv_w,
           gdn_dt_bias, gdn_a_log, gdn_norm_w):
    depth = norm_mix_pre.shape[0]
    yp, ys = x_prompt, x_sample
    ab_p, ab_s, cd_p, cd_s = [], [], [], []
    tm = 256
    for l in range(depth):
        j = l // 2
        if l % 2 == 0:
            lp = {'ssm_conv_w': ssm_conv_w[j], 'ssm_conv_b': ssm_conv_b[j], 'ssm_dt_bias': ssm_dt_bias[j],
                  'ssm_a_log': ssm_a_log[j], 'ssm_d': ssm_d[j], 'ssm_norm_w': ssm_norm_w[j],
                  'nsa_pe_k': nsa_pe_k[j], 'nsa_pe_v': nsa_pe_v[j], 'nsa_ck_w1': nsa_ck_w1[j],
                  'nsa_ck_w2': nsa_ck_w2[j], 'nsa_cv_w1': nsa_cv_w1[j], 'nsa_cv_w2': nsa_cv_w2[j]}
            past = {'ssm_conv': cache_ssm_conv[j], 'ssm': state_ssm[j], 'nsa_kv': cache_nsa_kv[j],
                    'nsa_win': cache_nsa_win_kv[j]}
            segs = _ab_weight_segs(ab_w_in[j])
            w_out = ab_w_out[j].astype(BF16)
            ka = SSM_INNER
            ap, bp, stp = _layer_ab(yp, norm_mix_pre[l], segs, lp, None, None, True, tm)
            as_, bs, sts = _layer_ab(ys, norm_mix_pre[l], segs, lp, past, page_table, False, tm)
            ab_p.append(stp)
            ab_s.append(sts)
        else:
            lp = {'gdn_conv_w': gdn_conv_w[j], 'gdn_dt_bias': gdn_dt_bias[j], 'gdn_a_log': gdn_a_log[j],
                  'gdn_norm_w': gdn_norm_w[j]}
            past = {'gdn_conv': cache_gdn_conv[j], 'gdn': state_gdn[j],
                    'dil': (cache_dil0_kv[j], cache_dil1_kv[j], cache_dil2_kv[j])}
            segs = _cd_weight_segs(cd_w_in[j])
            w_out = cd_w_out[j].astype(BF16)
            ka = GDN_WIDTH
            ap, bp, stp = _layer_cd(yp, norm_mix_pre[l], segs, lp, None, True, tm)
            as_, bs, sts = _layer_cd(ys, norm_mix_pre[l], segs, lp, past, False, tm)
            cd_p.append(stp)
            cd_s.append(sts)
        w1 = mlp_w1[l].astype(BF16)
        w2 = mlp_w2[l].astype(BF16)
        post = functools.partial(_post_block, wo_a=w_out[:ka], wo_b=w_out[ka:], w1=w1, w2=w2, n_mix=norm_mix_post[l],
                                 n_pre=norm_mlp_pre[l], n_post=norm_mlp_post[l], tm=tm)
        yp = post(ap, bp, yp.reshape(-1, D_MODEL)).reshape(yp.shape)
        ys = post(as_, bs, ys.reshape(-1, D_MODEL)).reshape(ys.shape)
    stack = lambda states, i: jnp.stack([s[i] for s in states])
    return (yp, ys,
            stack(ab_p, 0), stack(ab_s, 0), stack(ab_p, 1), stack(ab_s, 1),
            stack(ab_p, 2), stack(ab_s, 2), stack(ab_p, 3), stack(ab_s, 3),
            stack(cd_p, 0), stack(cd_s, 0), stack(cd_p, 1), stack(cd_s, 1),
            stack(cd_p, 2), stack(cd_s, 2), stack(cd_p, 3), stack(cd_s, 3),
            stack(cd_p, 4), stack(cd_s, 4))
```

```python
import functools
import math

import jax
import jax.numpy as jnp
from jax import lax
from jax.experimental import pallas as pl
from jax.experimental.pallas import tpu as pltpu

F32 = jnp.float32
BF16 = jnp.bfloat16

D_MODEL = 1024
HEAD_DIM = 64
CONV_K = 4
RMS_EPS = 1e-6
MLP_HIDDEN = 4 * D_MODEL
QUERY_BLOCK = 128
PAGE_SIZE = 128

SSM_HEADS = 16
SSM_HEAD_DIM = 64
SSM_INNER = SSM_HEADS * SSM_HEAD_DIM
SSM_GROUPS = 2
SSM_STATE = 128
SSM_CONV_DIM = SSM_INNER + 2 * SSM_GROUPS * SSM_STATE
SSM_CHUNK = 128

NSA_HEADS = 16
NSA_KV_HEADS = 2
NSA_REP = NSA_HEADS // NSA_KV_HEADS
NSA_BLOCK = 64
NSA_TOPK = 16
NSA_WINDOW = 512
NSA_CMP_HIDDEN = 256

GDN_HEADS = 8
GDN_HEAD_DIM = 128
GDN_WIDTH = GDN_HEADS * GDN_HEAD_DIM
GDN_CHUNK = 64

DIL_PATTERNS = ((128, 1), (512, 4), (2048, 16))
DIL_GROUPS = len(DIL_PATTERNS)
DIL_HEADS_PER_GROUP = 8
DIL_WIDTH = DIL_HEADS_PER_GROUP * HEAD_DIM
DIL_IN = 3 * DIL_GROUPS * DIL_WIDTH
NEG = -1e30

VMEM_LIMIT_BYTES = 56 * 1024 * 1024
LANE = 128


def _cparams(*sem):
    return pltpu.CompilerParams(dimension_semantics=sem, vmem_limit_bytes=VMEM_LIMIT_BYTES)


def _const_spec(shape):
    nd = len(shape)
    return pl.BlockSpec(shape, lambda *_: (0,) * nd, pipeline_mode=pl.Buffered(1))


def _rms(x, w):
    return x * lax.rsqrt(jnp.mean(x * x, axis=-1, keepdims=True) + RMS_EPS) * w


def _proj_body(x_ref, nw_ref, *refs, n_t):
    n = len(refs) // 2
    xn = _rms(x_ref[...], nw_ref[...]).astype(BF16)
    for k, (w_ref, o_ref) in enumerate(zip(refs[:n], refs[n:])):
        if k >= n - n_t:
            o_ref[...] = lax.dot_general(w_ref[...], xn, _NT, preferred_element_type=F32)
            continue
        y = jnp.dot(xn, w_ref[...], preferred_element_type=F32)
        if len(o_ref.shape) == 2:
            o_ref[...] = y
        else:
            for j in range(o_ref.shape[0]):
                o_ref[j] = y[:, j * LANE:(j + 1) * LANE]


def _norm_proj(x, norm_w, w_segs, tm, plane_segs=(), t_segs=(), seq_len=None):
    n_tok, d = x.shape
    tm = min(tm, n_tok)
    widths = [w.shape[1] for w in w_segs]
    planes = [k in plane_segs for k in range(len(w_segs))]
    out_spec = lambda n, p: (pl.BlockSpec((n // LANE, tm, LANE), lambda i: (0, i, 0)) if p
                             else pl.BlockSpec((tm, n), lambda i: (i, 0)))
    out_sds = lambda n, p: jax.ShapeDtypeStruct((n // LANE, n_tok, LANE) if p else (n_tok, n), F32)
    per_seq = seq_len // tm if t_segs else 1
    t_spec = lambda w: pl.BlockSpec((None, w.shape[0], tm), lambda i: (i // per_seq, 0, i % per_seq))
    t_sds = lambda w: jax.ShapeDtypeStruct((n_tok // seq_len, w.shape[0], seq_len), F32)
    return pl.pallas_call(
        functools.partial(_proj_body, n_t=len(t_segs)),
        grid=(n_tok // tm,),
        in_specs=[pl.BlockSpec((tm, d), lambda i: (i, 0)), _const_spec((1, d))]
        + [_const_spec((d, n)) for n in widths] + [_const_spec(w.shape) for w in t_segs],
        out_specs=[out_spec(n, p) for n, p in zip(widths, planes)] + [t_spec(w) for w in t_segs],
        out_shape=[out_sds(n, p) for n, p in zip(widths, planes)] + [t_sds(w) for w in t_segs],
        compiler_params=_cparams("arbitrary"),
        name="norm_proj",
    )(x, norm_w.reshape(1, d), *w_segs, *t_segs)


def _post_body(a_ref, b_ref, y_ref, woa_ref, wob_ref, w1_ref, w2_ref, nmix_ref, npre_ref, npost_ref, o_ref):
    m = jnp.dot(a_ref[...].astype(BF16), woa_ref[...], preferred_element_type=F32)
    m = m + jnp.dot(b_ref[...].astype(BF16), wob_ref[...], preferred_element_type=F32)
    y1 = y_ref[...] + _rms(m, nmix_ref[...])
    h = _rms(y1, npre_ref[...]).astype(BF16)
    a = jnp.maximum(jnp.dot(h, w1_ref[...], preferred_element_type=F32), 0.0)
    m2 = jnp.dot((a * a).astype(BF16), w2_ref[...], preferred_element_type=F32)
    o_ref[...] = y1 + _rms(m2, npost_ref[...])


def _post_block(mix_a, mix_b, y, wo_a, wo_b, w1, w2, n_mix, n_pre, n_post, tm):
    n_tok, d = y.shape
    tm = min(tm, n_tok)
    ka, kb = mix_a.shape[1], mix_b.shape[1]
    row = lambda n: pl.BlockSpec((tm, n), lambda i: (i, 0))
    return pl.pallas_call(
        _post_body,
        grid=(n_tok // tm,),
        in_specs=[row(ka), row(kb), row(d), _const_spec(wo_a.shape), _const_spec(wo_b.shape),
                  _const_spec(w1.shape), _const_spec(w2.shape), _const_spec((1, d)), _const_spec((1, d)),
                  _const_spec((1, d))],
        out_specs=row(d),
        out_shape=jax.ShapeDtypeStruct((n_tok, d), F32),
        compiler_params=_cparams("arbitrary"),
        name="post_block",
    )(mix_a, mix_b, y, wo_a, wo_b, w1, w2, n_mix.reshape(1, d), n_pre.reshape(1, d), n_post.reshape(1, d))


def _compress_rows(src_ref, pe_ref, w1_ref, w2_ref, nb):
    hd = HEAD_DIM

    def body(r, acc):
        x = src_ref[pl.ds(r, nb, stride=NSA_BLOCK), :] + pe_ref[pl.ds(r, 1), :]
        xg = jnp.concatenate([x[:, :hd], x[:, hd:]], axis=0).astype(BF16)
        return acc + jnp.dot(xg, w1_ref[r], preferred_element_type=F32)

    hid = lax.fori_loop(0, NSA_BLOCK, body, jnp.zeros((2 * nb, NSA_CMP_HIDDEN), F32), unroll=8)
    hid = hid * jax.nn.sigmoid(hid)
    out = jnp.dot(hid.astype(BF16), w2_ref[...], preferred_element_type=F32)
    return jnp.concatenate([out[:nb], out[nb:]], axis=1)


def _compress_body(kc_ref, vc_ref, pek_ref, pev_ref, wk1_ref, wk2_ref, wv1_ref, wv2_ref, o_ref, *, nb):
    o_ref[:, :LANE] = _compress_rows(kc_ref, pek_ref, wk1_ref, wk2_ref, nb)
    o_ref[:, LANE:] = _compress_rows(vc_ref, pev_ref, wv1_ref, wv2_ref, nb)


def _pe2(pe):
    return jnp.concatenate([pe, pe], axis=1)


def _nsa_compress_prompt(rows, lp):
    bsz, t, _ = rows.shape
    nb = t // NSA_BLOCK
    w1 = lambda w: w.reshape(NSA_BLOCK, HEAD_DIM, NSA_CMP_HIDDEN).astype(BF16)
    col = lambda c: pl.BlockSpec((None, t, LANE), lambda b: (b, 0, c))
    return pl.pallas_call(
        functools.partial(_compress_body, nb=nb),
        grid=(bsz,),
        in_specs=[col(0), col(1), _const_spec((NSA_BLOCK, LANE)), _const_spec((NSA_BLOCK, LANE)),
                  _const_spec((NSA_BLOCK, HEAD_DIM, NSA_CMP_HIDDEN)), _const_spec((NSA_CMP_HIDDEN, HEAD_DIM)),
                  _const_spec((NSA_BLOCK, HEAD_DIM, NSA_CMP_HIDDEN)), _const_spec((NSA_CMP_HIDDEN, HEAD_DIM))],
        out_specs=pl.BlockSpec((None, nb, 2 * LANE), lambda b: (b, 0, 0)),
        out_shape=jax.ShapeDtypeStruct((bsz, nb, 2 * LANE), F32),
        compiler_params=_cparams("arbitrary"),
        name="nsa_compress",
    )(rows, rows, _pe2(lp['nsa_pe_k']), _pe2(lp['nsa_pe_v']), w1(lp['nsa_ck_w1']), lp['nsa_ck_w2'].astype(BF16),
      w1(lp['nsa_cv_w1']), lp['nsa_cv_w2'].astype(BF16))


_NT = (((1,), (1,)), ((), ()))


def _half_mask(shape, g):
    lane = lax.broadcasted_iota(jnp.int32, shape, len(shape) - 1)
    return lane >= HEAD_DIM if g else lane < HEAD_DIM


def _stack_heads(x, heads, g):
    keep = _half_mask((x.shape[0], LANE), g)
    out = []
    for h in heads:
        blk = x[:, (h // 2) * LANE:(h // 2 + 1) * LANE]
        if h % 2 != g:
            blk = pltpu.roll(blk, HEAD_DIM, axis=1)
        out.append(jnp.where(keep, blk, 0.0))
    return jnp.concatenate(out, axis=0)


def _unstack_pair(a, b, g):
    if g == 0:
        b = pltpu.roll(b, HEAD_DIM, axis=1)
    else:
        a = pltpu.roll(a, HEAD_DIM, axis=1)
    return jnp.where(_half_mask(a.shape, 1), b, a)


def _row_gate(sig, lanes, rows):
    return jnp.concatenate([jnp.broadcast_to(sig[:, c:c + 1], (rows, LANE)) for c in lanes], axis=0)


def _nsa_select(p3, pos, nb, n_sel):
    nq = p3.shape[1]
    imp = jnp.sum(p3, axis=0)
    n_i = lax.broadcasted_iota(jnp.int32, (nq, nb), 1)
    cur = pos // NSA_BLOCK
    score = jnp.where(n_i == 0, NSA_REP + 1.0, imp)
    score = jnp.where(n_i == cur, NSA_REP + 1.0, score)
    score = jnp.where(n_i == cur - 1, NSA_REP + 1.0, score)
    score = jnp.where(n_i > cur, -1.0, score)
    rank = jnp.zeros((nq, nb), F32)
    for m in range(nb):
        col = score[:, m:m + 1]
        tie = jnp.where(n_i > m, 1.0, 0.0)
        rank = rank + jnp.where(col > score, 1.0, jnp.where(col == score, tie, 0.0))
    return jnp.where(rank < n_sel, 1.0, 0.0)


def _nsa_select_t(imp, pos, n_sel):
    nb, nq = imp.shape
    n_i = lax.broadcasted_iota(jnp.int32, (nb, nq), 0)
    cur = pos // NSA_BLOCK
    score = jnp.where(n_i == 0, NSA_REP + 1.0, imp)
    score = jnp.where(n_i == cur, NSA_REP + 1.0, score)
    score = jnp.where(n_i == cur - 1, NSA_REP + 1.0, score)
    score = jnp.where(n_i > cur, -1.0, score)
    rank = jnp.zeros((nb, nq), F32)
    for m in range(nb):
        row = score[m:m + 1, :]
        tie = jnp.where(n_i > m, 1.0, 0.0)
        rank = rank + jnp.where(row > score, 1.0, jnp.where(row == score, tie, 0.0))
    return jnp.where(rank < n_sel, 1.0, 0.0)


def _lane_tile(x, n):
    return jnp.concatenate([x] * n, axis=1)


_V_ROWS = HEAD_DIM + 16
_SEL_TILE = 4
_WIN_TILE = 2


def _flash_tile_t(kt, vt_aug, q_t, valid, m_ref, acc_ref, g):
    nk, nq = valid.shape
    rep = q_t.shape[1] // nq
    s = jnp.dot(kt, q_t, preferred_element_type=F32)
    s = jnp.concatenate([jnp.where(valid, s[:, r * nq:(r + 1) * nq], NEG) for r in range(rep)], axis=1)
    m_old = m_ref[g, 0:1, :]
    m_new = jnp.maximum(m_old, jnp.max(s, axis=0, keepdims=True))
    alpha = jnp.exp2(m_old - m_new)
    p = jnp.exp2(s - m_new)
    acc_ref[g] = alpha * acc_ref[g] + jnp.dot(vt_aug, p.astype(BF16), preferred_element_type=F32)
    m_ref[g] = jnp.broadcast_to(m_new, m_ref.shape[1:])


def _nsa_prompt_body(q_ref, small_ref, cmp_ref, ks_ref, vs_ref, kw_ref, vw_ref, o_ref, vst_ref, vwt_ref, m_ref, acc_ref,
                     *, nb, n_sel):
    qb = q_ref.shape[0]
    t = ks_ref.shape[0]
    i = pl.program_id(1)
    q0 = i * qb
    rep = NSA_REP
    hd = HEAD_DIM
    groups = range(NSA_KV_HEADS)

    @pl.when(i == 0)
    def _():
        ones = jnp.ones((_V_ROWS - hd, t), BF16)
        for g in groups:
            vst_ref[g, hd:, :] = ones
            vwt_ref[g, hd:, :] = ones

        def fill(j, carry):
            k0 = pl.multiple_of(j * qb, qb)
            for src, dst in ((vs_ref, vst_ref), (vw_ref, vwt_ref)):
                v_t = src[pl.ds(k0, qb), :].T.astype(BF16)
                for g in groups:
                    dst[g, :hd, pl.ds(k0, qb)] = v_t[g * hd:(g + 1) * hd, :]
            return carry

        lax.fori_loop(0, t // qb, fill, 0)

    pos_r = q0 + lax.broadcasted_iota(jnp.int32, (1, qb), 1)
    sig_t = jax.nn.sigmoid(small_ref[...]).T
    q = q_ref[...] * (hd ** -0.5)
    pairs_t = [q[:, c * LANE:(c + 1) * LANE].T for c in range(NSA_HEADS // 2)]
    head_t = lambda h: pairs_t[h // 2][(h % 2) * hd:(h % 2 + 1) * hd, :]
    zeros = jnp.zeros((hd, rep * qb), F32)
    q_ts, sels = [], []
    for g in groups:
        qg_t = jnp.concatenate([head_t(g * rep + r) for r in range(rep)], axis=1)
        stack = lambda x: jnp.concatenate([x, zeros] if g == 0 else [zeros, x], axis=0).astype(BF16)
        q_t = stack(qg_t)
        q_ts.append(stack(qg_t * math.log2(math.e)))
        gate_row = lambda j: jnp.concatenate(
            [sig_t[SSM_HEADS + j * NSA_HEADS + g * rep + r:SSM_HEADS + j * NSA_HEADS + g * rep + r + 1, :]
             for r in range(rep)], axis=1)
        kc = cmp_ref[:, :LANE].astype(BF16)
        vc_t = cmp_ref[:, LANE:].T[g * hd:(g + 1) * hd, :].astype(BF16)
        s = jnp.dot(kc, q_t, preferred_element_type=F32)
        n_i = lax.broadcasted_iota(jnp.int32, (nb, qb), 0)
        complete = _lane_tile(jnp.where((n_i + 1) * NSA_BLOCK <= pos_r + 1, 1.0, 0.0), rep)
        s = jnp.where(complete > 0.5, s, NEG)
        p = jnp.exp(s - jnp.max(s, axis=0, keepdims=True))
        p = p / jnp.sum(p, axis=0, keepdims=True)
        p = jnp.where(_lane_tile(pos_r, rep) >= NSA_BLOCK - 1, p, 0.0)
        o_cmp = jnp.dot(vc_t, p.astype(BF16), preferred_element_type=F32)
        acc_ref[2 + g, :hd, :] = gate_row(0) * o_cmp
        imp = p[:, :qb]
        for r in range(1, rep):
            imp = imp + p[:, r * qb:(r + 1) * qb]
        sels.append(_nsa_select_t(imp, pos_r, n_sel).astype(BF16))

    def flash_init():
        m_ref[...] = jnp.full(m_ref.shape, NEG, F32)
        acc_ref[0:2] = jnp.zeros((2,) + acc_ref.shape[1:], F32)

    def flash_out(g):
        acc = acc_ref[g]
        return acc[:hd, :] / acc[hd:hd + 1, :]

    def flash_pass(k_ref, vt_ref, kb, first_key, valid_fn):
        kpos_i = lax.broadcasted_iota(jnp.int32, (kb, qb), 0)
        qpos = q0 + lax.broadcasted_iota(jnp.int32, (kb, qb), 1)
        flash_init()

        def step(j, carry):
            k0 = pl.multiple_of(j * kb, kb)
            kt = k_ref[pl.ds(k0, kb), :].astype(BF16)
            valids = valid_fn(k0, k0 + kpos_i, qpos)
            for g in groups:
                _flash_tile_t(kt, vt_ref[g, :, pl.ds(k0, kb)], q_ts[g], valids[g], m_ref, acc_ref, g)
            return carry

        lax.fori_loop(first_key // kb, (q0 + qb - 1) // kb + 1, step, 0)

    tile = lambda n: n * qb if t % (n * qb) == 0 else qb

    def sel_valid(k0, kpos, qpos):
        kb = kpos.shape[0]
        blk = lax.broadcasted_iota(jnp.int32, (kb, nb), 1)
        key = lax.broadcasted_iota(jnp.int32, (kb, nb), 0)
        expand = jnp.where(blk == (k0 + key) // NSA_BLOCK, 1.0, 0.0).astype(BF16)
        causal = kpos <= qpos
        return [jnp.where(causal, jnp.dot(expand, sels[g], preferred_element_type=F32), 0.0) > 0.5 for g in groups]

    flash_pass(ks_ref, vst_ref, tile(_SEL_TILE), 0, sel_valid)
    for g in groups:
        gate_row = jnp.concatenate(
            [sig_t[SSM_HEADS + NSA_HEADS + g * rep + r:SSM_HEADS + NSA_HEADS + g * rep + r + 1, :] for r in range(rep)],
            axis=1)
        acc_ref[2 + g, :hd, :] += gate_row * flash_out(g)
    def win_valid(k0, kpos, qpos):
        diff = qpos - kpos
        valid = jnp.where(diff >= 0, diff, NSA_WINDOW + 1) <= NSA_WINDOW
        return [valid for _ in groups]

    flash_pass(kw_ref, vwt_ref, tile(_WIN_TILE), jnp.maximum(q0 - NSA_WINDOW, 0), win_valid)
    for g in groups:
        gate_row = jnp.concatenate(
            [sig_t[SSM_HEADS + 2 * NSA_HEADS + g * rep + r:SSM_HEADS + 2 * NSA_HEADS + g * rep + r + 1, :]
             for r in range(rep)], axis=1)
        comb = acc_ref[2 + g, :hd, :] + gate_row * flash_out(g)
        for c in range(rep // 2):
            pair = jnp.concatenate([comb[:, (2 * c) * qb:(2 * c + 1) * qb], comb[:, (2 * c + 1) * qb:(2 * c + 2) * qb]],
                                   axis=0)
            col = (g * rep // 2 + c) * LANE
            o_ref[:, col:col + LANE] = pair.T


def _nsa_prompt_attn(q, small, cmp, rows, win, bsz, t):
    qb = math.gcd(t, QUERY_BLOCK)
    nq = t // qb
    nb = cmp.shape[1]
    n_sel = min(NSA_TOPK, nb)
    tok = lambda n: pl.BlockSpec((qb, n), lambda b, i: (b * nq + i, 0))
    seq = lambda c: pl.BlockSpec((None, t, LANE), lambda b, i: (b, 0, c))
    v_t = pltpu.VMEM((NSA_KV_HEADS, _V_ROWS, t), BF16)
    return pl.pallas_call(
        functools.partial(_nsa_prompt_body, nb=nb, n_sel=n_sel),
        grid=(bsz, nq),
        in_specs=[tok(NSA_HEADS * HEAD_DIM), tok(LANE), pl.BlockSpec((None, nb, 2 * LANE), lambda b, i: (b, 0, 0)),
                  seq(2), seq(3), seq(0), seq(1)],
        out_specs=tok(NSA_HEADS * HEAD_DIM),
        out_shape=jax.ShapeDtypeStruct((bsz * t, NSA_HEADS * HEAD_DIM), F32),
        scratch_shapes=[v_t, v_t, pltpu.VMEM((NSA_KV_HEADS, 8, NSA_REP * qb), F32),
                        pltpu.VMEM((2 * NSA_KV_HEADS, _V_ROWS, NSA_REP * qb), F32)],
        compiler_params=_cparams("arbitrary", "arbitrary"),
        name="nsa_prompt_attn",
    )(q, small, cmp, rows, rows, win, win)


_HI = lax.Precision.HIGHEST
_TN = (((0,), (0,)), ((), ()))


def _silu(x):
    return x * jax.nn.sigmoid(x)


def _softplus(x):
    return jnp.maximum(x, 0.0) + jnp.log(1.0 + jnp.exp(-jnp.abs(x)))


def _conv_silu(x_ref, buf_ref, w_ref, b_ref, xp_ref, tail_ref, first):
    n = x_ref.shape[0]

    @pl.when(first)
    def _():
        xp_ref[8 - (CONV_K - 1):8, :] = buf_ref[...]

    xp_ref[8:8 + n, :] = x_ref[...]
    y = xp_ref[8:8 + n, :] * w_ref[CONV_K - 1:CONV_K, :]
    for k in range(CONV_K - 1):
        y = y + xp_ref[5 + k:5 + k + n, :] * w_ref[k:k + 1, :]
    if b_ref is not None:
        y = y + b_ref[...]
    tail = xp_ref[8 + n - (CONV_K - 1):8 + n, :]
    tail_ref[...] = tail
    xp_ref[8 - (CONV_K - 1):8, :] = tail
    return _silu(y)


def _cumsum_rows(x):
    n = x.shape[0]
    tri = jnp.where(lax.broadcasted_iota(jnp.int32, (n, n), 0) >= lax.broadcasted_iota(jnp.int32, (n, n), 1), 1.0, 0.0)
    return jnp.dot(tri, x, preferred_element_type=F32, precision=_HI)


def _expand_heads(x, width, lane0=0, n_out=D_MODEL):
    h_i = lax.broadcasted_iota(jnp.int32, (LANE, n_out), 0)
    c_i = lax.broadcasted_iota(jnp.int32, (LANE, n_out), 1)
    sel = jnp.where(c_i // width + lane0 == h_i, 1.0, 0.0)
    return jnp.dot(x, sel, preferred_element_type=F32, precision=_HI)


def _decay_matrix(col, row, strict=False):
    n = col.shape[0]
    i = lax.broadcasted_iota(jnp.int32, (n, n), 0)
    j = lax.broadcasted_iota(jnp.int32, (n, n), 1)
    keep = (i > j) if strict else (i >= j)
    return jnp.exp(jnp.where(keep, col - row, NEG))


def _ssd_body(xbc_ref, z_ref, small_ref, buf_ref, h0_ref, cw_ref, cb_ref, dtb_ref, alog_ref, dx_ref, nw_ref,
              y_ref, conv_ref, h_ref, xp_ref):
    c = pl.program_id(1)
    n = xbc_ref.shape[0]
    hp = SSM_HEAD_DIM
    rep = SSM_HEADS // SSM_GROUPS

    @pl.when(c == 0)
    def _():
        h_ref[...] = h0_ref[...]

    act = _conv_silu(xbc_ref, buf_ref, cw_ref, cb_ref, xp_ref, conv_ref, c == 0)
    xs = act[:, :SSM_INNER]
    head_lane = lax.broadcasted_iota(jnp.int32, (n, LANE), 1) < SSM_HEADS
    dt = jnp.where(head_lane, _softplus(small_ref[...] + dtb_ref[...]), 0.0)
    la = dt * (-jnp.exp(alog_ref[...]))
    acs = _cumsum_rows(la)
    acs_t = acs.T
    xd = xs * _expand_heads(dt, hp)
    e_acs = _expand_heads(jnp.exp(acs), hp)
    xdd = (xd * _expand_heads(jnp.exp(acs[n - 1:n, :] - acs), hp)).astype(BF16)
    xd = xd.astype(BF16)
    lane2 = _half_mask((n, LANE), 1)
    for g in range(SSM_GROUPS):
        bc = act[:, SSM_INNER + g * SSM_STATE:SSM_INNER + (g + 1) * SSM_STATE].astype(BF16)
        cc = act[:, SSM_INNER + (SSM_GROUPS + g) * SSM_STATE:SSM_INNER + (SSM_GROUPS + g + 1) * SSM_STATE].astype(BF16)
        cb = lax.dot_general(cc, bc, _NT, preferred_element_type=F32)
        h_prev = h_ref[g * rep:(g + 1) * rep].reshape(rep * hp, SSM_STATE)
        y_off = lax.dot_general(cc, h_prev.astype(BF16), _NT, preferred_element_type=F32)
        for pair in range(rep // 2):
            halves = []
            for k in range(2):
                h = g * rep + 2 * pair + k
                lm = _decay_matrix(acs[:, h:h + 1], acs_t[h:h + 1, :])
                halves.append(jnp.dot((cb * lm).astype(BF16), xd[:, (h // 2) * LANE:(h // 2 + 1) * LANE],
                                      preferred_element_type=F32))
            col = (g * rep + 2 * pair) * hp
            y_ref[:, col:col + LANE] = jnp.where(lane2, halves[1], halves[0]) + y_off[:, 2 * pair * hp:2 * pair * hp + LANE] * e_acs[:, col:col + LANE]
        st = lax.dot_general(xdd[:, g * rep * hp:(g + 1) * rep * hp], bc, _TN, preferred_element_type=F32)
        for r in range(rep):
            h = g * rep + r
            dec = jnp.exp(acs_t[h:h + 1, n - 1:n])
            h_ref[h] = h_ref[h] * dec + st[r * hp:(r + 1) * hp, :]
    y = y_ref[...] + dx_ref[...] * xs
    y_ref[...] = _rms(y * _silu(z_ref[...]), nw_ref[...])


def _ssd_mixer(xbc, z, small, conv_buf, h0, lp, bsz, t, chunk):
    nc = t // chunk
    tok = lambda n: pl.BlockSpec((chunk, n), lambda b, c: (b * nc + c, 0))
    per_b = lambda shape: pl.BlockSpec((None,) + shape, lambda b, c: (b,) + (0,) * len(shape))
    pad = lambda v: jnp.pad(v.astype(F32), (0, LANE - v.shape[0])).reshape(1, LANE)
    cdim = SSM_CONV_DIM
    state = (SSM_HEADS, SSM_HEAD_DIM, SSM_STATE)
    return pl.pallas_call(
        _ssd_body,
        grid=(bsz, nc),
        in_specs=[tok(cdim), tok(SSM_INNER), tok(LANE), per_b((CONV_K - 1, cdim)), per_b(state),
                  _const_spec((CONV_K, cdim)), _const_spec((1, cdim)), _const_spec((1, LANE)), _const_spec((1, LANE)),
                  _const_spec((1, SSM_INNER)), _const_spec((1, SSM_INNER))],
        out_specs=[tok(SSM_INNER), per_b((CONV_K - 1, cdim)), per_b(state)],
        out_shape=[jax.ShapeDtypeStruct((bsz * t, SSM_INNER), F32),
                   jax.ShapeDtypeStruct((bsz, CONV_K - 1, cdim), F32),
                   jax.ShapeDtypeStruct((bsz,) + state, F32)],
        scratch_shapes=[pltpu.VMEM((chunk + 8, cdim), F32)],
        compiler_params=_cparams("arbitrary", "arbitrary"),
        name="ssd_mixer",
    )(xbc, z, small, conv_buf, h0, lp['ssm_conv_w'], lp['ssm_conv_b'].reshape(1, cdim), pad(lp['ssm_dt_bias']),
      pad(lp['ssm_a_log']), jnp.repeat(lp['ssm_d'].astype(F32), SSM_HEAD_DIM).reshape(1, SSM_INNER),
      lp['ssm_norm_w'].reshape(1, SSM_INNER))


def _split_bf16(x):
    hi = x.astype(BF16)
    return hi, (x - hi.astype(F32)).astype(BF16)


def _bmm(a, b):
    return jnp.einsum('hmk,hkn->hmn', a, b, preferred_element_type=F32)


def _bmm_nt(a, b):
    return jnp.einsum('hmk,hnk->hmn', a, b, preferred_element_type=F32)


def _bmm_tn(a, b):
    return jnp.einsum('hkm,hkn->hmn', a, b, preferred_element_type=F32)


def _bmm_split(a, b):
    a_hi, a_lo = _split_bf16(a)
    b_hi, b_lo = _split_bf16(b)
    return _bmm(a_hi, b_hi) + _bmm(a_hi, b_lo) + _bmm(a_lo, b_hi)


def _unit_lower_inverse(a):
    n = a.shape[-1]
    eye = jnp.where(lax.broadcasted_iota(jnp.int32, (n, n), 0) == lax.broadcasted_iota(jnp.int32, (n, n), 1), 1.0, 0.0)
    p = -a
    t = eye + p
    span = 2
    while span < n:
        p = _bmm_split(p, p)
        t = t + _bmm_split(t, p)
        span *= 2
    return t


def _gdn_body(qkv_ref, z_ref, small_ref, buf_ref, s0_ref, cw_ref, alog_ref, dtb_ref, nw_ref,
              y_ref, conv_ref, s_ref, xp_ref):
    c = pl.program_id(1)
    n = qkv_ref.shape[0]
    dk = GDN_HEAD_DIM
    nh = GDN_HEADS

    @pl.when(c == 0)
    def _():
        s_ref[...] = s0_ref[...]

    act = _conv_silu(qkv_ref, buf_ref, cw_ref, None, xp_ref, conv_ref, c == 0)
    lane = lax.broadcasted_iota(jnp.int32, (n, LANE), 1)
    raw = small_ref[...]
    beta = jax.nn.sigmoid(raw)
    g = jnp.where((lane >= nh) & (lane < 2 * nh), -jnp.exp(alog_ref[...]) * _softplus(raw + dtb_ref[...]), 0.0)
    gc = _cumsum_rows(g)
    gc_t = gc.T
    beta_x = _expand_heads(beta, dk)
    egc_x = _expand_heads(jnp.exp(gc), dk, nh)
    edec_x = _expand_heads(jnp.exp(gc[n - 1:n, :] - gc), dk, nh)
    ii = lax.broadcasted_iota(jnp.int32, (n, n), 0)
    jj = lax.broadcasted_iota(jnp.int32, (n, n), 1)
    heads = lambda x, base=0: jnp.stack([x[:, base + h * dk:base + (h + 1) * dk] for h in range(nh)])
    q, k, v = heads(act), heads(act, nh * dk), heads(act, 2 * nh * dk)
    q = q * lax.rsqrt(jnp.sum(q * q, axis=-1, keepdims=True) + 1e-6) * (dk ** -0.5)
    k = k * lax.rsqrt(jnp.sum(k * k, axis=-1, keepdims=True) + 1e-6)
    beta3, egc, edec = heads(beta_x), heads(egc_x), heads(edec_x)
    kb = k * beta3
    decay = jnp.stack([_decay_matrix(gc[:, nh + h:nh + h + 1], gc_t[nh + h:nh + h + 1, :]) for h in range(nh)])
    k16 = k.astype(BF16)
    amat = jnp.where(ii > jj, _bmm_nt(kb.astype(BF16), k16) * decay, 0.0)
    tmat = _unit_lower_inverse(amat).astype(BF16)
    u = _bmm(tmat, (v * beta3).astype(BF16))
    w = _bmm(tmat, (kb * egc).astype(BF16))
    qk = _bmm_nt(q.astype(BF16), k16) * decay
    s = s_ref[...]
    s16 = s.astype(BF16)
    v16 = (u - _bmm(w.astype(BF16), s16)).astype(BF16)
    o = _bmm((q * egc).astype(BF16), s16) + _bmm(qk.astype(BF16), v16)
    gl = jnp.stack([jnp.exp(gc_t[nh + h:nh + h + 1, n - 1:n]) for h in range(nh)])
    s_ref[...] = s * gl + _bmm_tn((k * edec).astype(BF16), v16)
    y = _rms(o, nw_ref[...])
    for h in range(nh):
        sl = slice(h * dk, (h + 1) * dk)
        y_ref[:, sl] = y[h] * _silu(z_ref[:, sl])


def _gdn_mixer(qkv, z, small, conv_buf, s0, lp, bsz, t, chunk):
    nc = t // chunk
    tok = lambda n: pl.BlockSpec((chunk, n), lambda b, c: (b * nc + c, 0))
    per_b = lambda shape: pl.BlockSpec((None,) + shape, lambda b, c: (b,) + (0,) * len(shape))
    pad8 = lambda v: jnp.pad(v.astype(F32), (GDN_HEADS, LANE - 2 * GDN_HEADS)).reshape(1, LANE)
    cdim = 3 * GDN_WIDTH
    state = (GDN_HEADS, GDN_HEAD_DIM, GDN_HEAD_DIM)
    return pl.pallas_call(
        _gdn_body,
        grid=(bsz, nc),
        in_specs=[tok(cdim), tok(GDN_WIDTH), tok(LANE), per_b((CONV_K - 1, cdim)), per_b(state),
                  _const_spec((CONV_K, cdim)), _const_spec((1, LANE)), _const_spec((1, LANE)),
                  _const_spec((1, GDN_HEAD_DIM))],
        out_specs=[tok(GDN_WIDTH), per_b((CONV_K - 1, cdim)), per_b(state)],
        out_shape=[jax.ShapeDtypeStruct((bsz * t, GDN_WIDTH), F32),
                   jax.ShapeDtypeStruct((bsz, CONV_K - 1, cdim), F32),
                   jax.ShapeDtypeStruct((bsz,) + state, F32)],
        scratch_shapes=[pltpu.VMEM((chunk + 8, cdim), F32)],
        compiler_params=_cparams("arbitrary", "arbitrary"),
        name="gdn_mixer",
    )(qkv, z, small, conv_buf, s0, lp['gdn_conv_w'], pad8(lp['gdn_a_log']), pad8(lp['gdn_dt_bias']),
      lp['gdn_norm_w'].reshape(1, GDN_HEAD_DIM))


def _softmax_pv(s, valid, v16, n_rep=1):
    rows, nk = s.shape
    s = jnp.where(valid[None], s.reshape(n_rep, rows // n_rep, nk), NEG).reshape(rows, nk)
    m = jnp.max(s, axis=-1, keepdims=True)
    p = jnp.exp(s - m)
    l = jnp.sum(p, axis=-1, keepdims=True)
    return jnp.dot((p / l).astype(BF16), v16, preferred_element_type=F32), m + jnp.log(l)


_DIL_PAIRS = DIL_HEADS_PER_GROUP // 2


def _dil_prompt_body(q_ref, kp_ref, kc_ref, vp_ref, vc_ref, o_ref, lse_ref, *, window, step):
    i = pl.program_id(1)
    qb = q_ref.shape[1] // step
    scale = HEAD_DIM ** -0.5
    qpos = i * qb + lax.broadcasted_iota(jnp.int32, (qb, 2 * qb), 0)
    kpos = (i - 1) * qb + lax.broadcasted_iota(jnp.int32, (qb, 2 * qb), 1)
    diff = jnp.where(kpos >= 0, qpos - kpos, -1)
    valid = jnp.where(diff >= 0, diff, window + 1) <= window
    lower = _half_mask((qb, LANE), 0)

    def one_class(c, carry):
        rows = pl.ds(c, qb, stride=step)
        for pr in range(q_ref.shape[0]):
            q = q_ref[pr, rows, :] * scale
            kk = jnp.concatenate([kp_ref[pr, rows, :], kc_ref[pr, rows, :]], axis=0).astype(BF16)
            vv = jnp.concatenate([vp_ref[pr, rows, :], vc_ref[pr, rows, :]], axis=0).astype(BF16)
            o2, l2 = [], []
            for k in range(2):
                qh = jnp.where(_half_mask((qb, LANE), k), q, 0.0).astype(BF16)
                s = lax.dot_general(qh, kk, _NT, preferred_element_type=F32)
                o, lse = _softmax_pv(s, valid, vv)
                o2.append(o)
                l2.append(jnp.broadcast_to(lse, (qb, LANE)))
            o_ref[pr, rows, :] = jnp.where(lower, o2[0], o2[1])
            lse_ref[pr, rows, :] = jnp.where(lower, l2[0], l2[1])
        return carry

    lax.fori_loop(0, step, one_class, 0)


def _dil_prompt_attn(dil, gi, bsz, t):
    win, step = DIL_PATTERNS[gi]
    assert win % step == 0 and t % win == 0
    tile = win
    nq = t // tile
    pp = _DIL_PAIRS if _DIL_PAIRS * tile * LANE * 4 <= (1 << 20) else 1
    blk = lambda part, prev: pl.BlockSpec(
        (pp, tile, LANE),
        lambda b, i, h: ((part * DIL_GROUPS + gi) * (_DIL_PAIRS // pp) + h,
                         b * nq + (jnp.maximum(i - 1, 0) if prev else i), 0))
    out_spec = pl.BlockSpec((pp, tile, LANE), lambda b, i, h: (h, b * nq + i, 0))
    out_sds = jax.ShapeDtypeStruct((_DIL_PAIRS, bsz * t, LANE), F32)
    return pl.pallas_call(
        functools.partial(_dil_prompt_body, window=win // step, step=step),
        grid=(bsz, nq, _DIL_PAIRS // pp),
        in_specs=[blk(0, False), blk(1, True), blk(1, False), blk(2, True), blk(2, False)],
        out_specs=[out_spec, out_spec],
        out_shape=[out_sds, out_sds],
        compiler_params=_cparams("arbitrary", "arbitrary", "arbitrary"),
        name=f"dil_prompt_attn_{gi}",
    )(dil, dil, dil, dil, dil)


def _dil_combine_body(o0, o1, o2, l0, l1, l2, y_ref):
    for pr in range(_DIL_PAIRS):
        m = jnp.maximum(jnp.maximum(l0[pr], l1[pr]), l2[pr])
        e0, e1, e2 = jnp.exp(l0[pr] - m), jnp.exp(l1[pr] - m), jnp.exp(l2[pr] - m)
        den = e0 + e1 + e2
        y_ref[:, pr * LANE:(pr + 1) * LANE] = (e0 / den) * o0[pr] + (e1 / den) * o1[pr] + (e2 / den) * o2[pr]


def _dil_combine(outs, lses, tm):
    n_tok = outs[0].shape[1]
    spec = pl.BlockSpec((_DIL_PAIRS, tm, LANE), lambda i: (0, i, 0))
    return pl.pallas_call(
        _dil_combine_body,
        grid=(n_tok // tm,),
        in_specs=[spec] * 6,
        out_specs=pl.BlockSpec((tm, DIL_WIDTH), lambda i: (i, 0)),
        out_shape=jax.ShapeDtypeStruct((n_tok, DIL_WIDTH), F32),
        compiler_params=_cparams("arbitrary"),
        name="dil_combine",
    )(*outs, *lses)


def _attend_cached(qg, k_t, v_t, k_n, v_n, valid_c, valid_n, n_rep=1):
    rows = qg.shape[0]
    mask = lambda s, v: jnp.where(v[None], s.reshape(n_rep, rows // n_rep, s.shape[1]), NEG).reshape(rows, s.shape[1])
    s_c = mask(jnp.dot(qg, k_t, preferred_element_type=F32), valid_c)
    s_n = mask(lax.dot_general(qg, k_n, _NT, preferred_element_type=F32), valid_n)
    m = jnp.maximum(jnp.max(s_c, axis=-1, keepdims=True), jnp.max(s_n, axis=-1, keepdims=True))
    p_c = jnp.exp(s_c - m)
    p_n = jnp.exp(s_n - m)
    l = jnp.sum(p_c, axis=-1, keepdims=True) + jnp.sum(p_n, axis=-1, keepdims=True)
    o = lax.dot_general((p_c / l).astype(BF16), v_t, _NT, preferred_element_type=F32)
    o = o + jnp.dot((p_n / l).astype(BF16), v_n, preferred_element_type=F32)
    return o, m + jnp.log(l)


def _nsa_decode_body(pt_ref, q_ref, small_ref, rows_ref, win_ref, *refs, n_pages, lw, nb, nb_pad, n_sel):
    del pt_ref, nb
    pages = refs[:n_pages]
    (pastwin_ref, pek_ref, pev_ref, wk1_ref, wk2_ref, wv1_ref, wv2_ref, o_ref, neww_ref,
     kc_s, vc_s, comb_s) = refs[n_pages:]
    t = q_ref.shape[0]
    rep = NSA_REP
    past_len = n_pages * PAGE_SIZE
    for dst, c in ((kc_s, 0), (vc_s, 1)):
        for p in range(n_pages):
            dst[p * PAGE_SIZE:(p + 1) * PAGE_SIZE, :] = pages[p][c * LANE:(c + 1) * LANE, :].T
        dst[past_len:past_len + t, :] = rows_ref[:, c * LANE:(c + 1) * LANE]
        dst[past_len + t:, :] = jnp.zeros((dst.shape[0] - past_len - t, LANE), F32)
    kcmp = _compress_rows(kc_s, pek_ref, wk1_ref, wk2_ref, nb_pad).astype(BF16)
    vcmp = _compress_rows(vc_s, pev_ref, wv1_ref, wv2_ref, nb_pad).astype(BF16)
    page_rows = lambda c: jnp.concatenate([pages[p][c * LANE:(c + 1) * LANE, :] for p in range(n_pages)],
                                          axis=1).astype(BF16)
    ks_t, vs_t = page_rows(2), page_rows(3)
    ks_n, vs_n = rows_ref[:, 2 * LANE:3 * LANE].astype(BF16), rows_ref[:, 3 * LANE:].astype(BF16)
    kw_t, vw_t = pastwin_ref[:LANE, :].astype(BF16), pastwin_ref[LANE:, :].astype(BF16)
    kw_n, vw_n = win_ref[:, :LANE].astype(BF16), win_ref[:, LANE:].astype(BF16)
    neww_ref[...] = jnp.concatenate([pastwin_ref[:, t:], win_ref[...].T], axis=1)
    pos_c = past_len + lax.broadcasted_iota(jnp.int32, (t, 1), 0)
    new_i = lax.broadcasted_iota(jnp.int32, (t, t), 1)
    back = lax.broadcasted_iota(jnp.int32, (t, t), 0) - new_i
    sig = jax.nn.sigmoid(small_ref[...])
    q = q_ref[...] * (HEAD_DIM ** -0.5)
    for g in range(NSA_KV_HEADS):
        qg = _stack_heads(q, range(g * rep, (g + 1) * rep), g).astype(BF16)
        gate_lane = lambda j: [SSM_HEADS + j * NSA_HEADS + g * rep + r for r in range(rep)]
        s = lax.dot_general(qg, kcmp, _NT, preferred_element_type=F32).reshape(rep, t, nb_pad)
        n_i = lax.broadcasted_iota(jnp.int32, (t, nb_pad), 1)
        complete = (n_i + 1) * NSA_BLOCK <= pos_c + 1
        s = jnp.where(complete[None], s, NEG)
        p = jnp.exp(s - jnp.max(s, axis=-1, keepdims=True))
        p = p / jnp.sum(p, axis=-1, keepdims=True)
        p = jnp.where((pos_c >= NSA_BLOCK - 1)[None], p, 0.0)
        o_cmp = jnp.dot(p.reshape(rep * t, nb_pad).astype(BF16), vcmp, preferred_element_type=F32)
        comb = _row_gate(sig, gate_lane(0), t) * o_cmp
        sel = _nsa_select(p, pos_c, nb_pad, n_sel).astype(BF16)
        expand = lambda n_keys, first: jnp.where(
            lax.broadcasted_iota(jnp.int32, (nb_pad, n_keys), 0)
            == (first + lax.broadcasted_iota(jnp.int32, (nb_pad, n_keys), 1)) // NSA_BLOCK, 1.0, 0.0).astype(BF16)
        chosen_c = jnp.dot(sel, expand(past_len, 0), preferred_element_type=F32)
        chosen_n = jnp.dot(sel, expand(t, past_len), preferred_element_type=F32)
        kpos = lax.broadcasted_iota(jnp.int32, (t, past_len), 1)
        valid_c = jnp.where(kpos <= pos_c, chosen_c, 0.0) > 0.5
        valid_n = jnp.where(back >= 0, chosen_n, 0.0) > 0.5
        o_sel, _ = _attend_cached(qg, ks_t, vs_t, ks_n, vs_n, valid_c, valid_n, rep)
        comb = comb + _row_gate(sig, gate_lane(1), t) * o_sel
        diff = pos_c - (past_len - lw + lax.broadcasted_iota(jnp.int32, (t, lw), 1))
        valid_c = jnp.where(diff >= 0, diff, NSA_WINDOW + 1) <= NSA_WINDOW
        valid_n = jnp.where(back >= 0, back, NSA_WINDOW + 1) <= NSA_WINDOW
        o_win, _ = _attend_cached(qg, kw_t, vw_t, kw_n, vw_n, valid_c, valid_n, rep)
        comb_s[...] = comb + _row_gate(sig, gate_lane(2), t) * o_win
        for c in range(rep // 2):
            a = comb_s[(2 * c) * t:(2 * c + 1) * t, :]
            b = comb_s[(2 * c + 1) * t:(2 * c + 2) * t, :]
            col = (g * rep // 2 + c) * LANE
            o_ref[:, col:col + LANE] = _unstack_pair(a, b, g)


def _nsa_decode_attn(q, small, rows, win, cache_kv, cache_win, page_table, lp, bsz, t):
    n_pages = page_table.shape[1]
    past_len = n_pages * PAGE_SIZE
    lw = cache_win.shape[1]
    nb = -(-(past_len + t) // NSA_BLOCK)
    nb_pad = -(-nb // 8) * 8
    pages = jnp.transpose(cache_kv, (0, 2, 3, 4, 1)).reshape(cache_kv.shape[0], 4 * LANE, PAGE_SIZE)
    pastwin = jnp.transpose(cache_win, (0, 2, 3, 4, 1)).reshape(bsz, 2 * LANE, lw)
    w1 = lambda w: w.reshape(NSA_BLOCK, HEAD_DIM, NSA_CMP_HIDDEN).astype(BF16)
    tok = lambda n: pl.BlockSpec((t, n), lambda b, pt: (b, 0))
    const = lambda shape: pl.BlockSpec(shape, lambda b, pt: (0,) * len(shape), pipeline_mode=pl.Buffered(1))
    page_spec = lambda p: pl.BlockSpec((None, 4 * LANE, PAGE_SIZE), lambda b, pt: (pt[b * n_pages + p], 0, 0))
    win_spec = pl.BlockSpec((None, 2 * LANE, lw), lambda b, pt: (b, 0, 0))
    grid_spec = pltpu.PrefetchScalarGridSpec(
        num_scalar_prefetch=1,
        grid=(bsz,),
        in_specs=[tok(NSA_HEADS * HEAD_DIM), tok(LANE), tok(4 * LANE), tok(2 * LANE)]
        + [page_spec(p) for p in range(n_pages)]
        + [win_spec, const((NSA_BLOCK, LANE)), const((NSA_BLOCK, LANE)),
           const((NSA_BLOCK, HEAD_DIM, NSA_CMP_HIDDEN)), const((NSA_CMP_HIDDEN, HEAD_DIM)),
           const((NSA_BLOCK, HEAD_DIM, NSA_CMP_HIDDEN)), const((NSA_CMP_HIDDEN, HEAD_DIM))],
        out_specs=[tok(NSA_HEADS * HEAD_DIM), win_spec],
        scratch_shapes=[pltpu.VMEM((nb_pad * NSA_BLOCK, LANE), F32), pltpu.VMEM((nb_pad * NSA_BLOCK, LANE), F32),
                        pltpu.VMEM((NSA_REP * t, LANE), F32)],
    )
    y, new_win = pl.pallas_call(
        functools.partial(_nsa_decode_body, n_pages=n_pages, lw=lw, nb=nb, nb_pad=nb_pad, n_sel=min(NSA_TOPK, nb)),
        grid_spec=grid_spec,
        out_shape=[jax.ShapeDtypeStruct((bsz * t, NSA_HEADS * HEAD_DIM), F32),
                   jax.ShapeDtypeStruct((bsz, 2 * LANE, lw), F32)],
        compiler_params=_cparams("arbitrary"),
        name="nsa_decode_attn",
    )(page_table.reshape(-1), q, small, rows, win, *([pages] * n_pages), pastwin,
      _pe2(lp['nsa_pe_k']), _pe2(lp['nsa_pe_v']), w1(lp['nsa_ck_w1']), lp['nsa_ck_w2'].astype(BF16),
      w1(lp['nsa_cv_w1']), lp['nsa_cv_w2'].astype(BF16))
    new_win = jnp.transpose(new_win.reshape(bsz, 2, NSA_KV_HEADS, HEAD_DIM, lw), (0, 4, 1, 2, 3))
    return y, new_win


def _dil_decode_group(q, k_new, v_new, cache_ref, step):
    t = q.shape[0]
    lg = cache_ref.shape[1]
    pairs = DIL_HEADS_PER_GROUP // 2
    tok_c = lax.broadcasted_iota(jnp.int32, (2 * t, lg), 0) % t
    ahead = lax.broadcasted_iota(jnp.int32, (2 * t, lg), 1) - tok_c
    valid_c = jnp.where(ahead >= 0, ahead % step, 1) == 0
    tok_n = lax.broadcasted_iota(jnp.int32, (2 * t, t), 0) % t
    back = tok_n - lax.broadcasted_iota(jnp.int32, (2 * t, t), 1)
    valid_n = jnp.where(back >= 0, back % step, 1) == 0
    lower = _half_mask((t, LANE), 0)
    outs, lses = [], []
    for pr in range(pairs):
        sl = slice(pr * LANE, (pr + 1) * LANE)
        k_t = cache_ref[pr * LANE:(pr + 1) * LANE, :].astype(BF16)
        v_t = cache_ref[DIL_WIDTH + pr * LANE:DIL_WIDTH + (pr + 1) * LANE, :].astype(BF16)
        kn = k_new[:, sl].astype(BF16)
        vn = v_new[:, sl].astype(BF16)
        qp = q[:, sl]
        q2 = jnp.concatenate([jnp.where(lower, qp, 0.0), jnp.where(lower, 0.0, qp)], axis=0).astype(BF16)
        o, lse = _attend_cached(q2, k_t, v_t, kn, vn, valid_c, valid_n)
        lse = jnp.broadcast_to(lse, (2 * t, LANE))
        outs.append(jnp.where(lower, o[:t], o[t:]))
        lses.append(jnp.where(lower, lse[:t], lse[t:]))
    return jnp.concatenate(outs, axis=1), jnp.concatenate(lses, axis=1)


def _dil_decode_body(x_ref, c0_ref, c1_ref, c2_ref, y_ref, n0_ref, n1_ref, n2_ref):
    scale = HEAD_DIM ** -0.5
    t = x_ref.shape[1]
    res = []
    for gi, (cache_ref, new_ref) in enumerate(((c0_ref, n0_ref), (c1_ref, n1_ref), (c2_ref, n2_ref))):
        _, step = DIL_PATTERNS[gi]
        part = lambda p: jnp.concatenate(
            [x_ref[(p * DIL_GROUPS + gi) * _DIL_PAIRS + pr] for pr in range(_DIL_PAIRS)], axis=1)
        res.append(_dil_decode_group(part(0) * scale, part(1), part(2), cache_ref, step))
        new_t = jnp.concatenate([part(1).T, part(2).T], axis=0)
        new_ref[...] = jnp.concatenate([cache_ref[:, t:], new_t], axis=1)
    (o0, l0), (o1, l1), (o2, l2) = res
    m = jnp.maximum(jnp.maximum(l0, l1), l2)
    e0, e1, e2 = jnp.exp(l0 - m), jnp.exp(l1 - m), jnp.exp(l2 - m)
    den = e0 + e1 + e2
    y_ref[...] = (e0 / den) * o0 + (e1 / den) * o1 + (e2 / den) * o2


def _dil_decode_attn(dil, bufs, bsz, t):
    n_rows = 2 * DIL_WIDTH
    views = []
    for (win, _), buf in zip(DIL_PATTERNS, bufs):
        assert buf.shape[1] == win, "decode path needs a full window of cached rows"
        views.append(jnp.transpose(buf, (0, 2, 3, 4, 1)).reshape(bsz, n_rows, win))
    cache_spec = lambda v: pl.BlockSpec((None, n_rows, v.shape[2]), lambda b: (b, 0, 0))
    tok_spec = lambda n: pl.BlockSpec((t, n), lambda b: (b, 0))
    y, *new = pl.pallas_call(
        _dil_decode_body,
        grid=(bsz,),
        in_specs=[pl.BlockSpec((DIL_IN // LANE, t, LANE), lambda b: (0, b, 0))] + [cache_spec(v) for v in views],
        out_specs=[tok_spec(DIL_WIDTH)] + [cache_spec(v) for v in views],
        out_shape=[jax.ShapeDtypeStruct((bsz * t, DIL_WIDTH), F32)]
        + [jax.ShapeDtypeStruct(v.shape, F32) for v in views],
        compiler_params=_cparams("arbitrary"),
        name="dil_decode_attn",
    )(dil, *views)
    unview = lambda v: jnp.transpose(v.reshape(bsz, 2, DIL_HEADS_PER_GROUP, HEAD_DIM, v.shape[2]), (0, 4, 1, 2, 3))
    return y, [unview(v) for v in new]


_AB_CUTS = (0, SSM_INNER, SSM_INNER + SSM_CONV_DIM, SSM_INNER + SSM_CONV_DIM + SSM_HEADS)
_AB_Q0 = _AB_CUTS[3]
_AB_KV0 = _AB_Q0 + NSA_HEADS * HEAD_DIM
_AB_WIN0 = _AB_KV0 + 4 * NSA_KV_HEADS * HEAD_DIM
_AB_GATE0 = _AB_WIN0 + 2 * NSA_KV_HEADS * HEAD_DIM
_AB_END = _AB_GATE0 + 3 * NSA_HEADS


def _pad_cols(w, n):
    return jnp.pad(w, ((0, 0), (0, n - w.shape[1])))


def _ab_weight_segs(w_in):
    small = jnp.concatenate([w_in[:, _AB_CUTS[2]:_AB_CUTS[3]], w_in[:, _AB_GATE0:_AB_END]], axis=1)
    segs = [w_in[:, _AB_CUTS[0]:_AB_CUTS[1]], w_in[:, _AB_CUTS[1]:_AB_CUTS[2]], w_in[:, _AB_Q0:_AB_KV0],
            w_in[:, _AB_KV0:_AB_WIN0], w_in[:, _AB_WIN0:_AB_GATE0], _pad_cols(small, LANE)]
    return [s.astype(BF16) for s in segs]


_CD_Z0 = 3 * GDN_WIDTH
_CD_B0 = _CD_Z0 + GDN_WIDTH
_CD_DIL0 = _CD_B0 + 2 * GDN_HEADS
_CD_END = _CD_DIL0 + DIL_IN


def _cd_weight_segs(w_in):
    segs = [w_in[:, :_CD_Z0], w_in[:, _CD_Z0:_CD_B0], w_in[:, _CD_DIL0:_CD_END],
            _pad_cols(w_in[:, _CD_B0:_CD_DIL0], LANE)]
    return [s.astype(BF16) for s in segs]


def _layer_ab(y, norm_pre, segs, lp, past, page_table, is_prompt, tm):
    bsz, t, d = y.shape
    if is_prompt:
        z, xbc, q, rows, win, small, rows_t, win_t = _norm_proj(
            y.reshape(bsz * t, d), norm_pre, segs, tm, t_segs=(segs[3].T, segs[4].T), seq_len=t)
        to_rows = lambda x_t, kinds: jnp.transpose(
            x_t.reshape(bsz, kinds, NSA_KV_HEADS, HEAD_DIM, x_t.shape[-1]), (0, 4, 1, 2, 3))
        rows_new = to_rows(rows_t, 4)
        new_win = to_rows(win_t[:, :, t - min(NSA_WINDOW, t):], 2)
        conv_buf = jnp.zeros((bsz, CONV_K - 1, SSM_CONV_DIM), F32)
        h0 = jnp.zeros((bsz, SSM_HEADS, SSM_HEAD_DIM, SSM_STATE), F32)
        chunk = math.gcd(t, SSM_CHUNK)
        rows3 = rows.reshape(bsz, t, 4 * LANE)
        cmp = _nsa_compress_prompt(rows3, lp)
        y_nsa = _nsa_prompt_attn(q, small, cmp, rows3, win.reshape(bsz, t, 2 * LANE), bsz, t)
    else:
        z, xbc, q, rows, win, small = _norm_proj(y.reshape(bsz * t, d), norm_pre, segs, tm)
        rows_new = rows.reshape(bsz, t, 4, NSA_KV_HEADS, HEAD_DIM)
        conv_buf, h0, chunk = past['ssm_conv'], past['ssm'], t
        y_nsa, new_win = _nsa_decode_attn(q, small, rows, win, past['nsa_kv'], past['nsa_win'], page_table, lp,
                                          bsz, t)
    y_ssm, new_conv, new_ssm = _ssd_mixer(xbc, z, small, conv_buf, h0, lp, bsz, t, chunk)
    return y_ssm, y_nsa, (new_conv, new_ssm, rows_new, new_win)


def _layer_cd(y, norm_pre, segs, lp, past, is_prompt, tm):
    bsz, t, d = y.shape
    qkv, z, dil, small = _norm_proj(y.reshape(bsz * t, d), norm_pre, segs, tm, plane_segs=(2,))
    if is_prompt:
        conv_buf = jnp.zeros((bsz, CONV_K - 1, 3 * GDN_WIDTH), F32)
        s0 = jnp.zeros((bsz, GDN_HEADS, GDN_HEAD_DIM, GDN_HEAD_DIM), F32)
        chunk = math.gcd(t, GDN_CHUNK)
        parts = [_dil_prompt_attn(dil, gi, bsz, t) for gi in range(DIL_GROUPS)]
        y_dil = _dil_combine([p[0] for p in parts], [p[1] for p in parts], min(tm * 2, bsz * t))
        bufs = []
        for gi, (win, _) in enumerate(DIL_PATTERNS):
            w = min(win, t)
            tail = lambda part: dil[(part * DIL_GROUPS + gi) * _DIL_PAIRS:(part * DIL_GROUPS + gi + 1) * _DIL_PAIRS] \
                .reshape(_DIL_PAIRS, bsz, t, LANE)[:, :, t - w:]
            kv = lax.optimization_barrier(jnp.stack([tail(1), tail(2)]))
            kv = kv.reshape(2, _DIL_PAIRS, bsz, w, 2, HEAD_DIM)
            bufs.append(jnp.transpose(kv, (2, 3, 0, 1, 4, 5)).reshape(bsz, w, 2, DIL_HEADS_PER_GROUP, HEAD_DIM))
    else:
        conv_buf, s0, chunk = past['gdn_conv'], past['gdn'], t
        y_dil, bufs = _dil_decode_attn(dil, past['dil'], bsz, t)
    y_gdn, new_conv, new_gdn = _gdn_mixer(qkv, z, small, conv_buf, s0, lp, bsz, t, chunk)
    return y_gdn, y_dil, (new_conv, new_gdn, bufs[0], bufs[1], bufs[2])


def kernel(x_prompt, x_sample, cache_ssm_conv, state_ssm, cache_nsa_kv, cache_nsa_win_kv, cache_gdn_conv, state_gdn,
           cache_dil0_kv, cache_dil1_kv, cache_dil2_kv, page_table, norm_mix_pre, norm_mix_post, norm_mlp_pre,
           norm_mlp_post, mlp_w1, mlp_w2, ab_w_in, ab_w_out, ssm_conv_w, ssm_conv_b, ssm_dt_bias, ssm_a_log, ssm_d,
           ssm_norm_w, nsa_pe_k, nsa_pe_v, nsa_ck_w1, nsa_ck_w2, nsa_cv_w1, nsa_cv_w2, cd_w_in, cd_w_out, gdn_conv_w,
           gdn_dt_bias, gdn_a_log, gdn_norm_w):
    depth = norm_mix_pre.shape[0]
    yp, ys = x_prompt, x_sample
    ab_p, ab_s, cd_p, cd_s = [], [], [], []
    tm = 256
    for l in range(depth):
        j = l // 2
        if l % 2 == 0:
            lp = {'ssm_conv_w': ssm_conv_w[j], 'ssm_conv_b': ssm_conv_b[j], 'ssm_dt_bias': ssm_dt_bias[j],
                  'ssm_a_log': ssm_a_log[j], 'ssm_d': ssm_d[j], 'ssm_norm_w': ssm_norm_w[j],
                  'nsa_pe_k': nsa_pe_k[j], 'nsa_pe_v': nsa_pe_v[j], 'nsa_ck_w1': nsa_ck_w1[j],
                  'nsa_ck_w2': nsa_ck_w2[j], 'nsa_cv_w1': nsa_cv_w1[j], 'nsa_cv_w2': nsa_cv_w2[j]}
            past = {'ssm_conv': cache_ssm_conv[j], 'ssm': state_ssm[j], 'nsa_kv': cache_nsa_kv[j],
                    'nsa_win': cache_nsa_win_kv[j]}
            segs = _ab_weight_segs(ab_w_in[j])
            w_out = ab_w_out[j].astype(BF16)
            ka = SSM_INNER
            ap, bp, stp = _layer_ab(yp, norm_mix_pre[l], segs, lp, None, None, True, tm)
            as_, bs, sts = _layer_ab(ys, norm_mix_pre[l], segs, lp, past, page_table, False, tm)
            ab_p.append(stp)
            ab_s.append(sts)
        else:
            lp = {'gdn_conv_w': gdn_conv_w[j], 'gdn_dt_bias': gdn_dt_bias[j], 'gdn_a_log': gdn_a_log[j],
                  'gdn_norm_w': gdn_norm_w[j]}
            past = {'gdn_conv': cache_gdn_conv[j], 'gdn': state_gdn[j],
                    'dil': (cache_dil0_kv[j], cache_dil1_kv[j], cache_dil2_kv[j])}
            segs = _cd_weight_segs(cd_w_in[j])
            w_out = cd_w_out[j].astype(BF16)
            ka = GDN_WIDTH
            ap, bp, stp = _layer_cd(yp, norm_mix_pre[l], segs, lp, None, True, tm)
            as_, bs, sts = _layer_cd(ys, norm_mix_pre[l], segs, lp, past, False, tm)
            cd_p.append(stp)
            cd_s.append(sts)
        w1 = mlp_w1[l].astype(BF16)
        w2 = mlp_w2[l].astype(BF16)
        post = functools.partial(_post_block, wo_a=w_out[:ka], wo_b=w_out[ka:], w1=w1, w2=w2, n_mix=norm_mix_post[l],
                                 n_pre=norm_mlp_pre[l], n_post=norm_mlp_post[l], tm=tm)
        yp = post(ap, bp, yp.reshape(-1, D_MODEL)).reshape(yp.shape)
        ys = post(as_, bs, ys.reshape(-1, D_MODEL)).reshape(ys.shape)
    stack = lambda states, i: jnp.stack([s[i] for s in states])
    return (yp, ys,
            stack(ab_p, 0), stack(ab_s, 0), stack(ab_p, 1), stack(ab_s, 1),
            stack(ab_p, 2), stack(ab_s, 2), stack(ab_p, 3), stack(ab_s, 3),
            stack(cd_p, 0), stack(cd_s, 0), stack(cd_p, 1), stack(cd_s, 1),
            stack(cd_p, 2), stack(cd_s, 2), stack(cd_p, 3), stack(cd_s, 3),
            stack(cd_p, 4), stack(cd_s, 4))
```

```python
import functools
import math

import jax
import jax.numpy as jnp
from jax import lax
from jax.experimental import pallas as pl
from jax.experimental.pallas import tpu as pltpu

F32 = jnp.float32
BF16 = jnp.bfloat16

D_MODEL = 1024
HEAD_DIM = 64
CONV_K = 4
RMS_EPS = 1e-6
MLP_HIDDEN = 4 * D_MODEL
QUERY_BLOCK = 128
PAGE_SIZE = 128

SSM_HEADS = 16
SSM_HEAD_DIM = 64
SSM_INNER = SSM_HEADS * SSM_HEAD_DIM
SSM_GROUPS = 2
SSM_STATE = 128
SSM_CONV_DIM = SSM_INNER + 2 * SSM_GROUPS * SSM_STATE
SSM_CHUNK = 128

NSA_HEADS = 16
NSA_KV_HEADS = 2
NSA_REP = NSA_HEADS // NSA_KV_HEADS
NSA_BLOCK = 64
NSA_TOPK = 16
NSA_WINDOW = 512
NSA_CMP_HIDDEN = 256

GDN_HEADS = 8
GDN_HEAD_DIM = 128
GDN_WIDTH = GDN_HEADS * GDN_HEAD_DIM
GDN_CHUNK = 64

DIL_PATTERNS = ((128, 1), (512, 4), (2048, 16))
DIL_GROUPS = len(DIL_PATTERNS)
DIL_HEADS_PER_GROUP = 8
DIL_WIDTH = DIL_HEADS_PER_GROUP * HEAD_DIM
DIL_IN = 3 * DIL_GROUPS * DIL_WIDTH
NEG = -1e30

VMEM_LIMIT_BYTES = 56 * 1024 * 1024
LANE = 128


def _cparams(*sem):
    return pltpu.CompilerParams(dimension_semantics=sem, vmem_limit_bytes=VMEM_LIMIT_BYTES)


def _const_spec(shape):
    nd = len(shape)
    return pl.BlockSpec(shape, lambda *_: (0,) * nd, pipeline_mode=pl.Buffered(1))


def _rms(x, w):
    return x * lax.rsqrt(jnp.mean(x * x, axis=-1, keepdims=True) + RMS_EPS) * w


def _proj_body(x_ref, nw_ref, *refs, n_t):
    n = len(refs) // 2
    xn = _rms(x_ref[...], nw_ref[...]).astype(BF16)
    for k, (w_ref, o_ref) in enumerate(zip(refs[:n], refs[n:])):
        if k >= n - n_t:
            o_ref[...] = lax.dot_general(w_ref[...], xn, _NT, preferred_element_type=F32)
            continue
        y = jnp.dot(xn, w_ref[...], preferred_element_type=F32)
        if len(o_ref.shape) == 2:
            o_ref[...] = y
        else:
            for j in range(o_ref.shape[0]):
                o_ref[j] = y[:, j * LANE:(j + 1) * LANE]


def _norm_proj(x, norm_w, w_segs, tm, plane_segs=(), t_segs=(), seq_len=None):
    n_tok, d = x.shape
    tm = min(tm, n_tok)
    widths = [w.shape[1] for w in w_segs]
    planes = [k in plane_segs for k in range(len(w_segs))]
    out_spec = lambda n, p: (pl.BlockSpec((n // LANE, tm, LANE), lambda i: (0, i, 0)) if p
                             else pl.BlockSpec((tm, n), lambda i: (i, 0)))
    out_sds = lambda n, p: jax.ShapeDtypeStruct((n // LANE, n_tok, LANE) if p else (n_tok, n), F32)
    per_seq = seq_len // tm if t_segs else 1
    t_spec = lambda w: pl.BlockSpec((None, w.shape[0], tm), lambda i: (i // per_seq, 0, i % per_seq))
    t_sds = lambda w: jax.ShapeDtypeStruct((n_tok // seq_len, w.shape[0], seq_len), F32)
    return pl.pallas_call(
        functools.partial(_proj_body, n_t=len(t_segs)),
        grid=(n_tok // tm,),
        in_specs=[pl.BlockSpec((tm, d), lambda i: (i, 0)), _const_spec((1, d))]
        + [_const_spec((d, n)) for n in widths] + [_const_spec(w.shape) for w in t_segs],
        out_specs=[out_spec(n, p) for n, p in zip(widths, planes)] + [t_spec(w) for w in t_segs],
        out_shape=[out_sds(n, p) for n, p in zip(widths, planes)] + [t_sds(w) for w in t_segs],
        compiler_params=_cparams("arbitrary"),
        name="norm_proj",
    )(x, norm_w.reshape(1, d), *w_segs, *t_segs)


def _post_body(a_ref, b_ref, y_ref, woa_ref, wob_ref, w1_ref, w2_ref, nmix_ref, npre_ref, npost_ref, o_ref):
    m = jnp.dot(a_ref[...].astype(BF16), woa_ref[...], preferred_element_type=F32)
    m = m + jnp.dot(b_ref[...].astype(BF16), wob_ref[...], preferred_element_type=F32)
    y1 = y_ref[...] + _rms(m, nmix_ref[...])
    h = _rms(y1, npre_ref[...]).astype(BF16)
    a = jnp.maximum(jnp.dot(h, w1_ref[...], preferred_element_type=F32), 0.0)
    m2 = jnp.dot((a * a).astype(BF16), w2_ref[...], preferred_element_type=F32)
    o_ref[...] = y1 + _rms(m2, npost_ref[...])


def _post_block(mix_a, mix_b, y, wo_a, wo_b, w1, w2, n_mix, n_pre, n_post, tm):
    n_tok, d = y.shape
    tm = min(tm, n_tok)
    ka, kb = mix_a.shape[1], mix_b.shape[1]
    row = lambda n: pl.BlockSpec((tm, n), lambda i: (i, 0))
    return pl.pallas_call(
        _post_body,
        grid=(n_tok // tm,),
        in_specs=[row(ka), row(kb), row(d), _const_spec(wo_a.shape), _const_spec(wo_b.shape),
                  _const_spec(w1.shape), _const_spec(w2.shape), _const_spec((1, d)), _const_spec((1, d)),
                  _const_spec((1, d))],
        out_specs=row(d),
        out_shape=jax.ShapeDtypeStruct((n_tok, d), F32),
        compiler_params=_cparams("arbitrary"),
        name="post_block",
    )(mix_a, mix_b, y, wo_a, wo_b, w1, w2, n_mix.reshape(1, d), n_pre.reshape(1, d), n_post.reshape(1, d))


def _compress_rows(src_ref, pe_ref, w1_ref, w2_ref, nb):
    hd = HEAD_DIM
    low = _half_mask((nb, LANE), 0)
    rows_per_step = 4 * LANE // (2 * hd)

    def pack(a, b):
        top = jnp.where(low, a, pltpu.roll(b, hd, axis=1))
        bot = jnp.where(low, pltpu.roll(a, hd, axis=1), b)
        return jnp.concatenate([top, bot], axis=0)

    def body(i, acc):
        r = i * rows_per_step
        x = [src_ref[pl.ds(r + k, nb, stride=NSA_BLOCK), :] + pe_ref[pl.ds(r + k, 1), :] for k in range(rows_per_step)]
        x4 = jnp.concatenate([pack(x[0], x[1]), pack(x[2], x[3])], axis=1).astype(BF16)
        w4 = w1_ref[pl.ds(r, rows_per_step)].reshape(rows_per_step * hd, NSA_CMP_HIDDEN)
        return acc + jnp.dot(x4, w4, preferred_element_type=F32)

    hid = lax.fori_loop(0, NSA_BLOCK // rows_per_step, body, jnp.zeros((2 * nb, NSA_CMP_HIDDEN), F32), unroll=4)
    hid = hid * jax.nn.sigmoid(hid)
    out = jnp.dot(hid.astype(BF16), w2_ref[...], preferred_element_type=F32)
    return jnp.concatenate([out[:nb], out[nb:]], axis=1)


def _compress_body(kc_ref, vc_ref, pek_ref, pev_ref, wk1_ref, wk2_ref, wv1_ref, wv2_ref, o_ref, *, nb):
    o_ref[:, :LANE] = _compress_rows(kc_ref, pek_ref, wk1_ref, wk2_ref, nb)
    o_ref[:, LANE:] = _compress_rows(vc_ref, pev_ref, wv1_ref, wv2_ref, nb)


def _pe2(pe):
    return jnp.concatenate([pe, pe], axis=1)


def _nsa_compress_prompt(rows, lp):
    bsz, t, _ = rows.shape
    nb = t // NSA_BLOCK
    w1 = lambda w: w.reshape(NSA_BLOCK, HEAD_DIM, NSA_CMP_HIDDEN).astype(BF16)
    col = lambda c: pl.BlockSpec((None, t, LANE), lambda b: (b, 0, c))
    return pl.pallas_call(
        functools.partial(_compress_body, nb=nb),
        grid=(bsz,),
        in_specs=[col(0), col(1), _const_spec((NSA_BLOCK, LANE)), _const_spec((NSA_BLOCK, LANE)),
                  _const_spec((NSA_BLOCK, HEAD_DIM, NSA_CMP_HIDDEN)), _const_spec((NSA_CMP_HIDDEN, HEAD_DIM)),
                  _const_spec((NSA_BLOCK, HEAD_DIM, NSA_CMP_HIDDEN)), _const_spec((NSA_CMP_HIDDEN, HEAD_DIM))],
        out_specs=pl.BlockSpec((None, nb, 2 * LANE), lambda b: (b, 0, 0)),
        out_shape=jax.ShapeDtypeStruct((bsz, nb, 2 * LANE), F32),
        compiler_params=_cparams("arbitrary"),
        name="nsa_compress",
    )(rows, rows, _pe2(lp['nsa_pe_k']), _pe2(lp['nsa_pe_v']), w1(lp['nsa_ck_w1']), lp['nsa_ck_w2'].astype(BF16),
      w1(lp['nsa_cv_w1']), lp['nsa_cv_w2'].astype(BF16))


_NT = (((1,), (1,)), ((), ()))


def _half_mask(shape, g):
    lane = lax.broadcasted_iota(jnp.int32, shape, len(shape) - 1)
    return lane >= HEAD_DIM if g else lane < HEAD_DIM


def _stack_heads(x, heads, g):
    keep = _half_mask((x.shape[0], LANE), g)
    out = []
    for h in heads:
        blk = x[:, (h // 2) * LANE:(h // 2 + 1) * LANE]
        if h % 2 != g:
            blk = pltpu.roll(blk, HEAD_DIM, axis=1)
        out.append(jnp.where(keep, blk, 0.0))
    return jnp.concatenate(out, axis=0)


def _unstack_pair(a, b, g):
    if g == 0:
        b = pltpu.roll(b, HEAD_DIM, axis=1)
    else:
        a = pltpu.roll(a, HEAD_DIM, axis=1)
    return jnp.where(_half_mask(a.shape, 1), b, a)


def _row_gate(sig, lanes, rows):
    return jnp.concatenate([jnp.broadcast_to(sig[:, c:c + 1], (rows, LANE)) for c in lanes], axis=0)


def _nsa_select(p3, pos, nb, n_sel):
    nq = p3.shape[1]
    imp = jnp.sum(p3, axis=0)
    n_i = lax.broadcasted_iota(jnp.int32, (nq, nb), 1)
    cur = pos // NSA_BLOCK
    score = jnp.where(n_i == 0, NSA_REP + 1.0, imp)
    score = jnp.where(n_i == cur, NSA_REP + 1.0, score)
    score = jnp.where(n_i == cur - 1, NSA_REP + 1.0, score)
    score = jnp.where(n_i > cur, -1.0, score)
    rank = jnp.zeros((nq, nb), F32)
    for m in range(nb):
        col = score[:, m:m + 1]
        tie = jnp.where(n_i > m, 1.0, 0.0)
        rank = rank + jnp.where(col > score, 1.0, jnp.where(col == score, tie, 0.0))
    return jnp.where(rank < n_sel, 1.0, 0.0)


def _nsa_select_t(imp, pos, n_sel):
    nb, nq = imp.shape
    n_i = lax.broadcasted_iota(jnp.int32, (nb, nq), 0)
    cur = pos // NSA_BLOCK
    score = jnp.where(n_i == 0, NSA_REP + 1.0, imp)
    score = jnp.where(n_i == cur, NSA_REP + 1.0, score)
    score = jnp.where(n_i == cur - 1, NSA_REP + 1.0, score)
    score = jnp.where(n_i > cur, -1.0, score)
    rank = jnp.zeros((nb, nq), F32)
    for m in range(nb):
        row = score[m:m + 1, :]
        tie = jnp.where(n_i > m, 1.0, 0.0)
        rank = rank + jnp.where(row > score, 1.0, jnp.where(row == score, tie, 0.0))
    return jnp.where(rank < n_sel, 1.0, 0.0)


def _lane_tile(x, n):
    return jnp.concatenate([x] * n, axis=1)


_V_ROWS = HEAD_DIM + 16
_SEL_TILE = 4
_WIN_TILE = 2


def _flash_tile_t(kt, vt_aug, q_t, valid, m_ref, acc_ref, g):
    nk, nq = valid.shape
    rep = q_t.shape[1] // nq
    s = jnp.dot(kt, q_t, preferred_element_type=F32)
    s = jnp.concatenate([jnp.where(valid, s[:, r * nq:(r + 1) * nq], NEG) for r in range(rep)], axis=1)
    m_old = m_ref[g, 0:1, :]
    m_new = jnp.maximum(m_old, jnp.max(s, axis=0, keepdims=True))
    alpha = jnp.exp2(m_old - m_new)
    p = jnp.exp2(s - m_new)
    acc_ref[g] = alpha * acc_ref[g] + jnp.dot(vt_aug, p.astype(BF16), preferred_element_type=F32)
    m_ref[g] = jnp.broadcast_to(m_new, m_ref.shape[1:])


def _nsa_prompt_body(q_ref, small_ref, cmp_ref, ks_ref, vs_ref, kw_ref, vw_ref, o_ref, vst_ref, vwt_ref, m_ref, acc_ref,
                     *, nb, n_sel):
    qb = q_ref.shape[0]
    t = ks_ref.shape[0]
    i = pl.program_id(1)
    q0 = i * qb
    rep = NSA_REP
    hd = HEAD_DIM
    groups = range(NSA_KV_HEADS)

    @pl.when(i == 0)
    def _():
        ones = jnp.ones((_V_ROWS - hd, t), BF16)
        for g in groups:
            vst_ref[g, hd:, :] = ones
            vwt_ref[g, hd:, :] = ones

        def fill(j, carry):
            k0 = pl.multiple_of(j * qb, qb)
            for src, dst in ((vs_ref, vst_ref), (vw_ref, vwt_ref)):
                v_t = src[pl.ds(k0, qb), :].T.astype(BF16)
                for g in groups:
                    dst[g, :hd, pl.ds(k0, qb)] = v_t[g * hd:(g + 1) * hd, :]
            return carry

        lax.fori_loop(0, t // qb, fill, 0)

    pos_r = q0 + lax.broadcasted_iota(jnp.int32, (1, qb), 1)
    sig_t = jax.nn.sigmoid(small_ref[...]).T
    q = q_ref[...] * (hd ** -0.5)
    pairs_t = [q[:, c * LANE:(c + 1) * LANE].T for c in range(NSA_HEADS // 2)]
    head_t = lambda h: pairs_t[h // 2][(h % 2) * hd:(h % 2 + 1) * hd, :]
    zeros = jnp.zeros((hd, rep * qb), F32)
    q_ts, sels = [], []
    for g in groups:
        qg_t = jnp.concatenate([head_t(g * rep + r) for r in range(rep)], axis=1)
        stack = lambda x: jnp.concatenate([x, zeros] if g == 0 else [zeros, x], axis=0).astype(BF16)
        q_t = stack(qg_t)
        q_ts.append(stack(qg_t * math.log2(math.e)))
        gate_row = lambda j: jnp.concatenate(
            [sig_t[SSM_HEADS + j * NSA_HEADS + g * rep + r:SSM_HEADS + j * NSA_HEADS + g * rep + r + 1, :]
             for r in range(rep)], axis=1)
        kc = cmp_ref[:, :LANE].astype(BF16)
        vc_t = cmp_ref[:, LANE:].T[g * hd:(g + 1) * hd, :].astype(BF16)
        s = jnp.dot(kc, q_t, preferred_element_type=F32)
        n_i = lax.broadcasted_iota(jnp.int32, (nb, qb), 0)
        complete = _lane_tile(jnp.where((n_i + 1) * NSA_BLOCK <= pos_r + 1, 1.0, 0.0), rep)
        s = jnp.where(complete > 0.5, s, NEG)
        p = jnp.exp(s - jnp.max(s, axis=0, keepdims=True))
        p = p / jnp.sum(p, axis=0, keepdims=True)
        p = jnp.where(_lane_tile(pos_r, rep) >= NSA_BLOCK - 1, p, 0.0)
        o_cmp = jnp.dot(vc_t, p.astype(BF16), preferred_element_type=F32)
        acc_ref[2 + g, :hd, :] = gate_row(0) * o_cmp
        imp = p[:, :qb]
        for r in range(1, rep):
            imp = imp + p[:, r * qb:(r + 1) * qb]
        sels.append(_nsa_select_t(imp, pos_r, n_sel).astype(BF16))

    def flash_init():
        m_ref[...] = jnp.full(m_ref.shape, NEG, F32)
        acc_ref[0:2] = jnp.zeros((2,) + acc_ref.shape[1:], F32)

    def flash_out(g):
        acc = acc_ref[g]
        return acc[:hd, :] / acc[hd:hd + 1, :]

    def flash_pass(k_ref, vt_ref, kb, first_key, valid_fn):
        kpos_i = lax.broadcasted_iota(jnp.int32, (kb, qb), 0)
        qpos = q0 + lax.broadcasted_iota(jnp.int32, (kb, qb), 1)
        flash_init()

        def step(j, carry):
            k0 = pl.multiple_of(j * kb, kb)
            kt = k_ref[pl.ds(k0, kb), :].astype(BF16)
            valids = valid_fn(k0, k0 + kpos_i, qpos)
            for g in groups:
                _flash_tile_t(kt, vt_ref[g, :, pl.ds(k0, kb)], q_ts[g], valids[g], m_ref, acc_ref, g)
            return carry

        lax.fori_loop(first_key // kb, (q0 + qb - 1) // kb + 1, step, 0)

    tile = lambda n: n * qb if t % (n * qb) == 0 else qb

    def sel_valid(k0, kpos, qpos):
        kb = kpos.shape[0]
        blk = lax.broadcasted_iota(jnp.int32, (kb, nb), 1)
        key = lax.broadcasted_iota(jnp.int32, (kb, nb), 0)
        expand = jnp.where(blk == (k0 + key) // NSA_BLOCK, 1.0, 0.0).astype(BF16)
        causal = kpos <= qpos
        return [jnp.where(causal, jnp.dot(expand, sels[g], preferred_element_type=F32), 0.0) > 0.5 for g in groups]

    flash_pass(ks_ref, vst_ref, tile(_SEL_TILE), 0, sel_valid)
    for g in groups:
        gate_row = jnp.concatenate(
            [sig_t[SSM_HEADS + NSA_HEADS + g * rep + r:SSM_HEADS + NSA_HEADS + g * rep + r + 1, :] for r in range(rep)],
            axis=1)
        acc_ref[2 + g, :hd, :] += gate_row * flash_out(g)
    def win_valid(k0, kpos, qpos):
        diff = qpos - kpos
        valid = jnp.where(diff >= 0, diff, NSA_WINDOW + 1) <= NSA_WINDOW
        return [valid for _ in groups]

    flash_pass(kw_ref, vwt_ref, tile(_WIN_TILE), jnp.maximum(q0 - NSA_WINDOW, 0), win_valid)
    for g in groups:
        gate_row = jnp.concatenate(
            [sig_t[SSM_HEADS + 2 * NSA_HEADS + g * rep + r:SSM_HEADS + 2 * NSA_HEADS + g * rep + r + 1, :]
             for r in range(rep)], axis=1)
        comb = acc_ref[2 + g, :hd, :] + gate_row * flash_out(g)
        for c in range(rep // 2):
            pair = jnp.concatenate([comb[:, (2 * c) * qb:(2 * c + 1) * qb], comb[:, (2 * c + 1) * qb:(2 * c + 2) * qb]],
                                   axis=0)
            col = (g * rep // 2 + c) * LANE
            o_ref[:, col:col + LANE] = pair.T


def _nsa_prompt_attn(q, small, cmp, rows, win, bsz, t):
    qb = math.gcd(t, QUERY_BLOCK)
    nq = t // qb
    nb = cmp.shape[1]
    n_sel = min(NSA_TOPK, nb)
    tok = lambda n: pl.BlockSpec((qb, n), lambda b, i: (b * nq + i, 0))
    seq = lambda c: pl.BlockSpec((None, t, LANE), lambda b, i: (b, 0, c))
    v_t = pltpu.VMEM((NSA_KV_HEADS, _V_ROWS, t), BF16)
    return pl.pallas_call(
        functools.partial(_nsa_prompt_body, nb=nb, n_sel=n_sel),
        grid=(bsz, nq),
        in_specs=[tok(NSA_HEADS * HEAD_DIM), tok(LANE), pl.BlockSpec((None, nb, 2 * LANE), lambda b, i: (b, 0, 0)),
                  seq(2), seq(3), seq(0), seq(1)],
        out_specs=tok(NSA_HEADS * HEAD_DIM),
        out_shape=jax.ShapeDtypeStruct((bsz * t, NSA_HEADS * HEAD_DIM), F32),
        scratch_shapes=[v_t, v_t, pltpu.VMEM((NSA_KV_HEADS, 8, NSA_REP * qb), F32),
                        pltpu.VMEM((2 * NSA_KV_HEADS, _V_ROWS, NSA_REP * qb), F32)],
        compiler_params=_cparams("arbitrary", "arbitrary"),
        name="nsa_prompt_attn",
    )(q, small, cmp, rows, rows, win, win)


_HI = lax.Precision.HIGHEST
_TN = (((0,), (0,)), ((), ()))


def _silu(x):
    return x * jax.nn.sigmoid(x)


def _softplus(x):
    return jnp.maximum(x, 0.0) + jnp.log(1.0 + jnp.exp(-jnp.abs(x)))


def _conv_silu(x_ref, buf_ref, w_ref, b_ref, xp_ref, tail_ref, first):
    n = x_ref.shape[0]

    @pl.when(first)
    def _():
        xp_ref[8 - (CONV_K - 1):8, :] = buf_ref[...]

    xp_ref[8:8 + n, :] = x_ref[...]
    y = xp_ref[8:8 + n, :] * w_ref[CONV_K - 1:CONV_K, :]
    for k in range(CONV_K - 1):
        y = y + xp_ref[5 + k:5 + k + n, :] * w_ref[k:k + 1, :]
    if b_ref is not None:
        y = y + b_ref[...]
    tail = xp_ref[8 + n - (CONV_K - 1):8 + n, :]
    tail_ref[...] = tail
    xp_ref[8 - (CONV_K - 1):8, :] = tail
    return _silu(y)


def _cumsum_rows(x):
    n = x.shape[0]
    tri = jnp.where(lax.broadcasted_iota(jnp.int32, (n, n), 0) >= lax.broadcasted_iota(jnp.int32, (n, n), 1), 1.0, 0.0)
    return jnp.dot(tri, x, preferred_element_type=F32, precision=_HI)


def _expand_heads(x, width, lane0=0, n_out=D_MODEL):
    h_i = lax.broadcasted_iota(jnp.int32, (LANE, n_out), 0)
    c_i = lax.broadcasted_iota(jnp.int32, (LANE, n_out), 1)
    sel = jnp.where(c_i // width + lane0 == h_i, 1.0, 0.0)
    return jnp.dot(x, sel, preferred_element_type=F32, precision=_HI)


def _decay_matrix(col, row, strict=False):
    n = col.shape[0]
    i = lax.broadcasted_iota(jnp.int32, (n, n), 0)
    j = lax.broadcasted_iota(jnp.int32, (n, n), 1)
    keep = (i > j) if strict else (i >= j)
    return jnp.exp(jnp.where(keep, col - row, NEG))


def _ssd_body(xbc_ref, z_ref, small_ref, buf_ref, h0_ref, cw_ref, cb_ref, dtb_ref, alog_ref, dx_ref, nw_ref,
              y_ref, conv_ref, h_ref, xp_ref):
    c = pl.program_id(1)
    n = xbc_ref.shape[0]
    hp = SSM_HEAD_DIM
    rep = SSM_HEADS // SSM_GROUPS

    @pl.when(c == 0)
    def _():
        h_ref[...] = h0_ref[...]

    act = _conv_silu(xbc_ref, buf_ref, cw_ref, cb_ref, xp_ref, conv_ref, c == 0)
    xs = act[:, :SSM_INNER]
    head_lane = lax.broadcasted_iota(jnp.int32, (n, LANE), 1) < SSM_HEADS
    dt = jnp.where(head_lane, _softplus(small_ref[...] + dtb_ref[...]), 0.0)
    la = dt * (-jnp.exp(alog_ref[...]))
    acs = _cumsum_rows(la)
    acs_t = acs.T
    xd = xs * _expand_heads(dt, hp)
    e_acs = _expand_heads(jnp.exp(acs), hp)
    xdd = (xd * _expand_heads(jnp.exp(acs[n - 1:n, :] - acs), hp)).astype(BF16)
    xd = xd.astype(BF16)
    lane2 = _half_mask((n, LANE), 1)
    for g in range(SSM_GROUPS):
        bc = act[:, SSM_INNER + g * SSM_STATE:SSM_INNER + (g + 1) * SSM_STATE].astype(BF16)
        cc = act[:, SSM_INNER + (SSM_GROUPS + g) * SSM_STATE:SSM_INNER + (SSM_GROUPS + g + 1) * SSM_STATE].astype(BF16)
        cb = lax.dot_general(cc, bc, _NT, preferred_element_type=F32)
        h_prev = h_ref[g * rep:(g + 1) * rep].reshape(rep * hp, SSM_STATE)
        y_off = lax.dot_general(cc, h_prev.astype(BF16), _NT, preferred_element_type=F32)
        for pair in range(rep // 2):
            halves = []
            for k in range(2):
                h = g * rep + 2 * pair + k
                lm = _decay_matrix(acs[:, h:h + 1], acs_t[h:h + 1, :])
                halves.append(jnp.dot((cb * lm).astype(BF16), xd[:, (h // 2) * LANE:(h // 2 + 1) * LANE],
                                      preferred_element_type=F32))
            col = (g * rep + 2 * pair) * hp
            y_ref[:, col:col + LANE] = jnp.where(lane2, halves[1], halves[0]) + y_off[:, 2 * pair * hp:2 * pair * hp + LANE] * e_acs[:, col:col + LANE]
        st = lax.dot_general(xdd[:, g * rep * hp:(g + 1) * rep * hp], bc, _TN, preferred_element_type=F32)
        for r in range(rep):
            h = g * rep + r
            dec = jnp.exp(acs_t[h:h + 1, n - 1:n])
            h_ref[h] = h_ref[h] * dec + st[r * hp:(r + 1) * hp, :]
    y = y_ref[...] + dx_ref[...] * xs
    y_ref[...] = _rms(y * _silu(z_ref[...]), nw_ref[...])


def _ssd_mixer(xbc, z, small, conv_buf, h0, lp, bsz, t, chunk):
    nc = t // chunk
    tok = lambda n: pl.BlockSpec((chunk, n), lambda b, c: (b * nc + c, 0))
    per_b = lambda shape: pl.BlockSpec((None,) + shape, lambda b, c: (b,) + (0,) * len(shape))
    pad = lambda v: jnp.pad(v.astype(F32), (0, LANE - v.shape[0])).reshape(1, LANE)
    cdim = SSM_CONV_DIM
    state = (SSM_HEADS, SSM_HEAD_DIM, SSM_STATE)
    return pl.pallas_call(
        _ssd_body,
        grid=(bsz, nc),
        in_specs=[tok(cdim), tok(SSM_INNER), tok(LANE), per_b((CONV_K - 1, cdim)), per_b(state),
                  _const_spec((CONV_K, cdim)), _const_spec((1, cdim)), _const_spec((1, LANE)), _const_spec((1, LANE)),
                  _const_spec((1, SSM_INNER)), _const_spec((1, SSM_INNER))],
        out_specs=[tok(SSM_INNER), per_b((CONV_K - 1, cdim)), per_b(state)],
        out_shape=[jax.ShapeDtypeStruct((bsz * t, SSM_INNER), F32),
                   jax.ShapeDtypeStruct((bsz, CONV_K - 1, cdim), F32),
                   jax.ShapeDtypeStruct((bsz,) + state, F32)],
        scratch_shapes=[pltpu.VMEM((chunk + 8, cdim), F32)],
        compiler_params=_cparams("arbitrary", "arbitrary"),
        name="ssd_mixer",
    )(xbc, z, small, conv_buf, h0, lp['ssm_conv_w'], lp['ssm_conv_b'].reshape(1, cdim), pad(lp['ssm_dt_bias']),
      pad(lp['ssm_a_log']), jnp.repeat(lp['ssm_d'].astype(F32), SSM_HEAD_DIM).reshape(1, SSM_INNER),
      lp['ssm_norm_w'].reshape(1, SSM_INNER))


def _split_bf16(x):
    hi = x.astype(BF16)
    return hi, (x - hi.astype(F32)).astype(BF16)


def _bmm(a, b):
    return jnp.einsum('hmk,hkn->hmn', a, b, preferred_element_type=F32)


def _bmm_nt(a, b):
    return jnp.einsum('hmk,hnk->hmn', a, b, preferred_element_type=F32)


def _bmm_tn(a, b):
    return jnp.einsum('hkm,hkn->hmn', a, b, preferred_element_type=F32)


def _bmm_split(a, b):
    a_hi, a_lo = _split_bf16(a)
    b_hi, b_lo = _split_bf16(b)
    return _bmm(a_hi, b_hi) + _bmm(a_hi, b_lo) + _bmm(a_lo, b_hi)


def _unit_lower_inverse(a):
    n = a.shape[-1]
    eye = jnp.where(lax.broadcasted_iota(jnp.int32, (n, n), 0) == lax.broadcasted_iota(jnp.int32, (n, n), 1), 1.0, 0.0)
    p = -a
    t = eye + p
    span = 2
    while span < n:
        p = _bmm_split(p, p)
        t = t + _bmm_split(t, p)
        span *= 2
    return t


def _gdn_body(qkv_ref, z_ref, small_ref, buf_ref, s0_ref, cw_ref, alog_ref, dtb_ref, nw_ref,
              y_ref, conv_ref, s_ref, xp_ref):
    c = pl.program_id(1)
    n = qkv_ref.shape[0]
    dk = GDN_HEAD_DIM
    nh = GDN_HEADS

    @pl.when(c == 0)
    def _():
        s_ref[...] = s0_ref[...]

    act = _conv_silu(qkv_ref, buf_ref, cw_ref, None, xp_ref, conv_ref, c == 0)
    lane = lax.broadcasted_iota(jnp.int32, (n, LANE), 1)
    raw = small_ref[...]
    beta = jax.nn.sigmoid(raw)
    g = jnp.where((lane >= nh) & (lane < 2 * nh), -jnp.exp(alog_ref[...]) * _softplus(raw + dtb_ref[...]), 0.0)
    gc = _cumsum_rows(g)
    gc_t = gc.T
    beta_x = _expand_heads(beta, dk)
    egc_x = _expand_heads(jnp.exp(gc), dk, nh)
    edec_x = _expand_heads(jnp.exp(gc[n - 1:n, :] - gc), dk, nh)
    ii = lax.broadcasted_iota(jnp.int32, (n, n), 0)
    jj = lax.broadcasted_iota(jnp.int32, (n, n), 1)
    heads = lambda x, base=0: jnp.stack([x[:, base + h * dk:base + (h + 1) * dk] for h in range(nh)])
    q, k, v = heads(act), heads(act, nh * dk), heads(act, 2 * nh * dk)
    q = q * lax.rsqrt(jnp.sum(q * q, axis=-1, keepdims=True) + 1e-6) * (dk ** -0.5)
    k = k * lax.rsqrt(jnp.sum(k * k, axis=-1, keepdims=True) + 1e-6)
    beta3, egc, edec = heads(beta_x), heads(egc_x), heads(edec_x)
    kb = k * beta3
    decay = jnp.stack([_decay_matrix(gc[:, nh + h:nh + h + 1], gc_t[nh + h:nh + h + 1, :]) for h in range(nh)])
    k16 = k.astype(BF16)
    amat = jnp.where(ii > jj, _bmm_nt(kb.astype(BF16), k16) * decay, 0.0)
    tmat = _unit_lower_inverse(amat).astype(BF16)
    u = _bmm(tmat, (v * beta3).astype(BF16))
    w = _bmm(tmat, (kb * egc).astype(BF16))
    qk = _bmm_nt(q.astype(BF16), k16) * decay
    s = s_ref[...]
    s16 = s.astype(BF16)
    v16 = (u - _bmm(w.astype(BF16), s16)).astype(BF16)
    o = _bmm((q * egc).astype(BF16), s16) + _bmm(qk.astype(BF16), v16)
    gl = jnp.stack([jnp.exp(gc_t[nh + h:nh + h + 1, n - 1:n]) for h in range(nh)])
    s_ref[...] = s * gl + _bmm_tn((k * edec).astype(BF16), v16)
    y = _rms(o, nw_ref[...])
    for h in range(nh):
        sl = slice(h * dk, (h + 1) * dk)
        y_ref[:, sl] = y[h] * _silu(z_ref[:, sl])


def _gdn_mixer(qkv, z, small, conv_buf, s0, lp, bsz, t, chunk, dil_decode=None):
    nc = t // chunk
    tok = lambda n: pl.BlockSpec((chunk, n), lambda b, c: (b * nc + c, 0))
    per_b = lambda shape: pl.BlockSpec((None,) + shape, lambda b, c: (b,) + (0,) * len(shape))
    pad8 = lambda v: jnp.pad(v.astype(F32), (GDN_HEADS, LANE - 2 * GDN_HEADS)).reshape(1, LANE)
    cdim = 3 * GDN_WIDTH
    state = (GDN_HEADS, GDN_HEAD_DIM, GDN_HEAD_DIM)
    operands = [qkv, z, small, conv_buf, s0, lp['gdn_conv_w'], pad8(lp['gdn_a_log']), pad8(lp['gdn_dt_bias']),
                lp['gdn_norm_w'].reshape(1, GDN_HEAD_DIM)]
    in_specs = [tok(cdim), tok(GDN_WIDTH), tok(LANE), per_b((CONV_K - 1, cdim)), per_b(state),
                _const_spec((CONV_K, cdim)), _const_spec((1, LANE)), _const_spec((1, LANE)),
                _const_spec((1, GDN_HEAD_DIM))]
    out_specs = [tok(GDN_WIDTH), per_b((CONV_K - 1, cdim)), per_b(state)]
    out_shape = [jax.ShapeDtypeStruct((bsz * t, GDN_WIDTH), F32),
                 jax.ShapeDtypeStruct((bsz, CONV_K - 1, cdim), F32),
                 jax.ShapeDtypeStruct((bsz,) + state, F32)]
    body, name = _gdn_body, "gdn_mixer"
    if dil_decode is not None:
        assert nc == 1
        d_ops, d_in, d_out, d_shape, unview = _dil_decode_operands(*dil_decode, bsz, t)
        operands, in_specs = operands + d_ops, in_specs + d_in
        out_specs, out_shape = out_specs + d_out, out_shape + d_shape
        body, name = _gdn_dil_decode_body, "gdn_dil_decode"
    outs = pl.pallas_call(
        body,
        grid=(bsz, nc),
        in_specs=in_specs,
        out_specs=out_specs,
        out_shape=out_shape,
        scratch_shapes=[pltpu.VMEM((chunk + 8, cdim), F32)],
        compiler_params=_cparams("arbitrary", "arbitrary"),
        name=name,
    )(*operands)
    if dil_decode is None:
        return outs
    return list(outs[:4]) + [[unview(v) for v in outs[4:]]]


def _softmax_pv(s, valid, v16, n_rep=1):
    rows, nk = s.shape
    s = jnp.where(valid[None], s.reshape(n_rep, rows // n_rep, nk), NEG).reshape(rows, nk)
    m = jnp.max(s, axis=-1, keepdims=True)
    p = jnp.exp(s - m)
    l = jnp.sum(p, axis=-1, keepdims=True)
    return jnp.dot((p / l).astype(BF16), v16, preferred_element_type=F32), m + jnp.log(l)


_DIL_PAIRS = DIL_HEADS_PER_GROUP // 2


def _dil_prompt_body(q_ref, kp_ref, kc_ref, vp_ref, vc_ref, o_ref, lse_ref, *, window, step):
    i = pl.program_id(1)
    qb = q_ref.shape[1] // step
    scale = HEAD_DIM ** -0.5
    qpos = i * qb + lax.broadcasted_iota(jnp.int32, (qb, 2 * qb), 0)
    kpos = (i - 1) * qb + lax.broadcasted_iota(jnp.int32, (qb, 2 * qb), 1)
    diff = jnp.where(kpos >= 0, qpos - kpos, -1)
    valid = jnp.where(diff >= 0, diff, window + 1) <= window
    lower = _half_mask((qb, LANE), 0)

    def one_class(c, carry):
        rows = pl.ds(c, qb, stride=step)
        for pr in range(q_ref.shape[0]):
            q = q_ref[pr, rows, :] * scale
            kk = jnp.concatenate([kp_ref[pr, rows, :], kc_ref[pr, rows, :]], axis=0).astype(BF16)
            vv = jnp.concatenate([vp_ref[pr, rows, :], vc_ref[pr, rows, :]], axis=0).astype(BF16)
            o2, l2 = [], []
            for k in range(2):
                qh = jnp.where(_half_mask((qb, LANE), k), q, 0.0).astype(BF16)
                s = lax.dot_general(qh, kk, _NT, preferred_element_type=F32)
                o, lse = _softmax_pv(s, valid, vv)
                o2.append(o)
                l2.append(jnp.broadcast_to(lse, (qb, LANE)))
            o_ref[pr, rows, :] = jnp.where(lower, o2[0], o2[1])
            lse_ref[pr, rows, :] = jnp.where(lower, l2[0], l2[1])
        return carry

    lax.fori_loop(0, step, one_class, 0)


def _dil_prompt_attn(dil, gi, bsz, t):
    win, step = DIL_PATTERNS[gi]
    assert win % step == 0 and t % win == 0
    tile = win
    nq = t // tile
    pp = _DIL_PAIRS if _DIL_PAIRS * tile * LANE * 4 <= (1 << 20) else 1
    blk = lambda part, prev: pl.BlockSpec(
        (pp, tile, LANE),
        lambda b, i, h: ((part * DIL_GROUPS + gi) * (_DIL_PAIRS // pp) + h,
                         b * nq + (jnp.maximum(i - 1, 0) if prev else i), 0))
    out_spec = pl.BlockSpec((pp, tile, LANE), lambda b, i, h: (h, b * nq + i, 0))
    out_sds = jax.ShapeDtypeStruct((_DIL_PAIRS, bsz * t, LANE), F32)
    return pl.pallas_call(
        functools.partial(_dil_prompt_body, window=win // step, step=step),
        grid=(bsz, nq, _DIL_PAIRS // pp),
        in_specs=[blk(0, False), blk(1, True), blk(1, False), blk(2, True), blk(2, False)],
        out_specs=[out_spec, out_spec],
        out_shape=[out_sds, out_sds],
        compiler_params=_cparams("arbitrary", "arbitrary", "arbitrary"),
        name=f"dil_prompt_attn_{gi}",
    )(dil, dil, dil, dil, dil)


def _dil_combine_body(o0, o1, o2, l0, l1, l2, y_ref):
    for pr in range(_DIL_PAIRS):
        m = jnp.maximum(jnp.maximum(l0[pr], l1[pr]), l2[pr])
        e0, e1, e2 = jnp.exp(l0[pr] - m), jnp.exp(l1[pr] - m), jnp.exp(l2[pr] - m)
        den = e0 + e1 + e2
        y_ref[:, pr * LANE:(pr + 1) * LANE] = (e0 / den) * o0[pr] + (e1 / den) * o1[pr] + (e2 / den) * o2[pr]


def _dil_combine(outs, lses, tm):
    n_tok = outs[0].shape[1]
    spec = pl.BlockSpec((_DIL_PAIRS, tm, LANE), lambda i: (0, i, 0))
    return pl.pallas_call(
        _dil_combine_body,
        grid=(n_tok // tm,),
        in_specs=[spec] * 6,
        out_specs=pl.BlockSpec((tm, DIL_WIDTH), lambda i: (i, 0)),
        out_shape=jax.ShapeDtypeStruct((n_tok, DIL_WIDTH), F32),
        compiler_params=_cparams("arbitrary"),
        name="dil_combine",
    )(*outs, *lses)


def _attend_cached(qg, k_t, v_t, k_n, v_n, valid_c, valid_n, n_rep=1):
    rows = qg.shape[0]
    mask = lambda s, v: jnp.where(v[None], s.reshape(n_rep, rows // n_rep, s.shape[1]), NEG).reshape(rows, s.shape[1])
    s_c = mask(jnp.dot(qg, k_t, preferred_element_type=F32), valid_c)
    s_n = mask(lax.dot_general(qg, k_n, _NT, preferred_element_type=F32), valid_n)
    m = jnp.maximum(jnp.max(s_c, axis=-1, keepdims=True), jnp.max(s_n, axis=-1, keepdims=True))
    p_c = jnp.exp(s_c - m)
    p_n = jnp.exp(s_n - m)
    l = jnp.sum(p_c, axis=-1, keepdims=True) + jnp.sum(p_n, axis=-1, keepdims=True)
    o = lax.dot_general((p_c / l).astype(BF16), v_t, _NT, preferred_element_type=F32)
    o = o + jnp.dot((p_n / l).astype(BF16), v_n, preferred_element_type=F32)
    return o, m + jnp.log(l)


def _nsa_decode_body(pt_ref, q_ref, small_ref, rows_ref, win_ref, *refs, n_pages, lw, nb, nb_pad, n_sel):
    del pt_ref, nb
    pages = refs[:n_pages]
    (pastwin_ref, pek_ref, pev_ref, wk1_ref, wk2_ref, wv1_ref, wv2_ref, o_ref, neww_ref,
     kc_s, vc_s, comb_s) = refs[n_pages:]
    t = q_ref.shape[0]
    rep = NSA_REP
    past_len = n_pages * PAGE_SIZE
    for dst, c in ((kc_s, 0), (vc_s, 1)):
        for p in range(n_pages):
            dst[p * PAGE_SIZE:(p + 1) * PAGE_SIZE, :] = pages[p][c * LANE:(c + 1) * LANE, :].T
        dst[past_len:past_len + t, :] = rows_ref[:, c * LANE:(c + 1) * LANE]
        dst[past_len + t:, :] = jnp.zeros((dst.shape[0] - past_len - t, LANE), F32)
    kcmp = _compress_rows(kc_s, pek_ref, wk1_ref, wk2_ref, nb_pad).astype(BF16)
    vcmp = _compress_rows(vc_s, pev_ref, wv1_ref, wv2_ref, nb_pad).astype(BF16)
    page_rows = lambda c: jnp.concatenate([pages[p][c * LANE:(c + 1) * LANE, :] for p in range(n_pages)],
                                          axis=1).astype(BF16)
    ks_t, vs_t = page_rows(2), page_rows(3)
    ks_n, vs_n = rows_ref[:, 2 * LANE:3 * LANE].astype(BF16), rows_ref[:, 3 * LANE:].astype(BF16)
    kw_t, vw_t = pastwin_ref[:LANE, :].astype(BF16), pastwin_ref[LANE:, :].astype(BF16)
    kw_n, vw_n = win_ref[:, :LANE].astype(BF16), win_ref[:, LANE:].astype(BF16)
    neww_ref[...] = jnp.concatenate([pastwin_ref[:, t:], win_ref[...].T], axis=1)
    pos_c = past_len + lax.broadcasted_iota(jnp.int32, (t, 1), 0)
    new_i = lax.broadcasted_iota(jnp.int32, (t, t), 1)
    back = lax.broadcasted_iota(jnp.int32, (t, t), 0) - new_i
    sig = jax.nn.sigmoid(small_ref[...])
    q = q_ref[...] * (HEAD_DIM ** -0.5)
    for g in range(NSA_KV_HEADS):
        qg = _stack_heads(q, range(g * rep, (g + 1) * rep), g).astype(BF16)
        gate_lane = lambda j: [SSM_HEADS + j * NSA_HEADS + g * rep + r for r in range(rep)]
        s = lax.dot_general(qg, kcmp, _NT, preferred_element_type=F32).reshape(rep, t, nb_pad)
        n_i = lax.broadcasted_iota(jnp.int32, (t, nb_pad), 1)
        complete = (n_i + 1) * NSA_BLOCK <= pos_c + 1
        s = jnp.where(complete[None], s, NEG)
        p = jnp.exp(s - jnp.max(s, axis=-1, keepdims=True))
        p = p / jnp.sum(p, axis=-1, keepdims=True)
        p = jnp.where((pos_c >= NSA_BLOCK - 1)[None], p, 0.0)
        o_cmp = jnp.dot(p.reshape(rep * t, nb_pad).astype(BF16), vcmp, preferred_element_type=F32)
        comb = _row_gate(sig, gate_lane(0), t) * o_cmp
        sel = _nsa_select(p, pos_c, nb_pad, n_sel).astype(BF16)
        expand = lambda n_keys, first: jnp.where(
            lax.broadcasted_iota(jnp.int32, (nb_pad, n_keys), 0)
            == (first + lax.broadcasted_iota(jnp.int32, (nb_pad, n_keys), 1)) // NSA_BLOCK, 1.0, 0.0).astype(BF16)
        chosen_c = jnp.dot(sel, expand(past_len, 0), preferred_element_type=F32)
        chosen_n = jnp.dot(sel, expand(t, past_len), preferred_element_type=F32)
        kpos = lax.broadcasted_iota(jnp.int32, (t, past_len), 1)
        valid_c = jnp.where(kpos <= pos_c, chosen_c, 0.0) > 0.5
        valid_n = jnp.where(back >= 0, chosen_n, 0.0) > 0.5
        o_sel, _ = _attend_cached(qg, ks_t, vs_t, ks_n, vs_n, valid_c, valid_n, rep)
        comb = comb + _row_gate(sig, gate_lane(1), t) * o_sel
        diff = pos_c - (past_len - lw + lax.broadcasted_iota(jnp.int32, (t, lw), 1))
        valid_c = jnp.where(diff >= 0, diff, NSA_WINDOW + 1) <= NSA_WINDOW
        valid_n = jnp.where(back >= 0, back, NSA_WINDOW + 1) <= NSA_WINDOW
        o_win, _ = _attend_cached(qg, kw_t, vw_t, kw_n, vw_n, valid_c, valid_n, rep)
        comb_s[...] = comb + _row_gate(sig, gate_lane(2), t) * o_win
        for c in range(rep // 2):
            a = comb_s[(2 * c) * t:(2 * c + 1) * t, :]
            b = comb_s[(2 * c + 1) * t:(2 * c + 2) * t, :]
            col = (g * rep // 2 + c) * LANE
            o_ref[:, col:col + LANE] = _unstack_pair(a, b, g)


def _nsa_decode_attn(q, small, rows, win, cache_kv, cache_win, page_table, lp, bsz, t):
    n_pages = page_table.shape[1]
    past_len = n_pages * PAGE_SIZE
    lw = cache_win.shape[1]
    nb = -(-(past_len + t) // NSA_BLOCK)
    nb_pad = -(-nb // 8) * 8
    pages = jnp.transpose(cache_kv, (0, 2, 3, 4, 1)).reshape(cache_kv.shape[0], 4 * LANE, PAGE_SIZE)
    pastwin = jnp.transpose(cache_win, (0, 2, 3, 4, 1)).reshape(bsz, 2 * LANE, lw)
    w1 = lambda w: w.reshape(NSA_BLOCK, HEAD_DIM, NSA_CMP_HIDDEN).astype(BF16)
    tok = lambda n: pl.BlockSpec((t, n), lambda b, pt: (b, 0))
    const = lambda shape: pl.BlockSpec(shape, lambda b, pt: (0,) * len(shape), pipeline_mode=pl.Buffered(1))
    page_spec = lambda p: pl.BlockSpec((None, 4 * LANE, PAGE_SIZE), lambda b, pt: (pt[b * n_pages + p], 0, 0))
    win_spec = pl.BlockSpec((None, 2 * LANE, lw), lambda b, pt: (b, 0, 0))
    grid_spec = pltpu.PrefetchScalarGridSpec(
        num_scalar_prefetch=1,
        grid=(bsz,),
        in_specs=[tok(NSA_HEADS * HEAD_DIM), tok(LANE), tok(4 * LANE), tok(2 * LANE)]
        + [page_spec(p) for p in range(n_pages)]
        + [win_spec, const((NSA_BLOCK, LANE)), const((NSA_BLOCK, LANE)),
           const((NSA_BLOCK, HEAD_DIM, NSA_CMP_HIDDEN)), const((NSA_CMP_HIDDEN, HEAD_DIM)),
           const((NSA_BLOCK, HEAD_DIM, NSA_CMP_HIDDEN)), const((NSA_CMP_HIDDEN, HEAD_DIM))],
        out_specs=[tok(NSA_HEADS * HEAD_DIM), win_spec],
        scratch_shapes=[pltpu.VMEM((nb_pad * NSA_BLOCK, LANE), F32), pltpu.VMEM((nb_pad * NSA_BLOCK, LANE), F32),
                        pltpu.VMEM((NSA_REP * t, LANE), F32)],
    )
    y, new_win = pl.pallas_call(
        functools.partial(_nsa_decode_body, n_pages=n_pages, lw=lw, nb=nb, nb_pad=nb_pad, n_sel=min(NSA_TOPK, nb)),
        grid_spec=grid_spec,
        out_shape=[jax.ShapeDtypeStruct((bsz * t, NSA_HEADS * HEAD_DIM), F32),
                   jax.ShapeDtypeStruct((bsz, 2 * LANE, lw), F32)],
        compiler_params=_cparams("arbitrary"),
        name="nsa_decode_attn",
    )(page_table.reshape(-1), q, small, rows, win, *([pages] * n_pages), pastwin,
      _pe2(lp['nsa_pe_k']), _pe2(lp['nsa_pe_v']), w1(lp['nsa_ck_w1']), lp['nsa_ck_w2'].astype(BF16),
      w1(lp['nsa_cv_w1']), lp['nsa_cv_w2'].astype(BF16))
    new_win = jnp.transpose(new_win.reshape(bsz, 2, NSA_KV_HEADS, HEAD_DIM, lw), (0, 4, 1, 2, 3))
    return y, new_win


def _dil_decode_group(q, k_new, v_new, cache_ref, step):
    t = q.shape[0]
    lg = cache_ref.shape[1]
    pairs = DIL_HEADS_PER_GROUP // 2
    tok_c = lax.broadcasted_iota(jnp.int32, (2 * t, lg), 0) % t
    ahead = lax.broadcasted_iota(jnp.int32, (2 * t, lg), 1) - tok_c
    valid_c = jnp.where(ahead >= 0, ahead % step, 1) == 0
    tok_n = lax.broadcasted_iota(jnp.int32, (2 * t, t), 0) % t
    back = tok_n - lax.broadcasted_iota(jnp.int32, (2 * t, t), 1)
    valid_n = jnp.where(back >= 0, back % step, 1) == 0
    lower = _half_mask((t, LANE), 0)
    outs, lses = [], []
    for pr in range(pairs):
        sl = slice(pr * LANE, (pr + 1) * LANE)
        k_t = cache_ref[pr * LANE:(pr + 1) * LANE, :].astype(BF16)
        v_t = cache_ref[DIL_WIDTH + pr * LANE:DIL_WIDTH + (pr + 1) * LANE, :].astype(BF16)
        kn = k_new[:, sl].astype(BF16)
        vn = v_new[:, sl].astype(BF16)
        qp = q[:, sl]
        q2 = jnp.concatenate([jnp.where(lower, qp, 0.0), jnp.where(lower, 0.0, qp)], axis=0).astype(BF16)
        o, lse = _attend_cached(q2, k_t, v_t, kn, vn, valid_c, valid_n)
        lse = jnp.broadcast_to(lse, (2 * t, LANE))
        outs.append(jnp.where(lower, o[:t], o[t:]))
        lses.append(jnp.where(lower, lse[:t], lse[t:]))
    return jnp.concatenate(outs, axis=1), jnp.concatenate(lses, axis=1)


def _dil_decode_body(x_ref, c0_ref, c1_ref, c2_ref, y_ref, n0_ref, n1_ref, n2_ref):
    scale = HEAD_DIM ** -0.5
    t = x_ref.shape[1]
    res = []
    for gi, (cache_ref, new_ref) in enumerate(((c0_ref, n0_ref), (c1_ref, n1_ref), (c2_ref, n2_ref))):
        _, step = DIL_PATTERNS[gi]
        part = lambda p: jnp.concatenate(
            [x_ref[(p * DIL_GROUPS + gi) * _DIL_PAIRS + pr] for pr in range(_DIL_PAIRS)], axis=1)
        res.append(_dil_decode_group(part(0) * scale, part(1), part(2), cache_ref, step))
        new_t = jnp.concatenate([part(1).T, part(2).T], axis=0)
        new_ref[...] = jnp.concatenate([cache_ref[:, t:], new_t], axis=1)
    (o0, l0), (o1, l1), (o2, l2) = res
    m = jnp.maximum(jnp.maximum(l0, l1), l2)
    e0, e1, e2 = jnp.exp(l0 - m), jnp.exp(l1 - m), jnp.exp(l2 - m)
    den = e0 + e1 + e2
    y_ref[...] = (e0 / den) * o0 + (e1 / den) * o1 + (e2 / den) * o2


def _dil_decode_operands(dil, bufs, bsz, t):
    n_rows = 2 * DIL_WIDTH
    views = []
    for (win, _), buf in zip(DIL_PATTERNS, bufs):
        assert buf.shape[1] == win, "decode path needs a full window of cached rows"
        views.append(jnp.transpose(buf, (0, 2, 3, 4, 1)).reshape(bsz, n_rows, win))
    cache_spec = lambda v: pl.BlockSpec((None, n_rows, v.shape[2]), lambda b, *_: (b, 0, 0))
    in_specs = [pl.BlockSpec((DIL_IN // LANE, t, LANE), lambda b, *_: (0, b, 0))] + [cache_spec(v) for v in views]
    out_specs = [pl.BlockSpec((t, DIL_WIDTH), lambda b, *_: (b, 0))] + [cache_spec(v) for v in views]
    out_shape = [jax.ShapeDtypeStruct((bsz * t, DIL_WIDTH), F32)] + [jax.ShapeDtypeStruct(v.shape, F32) for v in views]
    unview = lambda v: jnp.transpose(v.reshape(bsz, 2, DIL_HEADS_PER_GROUP, HEAD_DIM, v.shape[2]), (0, 4, 1, 2, 3))
    return [dil] + views, in_specs, out_specs, out_shape, unview


def _gdn_dil_decode_body(*refs):
    n_gdn_in, n_dil_in, n_gdn_out, n_dil_out = 9, 4, 3, 4
    gdn_in, rest = refs[:n_gdn_in], refs[n_gdn_in:]
    dil_in, rest = rest[:n_dil_in], rest[n_dil_in:]
    gdn_out, rest = rest[:n_gdn_out], rest[n_gdn_out:]
    dil_out, scratch = rest[:n_dil_out], rest[n_dil_out:]
    _gdn_body(*gdn_in, *gdn_out, *scratch)
    _dil_decode_body(*dil_in, *dil_out)


_AB_CUTS = (0, SSM_INNER, SSM_INNER + SSM_CONV_DIM, SSM_INNER + SSM_CONV_DIM + SSM_HEADS)
_AB_Q0 = _AB_CUTS[3]
_AB_KV0 = _AB_Q0 + NSA_HEADS * HEAD_DIM
_AB_WIN0 = _AB_KV0 + 4 * NSA_KV_HEADS * HEAD_DIM
_AB_GATE0 = _AB_WIN0 + 2 * NSA_KV_HEADS * HEAD_DIM
_AB_END = _AB_GATE0 + 3 * NSA_HEADS


def _pad_cols(w, n):
    return jnp.pad(w, ((0, 0), (0, n - w.shape[1])))


def _ab_weight_segs(w_in):
    small = jnp.concatenate([w_in[:, _AB_CUTS[2]:_AB_CUTS[3]], w_in[:, _AB_GATE0:_AB_END]], axis=1)
    segs = [w_in[:, _AB_CUTS[0]:_AB_CUTS[1]], w_in[:, _AB_CUTS[1]:_AB_CUTS[2]], w_in[:, _AB_Q0:_AB_KV0],
            w_in[:, _AB_KV0:_AB_WIN0], w_in[:, _AB_WIN0:_AB_GATE0], _pad_cols(small, LANE)]
    return [s.astype(BF16) for s in segs]


_CD_Z0 = 3 * GDN_WIDTH
_CD_B0 = _CD_Z0 + GDN_WIDTH
_CD_DIL0 = _CD_B0 + 2 * GDN_HEADS
_CD_END = _CD_DIL0 + DIL_IN


def _cd_weight_segs(w_in):
    segs = [w_in[:, :_CD_Z0], w_in[:, _CD_Z0:_CD_B0], w_in[:, _CD_DIL0:_CD_END],
            _pad_cols(w_in[:, _CD_B0:_CD_DIL0], LANE)]
    return [s.astype(BF16) for s in segs]


def _layer_ab(y, norm_pre, segs, lp, past, page_table, is_prompt, tm):
    bsz, t, d = y.shape
    if is_prompt:
        z, xbc, q, rows, win, small, rows_t, win_t = _norm_proj(
            y.reshape(bsz * t, d), norm_pre, segs, tm, t_segs=(segs[3].T, segs[4].T), seq_len=t)
        to_rows = lambda x_t, kinds: jnp.transpose(
            x_t.reshape(bsz, kinds, NSA_KV_HEADS, HEAD_DIM, x_t.shape[-1]), (0, 4, 1, 2, 3))
        rows_new = to_rows(rows_t, 4)
        new_win = to_rows(win_t[:, :, t - min(NSA_WINDOW, t):], 2)
        conv_buf = jnp.zeros((bsz, CONV_K - 1, SSM_CONV_DIM), F32)
        h0 = jnp.zeros((bsz, SSM_HEADS, SSM_HEAD_DIM, SSM_STATE), F32)
        chunk = math.gcd(t, SSM_CHUNK)
        rows3 = rows.reshape(bsz, t, 4 * LANE)
        cmp = _nsa_compress_prompt(rows3, lp)
        y_nsa = _nsa_prompt_attn(q, small, cmp, rows3, win.reshape(bsz, t, 2 * LANE), bsz, t)
    else:
        z, xbc, q, rows, win, small = _norm_proj(y.reshape(bsz * t, d), norm_pre, segs, tm)
        rows_new = rows.reshape(bsz, t, 4, NSA_KV_HEADS, HEAD_DIM)
        conv_buf, h0, chunk = past['ssm_conv'], past['ssm'], t
        y_nsa, new_win = _nsa_decode_attn(q, small, rows, win, past['nsa_kv'], past['nsa_win'], page_table, lp,
                                          bsz, t)
    y_ssm, new_conv, new_ssm = _ssd_mixer(xbc, z, small, conv_buf, h0, lp, bsz, t, chunk)
    return y_ssm, y_nsa, (new_conv, new_ssm, rows_new, new_win)


def _layer_cd(y, norm_pre, segs, lp, past, is_prompt, tm):
    bsz, t, d = y.shape
    qkv, z, dil, small = _norm_proj(y.reshape(bsz * t, d), norm_pre, segs, tm, plane_segs=(2,))
    if is_prompt:
        conv_buf = jnp.zeros((bsz, CONV_K - 1, 3 * GDN_WIDTH), F32)
        s0 = jnp.zeros((bsz, GDN_HEADS, GDN_HEAD_DIM, GDN_HEAD_DIM), F32)
        chunk = math.gcd(t, GDN_CHUNK)
        parts = [_dil_prompt_attn(dil, gi, bsz, t) for gi in range(DIL_GROUPS)]
        y_dil = _dil_combine([p[0] for p in parts], [p[1] for p in parts], min(tm * 2, bsz * t))
        bufs = []
        for gi, (win, _) in enumerate(DIL_PATTERNS):
            w = min(win, t)
            dil4 = dil.reshape(DIL_IN // LANE, bsz, t, LANE)
            tail = lambda part: lax.slice(dil4, ((part * DIL_GROUPS + gi) * _DIL_PAIRS, 0, t - w, 0),
                                          ((part * DIL_GROUPS + gi + 1) * _DIL_PAIRS, bsz, t, LANE))
            kv = lax.optimization_barrier(jnp.stack([tail(1), tail(2)]))
            kv = kv.reshape(2, _DIL_PAIRS, bsz, w, 2, HEAD_DIM)
            bufs.append(jnp.transpose(kv, (2, 3, 0, 1, 4, 5)).reshape(bsz, w, 2, DIL_HEADS_PER_GROUP, HEAD_DIM))
        y_gdn, new_conv, new_gdn = _gdn_mixer(qkv, z, small, conv_buf, s0, lp, bsz, t, chunk)
    else:
        y_gdn, new_conv, new_gdn, y_dil, bufs = _gdn_mixer(qkv, z, small, past['gdn_conv'], past['gdn'], lp, bsz, t, t,
                                                            dil_decode=(dil, past['dil']))
    return y_gdn, y_dil, (new_conv, new_gdn, bufs[0], bufs[1], bufs[2])


def kernel(x_prompt, x_sample, cache_ssm_conv, state_ssm, cache_nsa_kv, cache_nsa_win_kv, cache_gdn_conv, state_gdn,
           cache_dil0_kv, cache_dil1_kv, cache_dil2_kv, page_table, norm_mix_pre, norm_mix_post, norm_mlp_pre,
           norm_mlp_post, mlp_w1, mlp_w2, ab_w_in, ab_w_out, ssm_conv_w, ssm_conv_b, ssm_dt_bias, ssm_a_log, ssm_d,
           ssm_norm_w, nsa_pe_k, nsa_pe_v, nsa_ck_w1, nsa_ck_w2, nsa_cv_w1, nsa_cv_w2, cd_w_in, cd_w_out, gdn_conv_w,
           gdn_dt_bias, gdn_a_log, gdn_norm_w):
    depth = norm_mix_pre.shape[0]
    yp, ys = x_prompt, x_sample
    ab_p, ab_s, cd_p, cd_s = [], [], [], []
    tm = 256
    for l in range(depth):
        j = l // 2
        if l % 2 == 0:
            lp = {'ssm_conv_w': ssm_conv_w[j], 'ssm_conv_b': ssm_conv_b[j], 'ssm_dt_bias': ssm_dt_bias[j],
                  'ssm_a_log': ssm_a_log[j], 'ssm_d': ssm_d[j], 'ssm_norm_w': ssm_norm_w[j],
                  'nsa_pe_k': nsa_pe_k[j], 'nsa_pe_v': nsa_pe_v[j], 'nsa_ck_w1': nsa_ck_w1[j],
                  'nsa_ck_w2': nsa_ck_w2[j], 'nsa_cv_w1': nsa_cv_w1[j], 'nsa_cv_w2': nsa_cv_w2[j]}
            past = {'ssm_conv': cache_ssm_conv[j], 'ssm': state_ssm[j], 'nsa_kv': cache_nsa_kv[j],
                    'nsa_win': cache_nsa_win_kv[j]}
            segs = _ab_weight_segs(ab_w_in[j])
            w_out = ab_w_out[j].astype(BF16)
            ka = SSM_INNER
            ap, bp, stp = _layer_ab(yp, norm_mix_pre[l], segs, lp, None, None, True, tm)
            as_, bs, sts = _layer_ab(ys, norm_mix_pre[l], segs, lp, past, page_table, False, tm)
            ab_p.append(stp)
            ab_s.append(sts)
        else:
            lp = {'gdn_conv_w': gdn_conv_w[j], 'gdn_dt_bias': gdn_dt_bias[j], 'gdn_a_log': gdn_a_log[j],
                  'gdn_norm_w': gdn_norm_w[j]}
            past = {'gdn_conv': cache_gdn_conv[j], 'gdn': state_gdn[j],
                    'dil': (cache_dil0_kv[j], cache_dil1_kv[j], cache_dil2_kv[j])}
            segs = _cd_weight_segs(cd_w_in[j])
            w_out = cd_w_out[j].astype(BF16)
            ka = GDN_WIDTH
            ap, bp, stp = _layer_cd(yp, norm_mix_pre[l], segs, lp, None, True, tm)
            as_, bs, sts = _layer_cd(ys, norm_mix_pre[l], segs, lp, past, False, tm)
            cd_p.append(stp)
            cd_s.append(sts)
        w1 = mlp_w1[l].astype(BF16)
        w2 = mlp_w2[l].astype(BF16)
        post = functools.partial(_post_block, wo_a=w_out[:ka], wo_b=w_out[ka:], w1=w1, w2=w2, n_mix=norm_mix_post[l],
                                 n_pre=norm_mlp_pre[l], n_post=norm_mlp_post[l], tm=tm)
        yp = post(ap, bp, yp.reshape(-1, D_MODEL)).reshape(yp.shape)
        ys = post(as_, bs, ys.reshape(-1, D_MODEL)).reshape(ys.shape)
    stack = lambda states, i: jnp.stack([s[i] for s in states])
    return (yp, ys,
            stack(ab_p, 0), stack(ab_s, 0), stack(ab_p, 1), stack(ab_s, 1),
            stack(ab_p, 2), stack(ab_s, 2), stack(ab_p, 3), stack(ab_s, 3),
            stack(cd_p, 0), stack(cd_s, 0), stack(cd_p, 1), stack(cd_s, 1),
            stack(cd_p, 2), stack(cd_s, 2), stack(cd_p, 3), stack(cd_s, 3),
            stack(cd_p, 4), stack(cd_s, 4))
```

```python
import functools
import math

import jax
import jax.numpy as jnp
from jax import lax
from jax.experimental import pallas as pl
from jax.experimental.pallas import tpu as pltpu

F32 = jnp.float32
BF16 = jnp.bfloat16

D_MODEL = 1024
HEAD_DIM = 64
CONV_K = 4
RMS_EPS = 1e-6
MLP_HIDDEN = 4 * D_MODEL
QUERY_BLOCK = 128
PAGE_SIZE = 128

SSM_HEADS = 16
SSM_HEAD_DIM = 64
SSM_INNER = SSM_HEADS * SSM_HEAD_DIM
SSM_GROUPS = 2
SSM_STATE = 128
SSM_CONV_DIM = SSM_INNER + 2 * SSM_GROUPS * SSM_STATE
SSM_CHUNK = 128

NSA_HEADS = 16
NSA_KV_HEADS = 2
NSA_REP = NSA_HEADS // NSA_KV_HEADS
NSA_BLOCK = 64
NSA_TOPK = 16
NSA_WINDOW = 512
NSA_CMP_HIDDEN = 256

GDN_HEADS = 8
GDN_HEAD_DIM = 128
GDN_WIDTH = GDN_HEADS * GDN_HEAD_DIM
GDN_CHUNK = 64

DIL_PATTERNS = ((128, 1), (512, 4), (2048, 16))
DIL_GROUPS = len(DIL_PATTERNS)
DIL_HEADS_PER_GROUP = 8
DIL_WIDTH = DIL_HEADS_PER_GROUP * HEAD_DIM
DIL_IN = 3 * DIL_GROUPS * DIL_WIDTH
NEG = -1e30

VMEM_LIMIT_BYTES = 56 * 1024 * 1024
LANE = 128


def _cparams(*sem):
    return pltpu.CompilerParams(dimension_semantics=sem, vmem_limit_bytes=VMEM_LIMIT_BYTES)


def _const_spec(shape):
    nd = len(shape)
    return pl.BlockSpec(shape, lambda *_: (0,) * nd, pipeline_mode=pl.Buffered(1))


def _rms(x, w):
    return x * lax.rsqrt(jnp.mean(x * x, axis=-1, keepdims=True) + RMS_EPS) * w


def _proj_body(x_ref, nw_ref, *refs, n_t):
    n = len(refs) // 2
    xn = _rms(x_ref[...], nw_ref[...]).astype(BF16)
    for k, (w_ref, o_ref) in enumerate(zip(refs[:n], refs[n:])):
        if k >= n - n_t:
            o_ref[...] = lax.dot_general(w_ref[...], xn, _NT, preferred_element_type=F32)
            continue
        y = jnp.dot(xn, w_ref[...], preferred_element_type=F32)
        if len(o_ref.shape) == 2:
            o_ref[...] = y
        else:
            for j in range(o_ref.shape[0]):
                o_ref[j] = y[:, j * LANE:(j + 1) * LANE]


def _norm_proj(x, norm_w, w_segs, tm, plane_segs=(), t_segs=(), seq_len=None):
    n_tok, d = x.shape
    tm = min(tm, n_tok)
    widths = [w.shape[1] for w in w_segs]
    planes = [k in plane_segs for k in range(len(w_segs))]
    out_spec = lambda n, p: (pl.BlockSpec((n // LANE, tm, LANE), lambda i: (0, i, 0)) if p
                             else pl.BlockSpec((tm, n), lambda i: (i, 0)))
    out_sds = lambda n, p: jax.ShapeDtypeStruct((n // LANE, n_tok, LANE) if p else (n_tok, n), F32)
    per_seq = seq_len // tm if t_segs else 1
    t_spec = lambda w: pl.BlockSpec((None, w.shape[0], tm), lambda i: (i // per_seq, 0, i % per_seq))
    t_sds = lambda w: jax.ShapeDtypeStruct((n_tok // seq_len, w.shape[0], seq_len), F32)
    return pl.pallas_call(
        functools.partial(_proj_body, n_t=len(t_segs)),
        grid=(n_tok // tm,),
        in_specs=[pl.BlockSpec((tm, d), lambda i: (i, 0)), _const_spec((1, d))]
        + [_const_spec((d, n)) for n in widths] + [_const_spec(w.shape) for w in t_segs],
        out_specs=[out_spec(n, p) for n, p in zip(widths, planes)] + [t_spec(w) for w in t_segs],
        out_shape=[out_sds(n, p) for n, p in zip(widths, planes)] + [t_sds(w) for w in t_segs],
        compiler_params=_cparams("arbitrary"),
        name="norm_proj",
    )(x, norm_w.reshape(1, d), *w_segs, *t_segs)


def _post_body(a_ref, b_ref, y_ref, woa_ref, wob_ref, w1_ref, w2_ref, nmix_ref, npre_ref, npost_ref, o_ref):
    m = jnp.dot(a_ref[...].astype(BF16), woa_ref[...], preferred_element_type=F32)
    m = m + jnp.dot(b_ref[...].astype(BF16), wob_ref[...], preferred_element_type=F32)
    y1 = y_ref[...] + _rms(m, nmix_ref[...])
    h = _rms(y1, npre_ref[...]).astype(BF16)
    a = jnp.maximum(jnp.dot(h, w1_ref[...], preferred_element_type=F32), 0.0)
    m2 = jnp.dot((a * a).astype(BF16), w2_ref[...], preferred_element_type=F32)
    o_ref[...] = y1 + _rms(m2, npost_ref[...])


def _post_block(mix_a, mix_b, y, wo_a, wo_b, w1, w2, n_mix, n_pre, n_post, tm):
    n_tok, d = y.shape
    tm = min(tm, n_tok)
    ka, kb = mix_a.shape[1], mix_b.shape[1]
    row = lambda n: pl.BlockSpec((tm, n), lambda i: (i, 0))
    return pl.pallas_call(
        _post_body,
        grid=(n_tok // tm,),
        in_specs=[row(ka), row(kb), row(d), _const_spec(wo_a.shape), _const_spec(wo_b.shape),
                  _const_spec(w1.shape), _const_spec(w2.shape), _const_spec((1, d)), _const_spec((1, d)),
                  _const_spec((1, d))],
        out_specs=row(d),
        out_shape=jax.ShapeDtypeStruct((n_tok, d), F32),
        compiler_params=_cparams("arbitrary"),
        name="post_block",
    )(mix_a, mix_b, y, wo_a, wo_b, w1, w2, n_mix.reshape(1, d), n_pre.reshape(1, d), n_post.reshape(1, d))


def _compress_rows(src_ref, pe_ref, w1_ref, w2_ref, nb):
    hd = HEAD_DIM
    low = _half_mask((nb, LANE), 0)
    rows_per_step = 4 * LANE // (2 * hd)

    def pack(a, b):
        top = jnp.where(low, a, pltpu.roll(b, hd, axis=1))
        bot = jnp.where(low, pltpu.roll(a, hd, axis=1), b)
        return jnp.concatenate([top, bot], axis=0)

    def body(i, acc):
        r = i * rows_per_step
        x = [src_ref[pl.ds(r + k, nb, stride=NSA_BLOCK), :] + pe_ref[pl.ds(r + k, 1), :] for k in range(rows_per_step)]
        x4 = jnp.concatenate([pack(x[0], x[1]), pack(x[2], x[3])], axis=1).astype(BF16)
        w4 = w1_ref[pl.ds(r, rows_per_step)].reshape(rows_per_step * hd, NSA_CMP_HIDDEN)
        return acc + jnp.dot(x4, w4, preferred_element_type=F32)

    hid = lax.fori_loop(0, NSA_BLOCK // rows_per_step, body, jnp.zeros((2 * nb, NSA_CMP_HIDDEN), F32), unroll=4)
    hid = hid * jax.nn.sigmoid(hid)
    out = jnp.dot(hid.astype(BF16), w2_ref[...], preferred_element_type=F32)
    return jnp.concatenate([out[:nb], out[nb:]], axis=1)


def _compress_body(kc_ref, vc_ref, pek_ref, pev_ref, wk1_ref, wk2_ref, wv1_ref, wv2_ref, o_ref, *, nb):
    o_ref[:, :LANE] = _compress_rows(kc_ref, pek_ref, wk1_ref, wk2_ref, nb)
    o_ref[:, LANE:] = _compress_rows(vc_ref, pev_ref, wv1_ref, wv2_ref, nb)


def _pe2(pe):
    return jnp.concatenate([pe, pe], axis=1)


def _nsa_compress_prompt(rows, lp):
    bsz, t, _ = rows.shape
    nb = t // NSA_BLOCK
    w1 = lambda w: w.reshape(NSA_BLOCK, HEAD_DIM, NSA_CMP_HIDDEN).astype(BF16)
    col = lambda c: pl.BlockSpec((None, t, LANE), lambda b: (b, 0, c))
    return pl.pallas_call(
        functools.partial(_compress_body, nb=nb),
        grid=(bsz,),
        in_specs=[col(0), col(1), _const_spec((NSA_BLOCK, LANE)), _const_spec((NSA_BLOCK, LANE)),
                  _const_spec((NSA_BLOCK, HEAD_DIM, NSA_CMP_HIDDEN)), _const_spec((NSA_CMP_HIDDEN, HEAD_DIM)),
                  _const_spec((NSA_BLOCK, HEAD_DIM, NSA_CMP_HIDDEN)), _const_spec((NSA_CMP_HIDDEN, HEAD_DIM))],
        out_specs=pl.BlockSpec((None, nb, 2 * LANE), lambda b: (b, 0, 0)),
        out_shape=jax.ShapeDtypeStruct((bsz, nb, 2 * LANE), F32),
        compiler_params=_cparams("arbitrary"),
        name="nsa_compress",
    )(rows, rows, _pe2(lp['nsa_pe_k']), _pe2(lp['nsa_pe_v']), w1(lp['nsa_ck_w1']), lp['nsa_ck_w2'].astype(BF16),
      w1(lp['nsa_cv_w1']), lp['nsa_cv_w2'].astype(BF16))


_NT = (((1,), (1,)), ((), ()))


def _half_mask(shape, g):
    lane = lax.broadcasted_iota(jnp.int32, shape, len(shape) - 1)
    return lane >= HEAD_DIM if g else lane < HEAD_DIM


def _stack_heads(x, heads, g):
    keep = _half_mask((x.shape[0], LANE), g)
    out = []
    for h in heads:
        blk = x[:, (h // 2) * LANE:(h // 2 + 1) * LANE]
        if h % 2 != g:
            blk = pltpu.roll(blk, HEAD_DIM, axis=1)
        out.append(jnp.where(keep, blk, 0.0))
    return jnp.concatenate(out, axis=0)


def _unstack_pair(a, b, g):
    if g == 0:
        b = pltpu.roll(b, HEAD_DIM, axis=1)
    else:
        a = pltpu.roll(a, HEAD_DIM, axis=1)
    return jnp.where(_half_mask(a.shape, 1), b, a)


def _row_gate(sig, lanes, rows):
    return jnp.concatenate([jnp.broadcast_to(sig[:, c:c + 1], (rows, LANE)) for c in lanes], axis=0)


def _nsa_select(p3, pos, nb, n_sel):
    nq = p3.shape[1]
    imp = jnp.sum(p3, axis=0)
    n_i = lax.broadcasted_iota(jnp.int32, (nq, nb), 1)
    cur = pos // NSA_BLOCK
    score = jnp.where(n_i == 0, NSA_REP + 1.0, imp)
    score = jnp.where(n_i == cur, NSA_REP + 1.0, score)
    score = jnp.where(n_i == cur - 1, NSA_REP + 1.0, score)
    score = jnp.where(n_i > cur, -1.0, score)
    rank = jnp.zeros((nq, nb), F32)
    for m in range(nb):
        col = score[:, m:m + 1]
        tie = jnp.where(n_i > m, 1.0, 0.0)
        rank = rank + jnp.where(col > score, 1.0, jnp.where(col == score, tie, 0.0))
    return jnp.where(rank < n_sel, 1.0, 0.0)


def _nsa_select_t(imp, pos, n_sel):
    nb, nq = imp.shape
    n_i = lax.broadcasted_iota(jnp.int32, (nb, nq), 0)
    cur = pos // NSA_BLOCK
    score = jnp.where(n_i == 0, NSA_REP + 1.0, imp)
    score = jnp.where(n_i == cur, NSA_REP + 1.0, score)
    score = jnp.where(n_i == cur - 1, NSA_REP + 1.0, score)
    score = jnp.where(n_i > cur, -1.0, score)
    rank = jnp.zeros((nb, nq), F32)
    for m in range(nb):
        row = score[m:m + 1, :]
        tie = jnp.where(n_i > m, 1.0, 0.0)
        rank = rank + jnp.where(row > score, 1.0, jnp.where(row == score, tie, 0.0))
    return jnp.where(rank < n_sel, 1.0, 0.0)


def _lane_tile(x, n):
    return jnp.concatenate([x] * n, axis=1)


_V_ROWS = HEAD_DIM + 16
_SEL_TILE = 4
_WIN_TILE = 2


def _flash_tile_t(kt, vt_aug, q_t, valid, m_ref, acc_ref, g):
    nk, nq = valid.shape
    rep = q_t.shape[1] // nq
    s = jnp.dot(kt, q_t, preferred_element_type=F32)
    s = jnp.concatenate([jnp.where(valid, s[:, r * nq:(r + 1) * nq], NEG) for r in range(rep)], axis=1)
    m_old = m_ref[g, 0:1, :]
    m_new = jnp.maximum(m_old, jnp.max(s, axis=0, keepdims=True))
    alpha = jnp.exp2(m_old - m_new)
    p = jnp.exp2(s - m_new)
    acc_ref[g] = alpha * acc_ref[g] + jnp.dot(vt_aug, p.astype(BF16), preferred_element_type=F32)
    m_ref[g] = jnp.broadcast_to(m_new, m_ref.shape[1:])


def _nsa_prompt_body(q_ref, small_ref, cmp_ref, ks_ref, vs_ref, kw_ref, vw_ref, o_ref, vst_ref, vwt_ref, m_ref, acc_ref,
                     *, nb, n_sel):
    qb = q_ref.shape[0]
    t = ks_ref.shape[0]
    i = pl.program_id(1)
    q0 = i * qb
    rep = NSA_REP
    hd = HEAD_DIM
    groups = range(NSA_KV_HEADS)

    @pl.when(i == 0)
    def _():
        ones = jnp.ones((_V_ROWS - hd, t), BF16)
        for g in groups:
            vst_ref[g, hd:, :] = ones
            vwt_ref[g, hd:, :] = ones

        def fill(j, carry):
            k0 = pl.multiple_of(j * qb, qb)
            for src, dst in ((vs_ref, vst_ref), (vw_ref, vwt_ref)):
                v_t = src[pl.ds(k0, qb), :].T.astype(BF16)
                for g in groups:
                    dst[g, :hd, pl.ds(k0, qb)] = v_t[g * hd:(g + 1) * hd, :]
            return carry

        lax.fori_loop(0, t // qb, fill, 0)

    pos_r = q0 + lax.broadcasted_iota(jnp.int32, (1, qb), 1)
    sig_t = jax.nn.sigmoid(small_ref[...]).T
    q = q_ref[...] * (hd ** -0.5)
    pairs_t = [q[:, c * LANE:(c + 1) * LANE].T for c in range(NSA_HEADS // 2)]
    head_t = lambda h: pairs_t[h // 2][(h % 2) * hd:(h % 2 + 1) * hd, :]
    zeros = jnp.zeros((hd, rep * qb), F32)
    q_ts, sels = [], []
    for g in groups:
        qg_t = jnp.concatenate([head_t(g * rep + r) for r in range(rep)], axis=1)
        stack = lambda x: jnp.concatenate([x, zeros] if g == 0 else [zeros, x], axis=0).astype(BF16)
        q_t = stack(qg_t)
        q_ts.append(stack(qg_t * math.log2(math.e)))
        gate_row = lambda j: jnp.concatenate(
            [sig_t[SSM_HEADS + j * NSA_HEADS + g * rep + r:SSM_HEADS + j * NSA_HEADS + g * rep + r + 1, :]
             for r in range(rep)], axis=1)
        kc = cmp_ref[:, :LANE].astype(BF16)
        vc_t = cmp_ref[:, LANE:].T[g * hd:(g + 1) * hd, :].astype(BF16)
        s = jnp.dot(kc, q_t, preferred_element_type=F32)
        n_i = lax.broadcasted_iota(jnp.int32, (nb, qb), 0)
        complete = _lane_tile(jnp.where((n_i + 1) * NSA_BLOCK <= pos_r + 1, 1.0, 0.0), rep)
        s = jnp.where(complete > 0.5, s, NEG)
        p = jnp.exp(s - jnp.max(s, axis=0, keepdims=True))
        p = p / jnp.sum(p, axis=0, keepdims=True)
        p = jnp.where(_lane_tile(pos_r, rep) >= NSA_BLOCK - 1, p, 0.0)
        o_cmp = jnp.dot(vc_t, p.astype(BF16), preferred_element_type=F32)
        acc_ref[2 + g, :hd, :] = gate_row(0) * o_cmp
        imp = p[:, :qb]
        for r in range(1, rep):
            imp = imp + p[:, r * qb:(r + 1) * qb]
        sels.append(_nsa_select_t(imp, pos_r, n_sel).astype(BF16))

    def flash_init():
        m_ref[...] = jnp.full(m_ref.shape, NEG, F32)
        acc_ref[0:2] = jnp.zeros((2,) + acc_ref.shape[1:], F32)

    def flash_out(g):
        acc = acc_ref[g]
        return acc[:hd, :] / acc[hd:hd + 1, :]

    def flash_pass(k_ref, vt_ref, kb, first_key, valid_fn):
        kpos_i = lax.broadcasted_iota(jnp.int32, (kb, qb), 0)
        qpos = q0 + lax.broadcasted_iota(jnp.int32, (kb, qb), 1)
        flash_init()

        def step(j, carry):
            k0 = pl.multiple_of(j * kb, kb)
            kt = k_ref[pl.ds(k0, kb), :].astype(BF16)
            valids = valid_fn(k0, k0 + kpos_i, qpos)
            for g in groups:
                _flash_tile_t(kt, vt_ref[g, :, pl.ds(k0, kb)], q_ts[g], valids[g], m_ref, acc_ref, g)
            return carry

        lax.fori_loop(first_key // kb, (q0 + qb - 1) // kb + 1, step, 0)

    tile = lambda n: n * qb if t % (n * qb) == 0 else qb

    def sel_valid(k0, kpos, qpos):
        kb = kpos.shape[0]
        blk = lax.broadcasted_iota(jnp.int32, (kb, nb), 1)
        key = lax.broadcasted_iota(jnp.int32, (kb, nb), 0)
        expand = jnp.where(blk == (k0 + key) // NSA_BLOCK, 1.0, 0.0).astype(BF16)
        causal = kpos <= qpos
        return [jnp.where(causal, jnp.dot(expand, sels[g], preferred_element_type=F32), 0.0) > 0.5 for g in groups]

    flash_pass(ks_ref, vst_ref, tile(_SEL_TILE), 0, sel_valid)
    for g in groups:
        gate_row = jnp.concatenate(
            [sig_t[SSM_HEADS + NSA_HEADS + g * rep + r:SSM_HEADS + NSA_HEADS + g * rep + r + 1, :] for r in range(rep)],
            axis=1)
        acc_ref[2 + g, :hd, :] += gate_row * flash_out(g)
    def win_valid(k0, kpos, qpos):
        diff = qpos - kpos
        valid = jnp.where(diff >= 0, diff, NSA_WINDOW + 1) <= NSA_WINDOW
        return [valid for _ in groups]

    flash_pass(kw_ref, vwt_ref, tile(_WIN_TILE), jnp.maximum(q0 - NSA_WINDOW, 0), win_valid)
    for g in groups:
        gate_row = jnp.concatenate(
            [sig_t[SSM_HEADS + 2 * NSA_HEADS + g * rep + r:SSM_HEADS + 2 * NSA_HEADS + g * rep + r + 1, :]
             for r in range(rep)], axis=1)
        comb = acc_ref[2 + g, :hd, :] + gate_row * flash_out(g)
        for c in range(rep // 2):
            pair = jnp.concatenate([comb[:, (2 * c) * qb:(2 * c + 1) * qb], comb[:, (2 * c + 1) * qb:(2 * c + 2) * qb]],
                                   axis=0)
            col = (g * rep // 2 + c) * LANE
            o_ref[:, col:col + LANE] = pair.T


def _nsa_prompt_attn(q, small, cmp, rows, win, bsz, t):
    qb = math.gcd(t, QUERY_BLOCK)
    nq = t // qb
    nb = cmp.shape[1]
    n_sel = min(NSA_TOPK, nb)
    tok = lambda n: pl.BlockSpec((qb, n), lambda b, i: (b * nq + i, 0))
    seq = lambda c: pl.BlockSpec((None, t, LANE), lambda b, i: (b, 0, c))
    v_t = pltpu.VMEM((NSA_KV_HEADS, _V_ROWS, t), BF16)
    return pl.pallas_call(
        functools.partial(_nsa_prompt_body, nb=nb, n_sel=n_sel),
        grid=(bsz, nq),
        in_specs=[tok(NSA_HEADS * HEAD_DIM), tok(LANE), pl.BlockSpec((None, nb, 2 * LANE), lambda b, i: (b, 0, 0)),
                  seq(2), seq(3), seq(0), seq(1)],
        out_specs=tok(NSA_HEADS * HEAD_DIM),
        out_shape=jax.ShapeDtypeStruct((bsz * t, NSA_HEADS * HEAD_DIM), F32),
        scratch_shapes=[v_t, v_t, pltpu.VMEM((NSA_KV_HEADS, 8, NSA_REP * qb), F32),
                        pltpu.VMEM((2 * NSA_KV_HEADS, _V_ROWS, NSA_REP * qb), F32)],
        compiler_params=_cparams("arbitrary", "arbitrary"),
        name="nsa_prompt_attn",
    )(q, small, cmp, rows, rows, win, win)


_TN = (((0,), (0,)), ((), ()))


def _silu(x):
    return x * jax.nn.sigmoid(x)


def _softplus(x):
    return jnp.maximum(x, 0.0) + jnp.log(1.0 + jnp.exp(-jnp.abs(x)))


def _conv_silu(x_ref, buf_ref, w_ref, b_ref, xp_ref, tail_ref, first):
    n = x_ref.shape[0]

    @pl.when(first)
    def _():
        xp_ref[8 - (CONV_K - 1):8, :] = buf_ref[...]

    xp_ref[8:8 + n, :] = x_ref[...]
    y = xp_ref[8:8 + n, :] * w_ref[CONV_K - 1:CONV_K, :]
    for k in range(CONV_K - 1):
        y = y + xp_ref[5 + k:5 + k + n, :] * w_ref[k:k + 1, :]
    if b_ref is not None:
        y = y + b_ref[...]
    tail = xp_ref[8 + n - (CONV_K - 1):8 + n, :]
    tail_ref[...] = tail
    xp_ref[8 - (CONV_K - 1):8, :] = tail
    return _silu(y)


def _bf16_terms(x):
    hi = x.astype(BF16)
    r = x - hi.astype(F32)
    mid = r.astype(BF16)
    return hi, mid, (r - mid.astype(F32)).astype(BF16)


def _cumsum_rows(x, seg=None):
    n = x.shape[0]
    i = lax.broadcasted_iota(jnp.int32, (n, n), 0)
    j = lax.broadcasted_iota(jnp.int32, (n, n), 1)
    keep = i >= j if seg in (None, n) else jnp.where(i // seg == j // seg, i - j, -1) >= 0
    tri = jnp.where(keep, 1.0, 0.0).astype(BF16)
    return sum(jnp.dot(tri, term, preferred_element_type=F32) for term in _bf16_terms(x))


def _expand_heads(x, width, lane0=0, n_out=D_MODEL):
    h_i = lax.broadcasted_iota(jnp.int32, (LANE, n_out), 0)
    c_i = lax.broadcasted_iota(jnp.int32, (LANE, n_out), 1)
    sel = jnp.where(c_i // width + lane0 == h_i, 1.0, 0.0).astype(BF16)
    return sum(jnp.dot(term, sel, preferred_element_type=F32) for term in _bf16_terms(x))


def _decay_matrix(col, row, strict=False):
    n = col.shape[0]
    i = lax.broadcasted_iota(jnp.int32, (n, n), 0)
    j = lax.broadcasted_iota(jnp.int32, (n, n), 1)
    keep = (i > j) if strict else (i >= j)
    return jnp.exp(jnp.where(keep, col - row, NEG))


def _ssd_body(xbc_ref, z_ref, small_ref, buf_ref, h0_ref, cw_ref, cb_ref, dtb_ref, alog_ref, dx_ref, nw_ref,
              y_ref, conv_ref, h_ref, xp_ref):
    c = pl.program_id(1)
    n = xbc_ref.shape[0]
    hp = SSM_HEAD_DIM
    rep = SSM_HEADS // SSM_GROUPS

    @pl.when(c == 0)
    def _():
        h_ref[...] = h0_ref[...]

    act = _conv_silu(xbc_ref, buf_ref, cw_ref, cb_ref, xp_ref, conv_ref, c == 0)
    xs = act[:, :SSM_INNER]
    head_lane = lax.broadcasted_iota(jnp.int32, (n, LANE), 1) < SSM_HEADS
    dt = jnp.where(head_lane, _softplus(small_ref[...] + dtb_ref[...]), 0.0)
    la = dt * (-jnp.exp(alog_ref[...]))
    acs = _cumsum_rows(la)
    acs_t = acs.T
    xd = xs * _expand_heads(dt, hp)
    e_acs = _expand_heads(jnp.exp(acs), hp)
    xdd = (xd * _expand_heads(jnp.exp(acs[n - 1:n, :] - acs), hp)).astype(BF16)
    xd = xd.astype(BF16)
    lane2 = _half_mask((n, LANE), 1)
    for g in range(SSM_GROUPS):
        bc = act[:, SSM_INNER + g * SSM_STATE:SSM_INNER + (g + 1) * SSM_STATE].astype(BF16)
        cc = act[:, SSM_INNER + (SSM_GROUPS + g) * SSM_STATE:SSM_INNER + (SSM_GROUPS + g + 1) * SSM_STATE].astype(BF16)
        cb = lax.dot_general(cc, bc, _NT, preferred_element_type=F32)
        h_prev = h_ref[g * rep:(g + 1) * rep].reshape(rep * hp, SSM_STATE)
        y_off = lax.dot_general(cc, h_prev.astype(BF16), _NT, preferred_element_type=F32)
        for pair in range(rep // 2):
            halves = []
            for k in range(2):
                h = g * rep + 2 * pair + k
                lm = _decay_matrix(acs[:, h:h + 1], acs_t[h:h + 1, :])
                halves.append(jnp.dot((cb * lm).astype(BF16), xd[:, (h // 2) * LANE:(h // 2 + 1) * LANE],
                                      preferred_element_type=F32))
            col = (g * rep + 2 * pair) * hp
            y_ref[:, col:col + LANE] = jnp.where(lane2, halves[1], halves[0]) + y_off[:, 2 * pair * hp:2 * pair * hp + LANE] * e_acs[:, col:col + LANE]
        st = lax.dot_general(xdd[:, g * rep * hp:(g + 1) * rep * hp], bc, _TN, preferred_element_type=F32)
        for r in range(rep):
            h = g * rep + r
            dec = jnp.exp(acs_t[h:h + 1, n - 1:n])
            h_ref[h] = h_ref[h] * dec + st[r * hp:(r + 1) * hp, :]
    y = y_ref[...] + dx_ref[...] * xs
    y_ref[...] = _rms(y * _silu(z_ref[...]), nw_ref[...])


def _ssd_mixer(xbc, z, small, conv_buf, h0, lp, bsz, t, chunk):
    nc = t // chunk
    tok = lambda n: pl.BlockSpec((chunk, n), lambda b, c: (b * nc + c, 0))
    per_b = lambda shape: pl.BlockSpec((None,) + shape, lambda b, c: (b,) + (0,) * len(shape))
    pad = lambda v: jnp.pad(v.astype(F32), (0, LANE - v.shape[0])).reshape(1, LANE)
    cdim = SSM_CONV_DIM
    state = (SSM_HEADS, SSM_HEAD_DIM, SSM_STATE)
    return pl.pallas_call(
        _ssd_body,
        grid=(bsz, nc),
        in_specs=[tok(cdim), tok(SSM_INNER), tok(LANE), per_b((CONV_K - 1, cdim)), per_b(state),
                  _const_spec((CONV_K, cdim)), _const_spec((1, cdim)), _const_spec((1, LANE)), _const_spec((1, LANE)),
                  _const_spec((1, SSM_INNER)), _const_spec((1, SSM_INNER))],
        out_specs=[tok(SSM_INNER), per_b((CONV_K - 1, cdim)), per_b(state)],
        out_shape=[jax.ShapeDtypeStruct((bsz * t, SSM_INNER), F32),
                   jax.ShapeDtypeStruct((bsz, CONV_K - 1, cdim), F32),
                   jax.ShapeDtypeStruct((bsz,) + state, F32)],
        scratch_shapes=[pltpu.VMEM((chunk + 8, cdim), F32)],
        compiler_params=_cparams("arbitrary", "arbitrary"),
        name="ssd_mixer",
    )(xbc, z, small, conv_buf, h0, lp['ssm_conv_w'], lp['ssm_conv_b'].reshape(1, cdim), pad(lp['ssm_dt_bias']),
      pad(lp['ssm_a_log']), jnp.repeat(lp['ssm_d'].astype(F32), SSM_HEAD_DIM).reshape(1, SSM_INNER),
      lp['ssm_norm_w'].reshape(1, SSM_INNER))


def _split_bf16(x):
    hi = x.astype(BF16)
    return hi, (x - hi.astype(F32)).astype(BF16)


def _bmm(a, b):
    return jnp.einsum('hmk,hkn->hmn', a, b, preferred_element_type=F32)


def _bmm_nt(a, b):
    return jnp.einsum('hmk,hnk->hmn', a, b, preferred_element_type=F32)


def _bmm_tn(a, b):
    return jnp.einsum('hkm,hkn->hmn', a, b, preferred_element_type=F32)


def _bmm_split(a, b):
    a_hi, a_lo = _split_bf16(a)
    b_hi, b_lo = _split_bf16(b)
    return _bmm(a_hi, b_hi) + _bmm(a_hi, b_lo) + _bmm(a_lo, b_hi)


def _unit_lower_inverse(a):
    n = a.shape[-1]
    eye = jnp.where(lax.broadcasted_iota(jnp.int32, (n, n), 0) == lax.broadcasted_iota(jnp.int32, (n, n), 1), 1.0, 0.0)
    p = -a
    t = eye + p
    span = 2
    while span < n:
        p = _bmm_split(p, p)
        t = t + _bmm_split(t, p)
        span *= 2
    return t


_GDN_CHUNKS_PER_STEP = 4


def _gdn_body(qkv_ref, z_ref, small_ref, buf_ref, s0_ref, cw_ref, alog_ref, dtb_ref, nw_ref,
              y_ref, conv_ref, s_ref, xp_ref, *, chunk):
    c = pl.program_id(1)
    n = qkv_ref.shape[0]
    dk = GDN_HEAD_DIM
    nh = GDN_HEADS
    n_ck = n // chunk

    @pl.when(c == 0)
    def _():
        s_ref[...] = s0_ref[...]

    act = _conv_silu(qkv_ref, buf_ref, cw_ref, None, xp_ref, conv_ref, c == 0)
    lane = lax.broadcasted_iota(jnp.int32, (n, LANE), 1)
    raw = small_ref[...]
    beta = jax.nn.sigmoid(raw)
    g = jnp.where((lane >= nh) & (lane < 2 * nh), -jnp.exp(alog_ref[...]) * _softplus(raw + dtb_ref[...]), 0.0)
    gc = _cumsum_rows(g, chunk)
    gc_t = gc.T
    gc_end = jnp.concatenate([jnp.broadcast_to(gc[(j + 1) * chunk - 1:(j + 1) * chunk, :], (chunk, LANE))
                              for j in range(n_ck)], axis=0)
    beta_x = _expand_heads(beta, dk)
    egc_x = _expand_heads(jnp.exp(gc), dk, nh)
    edec_x = _expand_heads(jnp.exp(gc_end - gc), dk, nh)
    ii = lax.broadcasted_iota(jnp.int32, (chunk, chunk), 0)
    jj = lax.broadcasted_iota(jnp.int32, (chunk, chunk), 1)
    heads = lambda x, base=0: jnp.stack([x[j * chunk:(j + 1) * chunk, base + h * dk:base + (h + 1) * dk]
                                         for j in range(n_ck) for h in range(nh)])
    q, k, v = heads(act), heads(act, nh * dk), heads(act, 2 * nh * dk)
    q = q * lax.rsqrt(jnp.sum(q * q, axis=-1, keepdims=True) + 1e-6) * (dk ** -0.5)
    k = k * lax.rsqrt(jnp.sum(k * k, axis=-1, keepdims=True) + 1e-6)
    beta3, egc, edec = heads(beta_x), heads(egc_x), heads(edec_x)
    kb = k * beta3
    decay = jnp.stack([_decay_matrix(gc[j * chunk:(j + 1) * chunk, nh + h:nh + h + 1],
                                     gc_t[nh + h:nh + h + 1, j * chunk:(j + 1) * chunk])
                       for j in range(n_ck) for h in range(nh)])
    k16 = k.astype(BF16)
    amat = jnp.where(ii > jj, _bmm_nt(kb.astype(BF16), k16) * decay, 0.0)
    tmat = _unit_lower_inverse(amat).astype(BF16)
    u = _bmm(tmat, (v * beta3).astype(BF16))
    w = _bmm(tmat, (kb * egc).astype(BF16)).astype(BF16)
    qk = (_bmm_nt(q.astype(BF16), k16) * decay).astype(BF16)
    qg = (q * egc).astype(BF16)
    kdec = (k * edec).astype(BF16)
    s = s_ref[...]
    for j in range(n_ck):
        ck = slice(j * nh, (j + 1) * nh)
        s16 = s.astype(BF16)
        v16 = (u[ck] - _bmm(w[ck], s16)).astype(BF16)
        o = _bmm(qg[ck], s16) + _bmm(qk[ck], v16)
        last = (j + 1) * chunk - 1
        gl = jnp.stack([jnp.exp(gc_t[nh + h:nh + h + 1, last:last + 1]) for h in range(nh)])
        s = s * gl + _bmm_tn(kdec[ck], v16)
        y = _rms(o, nw_ref[...])
        for h in range(nh):
            sl = slice(h * dk, (h + 1) * dk)
            y_ref[j * chunk:(j + 1) * chunk, sl] = y[h] * _silu(z_ref[j * chunk:(j + 1) * chunk, sl])
    s_ref[...] = s


def _gdn_mixer(qkv, z, small, conv_buf, s0, lp, bsz, t, chunk, dil_decode=None):
    per_step = _GDN_CHUNKS_PER_STEP if (t // chunk) % _GDN_CHUNKS_PER_STEP == 0 else 1
    rows = per_step * chunk
    nc = t // rows
    tok = lambda n: pl.BlockSpec((rows, n), lambda b, c: (b * nc + c, 0))
    per_b = lambda shape: pl.BlockSpec((None,) + shape, lambda b, c: (b,) + (0,) * len(shape))
    pad8 =lambda v: jnp.pad(v.astype(F32), (GDN_HEADS, LANE - 2 * GDN_HEADS)).reshape(1, LANE)
    cdim = 3 * GDN_WIDTH
    state = (GDN_HEADS, GDN_HEAD_DIM, GDN_HEAD_DIM)
    operands = [qkv, z, small, conv_buf, s0, lp['gdn_conv_w'], pad8(lp['gdn_a_log']), pad8(lp['gdn_dt_bias']),
                lp['gdn_norm_w'].reshape(1, GDN_HEAD_DIM)]
    in_specs = [tok(cdim), tok(GDN_WIDTH), tok(LANE), per_b((CONV_K - 1, cdim)), per_b(state),
                _const_spec((CONV_K, cdim)), _const_spec((1, LANE)), _const_spec((1, LANE)),
                _const_spec((1, GDN_HEAD_DIM))]
    out_specs = [tok(GDN_WIDTH), per_b((CONV_K - 1, cdim)), per_b(state)]
    out_shape = [jax.ShapeDtypeStruct((bsz * t, GDN_WIDTH), F32),
                 jax.ShapeDtypeStruct((bsz, CONV_K - 1, cdim), F32),
                 jax.ShapeDtypeStruct((bsz,) + state, F32)]
    body, name = functools.partial(_gdn_body, chunk=chunk), "gdn_mixer"
    if dil_decode is not None:
        assert nc == 1
        d_ops, d_in, d_out, d_shape, unview = _dil_decode_operands(*dil_decode, bsz, t)
        operands, in_specs = operands + d_ops, in_specs + d_in
        out_specs, out_shape = out_specs + d_out, out_shape + d_shape
        body, name = functools.partial(_gdn_dil_decode_body, chunk=chunk), "gdn_dil_decode"
    outs = pl.pallas_call(
        body,
        grid=(bsz, nc),
        in_specs=in_specs,
        out_specs=out_specs,
        out_shape=out_shape,
        scratch_shapes=[pltpu.VMEM((rows + 8, cdim), F32)],
        compiler_params=_cparams("arbitrary", "arbitrary"),
        name=name,
    )(*operands)
    if dil_decode is None:
        return outs
    return list(outs[:4]) + [[unview(v) for v in outs[4:]]]


def _softmax_pv(s, valid, v16, n_rep=1):
    rows, nk = s.shape
    s = jnp.where(valid[None], s.reshape(n_rep, rows // n_rep, nk), NEG).reshape(rows, nk)
    m = jnp.max(s, axis=-1, keepdims=True)
    p = jnp.exp(s - m)
    l = jnp.sum(p, axis=-1, keepdims=True)
    return jnp.dot((p / l).astype(BF16), v16, preferred_element_type=F32), m + jnp.log(l)


_DIL_PAIRS = DIL_HEADS_PER_GROUP // 2


def _dil_prompt_body(q_ref, kp_ref, kc_ref, vp_ref, vc_ref, o_ref, lse_ref, *, window, step):
    i = pl.program_id(1)
    qb = q_ref.shape[1] // step
    scale = HEAD_DIM ** -0.5
    qpos = i * qb + lax.broadcasted_iota(jnp.int32, (qb, 2 * qb), 0)
    kpos = (i - 1) * qb + lax.broadcasted_iota(jnp.int32, (qb, 2 * qb), 1)
    diff = jnp.where(kpos >= 0, qpos - kpos, -1)
    valid = jnp.where(diff >= 0, diff, window + 1) <= window
    lower = _half_mask((qb, LANE), 0)

    def one_class(c, carry):
        rows = pl.ds(c, qb, stride=step)
        for pr in range(q_ref.shape[0]):
            q = q_ref[pr, rows, :] * scale
            kk = jnp.concatenate([kp_ref[pr, rows, :], kc_ref[pr, rows, :]], axis=0).astype(BF16)
            vv = jnp.concatenate([vp_ref[pr, rows, :], vc_ref[pr, rows, :]], axis=0).astype(BF16)
            o2, l2 = [], []
            for k in range(2):
                qh = jnp.where(_half_mask((qb, LANE), k), q, 0.0).astype(BF16)
                s = lax.dot_general(qh, kk, _NT, preferred_element_type=F32)
                o, lse = _softmax_pv(s, valid, vv)
                o2.append(o)
                l2.append(jnp.broadcast_to(lse, (qb, LANE)))
            o_ref[pr, rows, :] = jnp.where(lower, o2[0], o2[1])
            lse_ref[pr, rows, :] = jnp.where(lower, l2[0], l2[1])
        return carry

    lax.fori_loop(0, step, one_class, 0)


def _dil_prompt_attn(dil, gi, bsz, t):
    win, step = DIL_PATTERNS[gi]
    assert win % step == 0 and t % win == 0
    tile = win
    nq = t // tile
    pp = _DIL_PAIRS if _DIL_PAIRS * tile * LANE * 4 <= (1 << 20) else 1
    blk = lambda part, prev: pl.BlockSpec(
        (pp, tile, LANE),
        lambda b, i, h: ((part * DIL_GROUPS + gi) * (_DIL_PAIRS // pp) + h,
                         b * nq + (jnp.maximum(i - 1, 0) if prev else i), 0))
    out_spec = pl.BlockSpec((pp, tile, LANE), lambda b, i, h: (h, b * nq + i, 0))
    out_sds = jax.ShapeDtypeStruct((_DIL_PAIRS, bsz * t, LANE), F32)
    return pl.pallas_call(
        functools.partial(_dil_prompt_body, window=win // step, step=step),
        grid=(bsz, nq, _DIL_PAIRS // pp),
        in_specs=[blk(0, False), blk(1, True), blk(1, False), blk(2, True), blk(2, False)],
        out_specs=[out_spec, out_spec],
        out_shape=[out_sds, out_sds],
        compiler_params=_cparams("arbitrary", "arbitrary", "arbitrary"),
        name=f"dil_prompt_attn_{gi}",
    )(dil, dil, dil, dil, dil)


def _dil_combine_body(o0, o1, o2, l0, l1, l2, y_ref):
    for pr in range(_DIL_PAIRS):
        m = jnp.maximum(jnp.maximum(l0[pr], l1[pr]), l2[pr])
        e0, e1, e2 = jnp.exp(l0[pr] - m), jnp.exp(l1[pr] - m), jnp.exp(l2[pr] - m)
        den = e0 + e1 + e2
        y_ref[:, pr * LANE:(pr + 1) * LANE] = (e0 / den) * o0[pr] + (e1 / den) * o1[pr] + (e2 / den) * o2[pr]


def _dil_combine(outs, lses, tm):
    n_tok = outs[0].shape[1]
    spec = pl.BlockSpec((_DIL_PAIRS, tm, LANE), lambda i: (0, i, 0))
    return pl.pallas_call(
        _dil_combine_body,
        grid=(n_tok // tm,),
        in_specs=[spec] * 6,
        out_specs=pl.BlockSpec((tm, DIL_WIDTH), lambda i: (i, 0)),
        out_shape=jax.ShapeDtypeStruct((n_tok, DIL_WIDTH), F32),
        compiler_params=_cparams("arbitrary"),
        name="dil_combine",
    )(*outs, *lses)


def _attend_cached(qg, k_t, v_t, k_n, v_n, valid_c, valid_n, n_rep=1):
    rows = qg.shape[0]
    mask = lambda s, v: jnp.where(v[None], s.reshape(n_rep, rows // n_rep, s.shape[1]), NEG).reshape(rows, s.shape[1])
    s_c = mask(jnp.dot(qg, k_t, preferred_element_type=F32), valid_c)
    s_n = mask(lax.dot_general(qg, k_n, _NT, preferred_element_type=F32), valid_n)
    m = jnp.maximum(jnp.max(s_c, axis=-1, keepdims=True), jnp.max(s_n, axis=-1, keepdims=True))
    p_c = jnp.exp(s_c - m)
    p_n = jnp.exp(s_n - m)
    l = jnp.sum(p_c, axis=-1, keepdims=True) + jnp.sum(p_n, axis=-1, keepdims=True)
    o = lax.dot_general((p_c / l).astype(BF16), v_t, _NT, preferred_element_type=F32)
    o = o + jnp.dot((p_n / l).astype(BF16), v_n, preferred_element_type=F32)
    return o, m + jnp.log(l)


def _nsa_decode_body(pt_ref, q_ref, small_ref, rows_ref, win_ref, *refs, n_pages, lw, nb, nb_pad, n_sel):
    del pt_ref, nb
    pages = refs[:n_pages]
    (pastwin_ref, pek_ref, pev_ref, wk1_ref, wk2_ref, wv1_ref, wv2_ref, o_ref, neww_ref,
     kc_s, vc_s, comb_s) = refs[n_pages:]
    t = q_ref.shape[0]
    rep = NSA_REP
    past_len = n_pages * PAGE_SIZE
    for dst, c in ((kc_s, 0), (vc_s, 1)):
        for p in range(n_pages):
            dst[p * PAGE_SIZE:(p + 1) * PAGE_SIZE, :] = pages[p][c * LANE:(c + 1) * LANE, :].T
        dst[past_len:past_len + t, :] = rows_ref[:, c * LANE:(c + 1) * LANE]
        dst[past_len + t:, :] = jnp.zeros((dst.shape[0] - past_len - t, LANE), F32)
    kcmp = _compress_rows(kc_s, pek_ref, wk1_ref, wk2_ref, nb_pad).astype(BF16)
    vcmp = _compress_rows(vc_s, pev_ref, wv1_ref, wv2_ref, nb_pad).astype(BF16)
    page_rows = lambda c: jnp.concatenate([pages[p][c * LANE:(c + 1) * LANE, :] for p in range(n_pages)],
                                          axis=1).astype(BF16)
    ks_t, vs_t = page_rows(2), page_rows(3)
    ks_n, vs_n = rows_ref[:, 2 * LANE:3 * LANE].astype(BF16), rows_ref[:, 3 * LANE:].astype(BF16)
    kw_t, vw_t = pastwin_ref[:LANE, :].astype(BF16), pastwin_ref[LANE:, :].astype(BF16)
    kw_n, vw_n = win_ref[:, :LANE].astype(BF16), win_ref[:, LANE:].astype(BF16)
    neww_ref[...] = jnp.concatenate([pastwin_ref[:, t:], win_ref[...].T], axis=1)
    pos_c = past_len + lax.broadcasted_iota(jnp.int32, (t, 1), 0)
    new_i = lax.broadcasted_iota(jnp.int32, (t, t), 1)
    back = lax.broadcasted_iota(jnp.int32, (t, t), 0) - new_i
    sig = jax.nn.sigmoid(small_ref[...])
    q = q_ref[...] * (HEAD_DIM ** -0.5)
    for g in range(NSA_KV_HEADS):
        qg = _stack_heads(q, range(g * rep, (g + 1) * rep), g).astype(BF16)
        gate_lane = lambda j: [SSM_HEADS + j * NSA_HEADS + g * rep + r for r in range(rep)]
        s = lax.dot_general(qg, kcmp, _NT, preferred_element_type=F32).reshape(rep, t, nb_pad)
        n_i = lax.broadcasted_iota(jnp.int32, (t, nb_pad), 1)
        complete = (n_i + 1) * NSA_BLOCK <= pos_c + 1
        s = jnp.where(complete[None], s, NEG)
        p = jnp.exp(s - jnp.max(s, axis=-1, keepdims=True))
        p = p / jnp.sum(p, axis=-1, keepdims=True)
        p = jnp.where((pos_c >= NSA_BLOCK - 1)[None], p, 0.0)
        o_cmp = jnp.dot(p.reshape(rep * t, nb_pad).astype(BF16), vcmp, preferred_element_type=F32)
        comb = _row_gate(sig, gate_lane(0), t) * o_cmp
        sel = _nsa_select(p, pos_c, nb_pad, n_sel).astype(BF16)
        expand = lambda n_keys, first: jnp.where(
            lax.broadcasted_iota(jnp.int32, (nb_pad, n_keys), 0)
            == (first + lax.broadcasted_iota(jnp.int32, (nb_pad, n_keys), 1)) // NSA_BLOCK, 1.0, 0.0).astype(BF16)
        chosen_c = jnp.dot(sel, expand(past_len, 0), preferred_element_type=F32)
        chosen_n = jnp.dot(sel, expand(t, past_len), preferred_element_type=F32)
        kpos = lax.broadcasted_iota(jnp.int32, (t, past_len), 1)
        valid_c = jnp.where(kpos <= pos_c, chosen_c, 0.0) > 0.5
        valid_n = jnp.where(back >= 0, chosen_n, 0.0) > 0.5
        o_sel, _ = _attend_cached(qg, ks_t, vs_t, ks_n, vs_n, valid_c, valid_n, rep)
        comb = comb + _row_gate(sig, gate_lane(1), t) * o_sel
        diff = pos_c - (past_len - lw + lax.broadcasted_iota(jnp.int32, (t, lw), 1))
        valid_c = jnp.where(diff >= 0, diff, NSA_WINDOW + 1) <= NSA_WINDOW
        valid_n = jnp.where(back >= 0, back, NSA_WINDOW + 1) <= NSA_WINDOW
        o_win, _ = _attend_cached(qg, kw_t, vw_t, kw_n, vw_n, valid_c, valid_n, rep)
        comb_s[...] = comb + _row_gate(sig, gate_lane(2), t) * o_win
        for c in range(rep // 2):
            a = comb_s[(2 * c) * t:(2 * c + 1) * t, :]
            b = comb_s[(2 * c + 1) * t:(2 * c + 2) * t, :]
            col = (g * rep // 2 + c) * LANE
            o_ref[:, col:col + LANE] = _unstack_pair(a, b, g)


def _nsa_decode_attn(q, small, rows, win, cache_kv, cache_win, page_table, lp, bsz, t):
    n_pages = page_table.shape[1]
    past_len = n_pages * PAGE_SIZE
    lw = cache_win.shape[1]
    nb = -(-(past_len + t) // NSA_BLOCK)
    nb_pad = -(-nb // 8) * 8
    pages = jnp.transpose(cache_kv, (0, 2, 3, 4, 1)).reshape(cache_kv.shape[0], 4 * LANE, PAGE_SIZE)
    pastwin = jnp.transpose(cache_win, (0, 2, 3, 4, 1)).reshape(bsz, 2 * LANE, lw)
    w1 = lambda w: w.reshape(NSA_BLOCK, HEAD_DIM, NSA_CMP_HIDDEN).astype(BF16)
    tok = lambda n: pl.BlockSpec((t, n), lambda b, pt: (b, 0))
    const = lambda shape: pl.BlockSpec(shape, lambda b, pt: (0,) * len(shape), pipeline_mode=pl.Buffered(1))
    page_spec = lambda p: pl.BlockSpec((None, 4 * LANE, PAGE_SIZE), lambda b, pt: (pt[b * n_pages + p], 0, 0))
    win_spec = pl.BlockSpec((None, 2 * LANE, lw), lambda b, pt: (b, 0, 0))
    grid_spec = pltpu.PrefetchScalarGridSpec(
        num_scalar_prefetch=1,
        grid=(bsz,),
        in_specs=[tok(NSA_HEADS * HEAD_DIM), tok(LANE), tok(4 * LANE), tok(2 * LANE)]
        + [page_spec(p) for p in range(n_pages)]
        + [win_spec, const((NSA_BLOCK, LANE)), const((NSA_BLOCK, LANE)),
           const((NSA_BLOCK, HEAD_DIM, NSA_CMP_HIDDEN)), const((NSA_CMP_HIDDEN, HEAD_DIM)),
           const((NSA_BLOCK, HEAD_DIM, NSA_CMP_HIDDEN)), const((NSA_CMP_HIDDEN, HEAD_DIM))],
        out_specs=[tok(NSA_HEADS * HEAD_DIM), win_spec],
        scratch_shapes=[pltpu.VMEM((nb_pad * NSA_BLOCK, LANE), F32), pltpu.VMEM((nb_pad * NSA_BLOCK, LANE), F32),
                        pltpu.VMEM((NSA_REP * t, LANE), F32)],
    )
    y, new_win = pl.pallas_call(
        functools.partial(_nsa_decode_body, n_pages=n_pages, lw=lw, nb=nb, nb_pad=nb_pad, n_sel=min(NSA_TOPK, nb)),
        grid_spec=grid_spec,
        out_shape=[jax.ShapeDtypeStruct((bsz * t, NSA_HEADS * HEAD_DIM), F32),
                   jax.ShapeDtypeStruct((bsz, 2 * LANE, lw), F32)],
        compiler_params=_cparams("arbitrary"),
        name="nsa_decode_attn",
    )(page_table.reshape(-1), q, small, rows, win, *([pages] * n_pages), pastwin,
      _pe2(lp['nsa_pe_k']), _pe2(lp['nsa_pe_v']), w1(lp['nsa_ck_w1']), lp['nsa_ck_w2'].astype(BF16),
      w1(lp['nsa_cv_w1']), lp['nsa_cv_w2'].astype(BF16))
    new_win = jnp.transpose(new_win.reshape(bsz, 2, NSA_KV_HEADS, HEAD_DIM, lw), (0, 4, 1, 2, 3))
    return y, new_win


def _dil_decode_group(q, k_new, v_new, cache_ref, step):
    t = q.shape[0]
    lg = cache_ref.shape[1]
    pairs = DIL_HEADS_PER_GROUP // 2
    tok_c = lax.broadcasted_iota(jnp.int32, (2 * t, lg), 0) % t
    ahead = lax.broadcasted_iota(jnp.int32, (2 * t, lg), 1) - tok_c
    valid_c = jnp.where(ahead >= 0, ahead % step, 1) == 0
    tok_n = lax.broadcasted_iota(jnp.int32, (2 * t, t), 0) % t
    back = tok_n - lax.broadcasted_iota(jnp.int32, (2 * t, t), 1)
    valid_n = jnp.where(back >= 0, back % step, 1) == 0
    lower = _half_mask((t, LANE), 0)
    outs, lses = [], []
    for pr in range(pairs):
        sl = slice(pr * LANE, (pr + 1) * LANE)
        k_t = cache_ref[pr * LANE:(pr + 1) * LANE, :].astype(BF16)
        v_t = cache_ref[DIL_WIDTH + pr * LANE:DIL_WIDTH + (pr + 1) * LANE, :].astype(BF16)
        kn = k_new[:, sl].astype(BF16)
        vn = v_new[:, sl].astype(BF16)
        qp = q[:, sl]
        q2 = jnp.concatenate([jnp.where(lower, qp, 0.0), jnp.where(lower, 0.0, qp)], axis=0).astype(BF16)
        o, lse = _attend_cached(q2, k_t, v_t, kn, vn, valid_c, valid_n)
        lse = jnp.broadcast_to(lse, (2 * t, LANE))
        outs.append(jnp.where(lower, o[:t], o[t:]))
        lses.append(jnp.where(lower, lse[:t], lse[t:]))
    return jnp.concatenate(outs, axis=1), jnp.concatenate(lses, axis=1)


def _dil_decode_body(x_ref, c0_ref, c1_ref, c2_ref, y_ref, n0_ref, n1_ref, n2_ref):
    scale = HEAD_DIM ** -0.5
    t = x_ref.shape[1]
    res = []
    for gi, (cache_ref, new_ref) in enumerate(((c0_ref, n0_ref), (c1_ref, n1_ref), (c2_ref, n2_ref))):
        _, step = DIL_PATTERNS[gi]
        part = lambda p: jnp.concatenate(
            [x_ref[(p * DIL_GROUPS + gi) * _DIL_PAIRS + pr] for pr in range(_DIL_PAIRS)], axis=1)
        res.append(_dil_decode_group(part(0) * scale, part(1), part(2), cache_ref, step))
        new_t = jnp.concatenate([part(1).T, part(2).T], axis=0)
        new_ref[...] = jnp.concatenate([cache_ref[:, t:], new_t], axis=1)
    (o0, l0), (o1, l1), (o2, l2) = res
    m = jnp.maximum(jnp.maximum(l0, l1), l2)
    e0, e1, e2 = jnp.exp(l0 - m), jnp.exp(l1 - m), jnp.exp(l2 - m)
    den = e0 + e1 + e2
    y_ref[...] = (e0 / den) * o0 + (e1 / den) * o1 + (e2 / den) * o2


def _dil_decode_operands(dil, bufs, bsz, t):
    n_rows = 2 * DIL_WIDTH
    views = []
    for (win, _), buf in zip(DIL_PATTERNS, bufs):
        assert buf.shape[1] == win, "decode path needs a full window of cached rows"
        views.append(jnp.transpose(buf, (0, 2, 3, 4, 1)).reshape(bsz, n_rows, win))
    cache_spec = lambda v: pl.BlockSpec((None, n_rows, v.shape[2]), lambda b, *_: (b, 0, 0))
    in_specs = [pl.BlockSpec((DIL_IN // LANE, t, LANE), lambda b, *_: (0, b, 0))] + [cache_spec(v) for v in views]
    out_specs = [pl.BlockSpec((t, DIL_WIDTH), lambda b, *_: (b, 0))] + [cache_spec(v) for v in views]
    out_shape = [jax.ShapeDtypeStruct((bsz * t, DIL_WIDTH), F32)] + [jax.ShapeDtypeStruct(v.shape, F32) for v in views]
    unview = lambda v: jnp.transpose(v.reshape(bsz, 2, DIL_HEADS_PER_GROUP, HEAD_DIM, v.shape[2]), (0, 4, 1, 2, 3))
    return [dil] + views, in_specs, out_specs, out_shape, unview


def _gdn_dil_decode_body(*refs, chunk):
    n_gdn_in, n_dil_in, n_gdn_out, n_dil_out = 9, 4, 3, 4
    gdn_in, rest = refs[:n_gdn_in], refs[n_gdn_in:]
    dil_in, rest = rest[:n_dil_in], rest[n_dil_in:]
    gdn_out, rest = rest[:n_gdn_out], rest[n_gdn_out:]
    dil_out, scratch = rest[:n_dil_out], rest[n_dil_out:]
    _gdn_body(*gdn_in, *gdn_out, *scratch, chunk=chunk)
    _dil_decode_body(*dil_in, *dil_out)


_AB_CUTS = (0, SSM_INNER, SSM_INNER + SSM_CONV_DIM, SSM_INNER + SSM_CONV_DIM + SSM_HEADS)
_AB_Q0 = _AB_CUTS[3]
_AB_KV0 = _AB_Q0 + NSA_HEADS * HEAD_DIM
_AB_WIN0 = _AB_KV0 + 4 * NSA_KV_HEADS * HEAD_DIM
_AB_GATE0 = _AB_WIN0 + 2 * NSA_KV_HEADS * HEAD_DIM
_AB_END = _AB_GATE0 + 3 * NSA_HEADS


def _pad_cols(w, n):
    return jnp.pad(w, ((0, 0), (0, n - w.shape[1])))


def _ab_weight_segs(w_in):
    small = jnp.concatenate([w_in[:, _AB_CUTS[2]:_AB_CUTS[3]], w_in[:, _AB_GATE0:_AB_END]], axis=1)
    segs = [w_in[:, _AB_CUTS[0]:_AB_CUTS[1]], w_in[:, _AB_CUTS[1]:_AB_CUTS[2]], w_in[:, _AB_Q0:_AB_KV0],
            w_in[:, _AB_KV0:_AB_WIN0], w_in[:, _AB_WIN0:_AB_GATE0], _pad_cols(small, LANE)]
    return [s.astype(BF16) for s in segs]


_CD_Z0 = 3 * GDN_WIDTH
_CD_B0 = _CD_Z0 + GDN_WIDTH
_CD_DIL0 = _CD_B0 + 2 * GDN_HEADS
_CD_END = _CD_DIL0 + DIL_IN


def _cd_weight_segs(w_in):
    segs = [w_in[:, :_CD_Z0], w_in[:, _CD_Z0:_CD_B0], w_in[:, _CD_DIL0:_CD_END],
            _pad_cols(w_in[:, _CD_B0:_CD_DIL0], LANE)]
    return [s.astype(BF16) for s in segs]


def _layer_ab(y, norm_pre, segs, lp, past, page_table, is_prompt, tm):
    bsz, t, d = y.shape
    if is_prompt:
        z, xbc, q, rows, win, small, rows_t, win_t = _norm_proj(
            y.reshape(bsz * t, d), norm_pre, segs, tm, t_segs=(segs[3].T, segs[4].T), seq_len=t)
        to_rows = lambda x_t, kinds: jnp.transpose(
            x_t.reshape(bsz, kinds, NSA_KV_HEADS, HEAD_DIM, x_t.shape[-1]), (0, 4, 1, 2, 3))
        rows_new = to_rows(rows_t, 4)
        new_win = to_rows(win_t[:, :, t - min(NSA_WINDOW, t):], 2)
        conv_buf = jnp.zeros((bsz, CONV_K - 1, SSM_CONV_DIM), F32)
        h0 = jnp.zeros((bsz, SSM_HEADS, SSM_HEAD_DIM, SSM_STATE), F32)
        chunk = math.gcd(t, SSM_CHUNK)
        rows3 = rows.reshape(bsz, t, 4 * LANE)
        cmp = _nsa_compress_prompt(rows3, lp)
        y_nsa = _nsa_prompt_attn(q, small, cmp, rows3, win.reshape(bsz, t, 2 * LANE), bsz, t)
    else:
        z, xbc, q, rows, win, small = _norm_proj(y.reshape(bsz * t, d), norm_pre, segs, tm)
        rows_new = rows.reshape(bsz, t, 4, NSA_KV_HEADS, HEAD_DIM)
        conv_buf, h0, chunk = past['ssm_conv'], past['ssm'], t
        y_nsa, new_win = _nsa_decode_attn(q, small, rows, win, past['nsa_kv'], past['nsa_win'], page_table, lp,
                                          bsz, t)
    y_ssm, new_conv, new_ssm = _ssd_mixer(xbc, z, small, conv_buf, h0, lp, bsz, t, chunk)
    return y_ssm, y_nsa, (new_conv, new_ssm, rows_new, new_win)


def _layer_cd(y, norm_pre, segs, lp, past, is_prompt, tm):
    bsz, t, d = y.shape
    qkv, z, dil, small = _norm_proj(y.reshape(bsz * t, d), norm_pre, segs, tm, plane_segs=(2,))
    if is_prompt:
        conv_buf = jnp.zeros((bsz, CONV_K - 1, 3 * GDN_WIDTH), F32)
        s0 = jnp.zeros((bsz, GDN_HEADS, GDN_HEAD_DIM, GDN_HEAD_DIM), F32)
        chunk = math.gcd(t, GDN_CHUNK)
        parts = [_dil_prompt_attn(dil, gi, bsz, t) for gi in range(DIL_GROUPS)]
        y_dil = _dil_combine([p[0] for p in parts], [p[1] for p in parts], min(tm * 2, bsz * t))
        bufs = []
        for gi, (win, _) in enumerate(DIL_PATTERNS):
            w = min(win, t)
            dil4 = dil.reshape(DIL_IN // LANE, bsz, t, LANE)
            tail = lambda part: lax.slice(dil4, ((part * DIL_GROUPS + gi) * _DIL_PAIRS, 0, t - w, 0),
                                          ((part * DIL_GROUPS + gi + 1) * _DIL_PAIRS, bsz, t, LANE))
            kv = lax.optimization_barrier(jnp.stack([tail(1), tail(2)]))
            kv = kv.reshape(2, _DIL_PAIRS, bsz, w, 2, HEAD_DIM)
            bufs.append(jnp.transpose(kv, (2, 3, 0, 1, 4, 5)).reshape(bsz, w, 2, DIL_HEADS_PER_GROUP, HEAD_DIM))
        y_gdn, new_conv, new_gdn = _gdn_mixer(qkv, z, small, conv_buf, s0, lp, bsz, t, chunk)
    else:
        y_gdn, new_conv, new_gdn, y_dil, bufs = _gdn_mixer(qkv, z, small, past['gdn_conv'], past['gdn'], lp, bsz, t, t,
                                                            dil_decode=(dil, past['dil']))
    return y_gdn, y_dil, (new_conv, new_gdn, bufs[0], bufs[1], bufs[2])


def kernel(x_prompt, x_sample, cache_ssm_conv, state_ssm, cache_nsa_kv, cache_nsa_win_kv, cache_gdn_conv, state_gdn,
           cache_dil0_kv, cache_dil1_kv, cache_dil2_kv, page_table, norm_mix_pre, norm_mix_post, norm_mlp_pre,
           norm_mlp_post, mlp_w1, mlp_w2, ab_w_in, ab_w_out, ssm_conv_w, ssm_conv_b, ssm_dt_bias, ssm_a_log, ssm_d,
           ssm_norm_w, nsa_pe_k, nsa_pe_v, nsa_ck_w1, nsa_ck_w2, nsa_cv_w1, nsa_cv_w2, cd_w_in, cd_w_out, gdn_conv_w,
           gdn_dt_bias, gdn_a_log, gdn_norm_w):
    depth = norm_mix_pre.shape[0]
    yp, ys = x_prompt, x_sample
    ab_p, ab_s, cd_p, cd_s = [], [], [], []
    tm = 256
    for l in range(depth):
        j = l // 2
        if l % 2 == 0:
            lp = {'ssm_conv_w': ssm_conv_w[j], 'ssm_conv_b': ssm_conv_b[j], 'ssm_dt_bias': ssm_dt_bias[j],
                  'ssm_a_log': ssm_a_log[j], 'ssm_d': ssm_d[j], 'ssm_norm_w': ssm_norm_w[j],
                  'nsa_pe_k': nsa_pe_k[j], 'nsa_pe_v': nsa_pe_v[j], 'nsa_ck_w1': nsa_ck_w1[j],
                  'nsa_ck_w2': nsa_ck_w2[j], 'nsa_cv_w1': nsa_cv_w1[j], 'nsa_cv_w2': nsa_cv_w2[j]}
            past = {'ssm_conv': cache_ssm_conv[j], 'ssm': state_ssm[j], 'nsa_kv': cache_nsa_kv[j],
                    'nsa_win': cache_nsa_win_kv[j]}
            segs = _ab_weight_segs(ab_w_in[j])
            w_out = ab_w_out[j].astype(BF16)
            ka = SSM_INNER
            ap, bp, stp = _layer_ab(yp, norm_mix_pre[l], segs, lp, None, None, True, tm)
            as_, bs, sts = _layer_ab(ys, norm_mix_pre[l], segs, lp, past, page_table, False, tm)
            ab_p.append(stp)
            ab_s.append(sts)
        else:
            lp = {'gdn_conv_w': gdn_conv_w[j], 'gdn_dt_bias': gdn_dt_bias[j], 'gdn_a_log': gdn_a_log[j],
                  'gdn_norm_w': gdn_norm_w[j]}
            past = {'gdn_conv': cache_gdn_conv[j], 'gdn': state_gdn[j],
                    'dil': (cache_dil0_kv[j], cache_dil1_kv[j], cache_dil2_kv[j])}
            segs = _cd_weight_segs(cd_w_in[j])
            w_out = cd_w_out[j].astype(BF16)
            ka = GDN_WIDTH
            ap, bp, stp = _layer_cd(yp, norm_mix_pre[l], segs, lp, None, True, tm)
            as_, bs, sts = _layer_cd(ys, norm_mix_pre[l], segs, lp, past, False, tm)
            cd_p.append(stp)
            cd_s.append(sts)
        w1 = mlp_w1[l].astype(BF16)
        w2 = mlp_w2[l].astype(BF16)
        post = functools.partial(_post_block, wo_a=w_out[:ka], wo_b=w_out[ka:], w1=w1, w2=w2, n_mix=norm_mix_post[l],
                                 n_pre=norm_mlp_pre[l], n_post=norm_mlp_post[l], tm=tm)
        yp = post(ap, bp, yp.reshape(-1, D_MODEL)).reshape(yp.shape)
        ys = post(as_, bs, ys.reshape(-1, D_MODEL)).reshape(ys.shape)
    stack = lambda states, i: jnp.stack([s[i] for s in states])
    return (yp, ys,
            stack(ab_p, 0), stack(ab_s, 0), stack(ab_p, 1), stack(ab_s, 1),
            stack(ab_p, 2), stack(ab_s, 2), stack(ab_p, 3), stack(ab_s, 3),
            stack(cd_p, 0), stack(cd_s, 0), stack(cd_p, 1), stack(cd_s, 1),
            stack(cd_p, 2), stack(cd_s, 2), stack(cd_p, 3), stack(cd_s, 3),
            stack(cd_p, 4), stack(cd_s, 4))
```

```python
import functools
import math

import jax
import jax.numpy as jnp
from jax import lax
from jax.experimental import pallas as pl
from jax.experimental.pallas import tpu as pltpu

F32 = jnp.float32
BF16 = jnp.bfloat16

D_MODEL = 1024
HEAD_DIM = 64
CONV_K = 4
RMS_EPS = 1e-6
PAGE_SIZE = 128

SSM_HEADS = 16
SSM_HEAD_DIM = 64
SSM_INNER = SSM_HEADS * SSM_HEAD_DIM
SSM_GROUPS = 2
SSM_STATE = 128
SSM_CONV_DIM = SSM_INNER + 2 * SSM_GROUPS * SSM_STATE
SSM_CHUNK = 128

NSA_HEADS = 16
NSA_KV_HEADS = 2
NSA_REP = NSA_HEADS // NSA_KV_HEADS
NSA_BLOCK = 64
NSA_TOPK = 16
NSA_WINDOW = 512
NSA_CMP_HIDDEN = 256

GDN_HEADS = 8
GDN_HEAD_DIM = 128
GDN_WIDTH = GDN_HEADS * GDN_HEAD_DIM
GDN_CHUNK = 64

DIL_PATTERNS = ((128, 1), (512, 4), (2048, 16))
DIL_GROUPS = len(DIL_PATTERNS)
DIL_HEADS_PER_GROUP = 8
DIL_WIDTH = DIL_HEADS_PER_GROUP * HEAD_DIM
DIL_IN = 3 * DIL_GROUPS * DIL_WIDTH
NEG = -1e30

VMEM_LIMIT_BYTES = 56 * 1024 * 1024
LANE = 128
TOKEN_TILE = 256


def _cparams(*sem):
    return pltpu.CompilerParams(dimension_semantics=sem, vmem_limit_bytes=VMEM_LIMIT_BYTES)


def _const_spec(shape):
    nd = len(shape)
    return pl.BlockSpec(shape, lambda *_: (0,) * nd, pipeline_mode=pl.Buffered(1))


def _rms(x, w):
    return x * lax.rsqrt(jnp.mean(x * x, axis=-1, keepdims=True) + RMS_EPS) * w


def _proj_body(x_ref, nw_ref, *refs, n_t):
    n = len(refs) // 2
    xn = _rms(x_ref[...], nw_ref[...]).astype(BF16)
    for k, (w_ref, o_ref) in enumerate(zip(refs[:n], refs[n:])):
        if k >= n - n_t:
            o_ref[...] = lax.dot_general(w_ref[...], xn, _NT, preferred_element_type=F32)
            continue
        y = jnp.dot(xn, w_ref[...], preferred_element_type=F32)
        if len(o_ref.shape) == 2:
            o_ref[...] = y
        else:
            for j in range(o_ref.shape[0]):
                o_ref[j] = y[:, j * LANE:(j + 1) * LANE]


def _norm_proj(x, norm_w, w_segs, tm, plane_segs=(), t_segs=(), seq_len=None):
    n_tok, d = x.shape
    tm = min(tm, n_tok)
    widths = [w.shape[1] for w in w_segs]
    planes = [k in plane_segs for k in range(len(w_segs))]
    out_spec = lambda n, p: (pl.BlockSpec((n // LANE, tm, LANE), lambda i: (0, i, 0)) if p
                             else pl.BlockSpec((tm, n), lambda i: (i, 0)))
    out_sds = lambda n, p: jax.ShapeDtypeStruct((n // LANE, n_tok, LANE) if p else (n_tok, n), F32)
    per_seq = seq_len // tm if t_segs else 1
    t_spec = lambda w: pl.BlockSpec((None, w.shape[0], tm), lambda i: (i // per_seq, 0, i % per_seq))
    t_sds = lambda w: jax.ShapeDtypeStruct((n_tok // seq_len, w.shape[0], seq_len), F32)
    return pl.pallas_call(
        functools.partial(_proj_body, n_t=len(t_segs)),
        grid=(n_tok // tm,),
        in_specs=[pl.BlockSpec((tm, d), lambda i: (i, 0)), _const_spec((1, d))]
        + [_const_spec((d, n)) for n in widths] + [_const_spec(w.shape) for w in t_segs],
        out_specs=[out_spec(n, p) for n, p in zip(widths, planes)] + [t_spec(w) for w in t_segs],
        out_shape=[out_sds(n, p) for n, p in zip(widths, planes)] + [t_sds(w) for w in t_segs],
        compiler_params=_cparams("arbitrary"),
        name="norm_proj",
    )(x, norm_w.reshape(1, d), *w_segs, *t_segs)


def _post_body(a_ref, b_ref, y_ref, woa_ref, wob_ref, w1_ref, w2_ref, nmix_ref, npre_ref, npost_ref, o_ref):
    m = jnp.dot(a_ref[...].astype(BF16), woa_ref[...], preferred_element_type=F32)
    m = m + jnp.dot(b_ref[...].astype(BF16), wob_ref[...], preferred_element_type=F32)
    y1 = y_ref[...] + _rms(m, nmix_ref[...])
    h = _rms(y1, npre_ref[...]).astype(BF16)
    a = jnp.maximum(jnp.dot(h, w1_ref[...], preferred_element_type=F32), 0.0)
    m2 = jnp.dot((a * a).astype(BF16), w2_ref[...], preferred_element_type=F32)
    o_ref[...] = y1 + _rms(m2, npost_ref[...])


def _post_block(mix_a, mix_b, y, wo_a, wo_b, w1, w2, n_mix, n_pre, n_post, tm):
    n_tok, d = y.shape
    tm = min(tm, n_tok)
    ka, kb = mix_a.shape[1], mix_b.shape[1]
    row = lambda n: pl.BlockSpec((tm, n), lambda i: (i, 0))
    return pl.pallas_call(
        _post_body,
        grid=(n_tok // tm,),
        in_specs=[row(ka), row(kb), row(d), _const_spec(wo_a.shape), _const_spec(wo_b.shape),
                  _const_spec(w1.shape), _const_spec(w2.shape), _const_spec((1, d)), _const_spec((1, d)),
                  _const_spec((1, d))],
        out_specs=row(d),
        out_shape=jax.ShapeDtypeStruct((n_tok, d), F32),
        compiler_params=_cparams("arbitrary"),
        name="post_block",
    )(mix_a, mix_b, y, wo_a, wo_b, w1, w2, n_mix.reshape(1, d), n_pre.reshape(1, d), n_post.reshape(1, d))


def _compress_rows(src_ref, pe_ref, w1_ref, w2_ref, nb):
    hd = HEAD_DIM
    low = _half_mask((nb, LANE), 0)
    rows_per_step = 4 * LANE // (2 * hd)

    def pack(a, b):
        top = jnp.where(low, a, pltpu.roll(b, hd, axis=1))
        bot = jnp.where(low, pltpu.roll(a, hd, axis=1), b)
        return jnp.concatenate([top, bot], axis=0)

    def body(i, acc):
        r = i * rows_per_step
        x = [src_ref[pl.ds(r + k, nb, stride=NSA_BLOCK), :] + pe_ref[pl.ds(r + k, 1), :] for k in range(rows_per_step)]
        x4 = jnp.concatenate([pack(x[0], x[1]), pack(x[2], x[3])], axis=1).astype(BF16)
        w4 = w1_ref[pl.ds(r, rows_per_step)].reshape(rows_per_step * hd, NSA_CMP_HIDDEN)
        return acc + jnp.dot(x4, w4, preferred_element_type=F32)

    hid = lax.fori_loop(0, NSA_BLOCK // rows_per_step, body, jnp.zeros((2 * nb, NSA_CMP_HIDDEN), F32), unroll=4)
    hid = hid * jax.nn.sigmoid(hid)
    out = jnp.dot(hid.astype(BF16), w2_ref[...], preferred_element_type=F32)
    return jnp.concatenate([out[:nb], out[nb:]], axis=1)


def _compress_body(kc_ref, vc_ref, pek_ref, pev_ref, wk1_ref, wk2_ref, wv1_ref, wv2_ref, o_ref, *, nb):
    o_ref[:, :LANE] = _compress_rows(kc_ref, pek_ref, wk1_ref, wk2_ref, nb)
    o_ref[:, LANE:] = _compress_rows(vc_ref, pev_ref, wv1_ref, wv2_ref, nb)


def _pe2(pe):
    return jnp.concatenate([pe, pe], axis=1)


def _nsa_compress_prompt(rows, lp):
    bsz, t, _ = rows.shape
    nb = t // NSA_BLOCK
    w1 = lambda w: w.reshape(NSA_BLOCK, HEAD_DIM, NSA_CMP_HIDDEN).astype(BF16)
    col = lambda c: pl.BlockSpec((None, t, LANE), lambda b: (b, 0, c))
    return pl.pallas_call(
        functools.partial(_compress_body, nb=nb),
        grid=(bsz,),
        in_specs=[col(0), col(1), _const_spec((NSA_BLOCK, LANE)), _const_spec((NSA_BLOCK, LANE)),
                  _const_spec((NSA_BLOCK, HEAD_DIM, NSA_CMP_HIDDEN)), _const_spec((NSA_CMP_HIDDEN, HEAD_DIM)),
                  _const_spec((NSA_BLOCK, HEAD_DIM, NSA_CMP_HIDDEN)), _const_spec((NSA_CMP_HIDDEN, HEAD_DIM))],
        out_specs=pl.BlockSpec((None, nb, 2 * LANE), lambda b: (b, 0, 0)),
        out_shape=jax.ShapeDtypeStruct((bsz, nb, 2 * LANE), F32),
        compiler_params=_cparams("arbitrary"),
        name="nsa_compress",
    )(rows, rows, _pe2(lp['nsa_pe_k']), _pe2(lp['nsa_pe_v']), w1(lp['nsa_ck_w1']), lp['nsa_ck_w2'].astype(BF16),
      w1(lp['nsa_cv_w1']), lp['nsa_cv_w2'].astype(BF16))


_NT = (((1,), (1,)), ((), ()))


def _half_mask(shape, g):
    lane = lax.broadcasted_iota(jnp.int32, shape, len(shape) - 1)
    return lane >= HEAD_DIM if g else lane < HEAD_DIM


def _stack_heads(x, heads, g):
    keep = _half_mask((x.shape[0], LANE), g)
    out = []
    for h in heads:
        blk = x[:, (h // 2) * LANE:(h // 2 + 1) * LANE]
        if h % 2 != g:
            blk = pltpu.roll(blk, HEAD_DIM, axis=1)
        out.append(jnp.where(keep, blk, 0.0))
    return jnp.concatenate(out, axis=0)


def _unstack_pair(a, b, g):
    if g == 0:
        b = pltpu.roll(b, HEAD_DIM, axis=1)
    else:
        a = pltpu.roll(a, HEAD_DIM, axis=1)
    return jnp.where(_half_mask(a.shape, 1), b, a)


def _row_gate(sig, lanes, rows):
    return jnp.concatenate([jnp.broadcast_to(sig[:, c:c + 1], (rows, LANE)) for c in lanes], axis=0)


def _nsa_select(p3, pos, nb, n_sel):
    nq = p3.shape[1]
    imp = jnp.sum(p3, axis=0)
    n_i = lax.broadcasted_iota(jnp.int32, (nq, nb), 1)
    cur = pos // NSA_BLOCK
    score = jnp.where(n_i == 0, NSA_REP + 1.0, imp)
    score = jnp.where(n_i == cur, NSA_REP + 1.0, score)
    score = jnp.where(n_i == cur - 1, NSA_REP + 1.0, score)
    score = jnp.where(n_i > cur, -1.0, score)
    rank = jnp.zeros((nq, nb), F32)
    for m in range(nb):
        col = score[:, m:m + 1]
        tie = jnp.where(n_i > m, 1.0, 0.0)
        rank = rank + jnp.where(col > score, 1.0, jnp.where(col == score, tie, 0.0))
    return jnp.where(rank < n_sel, 1.0, 0.0)


def _nsa_select_t(imp, pos, n_sel):
    nb, nq = imp.shape
    n_i = lax.broadcasted_iota(jnp.int32, (nb, nq), 0)
    cur = pos // NSA_BLOCK
    score = jnp.where(n_i == 0, NSA_REP + 1.0, imp)
    score = jnp.where(n_i == cur, NSA_REP + 1.0, score)
    score = jnp.where(n_i == cur - 1, NSA_REP + 1.0, score)
    score = jnp.where(n_i > cur, -1.0, score)
    rank = jnp.zeros((nb, nq), F32)
    for m in range(nb):
        row = score[m:m + 1, :]
        tie = jnp.where(n_i > m, 1.0, 0.0)
        rank = rank + jnp.where(row > score, 1.0, jnp.where(row == score, tie, 0.0))
    return jnp.where(rank < n_sel, 1.0, 0.0)


def _lane_tile(x, n):
    return jnp.concatenate([x] * n, axis=1)


_V_ROWS = HEAD_DIM + 16
_NSA_QUERY_BLOCK = 256
_SEL_TILE = 512
_WIN_TILE = 256


def _flash_tile_t(kt, vt_aug, q_t, valid, m_ref, acc_ref, g):
    nk, nq = valid.shape
    rep = q_t.shape[1] // nq
    s = jnp.dot(kt, q_t, preferred_element_type=F32)
    s = jnp.concatenate([jnp.where(valid, s[:, r * nq:(r + 1) * nq], NEG) for r in range(rep)], axis=1)
    m_old = m_ref[g, 0:1, :]
    m_new = jnp.maximum(m_old, jnp.max(s, axis=0, keepdims=True))
    alpha = jnp.exp2(m_old - m_new)
    p = jnp.exp2(s - m_new)
    acc_ref[g] = alpha * acc_ref[g] + jnp.dot(vt_aug, p.astype(BF16), preferred_element_type=F32)
    m_ref[g] = jnp.broadcast_to(m_new, m_ref.shape[1:])


def _nsa_prompt_body(q_ref, small_ref, cmp_ref, ks_ref, vs_ref, kw_ref, vw_ref, o_ref, vst_ref, vwt_ref, m_ref, acc_ref,
                     *, nb, n_sel):
    qb = q_ref.shape[0]
    t = ks_ref.shape[0]
    i = pl.program_id(1)
    q0 = i * qb
    rep = NSA_REP
    hd = HEAD_DIM
    groups = range(NSA_KV_HEADS)

    @pl.when(i == 0)
    def _():
        ones = jnp.ones((_V_ROWS - hd, t), BF16)
        for g in groups:
            vst_ref[g, hd:, :] = ones
            vwt_ref[g, hd:, :] = ones

        def fill(j, carry):
            k0 = pl.multiple_of(j * qb, qb)
            for src, dst in ((vs_ref, vst_ref), (vw_ref, vwt_ref)):
                v_t = src[pl.ds(k0, qb), :].T.astype(BF16)
                for g in groups:
                    dst[g, :hd, pl.ds(k0, qb)] = v_t[g * hd:(g + 1) * hd, :]
            return carry

        lax.fori_loop(0, t // qb, fill, 0)

    pos_r = q0 + lax.broadcasted_iota(jnp.int32, (1, qb), 1)
    sig_t = jax.nn.sigmoid(small_ref[...]).T
    q = q_ref[...] * (hd ** -0.5)
    pairs_t = [q[:, c * LANE:(c + 1) * LANE].T for c in range(NSA_HEADS // 2)]
    head_t = lambda h: pairs_t[h // 2][(h % 2) * hd:(h % 2 + 1) * hd, :]
    zeros = jnp.zeros((hd, rep * qb), F32)
    q_ts, sels = [], []
    for g in groups:
        qg_t = jnp.concatenate([head_t(g * rep + r) for r in range(rep)], axis=1)
        stack = lambda x: jnp.concatenate([x, zeros] if g == 0 else [zeros, x], axis=0).astype(BF16)
        q_t = stack(qg_t)
        q_ts.append(stack(qg_t * math.log2(math.e)))
        gate_row = lambda j: jnp.concatenate(
            [sig_t[SSM_HEADS + j * NSA_HEADS + g * rep + r:SSM_HEADS + j * NSA_HEADS + g * rep + r + 1, :]
             for r in range(rep)], axis=1)
        kc = cmp_ref[:, :LANE].astype(BF16)
        vc_t = cmp_ref[:, LANE:].T[g * hd:(g + 1) * hd, :].astype(BF16)
        s = jnp.dot(kc, q_t, preferred_element_type=F32)
        n_i = lax.broadcasted_iota(jnp.int32, (nb, qb), 0)
        complete = _lane_tile(jnp.where((n_i + 1) * NSA_BLOCK <= pos_r + 1, 1.0, 0.0), rep)
        s = jnp.where(complete > 0.5, s, NEG)
        p = jnp.exp(s - jnp.max(s, axis=0, keepdims=True))
        p = p / jnp.sum(p, axis=0, keepdims=True)
        p = jnp.where(_lane_tile(pos_r, rep) >= NSA_BLOCK - 1, p, 0.0)
        o_cmp = jnp.dot(vc_t, p.astype(BF16), preferred_element_type=F32)
        acc_ref[2 + g, :hd, :] = gate_row(0) * o_cmp
        imp = p[:, :qb]
        for r in range(1, rep):
            imp = imp + p[:, r * qb:(r + 1) * qb]
        sels.append(_nsa_select_t(imp, pos_r, n_sel).astype(BF16))

    def flash_init():
        m_ref[...] = jnp.full(m_ref.shape, NEG, F32)
        acc_ref[0:2] = jnp.zeros((2,) + acc_ref.shape[1:], F32)

    def flash_out(g):
        acc = acc_ref[g]
        return acc[:hd, :] / acc[hd:hd + 1, :]

    def flash_pass(k_ref, vt_ref, kb, first_key, valid_fn):
        kpos_i = lax.broadcasted_iota(jnp.int32, (kb, qb), 0)
        qpos = q0 + lax.broadcasted_iota(jnp.int32, (kb, qb), 1)
        flash_init()

        def step(j, carry):
            k0 = pl.multiple_of(j * kb, kb)
            kt = k_ref[pl.ds(k0, kb), :].astype(BF16)
            valids = valid_fn(k0, k0 + kpos_i, qpos)
            for g in groups:
                _flash_tile_t(kt, vt_ref[g, :, pl.ds(k0, kb)], q_ts[g], valids[g], m_ref, acc_ref, g)
            return carry

        lax.fori_loop(first_key // kb, (q0 + qb - 1) // kb + 1, step, 0)

    tile = lambda keys: keys if t % keys == 0 else qb

    def sel_valid(k0, kpos, qpos):
        kb = kpos.shape[0]
        blk = lax.broadcasted_iota(jnp.int32, (kb, nb), 1)
        key = lax.broadcasted_iota(jnp.int32, (kb, nb), 0)
        expand = jnp.where(blk == (k0 + key) // NSA_BLOCK, 1.0, 0.0).astype(BF16)
        causal = kpos <= qpos
        return [jnp.where(causal, jnp.dot(expand, sels[g], preferred_element_type=F32), 0.0) > 0.5 for g in groups]

    flash_pass(ks_ref, vst_ref, tile(_SEL_TILE), 0, sel_valid)
    for g in groups:
        gate_row = jnp.concatenate(
            [sig_t[SSM_HEADS + NSA_HEADS + g * rep + r:SSM_HEADS + NSA_HEADS + g * rep + r + 1, :] for r in range(rep)],
            axis=1)
        acc_ref[2 + g, :hd, :] += gate_row * flash_out(g)
    def win_valid(k0, kpos, qpos):
        diff = qpos - kpos
        valid = jnp.where(diff >= 0, diff, NSA_WINDOW + 1) <= NSA_WINDOW
        return [valid for _ in groups]

    flash_pass(kw_ref, vwt_ref, tile(_WIN_TILE), jnp.maximum(q0 - NSA_WINDOW, 0), win_valid)
    for g in groups:
        gate_row = jnp.concatenate(
            [sig_t[SSM_HEADS + 2 * NSA_HEADS + g * rep + r:SSM_HEADS + 2 * NSA_HEADS + g * rep + r + 1, :]
             for r in range(rep)], axis=1)
        comb = acc_ref[2 + g, :hd, :] + gate_row * flash_out(g)
        for c in range(rep // 2):
            pair = jnp.concatenate([comb[:, (2 * c) * qb:(2 * c + 1) * qb], comb[:, (2 * c + 1) * qb:(2 * c + 2) * qb]],
                                   axis=0)
            col = (g * rep // 2 + c) * LANE
            o_ref[:, col:col + LANE] = pair.T


def _nsa_prompt_attn(q, small, cmp, rows, win, bsz, t):
    qb = math.gcd(t, _NSA_QUERY_BLOCK)
    nq = t // qb
    nb = cmp.shape[1]
    n_sel = min(NSA_TOPK, nb)
    tok = lambda n: pl.BlockSpec((qb, n), lambda b, i: (b * nq + i, 0))
    seq = lambda c: pl.BlockSpec((None, t, LANE), lambda b, i: (b, 0, c))
    v_t = pltpu.VMEM((NSA_KV_HEADS, _V_ROWS, t), BF16)
    return pl.pallas_call(
        functools.partial(_nsa_prompt_body, nb=nb, n_sel=n_sel),
        grid=(bsz, nq),
        in_specs=[tok(NSA_HEADS * HEAD_DIM), tok(LANE), pl.BlockSpec((None, nb, 2 * LANE), lambda b, i: (b, 0, 0)),
                  seq(2), seq(3), seq(0), seq(1)],
        out_specs=tok(NSA_HEADS * HEAD_DIM),
        out_shape=jax.ShapeDtypeStruct((bsz * t, NSA_HEADS * HEAD_DIM), F32),
        scratch_shapes=[v_t, v_t, pltpu.VMEM((NSA_KV_HEADS, 8, NSA_REP * qb), F32),
                        pltpu.VMEM((2 * NSA_KV_HEADS, _V_ROWS, NSA_REP * qb), F32)],
        compiler_params=_cparams("arbitrary", "arbitrary"),
        name="nsa_prompt_attn",
    )(q, small, cmp, rows, rows, win, win)


_TN = (((0,), (0,)), ((), ()))


def _silu(x):
    return x * jax.nn.sigmoid(x)


def _softplus(x):
    return jnp.maximum(x, 0.0) + jnp.log(1.0 + jnp.exp(-jnp.abs(x)))


def _conv_silu(x_ref, buf_ref, w_ref, b_ref, xp_ref, tail_ref, first):
    n = x_ref.shape[0]

    @pl.when(first)
    def _():
        xp_ref[8 - (CONV_K - 1):8, :] = buf_ref[...]

    xp_ref[8:8 + n, :] = x_ref[...]
    y = xp_ref[8:8 + n, :] * w_ref[CONV_K - 1:CONV_K, :]
    for k in range(CONV_K - 1):
        y = y + xp_ref[5 + k:5 + k + n, :] * w_ref[k:k + 1, :]
    if b_ref is not None:
        y = y + b_ref[...]
    tail = xp_ref[8 + n - (CONV_K - 1):8 + n, :]
    tail_ref[...] = tail
    xp_ref[8 - (CONV_K - 1):8, :] = tail
    return _silu(y)


def _bf16_terms(x):
    hi = x.astype(BF16)
    r = x - hi.astype(F32)
    mid = r.astype(BF16)
    return hi, mid, (r - mid.astype(F32)).astype(BF16)


def _cumsum_rows(x, seg=None):
    n = x.shape[0]
    i = lax.broadcasted_iota(jnp.int32, (n, n), 0)
    j = lax.broadcasted_iota(jnp.int32, (n, n), 1)
    keep = i >= j if seg in (None, n) else jnp.where(i // seg == j // seg, i - j, -1) >= 0
    tri = jnp.where(keep, 1.0, 0.0).astype(BF16)
    return sum(jnp.dot(tri, term, preferred_element_type=F32) for term in _bf16_terms(x))


def _expand_heads(x, width, lane0=0, n_out=D_MODEL):
    h_i = lax.broadcasted_iota(jnp.int32, (LANE, n_out), 0)
    c_i = lax.broadcasted_iota(jnp.int32, (LANE, n_out), 1)
    sel = jnp.where(c_i // width + lane0 == h_i, 1.0, 0.0).astype(BF16)
    return sum(jnp.dot(term, sel, preferred_element_type=F32) for term in _bf16_terms(x))


def _decay_matrix(col, row, strict=False):
    n = col.shape[0]
    i = lax.broadcasted_iota(jnp.int32, (n, n), 0)
    j = lax.broadcasted_iota(jnp.int32, (n, n), 1)
    keep = (i > j) if strict else (i >= j)
    return jnp.exp(jnp.where(keep, col - row, NEG))


def _ssd_body(xbc_ref, z_ref, small_ref, buf_ref, h0_ref, cw_ref, cb_ref, dtb_ref, alog_ref, dx_ref, nw_ref,
              y_ref, conv_ref, h_ref, xp_ref):
    c = pl.program_id(1)
    n = xbc_ref.shape[0]
    hp = SSM_HEAD_DIM
    rep = SSM_HEADS // SSM_GROUPS

    @pl.when(c == 0)
    def _():
        h_ref[...] = h0_ref[...]

    act = _conv_silu(xbc_ref, buf_ref, cw_ref, cb_ref, xp_ref, conv_ref, c == 0)
    xs = act[:, :SSM_INNER]
    head_lane = lax.broadcasted_iota(jnp.int32, (n, LANE), 1) < SSM_HEADS
    dt = jnp.where(head_lane, _softplus(small_ref[...] + dtb_ref[...]), 0.0)
    la = dt * (-jnp.exp(alog_ref[...]))
    acs = _cumsum_rows(la)
    acs_t = acs.T
    xd = xs * _expand_heads(dt, hp)
    e_acs = _expand_heads(jnp.exp(acs), hp)
    xdd = (xd * _expand_heads(jnp.exp(acs[n - 1:n, :] - acs), hp)).astype(BF16)
    xd = xd.astype(BF16)
    lane2 = _half_mask((n, LANE), 1)
    for g in range(SSM_GROUPS):
        bc = act[:, SSM_INNER + g * SSM_STATE:SSM_INNER + (g + 1) * SSM_STATE].astype(BF16)
        cc = act[:, SSM_INNER + (SSM_GROUPS + g) * SSM_STATE:SSM_INNER + (SSM_GROUPS + g + 1) * SSM_STATE].astype(BF16)
        cb = lax.dot_general(cc, bc, _NT, preferred_element_type=F32)
        h_prev = h_ref[g * rep:(g + 1) * rep].reshape(rep * hp, SSM_STATE)
        y_off = lax.dot_general(cc, h_prev.astype(BF16), _NT, preferred_element_type=F32)
        for pair in range(rep // 2):
            halves = []
            for k in range(2):
                h = g * rep + 2 * pair + k
                lm = _decay_matrix(acs[:, h:h + 1], acs_t[h:h + 1, :])
                halves.append(jnp.dot((cb * lm).astype(BF16), xd[:, (h // 2) * LANE:(h // 2 + 1) * LANE],
                                      preferred_element_type=F32))
            col = (g * rep + 2 * pair) * hp
            y_ref[:, col:col + LANE] = jnp.where(lane2, halves[1], halves[0]) + y_off[:, 2 * pair * hp:2 * pair * hp + LANE] * e_acs[:, col:col + LANE]
        st = lax.dot_general(xdd[:, g * rep * hp:(g + 1) * rep * hp], bc, _TN, preferred_element_type=F32)
        for r in range(rep):
            h = g * rep + r
            dec = jnp.exp(acs_t[h:h + 1, n - 1:n])
            h_ref[h] = h_ref[h] * dec + st[r * hp:(r + 1) * hp, :]
    y = y_ref[...] + dx_ref[...] * xs
    y_ref[...] = _rms(y * _silu(z_ref[...]), nw_ref[...])


def _ssd_mixer(xbc, z, small, conv_buf, h0, lp, bsz, t, chunk):
    nc = t // chunk
    tok = lambda n: pl.BlockSpec((chunk, n), lambda b, c: (b * nc + c, 0))
    per_b = lambda shape: pl.BlockSpec((None,) + shape, lambda b, c: (b,) + (0,) * len(shape))
    pad = lambda v: jnp.pad(v.astype(F32), (0, LANE - v.shape[0])).reshape(1, LANE)
    cdim = SSM_CONV_DIM
    state = (SSM_HEADS, SSM_HEAD_DIM, SSM_STATE)
    return pl.pallas_call(
        _ssd_body,
        grid=(bsz, nc),
        in_specs=[tok(cdim), tok(SSM_INNER), tok(LANE), per_b((CONV_K - 1, cdim)), per_b(state),
                  _const_spec((CONV_K, cdim)), _const_spec((1, cdim)), _const_spec((1, LANE)), _const_spec((1, LANE)),
                  _const_spec((1, SSM_INNER)), _const_spec((1, SSM_INNER))],
        out_specs=[tok(SSM_INNER), per_b((CONV_K - 1, cdim)), per_b(state)],
        out_shape=[jax.ShapeDtypeStruct((bsz * t, SSM_INNER), F32),
                   jax.ShapeDtypeStruct((bsz, CONV_K - 1, cdim), F32),
                   jax.ShapeDtypeStruct((bsz,) + state, F32)],
        scratch_shapes=[pltpu.VMEM((chunk + 8, cdim), F32)],
        compiler_params=_cparams("arbitrary", "arbitrary"),
        name="ssd_mixer",
    )(xbc, z, small, conv_buf, h0, lp['ssm_conv_w'], lp['ssm_conv_b'].reshape(1, cdim), pad(lp['ssm_dt_bias']),
      pad(lp['ssm_a_log']), jnp.repeat(lp['ssm_d'].astype(F32), SSM_HEAD_DIM).reshape(1, SSM_INNER),
      lp['ssm_norm_w'].reshape(1, SSM_INNER))


def _split_bf16(x):
    hi = x.astype(BF16)
    return hi, (x - hi.astype(F32)).astype(BF16)


def _bmm(a, b):
    return jnp.einsum('hmk,hkn->hmn', a, b, preferred_element_type=F32)


def _bmm_nt(a, b):
    return jnp.einsum('hmk,hnk->hmn', a, b, preferred_element_type=F32)


def _bmm_tn(a, b):
    return jnp.einsum('hkm,hkn->hmn', a, b, preferred_element_type=F32)


def _bmm_split(a, b):
    a_hi, a_lo = _split_bf16(a)
    b_hi, b_lo = _split_bf16(b)
    return _bmm(a_hi, b_hi) + _bmm(a_hi, b_lo) + _bmm(a_lo, b_hi)


def _unit_lower_inverse(a):
    n = a.shape[-1]
    eye = jnp.where(lax.broadcasted_iota(jnp.int32, (n, n), 0) == lax.broadcasted_iota(jnp.int32, (n, n), 1), 1.0, 0.0)
    p = -a
    t = eye + p
    span = 2
    while span < n:
        p = _bmm_split(p, p)
        t = t + _bmm_split(t, p)
        span *= 2
    return t


_GDN_CHUNKS_PER_STEP = 4


def _gdn_body(qkv_ref, z_ref, small_ref, buf_ref, s0_ref, cw_ref, alog_ref, dtb_ref, nw_ref,
              y_ref, conv_ref, s_ref, xp_ref, *, chunk):
    c = pl.program_id(1)
    n = qkv_ref.shape[0]
    dk = GDN_HEAD_DIM
    nh = GDN_HEADS
    n_ck = n // chunk

    @pl.when(c == 0)
    def _():
        s_ref[...] = s0_ref[...]

    act = _conv_silu(qkv_ref, buf_ref, cw_ref, None, xp_ref, conv_ref, c == 0)
    lane = lax.broadcasted_iota(jnp.int32, (n, LANE), 1)
    raw = small_ref[...]
    beta = jax.nn.sigmoid(raw)
    g = jnp.where((lane >= nh) & (lane < 2 * nh), -jnp.exp(alog_ref[...]) * _softplus(raw + dtb_ref[...]), 0.0)
    gc = _cumsum_rows(g, chunk)
    gc_t = gc.T
    gc_end = jnp.concatenate([jnp.broadcast_to(gc[(j + 1) * chunk - 1:(j + 1) * chunk, :], (chunk, LANE))
                              for j in range(n_ck)], axis=0)
    beta_x = _expand_heads(beta, dk)
    egc_x = _expand_heads(jnp.exp(gc), dk, nh)
    edec_x = _expand_heads(jnp.exp(gc_end - gc), dk, nh)
    ii = lax.broadcasted_iota(jnp.int32, (chunk, chunk), 0)
    jj = lax.broadcasted_iota(jnp.int32, (chunk, chunk), 1)
    heads = lambda x, base=0: jnp.stack([x[j * chunk:(j + 1) * chunk, base + h * dk:base + (h + 1) * dk]
                                         for j in range(n_ck) for h in range(nh)])
    q, k, v = heads(act), heads(act, nh * dk), heads(act, 2 * nh * dk)
    q = q * lax.rsqrt(jnp.sum(q * q, axis=-1, keepdims=True) + 1e-6) * (dk ** -0.5)
    k = k * lax.rsqrt(jnp.sum(k * k, axis=-1, keepdims=True) + 1e-6)
    beta3, egc, edec = heads(beta_x), heads(egc_x), heads(edec_x)
    kb = k * beta3
    decay = jnp.stack([_decay_matrix(gc[j * chunk:(j + 1) * chunk, nh + h:nh + h + 1],
                                     gc_t[nh + h:nh + h + 1, j * chunk:(j + 1) * chunk])
                       for j in range(n_ck) for h in range(nh)])
    k16 = k.astype(BF16)
    amat = jnp.where(ii > jj, _bmm_nt(kb.astype(BF16), k16) * decay, 0.0)
    tmat = _unit_lower_inverse(amat).astype(BF16)
    u = _bmm(tmat, (v * beta3).astype(BF16))
    w = _bmm(tmat, (kb * egc).astype(BF16)).astype(BF16)
    qk = (_bmm_nt(q.astype(BF16), k16) * decay).astype(BF16)
    qg = (q * egc).astype(BF16)
    kdec = (k * edec).astype(BF16)
    s = s_ref[...]
    for j in range(n_ck):
        ck = slice(j * nh, (j + 1) * nh)
        s16 = s.astype(BF16)
        v16 = (u[ck] - _bmm(w[ck], s16)).astype(BF16)
        o = _bmm(qg[ck], s16) + _bmm(qk[ck], v16)
        last = (j + 1) * chunk - 1
        gl = jnp.stack([jnp.exp(gc_t[nh + h:nh + h + 1, last:last + 1]) for h in range(nh)])
        s = s * gl + _bmm_tn(kdec[ck], v16)
        y = _rms(o, nw_ref[...])
        for h in range(nh):
            sl = slice(h * dk, (h + 1) * dk)
            y_ref[j * chunk:(j + 1) * chunk, sl] = y[h] * _silu(z_ref[j * chunk:(j + 1) * chunk, sl])
    s_ref[...] = s


def _gdn_mixer(qkv, z, small, conv_buf, s0, lp, bsz, t, chunk, dil_decode=None):
    per_step = _GDN_CHUNKS_PER_STEP if (t // chunk) % _GDN_CHUNKS_PER_STEP == 0 else 1
    rows = per_step * chunk
    nc = t // rows
    tok = lambda n: pl.BlockSpec((rows, n), lambda b, c: (b * nc + c, 0))
    per_b = lambda shape: pl.BlockSpec((None,) + shape, lambda b, c: (b,) + (0,) * len(shape))
    pad8 =lambda v: jnp.pad(v.astype(F32), (GDN_HEADS, LANE - 2 * GDN_HEADS)).reshape(1, LANE)
    cdim = 3 * GDN_WIDTH
    state = (GDN_HEADS, GDN_HEAD_DIM, GDN_HEAD_DIM)
    operands = [qkv, z, small, conv_buf, s0, lp['gdn_conv_w'], pad8(lp['gdn_a_log']), pad8(lp['gdn_dt_bias']),
                lp['gdn_norm_w'].reshape(1, GDN_HEAD_DIM)]
    in_specs = [tok(cdim), tok(GDN_WIDTH), tok(LANE), per_b((CONV_K - 1, cdim)), per_b(state),
                _const_spec((CONV_K, cdim)), _const_spec((1, LANE)), _const_spec((1, LANE)),
                _const_spec((1, GDN_HEAD_DIM))]
    out_specs = [tok(GDN_WIDTH), per_b((CONV_K - 1, cdim)), per_b(state)]
    out_shape = [jax.ShapeDtypeStruct((bsz * t, GDN_WIDTH), F32),
                 jax.ShapeDtypeStruct((bsz, CONV_K - 1, cdim), F32),
                 jax.ShapeDtypeStruct((bsz,) + state, F32)]
    body, name = functools.partial(_gdn_body, chunk=chunk), "gdn_mixer"
    if dil_decode is not None:
        assert nc == 1
        d_ops, d_in, d_out, d_shape, unview = _dil_decode_operands(*dil_decode, bsz, t)
        operands, in_specs = operands + d_ops, in_specs + d_in
        out_specs, out_shape = out_specs + d_out, out_shape + d_shape
        body, name = functools.partial(_gdn_dil_decode_body, chunk=chunk), "gdn_dil_decode"
    outs = pl.pallas_call(
        body,
        grid=(bsz, nc),
        in_specs=in_specs,
        out_specs=out_specs,
        out_shape=out_shape,
        scratch_shapes=[pltpu.VMEM((rows + 8, cdim), F32)],
        compiler_params=_cparams("arbitrary", "arbitrary"),
        name=name,
    )(*operands)
    if dil_decode is None:
        return outs
    return list(outs[:4]) + [[unview(v) for v in outs[4:]]]


def _softmax_pv(s, valid, v16, n_rep=1):
    rows, nk = s.shape
    s = jnp.where(valid[None], s.reshape(n_rep, rows // n_rep, nk), NEG).reshape(rows, nk)
    m = jnp.max(s, axis=-1, keepdims=True)
    p = jnp.exp(s - m)
    l = jnp.sum(p, axis=-1, keepdims=True)
    return jnp.dot((p / l).astype(BF16), v16, preferred_element_type=F32), m + jnp.log(l)


_DIL_PAIRS = DIL_HEADS_PER_GROUP // 2


def _dil_prompt_body(q_ref, kp_ref, kc_ref, vp_ref, vc_ref, o_ref, lse_ref, *, window, step):
    i = pl.program_id(1)
    qb = q_ref.shape[1] // step
    scale = HEAD_DIM ** -0.5
    qpos = i * qb + lax.broadcasted_iota(jnp.int32, (qb, 2 * qb), 0)
    kpos = (i - 1) * qb + lax.broadcasted_iota(jnp.int32, (qb, 2 * qb), 1)
    diff = jnp.where(kpos >= 0, qpos - kpos, -1)
    valid = jnp.where(diff >= 0, diff, window + 1) <= window
    lower = _half_mask((qb, LANE), 0)

    def one_class(c, carry):
        rows = pl.ds(c, qb, stride=step)
        for pr in range(q_ref.shape[0]):
            q = q_ref[pr, rows, :] * scale
            kk = jnp.concatenate([kp_ref[pr, rows, :], kc_ref[pr, rows, :]], axis=0).astype(BF16)
            vv = jnp.concatenate([vp_ref[pr, rows, :], vc_ref[pr, rows, :]], axis=0).astype(BF16)
            o2, l2 = [], []
            for k in range(2):
                qh = jnp.where(_half_mask((qb, LANE), k), q, 0.0).astype(BF16)
                s = lax.dot_general(qh, kk, _NT, preferred_element_type=F32)
                o, lse = _softmax_pv(s, valid, vv)
                o2.append(o)
                l2.append(jnp.broadcast_to(lse, (qb, LANE)))
            o_ref[pr, rows, :] = jnp.where(lower, o2[0], o2[1])
            lse_ref[pr, rows, :] = jnp.where(lower, l2[0], l2[1])
        return carry

    lax.fori_loop(0, step, one_class, 0)


def _dil_prompt_attn(dil, gi, bsz, t):
    win, step = DIL_PATTERNS[gi]
    assert win % step == 0 and t % win == 0
    tile = win
    nq = t // tile
    pp = _DIL_PAIRS if _DIL_PAIRS * tile * LANE * 4 <= (1 << 20) else 1
    blk = lambda part, prev: pl.BlockSpec(
        (pp, tile, LANE),
        lambda b, i, h: ((part * DIL_GROUPS + gi) * (_DIL_PAIRS // pp) + h,
                         b * nq + (jnp.maximum(i - 1, 0) if prev else i), 0))
    out_spec = pl.BlockSpec((pp, tile, LANE), lambda b, i, h: (h, b * nq + i, 0))
    out_sds = jax.ShapeDtypeStruct((_DIL_PAIRS, bsz * t, LANE), F32)
    return pl.pallas_call(
        functools.partial(_dil_prompt_body, window=win // step, step=step),
        grid=(bsz, nq, _DIL_PAIRS // pp),
        in_specs=[blk(0, False), blk(1, True), blk(1, False), blk(2, True), blk(2, False)],
        out_specs=[out_spec, out_spec],
        out_shape=[out_sds, out_sds],
        compiler_params=_cparams("arbitrary", "arbitrary", "arbitrary"),
        name=f"dil_prompt_attn_{gi}",
    )(dil, dil, dil, dil, dil)


def _dil_combine_body(o0, o1, o2, l0, l1, l2, y_ref):
    for pr in range(_DIL_PAIRS):
        m = jnp.maximum(jnp.maximum(l0[pr], l1[pr]), l2[pr])
        e0, e1, e2 = jnp.exp(l0[pr] - m), jnp.exp(l1[pr] - m), jnp.exp(l2[pr] - m)
        den = e0 + e1 + e2
        y_ref[:, pr * LANE:(pr + 1) * LANE] = (e0 / den) * o0[pr] + (e1 / den) * o1[pr] + (e2 / den) * o2[pr]


def _dil_combine(outs, lses, tm):
    n_tok = outs[0].shape[1]
    spec = pl.BlockSpec((_DIL_PAIRS, tm, LANE), lambda i: (0, i, 0))
    return pl.pallas_call(
        _dil_combine_body,
        grid=(n_tok // tm,),
        in_specs=[spec] * 6,
        out_specs=pl.BlockSpec((tm, DIL_WIDTH), lambda i: (i, 0)),
        out_shape=jax.ShapeDtypeStruct((n_tok, DIL_WIDTH), F32),
        compiler_params=_cparams("arbitrary"),
        name="dil_combine",
    )(*outs, *lses)


def _attend_cached(qg, k_t, v_t, k_n, v_n, valid_c, valid_n, n_rep=1):
    rows = qg.shape[0]
    mask = lambda s, v: jnp.where(v[None], s.reshape(n_rep, rows // n_rep, s.shape[1]), NEG).reshape(rows, s.shape[1])
    s_c = mask(jnp.dot(qg, k_t, preferred_element_type=F32), valid_c)
    s_n = mask(lax.dot_general(qg, k_n, _NT, preferred_element_type=F32), valid_n)
    m = jnp.maximum(jnp.max(s_c, axis=-1, keepdims=True), jnp.max(s_n, axis=-1, keepdims=True))
    p_c = jnp.exp(s_c - m)
    p_n = jnp.exp(s_n - m)
    l = jnp.sum(p_c, axis=-1, keepdims=True) + jnp.sum(p_n, axis=-1, keepdims=True)
    o = lax.dot_general((p_c / l).astype(BF16), v_t, _NT, preferred_element_type=F32)
    o = o + jnp.dot((p_n / l).astype(BF16), v_n, preferred_element_type=F32)
    return o, m + jnp.log(l)


def _nsa_decode_body(pt_ref, q_ref, small_ref, rows_ref, win_ref, *refs, n_pages, lw, nb, nb_pad, n_sel):
    del pt_ref, nb
    pages = refs[:n_pages]
    (pastwin_ref, pek_ref, pev_ref, wk1_ref, wk2_ref, wv1_ref, wv2_ref, o_ref, neww_ref,
     kc_s, vc_s, comb_s) = refs[n_pages:]
    t = q_ref.shape[0]
    rep = NSA_REP
    past_len = n_pages * PAGE_SIZE
    for dst, c in ((kc_s, 0), (vc_s, 1)):
        for p in range(n_pages):
            dst[p * PAGE_SIZE:(p + 1) * PAGE_SIZE, :] = pages[p][c * LANE:(c + 1) * LANE, :].T
        dst[past_len:past_len + t, :] = rows_ref[:, c * LANE:(c + 1) * LANE]
        dst[past_len + t:, :] = jnp.zeros((dst.shape[0] - past_len - t, LANE), F32)
    kcmp = _compress_rows(kc_s, pek_ref, wk1_ref, wk2_ref, nb_pad).astype(BF16)
    vcmp = _compress_rows(vc_s, pev_ref, wv1_ref, wv2_ref, nb_pad).astype(BF16)
    page_rows = lambda c: jnp.concatenate([pages[p][c * LANE:(c + 1) * LANE, :] for p in range(n_pages)],
                                          axis=1).astype(BF16)
    ks_t, vs_t = page_rows(2), page_rows(3)
    ks_n, vs_n = rows_ref[:, 2 * LANE:3 * LANE].astype(BF16), rows_ref[:, 3 * LANE:].astype(BF16)
    kw_t, vw_t = pastwin_ref[:LANE, :].astype(BF16), pastwin_ref[LANE:, :].astype(BF16)
    kw_n, vw_n = win_ref[:, :LANE].astype(BF16), win_ref[:, LANE:].astype(BF16)
    neww_ref[...] = jnp.concatenate([pastwin_ref[:, t:], win_ref[...].T], axis=1)
    pos_c = past_len + lax.broadcasted_iota(jnp.int32, (t, 1), 0)
    new_i = lax.broadcasted_iota(jnp.int32, (t, t), 1)
    back = lax.broadcasted_iota(jnp.int32, (t, t), 0) - new_i
    sig = jax.nn.sigmoid(small_ref[...])
    q = q_ref[...] * (HEAD_DIM ** -0.5)
    for g in range(NSA_KV_HEADS):
        qg = _stack_heads(q, range(g * rep, (g + 1) * rep), g).astype(BF16)
        gate_lane = lambda j: [SSM_HEADS + j * NSA_HEADS + g * rep + r for r in range(rep)]
        s = lax.dot_general(qg, kcmp, _NT, preferred_element_type=F32).reshape(rep, t, nb_pad)
        n_i = lax.broadcasted_iota(jnp.int32, (t, nb_pad), 1)
        complete = (n_i + 1) * NSA_BLOCK <= pos_c + 1
        s = jnp.where(complete[None], s, NEG)
        p = jnp.exp(s - jnp.max(s, axis=-1, keepdims=True))
        p = p / jnp.sum(p, axis=-1, keepdims=True)
        p = jnp.where((pos_c >= NSA_BLOCK - 1)[None], p, 0.0)
        o_cmp = jnp.dot(p.reshape(rep * t, nb_pad).astype(BF16), vcmp, preferred_element_type=F32)
        comb = _row_gate(sig, gate_lane(0), t) * o_cmp
        sel = _nsa_select(p, pos_c, nb_pad, n_sel).astype(BF16)
        expand = lambda n_keys, first: jnp.where(
            lax.broadcasted_iota(jnp.int32, (nb_pad, n_keys), 0)
            == (first + lax.broadcasted_iota(jnp.int32, (nb_pad, n_keys), 1)) // NSA_BLOCK, 1.0, 0.0).astype(BF16)
        chosen_c = jnp.dot(sel, expand(past_len, 0), preferred_element_type=F32)
        chosen_n = jnp.dot(sel, expand(t, past_len), preferred_element_type=F32)
        kpos = lax.broadcasted_iota(jnp.int32, (t, past_len), 1)
        valid_c = jnp.where(kpos <= pos_c, chosen_c, 0.0) > 0.5
        valid_n = jnp.where(back >= 0, chosen_n, 0.0) > 0.5
        o_sel, _ = _attend_cached(qg, ks_t, vs_t, ks_n, vs_n, valid_c, valid_n, rep)
        comb = comb + _row_gate(sig, gate_lane(1), t) * o_sel
        diff = pos_c - (past_len - lw + lax.broadcasted_iota(jnp.int32, (t, lw), 1))
        valid_c = jnp.where(diff >= 0, diff, NSA_WINDOW + 1) <= NSA_WINDOW
        valid_n = jnp.where(back >= 0, back, NSA_WINDOW + 1) <= NSA_WINDOW
        o_win, _ = _attend_cached(qg, kw_t, vw_t, kw_n, vw_n, valid_c, valid_n, rep)
        comb_s[...] = comb + _row_gate(sig, gate_lane(2), t) * o_win
        for c in range(rep // 2):
            a = comb_s[(2 * c) * t:(2 * c + 1) * t, :]
            b = comb_s[(2 * c + 1) * t:(2 * c + 2) * t, :]
            col = (g * rep // 2 + c) * LANE
            o_ref[:, col:col + LANE] = _unstack_pair(a, b, g)


def _nsa_decode_attn(q, small, rows, win, cache_kv, cache_win, page_table, lp, bsz, t):
    n_pages = page_table.shape[1]
    past_len = n_pages * PAGE_SIZE
    lw = cache_win.shape[1]
    nb = -(-(past_len + t) // NSA_BLOCK)
    nb_pad = -(-nb // 8) * 8
    pages = jnp.transpose(cache_kv, (0, 2, 3, 4, 1)).reshape(cache_kv.shape[0], 4 * LANE, PAGE_SIZE)
    pastwin = jnp.transpose(cache_win, (0, 2, 3, 4, 1)).reshape(bsz, 2 * LANE, lw)
    w1 = lambda w: w.reshape(NSA_BLOCK, HEAD_DIM, NSA_CMP_HIDDEN).astype(BF16)
    tok = lambda n: pl.BlockSpec((t, n), lambda b, pt: (b, 0))
    const = lambda shape: pl.BlockSpec(shape, lambda b, pt: (0,) * len(shape), pipeline_mode=pl.Buffered(1))
    page_spec = lambda p: pl.BlockSpec((None, 4 * LANE, PAGE_SIZE), lambda b, pt: (pt[b * n_pages + p], 0, 0))
    win_spec = pl.BlockSpec((None, 2 * LANE, lw), lambda b, pt: (b, 0, 0))
    grid_spec = pltpu.PrefetchScalarGridSpec(
        num_scalar_prefetch=1,
        grid=(bsz,),
        in_specs=[tok(NSA_HEADS * HEAD_DIM), tok(LANE), tok(4 * LANE), tok(2 * LANE)]
        + [page_spec(p) for p in range(n_pages)]
        + [win_spec, const((NSA_BLOCK, LANE)), const((NSA_BLOCK, LANE)),
           const((NSA_BLOCK, HEAD_DIM, NSA_CMP_HIDDEN)), const((NSA_CMP_HIDDEN, HEAD_DIM)),
           const((NSA_BLOCK, HEAD_DIM, NSA_CMP_HIDDEN)), const((NSA_CMP_HIDDEN, HEAD_DIM))],
        out_specs=[tok(NSA_HEADS * HEAD_DIM), win_spec],
        scratch_shapes=[pltpu.VMEM((nb_pad * NSA_BLOCK, LANE), F32), pltpu.VMEM((nb_pad * NSA_BLOCK, LANE), F32),
                        pltpu.VMEM((NSA_REP * t, LANE), F32)],
    )
    y, new_win = pl.pallas_call(
        functools.partial(_nsa_decode_body, n_pages=n_pages, lw=lw, nb=nb, nb_pad=nb_pad, n_sel=min(NSA_TOPK, nb)),
        grid_spec=grid_spec,
        out_shape=[jax.ShapeDtypeStruct((bsz * t, NSA_HEADS * HEAD_DIM), F32),
                   jax.ShapeDtypeStruct((bsz, 2 * LANE, lw), F32)],
        compiler_params=_cparams("arbitrary"),
        name="nsa_decode_attn",
    )(page_table.reshape(-1), q, small, rows, win, *([pages] * n_pages), pastwin,
      _pe2(lp['nsa_pe_k']), _pe2(lp['nsa_pe_v']), w1(lp['nsa_ck_w1']), lp['nsa_ck_w2'].astype(BF16),
      w1(lp['nsa_cv_w1']), lp['nsa_cv_w2'].astype(BF16))
    new_win = jnp.transpose(new_win.reshape(bsz, 2, NSA_KV_HEADS, HEAD_DIM, lw), (0, 4, 1, 2, 3))
    return y, new_win


def _dil_decode_group(q, k_new, v_new, cache_ref, step):
    t = q.shape[0]
    lg = cache_ref.shape[1]
    pairs = DIL_HEADS_PER_GROUP // 2
    tok_c = lax.broadcasted_iota(jnp.int32, (2 * t, lg), 0) % t
    ahead = lax.broadcasted_iota(jnp.int32, (2 * t, lg), 1) - tok_c
    valid_c = jnp.where(ahead >= 0, ahead % step, 1) == 0
    tok_n = lax.broadcasted_iota(jnp.int32, (2 * t, t), 0) % t
    back = tok_n - lax.broadcasted_iota(jnp.int32, (2 * t, t), 1)
    valid_n = jnp.where(back >= 0, back % step, 1) == 0
    lower = _half_mask((t, LANE), 0)
    outs, lses = [], []
    for pr in range(pairs):
        sl = slice(pr * LANE, (pr + 1) * LANE)
        k_t = cache_ref[pr * LANE:(pr + 1) * LANE, :].astype(BF16)
        v_t = cache_ref[DIL_WIDTH + pr * LANE:DIL_WIDTH + (pr + 1) * LANE, :].astype(BF16)
        kn = k_new[:, sl].astype(BF16)
        vn = v_new[:, sl].astype(BF16)
        qp = q[:, sl]
        q2 = jnp.concatenate([jnp.where(lower, qp, 0.0), jnp.where(lower, 0.0, qp)], axis=0).astype(BF16)
        o, lse = _attend_cached(q2, k_t, v_t, kn, vn, valid_c, valid_n)
        lse = jnp.broadcast_to(lse, (2 * t, LANE))
        outs.append(jnp.where(lower, o[:t], o[t:]))
        lses.append(jnp.where(lower, lse[:t], lse[t:]))
    return jnp.concatenate(outs, axis=1), jnp.concatenate(lses, axis=1)


def _dil_decode_body(x_ref, c0_ref, c1_ref, c2_ref, y_ref, n0_ref, n1_ref, n2_ref):
    scale = HEAD_DIM ** -0.5
    t = x_ref.shape[1]
    res = []
    for gi, (cache_ref, new_ref) in enumerate(((c0_ref, n0_ref), (c1_ref, n1_ref), (c2_ref, n2_ref))):
        _, step = DIL_PATTERNS[gi]
        part = lambda p: jnp.concatenate(
            [x_ref[(p * DIL_GROUPS + gi) * _DIL_PAIRS + pr] for pr in range(_DIL_PAIRS)], axis=1)
        res.append(_dil_decode_group(part(0) * scale, part(1), part(2), cache_ref, step))
        new_t = jnp.concatenate([part(1).T, part(2).T], axis=0)
        new_ref[...] = jnp.concatenate([cache_ref[:, t:], new_t], axis=1)
    (o0, l0), (o1, l1), (o2, l2) = res
    m = jnp.maximum(jnp.maximum(l0, l1), l2)
    e0, e1, e2 = jnp.exp(l0 - m), jnp.exp(l1 - m), jnp.exp(l2 - m)
    den = e0 + e1 + e2
    y_ref[...] = (e0 / den) * o0 + (e1 / den) * o1 + (e2 / den) * o2


def _dil_decode_operands(dil, bufs, bsz, t):
    n_rows = 2 * DIL_WIDTH
    views = []
    for (win, _), buf in zip(DIL_PATTERNS, bufs):
        assert buf.shape[1] == win, "decode path needs a full window of cached rows"
        views.append(jnp.transpose(buf, (0, 2, 3, 4, 1)).reshape(bsz, n_rows, win))
    cache_spec = lambda v: pl.BlockSpec((None, n_rows, v.shape[2]), lambda b, *_: (b, 0, 0))
    in_specs = [pl.BlockSpec((DIL_IN // LANE, t, LANE), lambda b, *_: (0, b, 0))] + [cache_spec(v) for v in views]
    out_specs = [pl.BlockSpec((t, DIL_WIDTH), lambda b, *_: (b, 0))] + [cache_spec(v) for v in views]
    out_shape = [jax.ShapeDtypeStruct((bsz * t, DIL_WIDTH), F32)] + [jax.ShapeDtypeStruct(v.shape, F32) for v in views]
    unview = lambda v: jnp.transpose(v.reshape(bsz, 2, DIL_HEADS_PER_GROUP, HEAD_DIM, v.shape[2]), (0, 4, 1, 2, 3))
    return [dil] + views, in_specs, out_specs, out_shape, unview


def _gdn_dil_decode_body(*refs, chunk):
    n_gdn_in, n_dil_in, n_gdn_out, n_dil_out = 9, 4, 3, 4
    gdn_in, rest = refs[:n_gdn_in], refs[n_gdn_in:]
    dil_in, rest = rest[:n_dil_in], rest[n_dil_in:]
    gdn_out, rest = rest[:n_gdn_out], rest[n_gdn_out:]
    dil_out, scratch = rest[:n_dil_out], rest[n_dil_out:]
    _gdn_body(*gdn_in, *gdn_out, *scratch, chunk=chunk)
    _dil_decode_body(*dil_in, *dil_out)


_AB_CUTS = (0, SSM_INNER, SSM_INNER + SSM_CONV_DIM, SSM_INNER + SSM_CONV_DIM + SSM_HEADS)
_AB_Q0 = _AB_CUTS[3]
_AB_KV0 = _AB_Q0 + NSA_HEADS * HEAD_DIM
_AB_WIN0 = _AB_KV0 + 4 * NSA_KV_HEADS * HEAD_DIM
_AB_GATE0 = _AB_WIN0 + 2 * NSA_KV_HEADS * HEAD_DIM
_AB_END = _AB_GATE0 + 3 * NSA_HEADS


def _pad_cols(w, n):
    return jnp.pad(w, ((0, 0), (0, n - w.shape[1])))


def _ab_weight_segs(w_in):
    small = jnp.concatenate([w_in[:, _AB_CUTS[2]:_AB_CUTS[3]], w_in[:, _AB_GATE0:_AB_END]], axis=1)
    segs = [w_in[:, _AB_CUTS[0]:_AB_CUTS[1]], w_in[:, _AB_CUTS[1]:_AB_CUTS[2]], w_in[:, _AB_Q0:_AB_KV0],
            w_in[:, _AB_KV0:_AB_WIN0], w_in[:, _AB_WIN0:_AB_GATE0], _pad_cols(small, LANE)]
    return [s.astype(BF16) for s in segs]


_CD_Z0 = 3 * GDN_WIDTH
_CD_B0 = _CD_Z0 + GDN_WIDTH
_CD_DIL0 = _CD_B0 + 2 * GDN_HEADS
_CD_END = _CD_DIL0 + DIL_IN


def _cd_weight_segs(w_in):
    segs = [w_in[:, :_CD_Z0], w_in[:, _CD_Z0:_CD_B0], w_in[:, _CD_DIL0:_CD_END],
            _pad_cols(w_in[:, _CD_B0:_CD_DIL0], LANE)]
    return [s.astype(BF16) for s in segs]


def _layer_ab(y, norm_pre, segs, lp, past, page_table, is_prompt, tm):
    bsz, t, d = y.shape
    if is_prompt:
        z, xbc, q, rows, win, small, rows_t, win_t = _norm_proj(
            y.reshape(bsz * t, d), norm_pre, segs, tm, t_segs=(segs[3].T, segs[4].T), seq_len=t)
        to_rows = lambda x_t, kinds: jnp.transpose(
            x_t.reshape(bsz, kinds, NSA_KV_HEADS, HEAD_DIM, x_t.shape[-1]), (0, 4, 1, 2, 3))
        rows_new = to_rows(rows_t, 4)
        new_win = to_rows(win_t[:, :, t - min(NSA_WINDOW, t):], 2)
        conv_buf = jnp.zeros((bsz, CONV_K - 1, SSM_CONV_DIM), F32)
        h0 = jnp.zeros((bsz, SSM_HEADS, SSM_HEAD_DIM, SSM_STATE), F32)
        chunk = math.gcd(t, SSM_CHUNK)
        rows3 = rows.reshape(bsz, t, 4 * LANE)
        cmp = _nsa_compress_prompt(rows3, lp)
        y_nsa = _nsa_prompt_attn(q, small, cmp, rows3, win.reshape(bsz, t, 2 * LANE), bsz, t)
    else:
        z, xbc, q, rows, win, small = _norm_proj(y.reshape(bsz * t, d), norm_pre, segs, tm)
        rows_new = rows.reshape(bsz, t, 4, NSA_KV_HEADS, HEAD_DIM)
        conv_buf, h0, chunk = past['ssm_conv'], past['ssm'], t
        y_nsa, new_win = _nsa_decode_attn(q, small, rows, win, past['nsa_kv'], past['nsa_win'], page_table, lp,
                                          bsz, t)
    y_ssm, new_conv, new_ssm = _ssd_mixer(xbc, z, small, conv_buf, h0, lp, bsz, t, chunk)
    return y_ssm, y_nsa, (new_conv, new_ssm, rows_new, new_win)


def _layer_cd(y, norm_pre, segs, lp, past, is_prompt, tm):
    bsz, t, d = y.shape
    qkv, z, dil, small = _norm_proj(y.reshape(bsz * t, d), norm_pre, segs, tm, plane_segs=(2,))
    if is_prompt:
        conv_buf = jnp.zeros((bsz, CONV_K - 1, 3 * GDN_WIDTH), F32)
        s0 = jnp.zeros((bsz, GDN_HEADS, GDN_HEAD_DIM, GDN_HEAD_DIM), F32)
        chunk = math.gcd(t, GDN_CHUNK)
        parts = [_dil_prompt_attn(dil, gi, bsz, t) for gi in range(DIL_GROUPS)]
        y_dil = _dil_combine([p[0] for p in parts], [p[1] for p in parts], min(tm * 2, bsz * t))
        bufs = []
        for gi, (win, _) in enumerate(DIL_PATTERNS):
            w = min(win, t)
            dil4 = dil.reshape(DIL_IN // LANE, bsz, t, LANE)
            tail = lambda part: lax.slice(dil4, ((part * DIL_GROUPS + gi) * _DIL_PAIRS, 0, t - w, 0),
                                          ((part * DIL_GROUPS + gi + 1) * _DIL_PAIRS, bsz, t, LANE))
            kv = lax.optimization_barrier(jnp.stack([tail(1), tail(2)]))
            kv = kv.reshape(2, _DIL_PAIRS, bsz, w, 2, HEAD_DIM)
            bufs.append(jnp.transpose(kv, (2, 3, 0, 1, 4, 5)).reshape(bsz, w, 2, DIL_HEADS_PER_GROUP, HEAD_DIM))
        y_gdn, new_conv, new_gdn = _gdn_mixer(qkv, z, small, conv_buf, s0, lp, bsz, t, chunk)
    else:
        y_gdn, new_conv, new_gdn, y_dil, bufs = _gdn_mixer(qkv, z, small, past['gdn_conv'], past['gdn'], lp, bsz, t, t,
                                                            dil_decode=(dil, past['dil']))
    return y_gdn, y_dil, (new_conv, new_gdn, bufs[0], bufs[1], bufs[2])


def kernel(x_prompt, x_sample, cache_ssm_conv, state_ssm, cache_nsa_kv, cache_nsa_win_kv, cache_gdn_conv, state_gdn,
           cache_dil0_kv, cache_dil1_kv, cache_dil2_kv, page_table, norm_mix_pre, norm_mix_post, norm_mlp_pre,
           norm_mlp_post, mlp_w1, mlp_w2, ab_w_in, ab_w_out, ssm_conv_w, ssm_conv_b, ssm_dt_bias, ssm_a_log, ssm_d,
           ssm_norm_w, nsa_pe_k, nsa_pe_v, nsa_ck_w1, nsa_ck_w2, nsa_cv_w1, nsa_cv_w2, cd_w_in, cd_w_out, gdn_conv_w,
           gdn_dt_bias, gdn_a_log, gdn_norm_w):
    depth = norm_mix_pre.shape[0]
    yp, ys = x_prompt, x_sample
    ab_p, ab_s, cd_p, cd_s = [], [], [], []
    tm = TOKEN_TILE
    for l in range(depth):
        j = l // 2
        if l % 2 == 0:
            lp = {'ssm_conv_w': ssm_conv_w[j], 'ssm_conv_b': ssm_conv_b[j], 'ssm_dt_bias': ssm_dt_bias[j],
                  'ssm_a_log': ssm_a_log[j], 'ssm_d': ssm_d[j], 'ssm_norm_w': ssm_norm_w[j],
                  'nsa_pe_k': nsa_pe_k[j], 'nsa_pe_v': nsa_pe_v[j], 'nsa_ck_w1': nsa_ck_w1[j],
                  'nsa_ck_w2': nsa_ck_w2[j], 'nsa_cv_w1': nsa_cv_w1[j], 'nsa_cv_w2': nsa_cv_w2[j]}
            past = {'ssm_conv': cache_ssm_conv[j], 'ssm': state_ssm[j], 'nsa_kv': cache_nsa_kv[j],
                    'nsa_win': cache_nsa_win_kv[j]}
            segs = _ab_weight_segs(ab_w_in[j])
            w_out = ab_w_out[j].astype(BF16)
            ka = SSM_INNER
            ap, bp, stp = _layer_ab(yp, norm_mix_pre[l], segs, lp, None, None, True, tm)
            as_, bs, sts = _layer_ab(ys, norm_mix_pre[l], segs, lp, past, page_table, False, tm)
            ab_p.append(stp)
            ab_s.append(sts)
        else:
            lp = {'gdn_conv_w': gdn_conv_w[j], 'gdn_dt_bias': gdn_dt_bias[j], 'gdn_a_log': gdn_a_log[j],
                  'gdn_norm_w': gdn_norm_w[j]}
            past = {'gdn_conv': cache_gdn_conv[j], 'gdn': state_gdn[j],
                    'dil': (cache_dil0_kv[j], cache_dil1_kv[j], cache_dil2_kv[j])}
            segs = _cd_weight_segs(cd_w_in[j])
            w_out = cd_w_out[j].astype(BF16)
            ka = GDN_WIDTH
            ap, bp, stp = _layer_cd(yp, norm_mix_pre[l], segs, lp, None, True, tm)
            as_, bs, sts = _layer_cd(ys, norm_mix_pre[l], segs, lp, past, False, tm)
            cd_p.append(stp)
            cd_s.append(sts)
        w1 = mlp_w1[l].astype(BF16)
        w2 = mlp_w2[l].astype(BF16)
        post = functools.partial(_post_block, wo_a=w_out[:ka], wo_b=w_out[ka:], w1=w1, w2=w2, n_mix=norm_mix_post[l],
                                 n_pre=norm_mlp_pre[l], n_post=norm_mlp_post[l], tm=tm)
        yp = post(ap, bp, yp.reshape(-1, D_MODEL)).reshape(yp.shape)
        ys = post(as_, bs, ys.reshape(-1, D_MODEL)).reshape(ys.shape)
    stack = lambda states, i: jnp.stack([s[i] for s in states])
    return (yp, ys,
            stack(ab_p, 0), stack(ab_s, 0), stack(ab_p, 1), stack(ab_s, 1),
            stack(ab_p, 2), stack(ab_s, 2), stack(ab_p, 3), stack(ab_s, 3),
            stack(cd_p, 0), stack(cd_s, 0), stack(cd_p, 1), stack(cd_s, 1),
            stack(cd_p, 2), stack(cd_s, 2), stack(cd_p, 3), stack(cd_s, 3),
            stack(cd_p, 4), stack(cd_s, 4))
```

```python
import functools
import math

import jax
import jax.numpy as jnp
from jax import lax
from jax.experimental import pallas as pl
from jax.experimental.pallas import tpu as pltpu

F32 = jnp.float32
BF16 = jnp.bfloat16

D_MODEL = 1024
HEAD_DIM = 64
CONV_K = 4
RMS_EPS = 1e-6
PAGE_SIZE = 128

SSM_HEADS = 16
SSM_HEAD_DIM = 64
SSM_INNER = SSM_HEADS * SSM_HEAD_DIM
SSM_GROUPS = 2
SSM_STATE = 128
SSM_CONV_DIM = SSM_INNER + 2 * SSM_GROUPS * SSM_STATE
SSM_CHUNK = 128

NSA_HEADS = 16
NSA_KV_HEADS = 2
NSA_REP = NSA_HEADS // NSA_KV_HEADS
NSA_BLOCK = 64
NSA_TOPK = 16
NSA_WINDOW = 512
NSA_CMP_HIDDEN = 256

GDN_HEADS = 8
GDN_HEAD_DIM = 128
GDN_WIDTH = GDN_HEADS * GDN_HEAD_DIM
GDN_CHUNK = 64

DIL_PATTERNS = ((128, 1), (512, 4), (2048, 16))
DIL_GROUPS = len(DIL_PATTERNS)
DIL_HEADS_PER_GROUP = 8
DIL_WIDTH = DIL_HEADS_PER_GROUP * HEAD_DIM
DIL_IN = 3 * DIL_GROUPS * DIL_WIDTH
NEG = -1e30

VMEM_LIMIT_BYTES = 56 * 1024 * 1024
LANE = 128
TOKEN_TILE = 256


def _cparams(*sem):
    return pltpu.CompilerParams(dimension_semantics=sem, vmem_limit_bytes=VMEM_LIMIT_BYTES)


def _const_spec(shape):
    nd = len(shape)
    return pl.BlockSpec(shape, lambda *_: (0,) * nd, pipeline_mode=pl.Buffered(1))


def _rms(x, w):
    return x * lax.rsqrt(jnp.mean(x * x, axis=-1, keepdims=True) + RMS_EPS) * w


def _proj_body(x_ref, nw_ref, *refs, n_t):
    n = len(refs) // 2
    xn = _rms(x_ref[...], nw_ref[...]).astype(BF16)
    for k, (w_ref, o_ref) in enumerate(zip(refs[:n], refs[n:])):
        if k >= n - n_t:
            o_ref[...] = lax.dot_general(w_ref[...], xn, _NT, preferred_element_type=F32)
            continue
        y = jnp.dot(xn, w_ref[...], preferred_element_type=F32)
        if len(o_ref.shape) == 2:
            o_ref[...] = y
        else:
            for j in range(o_ref.shape[0]):
                o_ref[j] = y[:, j * LANE:(j + 1) * LANE]


def _norm_proj(x, norm_w, w_segs, tm, plane_segs=(), t_segs=(), seq_len=None):
    n_tok, d = x.shape
    tm = min(tm, n_tok)
    widths = [w.shape[1] for w in w_segs]
    planes = [k in plane_segs for k in range(len(w_segs))]
    out_spec = lambda n, p: (pl.BlockSpec((n // LANE, tm, LANE), lambda i: (0, i, 0)) if p
                             else pl.BlockSpec((tm, n), lambda i: (i, 0)))
    out_sds = lambda n, p: jax.ShapeDtypeStruct((n // LANE, n_tok, LANE) if p else (n_tok, n), F32)
    per_seq = seq_len // tm if t_segs else 1
    t_spec = lambda w: pl.BlockSpec((None, w.shape[0], tm), lambda i: (i // per_seq, 0, i % per_seq))
    t_sds = lambda w: jax.ShapeDtypeStruct((n_tok // seq_len, w.shape[0], seq_len), F32)
    return pl.pallas_call(
        functools.partial(_proj_body, n_t=len(t_segs)),
        grid=(n_tok // tm,),
        in_specs=[pl.BlockSpec((tm, d), lambda i: (i, 0)), _const_spec((1, d))]
        + [_const_spec((d, n)) for n in widths] + [_const_spec(w.shape) for w in t_segs],
        out_specs=[out_spec(n, p) for n, p in zip(widths, planes)] + [t_spec(w) for w in t_segs],
        out_shape=[out_sds(n, p) for n, p in zip(widths, planes)] + [t_sds(w) for w in t_segs],
        compiler_params=_cparams("arbitrary"),
        name="norm_proj",
    )(x, norm_w.reshape(1, d), *w_segs, *t_segs)


def _post_body(a_ref, b_ref, y_ref, woa_ref, wob_ref, w1_ref, w2_ref, nmix_ref, npre_ref, npost_ref, o_ref):
    m = jnp.dot(a_ref[...].astype(BF16), woa_ref[...], preferred_element_type=F32)
    m = m + jnp.dot(b_ref[...].astype(BF16), wob_ref[...], preferred_element_type=F32)
    y1 = y_ref[...] + _rms(m, nmix_ref[...])
    h = _rms(y1, npre_ref[...]).astype(BF16)
    a = jnp.maximum(jnp.dot(h, w1_ref[...], preferred_element_type=F32), 0.0)
    m2 = jnp.dot((a * a).astype(BF16), w2_ref[...], preferred_element_type=F32)
    o_ref[...] = y1 + _rms(m2, npost_ref[...])


def _post_block(mix_a, mix_b, y, wo_a, wo_b, w1, w2, n_mix, n_pre, n_post, tm):
    n_tok, d = y.shape
    tm = min(tm, n_tok)
    ka, kb = mix_a.shape[1], mix_b.shape[1]
    row = lambda n: pl.BlockSpec((tm, n), lambda i: (i, 0))
    return pl.pallas_call(
        _post_body,
        grid=(n_tok // tm,),
        in_specs=[row(ka), row(kb), row(d), _const_spec(wo_a.shape), _const_spec(wo_b.shape),
                  _const_spec(w1.shape), _const_spec(w2.shape), _const_spec((1, d)), _const_spec((1, d)),
                  _const_spec((1, d))],
        out_specs=row(d),
        out_shape=jax.ShapeDtypeStruct((n_tok, d), F32),
        compiler_params=_cparams("arbitrary"),
        name="post_block",
    )(mix_a, mix_b, y, wo_a, wo_b, w1, w2, n_mix.reshape(1, d), n_pre.reshape(1, d), n_post.reshape(1, d))


def _compress_rows(src_ref, pe_ref, w1_ref, w2_ref, nb):
    hd = HEAD_DIM
    low = _half_mask((nb, LANE), 0)
    rows_per_step = 4 * LANE // (2 * hd)

    def pack(a, b):
        top = jnp.where(low, a, pltpu.roll(b, hd, axis=1))
        bot = jnp.where(low, pltpu.roll(a, hd, axis=1), b)
        return jnp.concatenate([top, bot], axis=0)

    def body(i, acc):
        r = i * rows_per_step
        x = [src_ref[pl.ds(r + k, nb, stride=NSA_BLOCK), :] + pe_ref[pl.ds(r + k, 1), :] for k in range(rows_per_step)]
        x4 = jnp.concatenate([pack(x[0], x[1]), pack(x[2], x[3])], axis=1).astype(BF16)
        w4 = w1_ref[pl.ds(r, rows_per_step)].reshape(rows_per_step * hd, NSA_CMP_HIDDEN)
        return acc + jnp.dot(x4, w4, preferred_element_type=F32)

    hid = lax.fori_loop(0, NSA_BLOCK // rows_per_step, body, jnp.zeros((2 * nb, NSA_CMP_HIDDEN), F32), unroll=4)
    hid = hid * jax.nn.sigmoid(hid)
    out = jnp.dot(hid.astype(BF16), w2_ref[...], preferred_element_type=F32)
    return jnp.concatenate([out[:nb], out[nb:]], axis=1)


def _compress_body(kc_ref, vc_ref, pek_ref, pev_ref, wk1_ref, wk2_ref, wv1_ref, wv2_ref, o_ref, *, nb):
    o_ref[:, :LANE] = _compress_rows(kc_ref, pek_ref, wk1_ref, wk2_ref, nb)
    o_ref[:, LANE:] = _compress_rows(vc_ref, pev_ref, wv1_ref, wv2_ref, nb)


def _pe2(pe):
    return jnp.concatenate([pe, pe], axis=1)


def _nsa_compress_prompt(rows, lp):
    bsz, t, _ = rows.shape
    nb = t // NSA_BLOCK
    w1 = lambda w: w.reshape(NSA_BLOCK, HEAD_DIM, NSA_CMP_HIDDEN).astype(BF16)
    col = lambda c: pl.BlockSpec((None, t, LANE), lambda b: (b, 0, c))
    return pl.pallas_call(
        functools.partial(_compress_body, nb=nb),
        grid=(bsz,),
        in_specs=[col(0), col(1), _const_spec((NSA_BLOCK, LANE)), _const_spec((NSA_BLOCK, LANE)),
                  _const_spec((NSA_BLOCK, HEAD_DIM, NSA_CMP_HIDDEN)), _const_spec((NSA_CMP_HIDDEN, HEAD_DIM)),
                  _const_spec((NSA_BLOCK, HEAD_DIM, NSA_CMP_HIDDEN)), _const_spec((NSA_CMP_HIDDEN, HEAD_DIM))],
        out_specs=pl.BlockSpec((None, nb, 2 * LANE), lambda b: (b, 0, 0)),
        out_shape=jax.ShapeDtypeStruct((bsz, nb, 2 * LANE), F32),
        compiler_params=_cparams("arbitrary"),
        name="nsa_compress",
    )(rows, rows, _pe2(lp['nsa_pe_k']), _pe2(lp['nsa_pe_v']), w1(lp['nsa_ck_w1']), lp['nsa_ck_w2'].astype(BF16),
      w1(lp['nsa_cv_w1']), lp['nsa_cv_w2'].astype(BF16))


_NT = (((1,), (1,)), ((), ()))


def _half_mask(shape, g):
    lane = lax.broadcasted_iota(jnp.int32, shape, len(shape) - 1)
    return lane >= HEAD_DIM if g else lane < HEAD_DIM


def _stack_heads(x, heads, g):
    keep = _half_mask((x.shape[0], LANE), g)
    out = []
    for h in heads:
        blk = x[:, (h // 2) * LANE:(h // 2 + 1) * LANE]
        if h % 2 != g:
            blk = pltpu.roll(blk, HEAD_DIM, axis=1)
        out.append(jnp.where(keep, blk, 0.0))
    return jnp.concatenate(out, axis=0)


def _unstack_pair(a, b, g):
    if g == 0:
        b = pltpu.roll(b, HEAD_DIM, axis=1)
    else:
        a = pltpu.roll(a, HEAD_DIM, axis=1)
    return jnp.where(_half_mask(a.shape, 1), b, a)


def _row_gate(sig, lanes, rows):
    return jnp.concatenate([jnp.broadcast_to(sig[:, c:c + 1], (rows, LANE)) for c in lanes], axis=0)


def _nsa_select(p3, pos, nb, n_sel):
    nq = p3.shape[1]
    imp = jnp.sum(p3, axis=0)
    n_i = lax.broadcasted_iota(jnp.int32, (nq, nb), 1)
    cur = pos // NSA_BLOCK
    score = jnp.where(n_i == 0, NSA_REP + 1.0, imp)
    score = jnp.where(n_i == cur, NSA_REP + 1.0, score)
    score = jnp.where(n_i == cur - 1, NSA_REP + 1.0, score)
    score = jnp.where(n_i > cur, -1.0, score)
    rank = jnp.zeros((nq, nb), F32)
    for m in range(nb):
        col = score[:, m:m + 1]
        tie = jnp.where(n_i > m, 1.0, 0.0)
        rank = rank + jnp.where(col > score, 1.0, jnp.where(col == score, tie, 0.0))
    return jnp.where(rank < n_sel, 1.0, 0.0)


def _nsa_select_t(imp, pos, n_sel):
    nb, nq = imp.shape
    n_i = lax.broadcasted_iota(jnp.int32, (nb, nq), 0)
    cur = pos // NSA_BLOCK
    score = jnp.where(n_i == 0, NSA_REP + 1.0, imp)
    score = jnp.where(n_i == cur, NSA_REP + 1.0, score)
    score = jnp.where(n_i == cur - 1, NSA_REP + 1.0, score)
    score = jnp.where(n_i > cur, -1.0, score)
    rank = jnp.zeros((nb, nq), F32)
    for m in range(nb):
        row = score[m:m + 1, :]
        tie = jnp.where(n_i > m, 1.0, 0.0)
        rank = rank + jnp.where(row > score, 1.0, jnp.where(row == score, tie, 0.0))
    return jnp.where(rank < n_sel, 1.0, 0.0)


def _lane_tile(x, n):
    return jnp.concatenate([x] * n, axis=1)


_V_ROWS = HEAD_DIM + 16
_NSA_QUERY_BLOCK = 256
_SEL_TILE = 512
_WIN_TILE = 256


def _flash_tile_t(kt, vt_aug, q_t, valid, m_ref, acc_ref, g):
    nk, nq = valid.shape
    rep = q_t.shape[1] // nq
    s = jnp.dot(kt, q_t, preferred_element_type=F32)
    s = jnp.concatenate([jnp.where(valid, s[:, r * nq:(r + 1) * nq], NEG) for r in range(rep)], axis=1)
    m_old = m_ref[g, 0:1, :]
    m_new = jnp.maximum(m_old, jnp.max(s, axis=0, keepdims=True))
    alpha = jnp.exp2(m_old - m_new)
    p = jnp.exp2(s - m_new)
    acc_ref[g] = alpha * acc_ref[g] + jnp.dot(vt_aug, p.astype(BF16), preferred_element_type=F32)
    m_ref[g] = jnp.broadcast_to(m_new, m_ref.shape[1:])


def _nsa_prompt_body(q_ref, small_ref, cmp_ref, ks_ref, vs_ref, kw_ref, vw_ref, o_ref, vst_ref, vwt_ref, m_ref, acc_ref,
                     *, nb, n_sel):
    qb = q_ref.shape[0]
    t = ks_ref.shape[0]
    i = pl.program_id(1)
    q0 = i * qb
    rep = NSA_REP
    hd = HEAD_DIM
    groups = range(NSA_KV_HEADS)

    @pl.when(i == 0)
    def _():
        ones = jnp.ones((_V_ROWS - hd, t), BF16)
        for g in groups:
            vst_ref[g, hd:, :] = ones
            vwt_ref[g, hd:, :] = ones

        def fill(j, carry):
            k0 = pl.multiple_of(j * qb, qb)
            for src, dst in ((vs_ref, vst_ref), (vw_ref, vwt_ref)):
                v_t = src[pl.ds(k0, qb), :].T.astype(BF16)
                for g in groups:
                    dst[g, :hd, pl.ds(k0, qb)] = v_t[g * hd:(g + 1) * hd, :]
            return carry

        lax.fori_loop(0, t // qb, fill, 0)

    pos_r = q0 + lax.broadcasted_iota(jnp.int32, (1, qb), 1)
    sig_t = jax.nn.sigmoid(small_ref[...]).T
    q = q_ref[...] * (hd ** -0.5)
    pairs_t = [q[:, c * LANE:(c + 1) * LANE].T for c in range(NSA_HEADS // 2)]
    head_t = lambda h: pairs_t[h // 2][(h % 2) * hd:(h % 2 + 1) * hd, :]
    zeros = jnp.zeros((hd, rep * qb), F32)
    q_ts, sels = [], []
    for g in groups:
        qg_t = jnp.concatenate([head_t(g * rep + r) for r in range(rep)], axis=1)
        stack = lambda x: jnp.concatenate([x, zeros] if g == 0 else [zeros, x], axis=0).astype(BF16)
        q_t = stack(qg_t)
        q_ts.append(stack(qg_t * math.log2(math.e)))
        gate_row = lambda j: jnp.concatenate(
            [sig_t[SSM_HEADS + j * NSA_HEADS + g * rep + r:SSM_HEADS + j * NSA_HEADS + g * rep + r + 1, :]
             for r in range(rep)], axis=1)
        kc = cmp_ref[:, :LANE].astype(BF16)
        vc_t = cmp_ref[:, LANE:].T[g * hd:(g + 1) * hd, :].astype(BF16)
        s = jnp.dot(kc, q_t, preferred_element_type=F32)
        n_i = lax.broadcasted_iota(jnp.int32, (nb, qb), 0)
        complete = _lane_tile(jnp.where((n_i + 1) * NSA_BLOCK <= pos_r + 1, 1.0, 0.0), rep)
        s = jnp.where(complete > 0.5, s, NEG)
        p = jnp.exp(s - jnp.max(s, axis=0, keepdims=True))
        p = p / jnp.sum(p, axis=0, keepdims=True)
        p = jnp.where(_lane_tile(pos_r, rep) >= NSA_BLOCK - 1, p, 0.0)
        o_cmp = jnp.dot(vc_t, p.astype(BF16), preferred_element_type=F32)
        acc_ref[2 + g, :hd, :] = gate_row(0) * o_cmp
        imp = p[:, :qb]
        for r in range(1, rep):
            imp = imp + p[:, r * qb:(r + 1) * qb]
        sels.append(_nsa_select_t(imp, pos_r, n_sel).astype(BF16))

    def flash_init():
        m_ref[...] = jnp.full(m_ref.shape, NEG, F32)
        acc_ref[0:2] = jnp.zeros((2,) + acc_ref.shape[1:], F32)

    def flash_out(g):
        acc = acc_ref[g]
        return acc[:hd, :] / acc[hd:hd + 1, :]

    def flash_pass(k_ref, vt_ref, kb, first_key, valid_fn):
        kpos_i = lax.broadcasted_iota(jnp.int32, (kb, qb), 0)
        qpos = q0 + lax.broadcasted_iota(jnp.int32, (kb, qb), 1)
        flash_init()

        def step(j, carry):
            k0 = pl.multiple_of(j * kb, kb)
            kt = k_ref[pl.ds(k0, kb), :].astype(BF16)
            valids = valid_fn(k0, k0 + kpos_i, qpos)
            for g in groups:
                _flash_tile_t(kt, vt_ref[g, :, pl.ds(k0, kb)], q_ts[g], valids[g], m_ref, acc_ref, g)
            return carry

        lax.fori_loop(first_key // kb, (q0 + qb - 1) // kb + 1, step, 0)

    tile = lambda keys: keys if t % keys == 0 else qb

    def sel_valid(k0, kpos, qpos):
        kb = kpos.shape[0]
        blk = lax.broadcasted_iota(jnp.int32, (kb, nb), 1)
        key = lax.broadcasted_iota(jnp.int32, (kb, nb), 0)
        expand = jnp.where(blk == (k0 + key) // NSA_BLOCK, 1.0, 0.0).astype(BF16)
        causal = kpos <= qpos
        return [jnp.where(causal, jnp.dot(expand, sels[g], preferred_element_type=F32), 0.0) > 0.5 for g in groups]

    flash_pass(ks_ref, vst_ref, tile(_SEL_TILE), 0, sel_valid)
    for g in groups:
        gate_row = jnp.concatenate(
            [sig_t[SSM_HEADS + NSA_HEADS + g * rep + r:SSM_HEADS + NSA_HEADS + g * rep + r + 1, :] for r in range(rep)],
            axis=1)
        acc_ref[2 + g, :hd, :] += gate_row * flash_out(g)
    def win_valid(k0, kpos, qpos):
        diff = qpos - kpos
        valid = jnp.where(diff >= 0, diff, NSA_WINDOW + 1) <= NSA_WINDOW
        return [valid for _ in groups]

    flash_pass(kw_ref, vwt_ref, tile(_WIN_TILE), jnp.maximum(q0 - NSA_WINDOW, 0), win_valid)
    for g in groups:
        gate_row = jnp.concatenate(
            [sig_t[SSM_HEADS + 2 * NSA_HEADS + g * rep + r:SSM_HEADS + 2 * NSA_HEADS + g * rep + r + 1, :]
             for r in range(rep)], axis=1)
        comb = acc_ref[2 + g, :hd, :] + gate_row * flash_out(g)
        for c in range(rep // 2):
            pair = jnp.concatenate([comb[:, (2 * c) * qb:(2 * c + 1) * qb], comb[:, (2 * c + 1) * qb:(2 * c + 2) * qb]],
                                   axis=0)
            col = (g * rep // 2 + c) * LANE
            o_ref[:, col:col + LANE] = pair.T


def _nsa_prompt_attn(q, small, cmp, rows, win, bsz, t):
    qb = math.gcd(t, _NSA_QUERY_BLOCK)
    nq = t // qb
    nb = cmp.shape[1]
    n_sel = min(NSA_TOPK, nb)
    tok = lambda n: pl.BlockSpec((qb, n), lambda b, i: (b * nq + i, 0))
    seq = lambda c: pl.BlockSpec((None, t, LANE), lambda b, i: (b, 0, c))
    v_t = pltpu.VMEM((NSA_KV_HEADS, _V_ROWS, t), BF16)
    return pl.pallas_call(
        functools.partial(_nsa_prompt_body, nb=nb, n_sel=n_sel),
        grid=(bsz, nq),
        in_specs=[tok(NSA_HEADS * HEAD_DIM), tok(LANE), pl.BlockSpec((None, nb, 2 * LANE), lambda b, i: (b, 0, 0)),
                  seq(2), seq(3), seq(0), seq(1)],
        out_specs=tok(NSA_HEADS * HEAD_DIM),
        out_shape=jax.ShapeDtypeStruct((bsz * t, NSA_HEADS * HEAD_DIM), F32),
        scratch_shapes=[v_t, v_t, pltpu.VMEM((NSA_KV_HEADS, 8, NSA_REP * qb), F32),
                        pltpu.VMEM((2 * NSA_KV_HEADS, _V_ROWS, NSA_REP * qb), F32)],
        compiler_params=_cparams("arbitrary", "arbitrary"),
        name="nsa_prompt_attn",
    )(q, small, cmp, rows, rows, win, win)


_TN = (((0,), (0,)), ((), ()))


def _silu(x):
    return x * jax.nn.sigmoid(x)


def _softplus(x):
    return jnp.maximum(x, 0.0) + jnp.log(1.0 + jnp.exp(-jnp.abs(x)))


def _conv_silu(x_ref, buf_ref, w_ref, b_ref, xp_ref, tail_ref, first):
    n = x_ref.shape[0]

    @pl.when(first)
    def _():
        xp_ref[8 - (CONV_K - 1):8, :] = buf_ref[...]

    xp_ref[8:8 + n, :] = x_ref[...]
    y = xp_ref[8:8 + n, :] * w_ref[CONV_K - 1:CONV_K, :]
    for k in range(CONV_K - 1):
        y = y + xp_ref[5 + k:5 + k + n, :] * w_ref[k:k + 1, :]
    if b_ref is not None:
        y = y + b_ref[...]
    tail = xp_ref[8 + n - (CONV_K - 1):8 + n, :]
    tail_ref[...] = tail
    xp_ref[8 - (CONV_K - 1):8, :] = tail
    return _silu(y)


def _bf16_terms(x):
    hi = x.astype(BF16)
    r = x - hi.astype(F32)
    mid = r.astype(BF16)
    return hi, mid, (r - mid.astype(F32)).astype(BF16)


def _cumsum_rows(x, seg=None):
    n = x.shape[0]
    i = lax.broadcasted_iota(jnp.int32, (n, n), 0)
    j = lax.broadcasted_iota(jnp.int32, (n, n), 1)
    keep = i >= j if seg in (None, n) else jnp.where(i // seg == j // seg, i - j, -1) >= 0
    tri = jnp.where(keep, 1.0, 0.0).astype(BF16)
    return sum(jnp.dot(tri, term, preferred_element_type=F32) for term in _bf16_terms(x))


def _expand_heads(x, width, lane0=0, n_out=D_MODEL):
    h_i = lax.broadcasted_iota(jnp.int32, (LANE, n_out), 0)
    c_i = lax.broadcasted_iota(jnp.int32, (LANE, n_out), 1)
    sel = jnp.where(c_i // width + lane0 == h_i, 1.0, 0.0).astype(BF16)
    return sum(jnp.dot(term, sel, preferred_element_type=F32) for term in _bf16_terms(x))


def _decay_matrix(col, row, strict=False):
    n = col.shape[0]
    i = lax.broadcasted_iota(jnp.int32, (n, n), 0)
    j = lax.broadcasted_iota(jnp.int32, (n, n), 1)
    keep = (i > j) if strict else (i >= j)
    return jnp.exp(jnp.where(keep, col - row, NEG))


def _ssd_body(xbc_ref, z_ref, small_ref, buf_ref, h0_ref, cw_ref, cb_ref, dtb_ref, alog_ref, dx_ref, nw_ref,
              y_ref, conv_ref, h_ref, xp_ref):
    c = pl.program_id(1)
    n = xbc_ref.shape[0]
    hp = SSM_HEAD_DIM
    rep = SSM_HEADS // SSM_GROUPS

    @pl.when(c == 0)
    def _():
        h_ref[...] = h0_ref[...]

    act = _conv_silu(xbc_ref, buf_ref, cw_ref, cb_ref, xp_ref, conv_ref, c == 0)
    xs = act[:, :SSM_INNER]
    head_lane = lax.broadcasted_iota(jnp.int32, (n, LANE), 1) < SSM_HEADS
    dt = jnp.where(head_lane, _softplus(small_ref[...] + dtb_ref[...]), 0.0)
    la = dt * (-jnp.exp(alog_ref[...]))
    acs = _cumsum_rows(la)
    acs_t = acs.T
    xd = xs * _expand_heads(dt, hp)
    e_acs = _expand_heads(jnp.exp(acs), hp)
    xdd = (xd * _expand_heads(jnp.exp(acs[n - 1:n, :] - acs), hp)).astype(BF16)
    xd = xd.astype(BF16)
    lane2 = _half_mask((n, LANE), 1)
    for g in range(SSM_GROUPS):
        bc = act[:, SSM_INNER + g * SSM_STATE:SSM_INNER + (g + 1) * SSM_STATE].astype(BF16)
        cc = act[:, SSM_INNER + (SSM_GROUPS + g) * SSM_STATE:SSM_INNER + (SSM_GROUPS + g + 1) * SSM_STATE].astype(BF16)
        cb = lax.dot_general(cc, bc, _NT, preferred_element_type=F32)
        h_prev = h_ref[g * rep:(g + 1) * rep].reshape(rep * hp, SSM_STATE)
        y_off = lax.dot_general(cc, h_prev.astype(BF16), _NT, preferred_element_type=F32)
        for pair in range(rep // 2):
            halves = []
            for k in range(2):
                h = g * rep + 2 * pair + k
                lm = _decay_matrix(acs[:, h:h + 1], acs_t[h:h + 1, :])
                halves.append(jnp.dot((cb * lm).astype(BF16), xd[:, (h // 2) * LANE:(h // 2 + 1) * LANE],
                                      preferred_element_type=F32))
            col = (g * rep + 2 * pair) * hp
            y_ref[:, col:col + LANE] = jnp.where(lane2, halves[1], halves[0]) + y_off[:, 2 * pair * hp:2 * pair * hp + LANE] * e_acs[:, col:col + LANE]
        st = lax.dot_general(xdd[:, g * rep * hp:(g + 1) * rep * hp], bc, _TN, preferred_element_type=F32)
        for r in range(rep):
            h = g * rep + r
            dec = jnp.exp(acs_t[h:h + 1, n - 1:n])
            h_ref[h] = h_ref[h] * dec + st[r * hp:(r + 1) * hp, :]
    y = y_ref[...] + dx_ref[...] * xs
    y_ref[...] = _rms(y * _silu(z_ref[...]), nw_ref[...])


def _ssd_mixer(xbc, z, small, conv_buf, h0, lp, bsz, t, chunk):
    nc = t // chunk
    tok = lambda n: pl.BlockSpec((chunk, n), lambda b, c: (b * nc + c, 0))
    per_b = lambda shape: pl.BlockSpec((None,) + shape, lambda b, c: (b,) + (0,) * len(shape))
    pad = lambda v: jnp.pad(v.astype(F32), (0, LANE - v.shape[0])).reshape(1, LANE)
    cdim = SSM_CONV_DIM
    state = (SSM_HEADS, SSM_HEAD_DIM, SSM_STATE)
    return pl.pallas_call(
        _ssd_body,
        grid=(bsz, nc),
        in_specs=[tok(cdim), tok(SSM_INNER), tok(LANE), per_b((CONV_K - 1, cdim)), per_b(state),
                  _const_spec((CONV_K, cdim)), _const_spec((1, cdim)), _const_spec((1, LANE)), _const_spec((1, LANE)),
                  _const_spec((1, SSM_INNER)), _const_spec((1, SSM_INNER))],
        out_specs=[tok(SSM_INNER), per_b((CONV_K - 1, cdim)), per_b(state)],
        out_shape=[jax.ShapeDtypeStruct((bsz * t, SSM_INNER), F32),
                   jax.ShapeDtypeStruct((bsz, CONV_K - 1, cdim), F32),
                   jax.ShapeDtypeStruct((bsz,) + state, F32)],
        scratch_shapes=[pltpu.VMEM((chunk + 8, cdim), F32)],
        compiler_params=_cparams("arbitrary", "arbitrary"),
        name="ssd_mixer",
    )(xbc, z, small, conv_buf, h0, lp['ssm_conv_w'], lp['ssm_conv_b'].reshape(1, cdim), pad(lp['ssm_dt_bias']),
      pad(lp['ssm_a_log']), jnp.repeat(lp['ssm_d'].astype(F32), SSM_HEAD_DIM).reshape(1, SSM_INNER),
      lp['ssm_norm_w'].reshape(1, SSM_INNER))


def _split_bf16(x):
    hi = x.astype(BF16)
    return hi, (x - hi.astype(F32)).astype(BF16)


def _bmm(a, b):
    return jnp.einsum('hmk,hkn->hmn', a, b, preferred_element_type=F32)


def _bmm_nt(a, b):
    return jnp.einsum('hmk,hnk->hmn', a, b, preferred_element_type=F32)


def _bmm_tn(a, b):
    return jnp.einsum('hkm,hkn->hmn', a, b, preferred_element_type=F32)


def _bmm_split(a, b):
    a_hi, a_lo = _split_bf16(a)
    b_hi, b_lo = _split_bf16(b)
    return _bmm(a_hi, b_hi) + _bmm(a_hi, b_lo) + _bmm(a_lo, b_hi)


def _unit_lower_inverse(a):
    n = a.shape[-1]
    eye = jnp.where(lax.broadcasted_iota(jnp.int32, (n, n), 0) == lax.broadcasted_iota(jnp.int32, (n, n), 1), 1.0, 0.0)
    p = -a
    t = eye + p
    span = 2
    while span < n:
        p = _bmm_split(p, p)
        t = t + _bmm_split(t, p)
        span *= 2
    return t


_GDN_CHUNKS_PER_STEP = 4


def _gdn_body(qkv_ref, z_ref, small_ref, buf_ref, s0_ref, cw_ref, alog_ref, dtb_ref, nw_ref,
              y_ref, conv_ref, s_ref, xp_ref, *, chunk):
    c = pl.program_id(1)
    n = qkv_ref.shape[0]
    dk = GDN_HEAD_DIM
    nh = GDN_HEADS
    n_ck = n // chunk

    @pl.when(c == 0)
    def _():
        s_ref[...] = s0_ref[...]

    act = _conv_silu(qkv_ref, buf_ref, cw_ref, None, xp_ref, conv_ref, c == 0)
    lane = lax.broadcasted_iota(jnp.int32, (n, LANE), 1)
    raw = small_ref[...]
    beta = jax.nn.sigmoid(raw)
    g = jnp.where((lane >= nh) & (lane < 2 * nh), -jnp.exp(alog_ref[...]) * _softplus(raw + dtb_ref[...]), 0.0)
    gc = _cumsum_rows(g, chunk)
    gc_t = gc.T
    gc_end = jnp.concatenate([jnp.broadcast_to(gc[(j + 1) * chunk - 1:(j + 1) * chunk, :], (chunk, LANE))
                              for j in range(n_ck)], axis=0)
    beta_x = _expand_heads(beta, dk)
    egc_x = _expand_heads(jnp.exp(gc), dk, nh)
    edec_x = _expand_heads(jnp.exp(gc_end - gc), dk, nh)
    ii = lax.broadcasted_iota(jnp.int32, (chunk, chunk), 0)
    jj = lax.broadcasted_iota(jnp.int32, (chunk, chunk), 1)
    heads = lambda x, base=0: jnp.stack([x[j * chunk:(j + 1) * chunk, base + h * dk:base + (h + 1) * dk]
                                         for j in range(n_ck) for h in range(nh)])
    q, k, v = heads(act), heads(act, nh * dk), heads(act, 2 * nh * dk)
    q = q * lax.rsqrt(jnp.sum(q * q, axis=-1, keepdims=True) + 1e-6) * (dk ** -0.5)
    k = k * lax.rsqrt(jnp.sum(k * k, axis=-1, keepdims=True) + 1e-6)
    beta3, egc, edec = heads(beta_x), heads(egc_x), heads(edec_x)
    kb = k * beta3
    decay = jnp.stack([_decay_matrix(gc[j * chunk:(j + 1) * chunk, nh + h:nh + h + 1],
                                     gc_t[nh + h:nh + h + 1, j * chunk:(j + 1) * chunk])
                       for j in range(n_ck) for h in range(nh)])
    k16 = k.astype(BF16)
    amat = jnp.where(ii > jj, _bmm_nt(kb.astype(BF16), k16) * decay, 0.0)
    tmat = _unit_lower_inverse(amat).astype(BF16)
    u = _bmm(tmat, (v * beta3).astype(BF16))
    w = _bmm(tmat, (kb * egc).astype(BF16)).astype(BF16)
    qk = (_bmm_nt(q.astype(BF16), k16) * decay).astype(BF16)
    qg = (q * egc).astype(BF16)
    kdec = (k * edec).astype(BF16)
    s = s_ref[...]
    for j in range(n_ck):
        ck = slice(j * nh, (j + 1) * nh)
        s16 = s.astype(BF16)
        v16 = (u[ck] - _bmm(w[ck], s16)).astype(BF16)
        o = _bmm(qg[ck], s16) + _bmm(qk[ck], v16)
        last = (j + 1) * chunk - 1
        gl = jnp.stack([jnp.exp(gc_t[nh + h:nh + h + 1, last:last + 1]) for h in range(nh)])
        s = s * gl + _bmm_tn(kdec[ck], v16)
        y = _rms(o, nw_ref[...])
        for h in range(nh):
            sl = slice(h * dk, (h + 1) * dk)
            y_ref[j * chunk:(j + 1) * chunk, sl] = y[h] * _silu(z_ref[j * chunk:(j + 1) * chunk, sl])
    s_ref[...] = s


def _gdn_mixer(qkv, z, small, conv_buf, s0, lp, bsz, t, chunk, dil_decode=None):
    per_step = _GDN_CHUNKS_PER_STEP if (t // chunk) % _GDN_CHUNKS_PER_STEP == 0 else 1
    rows = per_step * chunk
    nc = t // rows
    tok = lambda n: pl.BlockSpec((rows, n), lambda b, c: (b * nc + c, 0))
    per_b = lambda shape: pl.BlockSpec((None,) + shape, lambda b, c: (b,) + (0,) * len(shape))
    pad8 =lambda v: jnp.pad(v.astype(F32), (GDN_HEADS, LANE - 2 * GDN_HEADS)).reshape(1, LANE)
    cdim = 3 * GDN_WIDTH
    state = (GDN_HEADS, GDN_HEAD_DIM, GDN_HEAD_DIM)
    operands = [qkv, z, small, conv_buf, s0, lp['gdn_conv_w'], pad8(lp['gdn_a_log']), pad8(lp['gdn_dt_bias']),
                lp['gdn_norm_w'].reshape(1, GDN_HEAD_DIM)]
    in_specs = [tok(cdim), tok(GDN_WIDTH), tok(LANE), per_b((CONV_K - 1, cdim)), per_b(state),
                _const_spec((CONV_K, cdim)), _const_spec((1, LANE)), _const_spec((1, LANE)),
                _const_spec((1, GDN_HEAD_DIM))]
    out_specs = [tok(GDN_WIDTH), per_b((CONV_K - 1, cdim)), per_b(state)]
    out_shape = [jax.ShapeDtypeStruct((bsz * t, GDN_WIDTH), F32),
                 jax.ShapeDtypeStruct((bsz, CONV_K - 1, cdim), F32),
                 jax.ShapeDtypeStruct((bsz,) + state, F32)]
    body, name = functools.partial(_gdn_body, chunk=chunk), "gdn_mixer"
    if dil_decode is not None:
        assert nc == 1
        d_ops, d_in, d_out, d_shape, unview = _dil_decode_operands(*dil_decode, bsz, t)
        operands, in_specs = operands + d_ops, in_specs + d_in
        out_specs, out_shape = out_specs + d_out, out_shape + d_shape
        body, name = functools.partial(_gdn_dil_decode_body, chunk=chunk), "gdn_dil_decode"
    outs = pl.pallas_call(
        body,
        grid=(bsz, nc),
        in_specs=in_specs,
        out_specs=out_specs,
        out_shape=out_shape,
        scratch_shapes=[pltpu.VMEM((rows + 8, cdim), F32)],
        compiler_params=_cparams("arbitrary", "arbitrary"),
        name=name,
    )(*operands)
    if dil_decode is None:
        return outs
    return list(outs[:4]) + [[unview(v) for v in outs[4:]]]


def _softmax_pv(s, valid, v16, n_rep=1):
    rows, nk = s.shape
    s = jnp.where(valid[None], s.reshape(n_rep, rows // n_rep, nk), NEG).reshape(rows, nk)
    m = jnp.max(s, axis=-1, keepdims=True)
    p = jnp.exp(s - m)
    l = jnp.sum(p, axis=-1, keepdims=True)
    return jnp.dot((p / l).astype(BF16), v16, preferred_element_type=F32), m + jnp.log(l)


_DIL_PAIRS = DIL_HEADS_PER_GROUP // 2


def _dil_prompt_body(q_ref, k_ref, v_ref, o_ref, lse_ref, kprev_ref, vprev_ref, *, window, step):
    i, h = pl.program_id(1), pl.program_id(2)
    pp = q_ref.shape[0]
    qb = q_ref.shape[1] // step
    scale = HEAD_DIM ** -0.5
    qpos = i * qb + lax.broadcasted_iota(jnp.int32, (qb, 2 * qb), 0)
    kpos = (i - 1) * qb + lax.broadcasted_iota(jnp.int32, (qb, 2 * qb), 1)
    diff = jnp.where(kpos >= 0, qpos - kpos, -1)
    valid = jnp.where(diff >= 0, diff, window + 1) <= window
    lower = _half_mask((qb, LANE), 0)

    @pl.when(i == 0)
    def _():
        for pr in range(pp):
            kprev_ref[h * pp + pr] = jnp.zeros(kprev_ref.shape[1:], BF16)
            vprev_ref[h * pp + pr] = jnp.zeros(vprev_ref.shape[1:], BF16)

    def one_class(c, carry):
        rows = pl.ds(c, qb, stride=step)
        kept = pl.ds(pl.multiple_of(c * qb, qb), qb)
        for pr in range(pp):
            q = q_ref[pr, rows, :] * scale
            k_cur = k_ref[pr, rows, :].astype(BF16)
            v_cur = v_ref[pr, rows, :].astype(BF16)
            kk = jnp.concatenate([kprev_ref[h * pp + pr, kept, :], k_cur], axis=0)
            vv = jnp.concatenate([vprev_ref[h * pp + pr, kept, :], v_cur], axis=0)
            kprev_ref[h * pp + pr, kept, :] = k_cur
            vprev_ref[h * pp + pr, kept, :] = v_cur
            o2, l2 = [], []
            for k in range(2):
                qh = jnp.where(_half_mask((qb, LANE), k), q, 0.0).astype(BF16)
                s = lax.dot_general(qh, kk, _NT, preferred_element_type=F32)
                o, lse = _softmax_pv(s, valid, vv)
                o2.append(o)
                l2.append(jnp.broadcast_to(lse, (qb, LANE)))
            o_ref[pr, rows, :] = jnp.where(lower, o2[0], o2[1])
            lse_ref[pr, rows, :] = jnp.where(lower, l2[0], l2[1])
        return carry

    lax.fori_loop(0, step, one_class, 0)


def _dil_prompt_attn(dil, gi, bsz, t):
    win, step = DIL_PATTERNS[gi]
    assert win % step == 0 and t % win == 0
    tile = win
    nq = t // tile
    pp = _DIL_PAIRS if _DIL_PAIRS * tile * LANE * 4 <= (1 << 20) else 1
    blk = lambda part: pl.BlockSpec(
        (pp, tile, LANE), lambda b, i, h: ((part * DIL_GROUPS + gi) * (_DIL_PAIRS // pp) + h, b * nq + i, 0))
    out_spec = pl.BlockSpec((pp, tile, LANE), lambda b, i, h: (h, b * nq + i, 0))
    out_sds = jax.ShapeDtypeStruct((_DIL_PAIRS, bsz * t, LANE), F32)
    carry = pltpu.VMEM((_DIL_PAIRS, tile, LANE), BF16)
    return pl.pallas_call(
        functools.partial(_dil_prompt_body, window=win // step, step=step),
        grid=(bsz, nq, _DIL_PAIRS // pp),
        in_specs=[blk(0), blk(1), blk(2)],
        out_specs=[out_spec, out_spec],
        out_shape=[out_sds, out_sds],
        scratch_shapes=[carry, carry],
        compiler_params=_cparams("arbitrary", "arbitrary", "arbitrary"),
        name=f"dil_prompt_attn_{gi}",
    )(dil, dil, dil)


def _dil_combine_body(o0, o1, o2, l0, l1, l2, y_ref):
    for pr in range(_DIL_PAIRS):
        m = jnp.maximum(jnp.maximum(l0[pr], l1[pr]), l2[pr])
        e0, e1, e2 = jnp.exp(l0[pr] - m), jnp.exp(l1[pr] - m), jnp.exp(l2[pr] - m)
        den = e0 + e1 + e2
        y_ref[:, pr * LANE:(pr + 1) * LANE] = (e0 / den) * o0[pr] + (e1 / den) * o1[pr] + (e2 / den) * o2[pr]


def _dil_combine(outs, lses, tm):
    n_tok = outs[0].shape[1]
    spec = pl.BlockSpec((_DIL_PAIRS, tm, LANE), lambda i: (0, i, 0))
    return pl.pallas_call(
        _dil_combine_body,
        grid=(n_tok // tm,),
        in_specs=[spec] * 6,
        out_specs=pl.BlockSpec((tm, DIL_WIDTH), lambda i: (i, 0)),
        out_shape=jax.ShapeDtypeStruct((n_tok, DIL_WIDTH), F32),
        compiler_params=_cparams("arbitrary"),
        name="dil_combine",
    )(*outs, *lses)


def _attend_cached(qg, k_t, v_t, k_n, v_n, valid_c, valid_n, n_rep=1):
    rows = qg.shape[0]
    mask = lambda s, v: jnp.where(v[None], s.reshape(n_rep, rows // n_rep, s.shape[1]), NEG).reshape(rows, s.shape[1])
    s_c = mask(jnp.dot(qg, k_t, preferred_element_type=F32), valid_c)
    s_n = mask(lax.dot_general(qg, k_n, _NT, preferred_element_type=F32), valid_n)
    m = jnp.maximum(jnp.max(s_c, axis=-1, keepdims=True), jnp.max(s_n, axis=-1, keepdims=True))
    p_c = jnp.exp(s_c - m)
    p_n = jnp.exp(s_n - m)
    l = jnp.sum(p_c, axis=-1, keepdims=True) + jnp.sum(p_n, axis=-1, keepdims=True)
    o = lax.dot_general((p_c / l).astype(BF16), v_t, _NT, preferred_element_type=F32)
    o = o + jnp.dot((p_n / l).astype(BF16), v_n, preferred_element_type=F32)
    return o, m + jnp.log(l)


def _nsa_decode_body(pt_ref, q_ref, small_ref, rows_ref, win_ref, *refs, n_pages, lw, nb, nb_pad, n_sel):
    del pt_ref, nb
    pages = refs[:n_pages]
    (pastwin_ref, pek_ref, pev_ref, wk1_ref, wk2_ref, wv1_ref, wv2_ref, o_ref, neww_ref,
     kc_s, vc_s, comb_s) = refs[n_pages:]
    t = q_ref.shape[0]
    rep = NSA_REP
    past_len = n_pages * PAGE_SIZE
    for dst, c in ((kc_s, 0), (vc_s, 1)):
        for p in range(n_pages):
            dst[p * PAGE_SIZE:(p + 1) * PAGE_SIZE, :] = pages[p][c * LANE:(c + 1) * LANE, :].T
        dst[past_len:past_len + t, :] = rows_ref[:, c * LANE:(c + 1) * LANE]
        dst[past_len + t:, :] = jnp.zeros((dst.shape[0] - past_len - t, LANE), F32)
    kcmp = _compress_rows(kc_s, pek_ref, wk1_ref, wk2_ref, nb_pad).astype(BF16)
    vcmp = _compress_rows(vc_s, pev_ref, wv1_ref, wv2_ref, nb_pad).astype(BF16)
    page_rows = lambda c: jnp.concatenate([pages[p][c * LANE:(c + 1) * LANE, :] for p in range(n_pages)],
                                          axis=1).astype(BF16)
    ks_t, vs_t = page_rows(2), page_rows(3)
    ks_n, vs_n = rows_ref[:, 2 * LANE:3 * LANE].astype(BF16), rows_ref[:, 3 * LANE:].astype(BF16)
    kw_t, vw_t = pastwin_ref[:LANE, :].astype(BF16), pastwin_ref[LANE:, :].astype(BF16)
    kw_n, vw_n = win_ref[:, :LANE].astype(BF16), win_ref[:, LANE:].astype(BF16)
    neww_ref[...] = jnp.concatenate([pastwin_ref[:, t:], win_ref[...].T], axis=1)
    pos_c = past_len + lax.broadcasted_iota(jnp.int32, (t, 1), 0)
    new_i = lax.broadcasted_iota(jnp.int32, (t, t), 1)
    back = lax.broadcasted_iota(jnp.int32, (t, t), 0) - new_i
    sig = jax.nn.sigmoid(small_ref[...])
    q = q_ref[...] * (HEAD_DIM ** -0.5)
    for g in range(NSA_KV_HEADS):
        qg = _stack_heads(q, range(g * rep, (g + 1) * rep), g).astype(BF16)
        gate_lane = lambda j: [SSM_HEADS + j * NSA_HEADS + g * rep + r for r in range(rep)]
        s = lax.dot_general(qg, kcmp, _NT, preferred_element_type=F32).reshape(rep, t, nb_pad)
        n_i = lax.broadcasted_iota(jnp.int32, (t, nb_pad), 1)
        complete = (n_i + 1) * NSA_BLOCK <= pos_c + 1
        s = jnp.where(complete[None], s, NEG)
        p = jnp.exp(s - jnp.max(s, axis=-1, keepdims=True))
        p = p / jnp.sum(p, axis=-1, keepdims=True)
        p = jnp.where((pos_c >= NSA_BLOCK - 1)[None], p, 0.0)
        o_cmp = jnp.dot(p.reshape(rep * t, nb_pad).astype(BF16), vcmp, preferred_element_type=F32)
        comb = _row_gate(sig, gate_lane(0), t) * o_cmp
        sel = _nsa_select(p, pos_c, nb_pad, n_sel).astype(BF16)
        expand = lambda n_keys, first: jnp.where(
            lax.broadcasted_iota(jnp.int32, (nb_pad, n_keys), 0)
            == (first + lax.broadcasted_iota(jnp.int32, (nb_pad, n_keys), 1)) // NSA_BLOCK, 1.0, 0.0).astype(BF16)
        chosen_c = jnp.dot(sel, expand(past_len, 0), preferred_element_type=F32)
        chosen_n = jnp.dot(sel, expand(t, past_len), preferred_element_type=F32)
        kpos = lax.broadcasted_iota(jnp.int32, (t, past_len), 1)
        valid_c = jnp.where(kpos <= pos_c, chosen_c, 0.0) > 0.5
        valid_n = jnp.where(back >= 0, chosen_n, 0.0) > 0.5
        o_sel, _ = _attend_cached(qg, ks_t, vs_t, ks_n, vs_n, valid_c, valid_n, rep)
        comb = comb + _row_gate(sig, gate_lane(1), t) * o_sel
        diff = pos_c - (past_len - lw + lax.broadcasted_iota(jnp.int32, (t, lw), 1))
        valid_c = jnp.where(diff >= 0, diff, NSA_WINDOW + 1) <= NSA_WINDOW
        valid_n = jnp.where(back >= 0, back, NSA_WINDOW + 1) <= NSA_WINDOW
        o_win, _ = _attend_cached(qg, kw_t, vw_t, kw_n, vw_n, valid_c, valid_n, rep)
        comb_s[...] = comb + _row_gate(sig, gate_lane(2), t) * o_win
        for c in range(rep // 2):
            a = comb_s[(2 * c) * t:(2 * c + 1) * t, :]
            b = comb_s[(2 * c + 1) * t:(2 * c + 2) * t, :]
            col = (g * rep // 2 + c) * LANE
            o_ref[:, col:col + LANE] = _unstack_pair(a, b, g)


def _nsa_decode_attn(q, small, rows, win, cache_kv, cache_win, page_table, lp, bsz, t):
    n_pages = page_table.shape[1]
    past_len = n_pages * PAGE_SIZE
    lw = cache_win.shape[1]
    nb = -(-(past_len + t) // NSA_BLOCK)
    nb_pad = -(-nb // 8) * 8
    pages = jnp.transpose(cache_kv, (0, 2, 3, 4, 1)).reshape(cache_kv.shape[0], 4 * LANE, PAGE_SIZE)
    pastwin = jnp.transpose(cache_win, (0, 2, 3, 4, 1)).reshape(bsz, 2 * LANE, lw)
    w1 = lambda w: w.reshape(NSA_BLOCK, HEAD_DIM, NSA_CMP_HIDDEN).astype(BF16)
    tok = lambda n: pl.BlockSpec((t, n), lambda b, pt: (b, 0))
    const = lambda shape: pl.BlockSpec(shape, lambda b, pt: (0,) * len(shape), pipeline_mode=pl.Buffered(1))
    page_spec = lambda p: pl.BlockSpec((None, 4 * LANE, PAGE_SIZE), lambda b, pt: (pt[b * n_pages + p], 0, 0))
    win_spec = pl.BlockSpec((None, 2 * LANE, lw), lambda b, pt: (b, 0, 0))
    grid_spec = pltpu.PrefetchScalarGridSpec(
        num_scalar_prefetch=1,
        grid=(bsz,),
        in_specs=[tok(NSA_HEADS * HEAD_DIM), tok(LANE), tok(4 * LANE), tok(2 * LANE)]
        + [page_spec(p) for p in range(n_pages)]
        + [win_spec, const((NSA_BLOCK, LANE)), const((NSA_BLOCK, LANE)),
           const((NSA_BLOCK, HEAD_DIM, NSA_CMP_HIDDEN)), const((NSA_CMP_HIDDEN, HEAD_DIM)),
           const((NSA_BLOCK, HEAD_DIM, NSA_CMP_HIDDEN)), const((NSA_CMP_HIDDEN, HEAD_DIM))],
        out_specs=[tok(NSA_HEADS * HEAD_DIM), win_spec],
        scratch_shapes=[pltpu.VMEM((nb_pad * NSA_BLOCK, LANE), F32), pltpu.VMEM((nb_pad * NSA_BLOCK, LANE), F32),
                        pltpu.VMEM((NSA_REP * t, LANE), F32)],
    )
    y, new_win = pl.pallas_call(
        functools.partial(_nsa_decode_body, n_pages=n_pages, lw=lw, nb=nb, nb_pad=nb_pad, n_sel=min(NSA_TOPK, nb)),
        grid_spec=grid_spec,
        out_shape=[jax.ShapeDtypeStruct((bsz * t, NSA_HEADS * HEAD_DIM), F32),
                   jax.ShapeDtypeStruct((bsz, 2 * LANE, lw), F32)],
        compiler_params=_cparams("arbitrary"),
        name="nsa_decode_attn",
    )(page_table.reshape(-1), q, small, rows, win, *([pages] * n_pages), pastwin,
      _pe2(lp['nsa_pe_k']), _pe2(lp['nsa_pe_v']), w1(lp['nsa_ck_w1']), lp['nsa_ck_w2'].astype(BF16),
      w1(lp['nsa_cv_w1']), lp['nsa_cv_w2'].astype(BF16))
    new_win = jnp.transpose(new_win.reshape(bsz, 2, NSA_KV_HEADS, HEAD_DIM, lw), (0, 4, 1, 2, 3))
    return y, new_win


def _dil_decode_group(q, k_new, v_new, cache_ref, step):
    t = q.shape[0]
    lg = cache_ref.shape[1]
    pairs = DIL_HEADS_PER_GROUP // 2
    tok_c = lax.broadcasted_iota(jnp.int32, (2 * t, lg), 0) % t
    ahead = lax.broadcasted_iota(jnp.int32, (2 * t, lg), 1) - tok_c
    valid_c = jnp.where(ahead >= 0, ahead % step, 1) == 0
    tok_n = lax.broadcasted_iota(jnp.int32, (2 * t, t), 0) % t
    back = tok_n - lax.broadcasted_iota(jnp.int32, (2 * t, t), 1)
    valid_n = jnp.where(back >= 0, back % step, 1) == 0
    lower = _half_mask((t, LANE), 0)
    outs, lses = [], []
    for pr in range(pairs):
        sl = slice(pr * LANE, (pr + 1) * LANE)
        k_t = cache_ref[pr * LANE:(pr + 1) * LANE, :].astype(BF16)
        v_t = cache_ref[DIL_WIDTH + pr * LANE:DIL_WIDTH + (pr + 1) * LANE, :].astype(BF16)
        kn = k_new[:, sl].astype(BF16)
        vn = v_new[:, sl].astype(BF16)
        qp = q[:, sl]
        q2 = jnp.concatenate([jnp.where(lower, qp, 0.0), jnp.where(lower, 0.0, qp)], axis=0).astype(BF16)
        o, lse = _attend_cached(q2, k_t, v_t, kn, vn, valid_c, valid_n)
        lse = jnp.broadcast_to(lse, (2 * t, LANE))
        outs.append(jnp.where(lower, o[:t], o[t:]))
        lses.append(jnp.where(lower, lse[:t], lse[t:]))
    return jnp.concatenate(outs, axis=1), jnp.concatenate(lses, axis=1)


def _dil_decode_body(x_ref, c0_ref, c1_ref, c2_ref, y_ref, n0_ref, n1_ref, n2_ref):
    scale = HEAD_DIM ** -0.5
    t = x_ref.shape[1]
    res = []
    for gi, (cache_ref, new_ref) in enumerate(((c0_ref, n0_ref), (c1_ref, n1_ref), (c2_ref, n2_ref))):
        _, step = DIL_PATTERNS[gi]
        part = lambda p: jnp.concatenate(
            [x_ref[(p * DIL_GROUPS + gi) * _DIL_PAIRS + pr] for pr in range(_DIL_PAIRS)], axis=1)
        res.append(_dil_decode_group(part(0) * scale, part(1), part(2), cache_ref, step))
        new_t = jnp.concatenate([part(1).T, part(2).T], axis=0)
        new_ref[...] = jnp.concatenate([cache_ref[:, t:], new_t], axis=1)
    (o0, l0), (o1, l1), (o2, l2) = res
    m = jnp.maximum(jnp.maximum(l0, l1), l2)
    e0, e1, e2 = jnp.exp(l0 - m), jnp.exp(l1 - m), jnp.exp(l2 - m)
    den = e0 + e1 + e2
    y_ref[...] = (e0 / den) * o0 + (e1 / den) * o1 + (e2 / den) * o2


def _dil_decode_operands(dil, bufs, bsz, t):
    n_rows = 2 * DIL_WIDTH
    views = []
    for (win, _), buf in zip(DIL_PATTERNS, bufs):
        assert buf.shape[1] == win, "decode path needs a full window of cached rows"
        views.append(jnp.transpose(buf, (0, 2, 3, 4, 1)).reshape(bsz, n_rows, win))
    cache_spec = lambda v: pl.BlockSpec((None, n_rows, v.shape[2]), lambda b, *_: (b, 0, 0))
    in_specs = [pl.BlockSpec((DIL_IN // LANE, t, LANE), lambda b, *_: (0, b, 0))] + [cache_spec(v) for v in views]
    out_specs = [pl.BlockSpec((t, DIL_WIDTH), lambda b, *_: (b, 0))] + [cache_spec(v) for v in views]
    out_shape = [jax.ShapeDtypeStruct((bsz * t, DIL_WIDTH), F32)] + [jax.ShapeDtypeStruct(v.shape, F32) for v in views]
    unview = lambda v: jnp.transpose(v.reshape(bsz, 2, DIL_HEADS_PER_GROUP, HEAD_DIM, v.shape[2]), (0, 4, 1, 2, 3))
    return [dil] + views, in_specs, out_specs, out_shape, unview


def _gdn_dil_decode_body(*refs, chunk):
    n_gdn_in, n_dil_in, n_gdn_out, n_dil_out = 9, 4, 3, 4
    gdn_in, rest = refs[:n_gdn_in], refs[n_gdn_in:]
    dil_in, rest = rest[:n_dil_in], rest[n_dil_in:]
    gdn_out, rest = rest[:n_gdn_out], rest[n_gdn_out:]
    dil_out, scratch = rest[:n_dil_out], rest[n_dil_out:]
    _gdn_body(*gdn_in, *gdn_out, *scratch, chunk=chunk)
    _dil_decode_body(*dil_in, *dil_out)


_AB_CUTS = (0, SSM_INNER, SSM_INNER + SSM_CONV_DIM, SSM_INNER + SSM_CONV_DIM + SSM_HEADS)
_AB_Q0 = _AB_CUTS[3]
_AB_KV0 = _AB_Q0 + NSA_HEADS * HEAD_DIM
_AB_WIN0 = _AB_KV0 + 4 * NSA_KV_HEADS * HEAD_DIM
_AB_GATE0 = _AB_WIN0 + 2 * NSA_KV_HEADS * HEAD_DIM
_AB_END = _AB_GATE0 + 3 * NSA_HEADS


def _pad_cols(w, n):
    return jnp.pad(w, ((0, 0), (0, n - w.shape[1])))


def _ab_weight_segs(w_in):
    small = jnp.concatenate([w_in[:, _AB_CUTS[2]:_AB_CUTS[3]], w_in[:, _AB_GATE0:_AB_END]], axis=1)
    segs = [w_in[:, _AB_CUTS[0]:_AB_CUTS[1]], w_in[:, _AB_CUTS[1]:_AB_CUTS[2]], w_in[:, _AB_Q0:_AB_KV0],
            w_in[:, _AB_KV0:_AB_WIN0], w_in[:, _AB_WIN0:_AB_GATE0], _pad_cols(small, LANE)]
    return [s.astype(BF16) for s in segs]


_CD_Z0 = 3 * GDN_WIDTH
_CD_B0 = _CD_Z0 + GDN_WIDTH
_CD_DIL0 = _CD_B0 + 2 * GDN_HEADS
_CD_END = _CD_DIL0 + DIL_IN


def _cd_weight_segs(w_in):
    segs = [w_in[:, :_CD_Z0], w_in[:, _CD_Z0:_CD_B0], w_in[:, _CD_DIL0:_CD_END],
            _pad_cols(w_in[:, _CD_B0:_CD_DIL0], LANE)]
    return [s.astype(BF16) for s in segs]


def _layer_ab(y, norm_pre, segs, lp, past, page_table, is_prompt, tm):
    bsz, t, d = y.shape
    if is_prompt:
        z, xbc, q, rows, win, small, rows_t, win_t = _norm_proj(
            y.reshape(bsz * t, d), norm_pre, segs, tm, t_segs=(segs[3].T, segs[4].T), seq_len=t)
        to_rows = lambda x_t, kinds: jnp.transpose(
            x_t.reshape(bsz, kinds, NSA_KV_HEADS, HEAD_DIM, x_t.shape[-1]), (0, 4, 1, 2, 3))
        rows_new = to_rows(rows_t, 4)
        new_win = to_rows(win_t[:, :, t - min(NSA_WINDOW, t):], 2)
        conv_buf = jnp.zeros((bsz, CONV_K - 1, SSM_CONV_DIM), F32)
        h0 = jnp.zeros((bsz, SSM_HEADS, SSM_HEAD_DIM, SSM_STATE), F32)
        chunk = math.gcd(t, SSM_CHUNK)
        rows3 = rows.reshape(bsz, t, 4 * LANE)
        cmp = _nsa_compress_prompt(rows3, lp)
        y_nsa = _nsa_prompt_attn(q, small, cmp, rows3, win.reshape(bsz, t, 2 * LANE), bsz, t)
    else:
        z, xbc, q, rows, win, small = _norm_proj(y.reshape(bsz * t, d), norm_pre, segs, tm)
        rows_new = rows.reshape(bsz, t, 4, NSA_KV_HEADS, HEAD_DIM)
        conv_buf, h0, chunk = past['ssm_conv'], past['ssm'], t
        y_nsa, new_win = _nsa_decode_attn(q, small, rows, win, past['nsa_kv'], past['nsa_win'], page_table, lp,
                                          bsz, t)
    y_ssm, new_conv, new_ssm = _ssd_mixer(xbc, z, small, conv_buf, h0, lp, bsz, t, chunk)
    return y_ssm, y_nsa, (new_conv, new_ssm, rows_new, new_win)


def _layer_cd(y, norm_pre, segs, lp, past, is_prompt, tm):
    bsz, t, d = y.shape
    qkv, z, dil, small = _norm_proj(y.reshape(bsz * t, d), norm_pre, segs, tm, plane_segs=(2,))
    if is_prompt:
        conv_buf = jnp.zeros((bsz, CONV_K - 1, 3 * GDN_WIDTH), F32)
        s0 = jnp.zeros((bsz, GDN_HEADS, GDN_HEAD_DIM, GDN_HEAD_DIM), F32)
        chunk = math.gcd(t, GDN_CHUNK)
        parts = [_dil_prompt_attn(dil, gi, bsz, t) for gi in range(DIL_GROUPS)]
        y_dil = _dil_combine([p[0] for p in parts], [p[1] for p in parts], min(tm * 2, bsz * t))
        bufs = []
        for gi, (win, _) in enumerate(DIL_PATTERNS):
            w = min(win, t)
            dil4 = dil.reshape(DIL_IN // LANE, bsz, t, LANE)
            tail = lambda part: lax.slice(dil4, ((part * DIL_GROUPS + gi) * _DIL_PAIRS, 0, t - w, 0),
                                          ((part * DIL_GROUPS + gi + 1) * _DIL_PAIRS, bsz, t, LANE))
            kv = lax.optimization_barrier(jnp.stack([tail(1), tail(2)]))
            kv = kv.reshape(2, _DIL_PAIRS, bsz, w, 2, HEAD_DIM)
            bufs.append(jnp.transpose(kv, (2, 3, 0, 1, 4, 5)).reshape(bsz, w, 2, DIL_HEADS_PER_GROUP, HEAD_DIM))
        y_gdn, new_conv, new_gdn = _gdn_mixer(qkv, z, small, conv_buf, s0, lp, bsz, t, chunk)
    else:
        y_gdn, new_conv, new_gdn, y_dil, bufs = _gdn_mixer(qkv, z, small, past['gdn_conv'], past['gdn'], lp, bsz, t, t,
                                                            dil_decode=(dil, past['dil']))
    return y_gdn, y_dil, (new_conv, new_gdn, bufs[0], bufs[1], bufs[2])


def kernel(x_prompt, x_sample, cache_ssm_conv, state_ssm, cache_nsa_kv, cache_nsa_win_kv, cache_gdn_conv, state_gdn,
           cache_dil0_kv, cache_dil1_kv, cache_dil2_kv, page_table, norm_mix_pre, norm_mix_post, norm_mlp_pre,
           norm_mlp_post, mlp_w1, mlp_w2, ab_w_in, ab_w_out, ssm_conv_w, ssm_conv_b, ssm_dt_bias, ssm_a_log, ssm_d,
           ssm_norm_w, nsa_pe_k, nsa_pe_v, nsa_ck_w1, nsa_ck_w2, nsa_cv_w1, nsa_cv_w2, cd_w_in, cd_w_out, gdn_conv_w,
           gdn_dt_bias, gdn_a_log, gdn_norm_w):
    depth = norm_mix_pre.shape[0]
    yp, ys = x_prompt, x_sample
    ab_p, ab_s, cd_p, cd_s = [], [], [], []
    tm = TOKEN_TILE
    for l in range(depth):
        j = l // 2
        if l % 2 == 0:
            lp = {'ssm_conv_w': ssm_conv_w[j], 'ssm_conv_b': ssm_conv_b[j], 'ssm_dt_bias': ssm_dt_bias[j],
                  'ssm_a_log': ssm_a_log[j], 'ssm_d': ssm_d[j], 'ssm_norm_w': ssm_norm_w[j],
                  'nsa_pe_k': nsa_pe_k[j], 'nsa_pe_v': nsa_pe_v[j], 'nsa_ck_w1': nsa_ck_w1[j],
                  'nsa_ck_w2': nsa_ck_w2[j], 'nsa_cv_w1': nsa_cv_w1[j], 'nsa_cv_w2': nsa_cv_w2[j]}
            past = {'ssm_conv': cache_ssm_conv[j], 'ssm': state_ssm[j], 'nsa_kv': cache_nsa_kv[j],
                    'nsa_win': cache_nsa_win_kv[j]}
            segs = _ab_weight_segs(ab_w_in[j])
            w_out = ab_w_out[j].astype(BF16)
            ka = SSM_INNER
            ap, bp, stp = _layer_ab(yp, norm_mix_pre[l], segs, lp, None, None, True, tm)
            as_, bs, sts = _layer_ab(ys, norm_mix_pre[l], segs, lp, past, page_table, False, tm)
            ab_p.append(stp)
            ab_s.append(sts)
        else:
            lp = {'gdn_conv_w': gdn_conv_w[j], 'gdn_dt_bias': gdn_dt_bias[j], 'gdn_a_log': gdn_a_log[j],
                  'gdn_norm_w': gdn_norm_w[j]}
            past = {'gdn_conv': cache_gdn_conv[j], 'gdn': state_gdn[j],
                    'dil': (cache_dil0_kv[j], cache_dil1_kv[j], cache_dil2_kv[j])}
            segs = _cd_weight_segs(cd_w_in[j])
            w_out = cd_w_out[j].astype(BF16)
            ka = GDN_WIDTH
            ap, bp, stp = _layer_cd(yp, norm_mix_pre[l], segs, lp, None, True, tm)
            as_, bs, sts = _layer_cd(ys, norm_mix_pre[l], segs, lp, past, False, tm)
            cd_p.append(stp)
            cd_s.append(sts)
        w1 = mlp_w1[l].astype(BF16)
        w2 = mlp_w2[l].astype(BF16)
        post = functools.partial(_post_block, wo_a=w_out[:ka], wo_b=w_out[ka:], w1=w1, w2=w2, n_mix=norm_mix_post[l],
                                 n_pre=norm_mlp_pre[l], n_post=norm_mlp_post[l], tm=tm)
        yp = post(ap, bp, yp.reshape(-1, D_MODEL)).reshape(yp.shape)
        ys = post(as_, bs, ys.reshape(-1, D_MODEL)).reshape(ys.shape)
    stack = lambda states, i: jnp.stack([s[i] for s in states])
    return (yp, ys,
            stack(ab_p, 0), stack(ab_s, 0), stack(ab_p, 1), stack(ab_s, 1),
            stack(ab_p, 2), stack(ab_s, 2), stack(ab_p, 3), stack(ab_s, 3),
            stack(cd_p, 0), stack(cd_s, 0), stack(cd_p, 1), stack(cd_s, 1),
            stack(cd_p, 2), stack(cd_s, 2), stack(cd_p, 3), stack(cd_s, 3),
            stack(cd_p, 4), stack(cd_s, 4))
```

```python
import functools
import math

import jax
import jax.numpy as jnp
from jax import lax
from jax.experimental import pallas as pl
from jax.experimental.pallas import tpu as pltpu

F32 = jnp.float32
BF16 = jnp.bfloat16

D_MODEL = 1024
HEAD_DIM = 64
CONV_K = 4
RMS_EPS = 1e-6
PAGE_SIZE = 128

SSM_HEADS = 16
SSM_HEAD_DIM = 64
SSM_INNER = SSM_HEADS * SSM_HEAD_DIM
SSM_GROUPS = 2
SSM_STATE = 128
SSM_CONV_DIM = SSM_INNER + 2 * SSM_GROUPS * SSM_STATE
SSM_CHUNK = 128

NSA_HEADS = 16
NSA_KV_HEADS = 2
NSA_REP = NSA_HEADS // NSA_KV_HEADS
NSA_BLOCK = 64
NSA_TOPK = 16
NSA_WINDOW = 512
NSA_CMP_HIDDEN = 256

GDN_HEADS = 8
GDN_HEAD_DIM = 128
GDN_WIDTH = GDN_HEADS * GDN_HEAD_DIM
GDN_CHUNK = 64

DIL_PATTERNS = ((128, 1), (512, 4), (2048, 16))
DIL_GROUPS = len(DIL_PATTERNS)
DIL_HEADS_PER_GROUP = 8
DIL_WIDTH = DIL_HEADS_PER_GROUP * HEAD_DIM
DIL_IN = 3 * DIL_GROUPS * DIL_WIDTH
NEG = -1e30

VMEM_LIMIT_BYTES = 56 * 1024 * 1024
LANE = 128
TOKEN_TILE = 256


def _cparams(*sem):
    return pltpu.CompilerParams(dimension_semantics=sem, vmem_limit_bytes=VMEM_LIMIT_BYTES)


def _const_spec(shape):
    nd = len(shape)
    return pl.BlockSpec(shape, lambda *_: (0,) * nd, pipeline_mode=pl.Buffered(1))


def _rms(x, w):
    return x * lax.rsqrt(jnp.mean(x * x, axis=-1, keepdims=True) + RMS_EPS) * w


def _proj_body(x_ref, nw_ref, *refs, n_t):
    n = len(refs) // 2
    xn = _rms(x_ref[...], nw_ref[...]).astype(BF16)
    for k, (w_ref, o_ref) in enumerate(zip(refs[:n], refs[n:])):
        if k >= n - n_t:
            o_ref[...] = lax.dot_general(w_ref[...], xn, _NT, preferred_element_type=F32)
            continue
        y = jnp.dot(xn, w_ref[...], preferred_element_type=F32)
        if len(o_ref.shape) == 2:
            o_ref[...] = y
        else:
            for j in range(o_ref.shape[0]):
                o_ref[j] = y[:, j * LANE:(j + 1) * LANE]


def _norm_proj(x, norm_w, w_segs, tm, plane_segs=(), t_segs=(), seq_len=None):
    n_tok, d = x.shape
    tm = min(tm, n_tok)
    widths = [w.shape[1] for w in w_segs]
    planes = [k in plane_segs for k in range(len(w_segs))]
    out_spec = lambda n, p: (pl.BlockSpec((n // LANE, tm, LANE), lambda i: (0, i, 0)) if p
                             else pl.BlockSpec((tm, n), lambda i: (i, 0)))
    out_sds = lambda n, p: jax.ShapeDtypeStruct((n // LANE, n_tok, LANE) if p else (n_tok, n), F32)
    per_seq = seq_len // tm if t_segs else 1
    t_spec = lambda w: pl.BlockSpec((None, w.shape[0], tm), lambda i: (i // per_seq, 0, i % per_seq))
    t_sds = lambda w: jax.ShapeDtypeStruct((n_tok // seq_len, w.shape[0], seq_len), F32)
    return pl.pallas_call(
        functools.partial(_proj_body, n_t=len(t_segs)),
        grid=(n_tok // tm,),
        in_specs=[pl.BlockSpec((tm, d), lambda i: (i, 0)), _const_spec((1, d))]
        + [_const_spec((d, n)) for n in widths] + [_const_spec(w.shape) for w in t_segs],
        out_specs=[out_spec(n, p) for n, p in zip(widths, planes)] + [t_spec(w) for w in t_segs],
        out_shape=[out_sds(n, p) for n, p in zip(widths, planes)] + [t_sds(w) for w in t_segs],
        compiler_params=_cparams("arbitrary"),
        name="norm_proj",
    )(x, norm_w.reshape(1, d), *w_segs, *t_segs)


def _post_body(a_ref, b_ref, y_ref, woa_ref, wob_ref, w1_ref, w2_ref, nmix_ref, npre_ref, npost_ref, o_ref):
    m = jnp.dot(a_ref[...].astype(BF16), woa_ref[...], preferred_element_type=F32)
    m = m + jnp.dot(b_ref[...].astype(BF16), wob_ref[...], preferred_element_type=F32)
    y1 = y_ref[...] + _rms(m, nmix_ref[...])
    h = _rms(y1, npre_ref[...]).astype(BF16)
    a = jnp.maximum(jnp.dot(h, w1_ref[...], preferred_element_type=F32), 0.0)
    m2 = jnp.dot((a * a).astype(BF16), w2_ref[...], preferred_element_type=F32)
    o_ref[...] = y1 + _rms(m2, npost_ref[...])


def _post_block(mix_a, mix_b, y, wo_a, wo_b, w1, w2, n_mix, n_pre, n_post, tm):
    n_tok, d = y.shape
    tm = min(tm, n_tok)
    ka, kb = mix_a.shape[1], mix_b.shape[1]
    row = lambda n: pl.BlockSpec((tm, n), lambda i: (i, 0))
    return pl.pallas_call(
        _post_body,
        grid=(n_tok // tm,),
        in_specs=[row(ka), row(kb), row(d), _const_spec(wo_a.shape), _const_spec(wo_b.shape),
                  _const_spec(w1.shape), _const_spec(w2.shape), _const_spec((1, d)), _const_spec((1, d)),
                  _const_spec((1, d))],
        out_specs=row(d),
        out_shape=jax.ShapeDtypeStruct((n_tok, d), F32),
        compiler_params=_cparams("arbitrary"),
        name="post_block",
    )(mix_a, mix_b, y, wo_a, wo_b, w1, w2, n_mix.reshape(1, d), n_pre.reshape(1, d), n_post.reshape(1, d))


def _compress_rows(src_ref, pe_ref, w1_ref, w2_ref, nb):
    hd = HEAD_DIM
    low = _half_mask((nb, LANE), 0)
    rows_per_step = 4 * LANE // (2 * hd)

    def pack(a, b):
        top = jnp.where(low, a, pltpu.roll(b, hd, axis=1))
        bot = jnp.where(low, pltpu.roll(a, hd, axis=1), b)
        return jnp.concatenate([top, bot], axis=0)

    def body(i, acc):
        r = i * rows_per_step
        x = [src_ref[pl.ds(r + k, nb, stride=NSA_BLOCK), :] + pe_ref[pl.ds(r + k, 1), :] for k in range(rows_per_step)]
        x4 = jnp.concatenate([pack(x[0], x[1]), pack(x[2], x[3])], axis=1).astype(BF16)
        w4 = w1_ref[pl.ds(r, rows_per_step)].reshape(rows_per_step * hd, NSA_CMP_HIDDEN)
        return acc + jnp.dot(x4, w4, preferred_element_type=F32)

    hid = lax.fori_loop(0, NSA_BLOCK // rows_per_step, body, jnp.zeros((2 * nb, NSA_CMP_HIDDEN), F32), unroll=True)
    hid = hid * jax.nn.sigmoid(hid)
    out = jnp.dot(hid.astype(BF16), w2_ref[...], preferred_element_type=F32)
    return jnp.concatenate([out[:nb], out[nb:]], axis=1)


def _compress_body(kc_ref, vc_ref, pek_ref, pev_ref, wk1_ref, wk2_ref, wv1_ref, wv2_ref, o_ref, *, nb):
    o_ref[:, :LANE] = _compress_rows(kc_ref, pek_ref, wk1_ref, wk2_ref, nb)
    o_ref[:, LANE:] = _compress_rows(vc_ref, pev_ref, wv1_ref, wv2_ref, nb)


def _pe2(pe):
    return jnp.concatenate([pe, pe], axis=1)


def _nsa_compress_prompt(rows, lp):
    bsz, t, _ = rows.shape
    nb = t // NSA_BLOCK
    w1 = lambda w: w.reshape(NSA_BLOCK, HEAD_DIM, NSA_CMP_HIDDEN).astype(BF16)
    col = lambda c: pl.BlockSpec((None, t, LANE), lambda b: (b, 0, c))
    return pl.pallas_call(
        functools.partial(_compress_body, nb=nb),
        grid=(bsz,),
        in_specs=[col(0), col(1), _const_spec((NSA_BLOCK, LANE)), _const_spec((NSA_BLOCK, LANE)),
                  _const_spec((NSA_BLOCK, HEAD_DIM, NSA_CMP_HIDDEN)), _const_spec((NSA_CMP_HIDDEN, HEAD_DIM)),
                  _const_spec((NSA_BLOCK, HEAD_DIM, NSA_CMP_HIDDEN)), _const_spec((NSA_CMP_HIDDEN, HEAD_DIM))],
        out_specs=pl.BlockSpec((None, nb, 2 * LANE), lambda b: (b, 0, 0)),
        out_shape=jax.ShapeDtypeStruct((bsz, nb, 2 * LANE), F32),
        compiler_params=_cparams("arbitrary"),
        name="nsa_compress",
    )(rows, rows, _pe2(lp['nsa_pe_k']), _pe2(lp['nsa_pe_v']), w1(lp['nsa_ck_w1']), lp['nsa_ck_w2'].astype(BF16),
      w1(lp['nsa_cv_w1']), lp['nsa_cv_w2'].astype(BF16))


_NT = (((1,), (1,)), ((), ()))


def _half_mask(shape, g):
    lane = lax.broadcasted_iota(jnp.int32, shape, len(shape) - 1)
    return lane >= HEAD_DIM if g else lane < HEAD_DIM


def _stack_heads(x, heads, g):
    keep = _half_mask((x.shape[0], LANE), g)
    out = []
    for h in heads:
        blk = x[:, (h // 2) * LANE:(h // 2 + 1) * LANE]
        if h % 2 != g:
            blk = pltpu.roll(blk, HEAD_DIM, axis=1)
        out.append(jnp.where(keep, blk, 0.0))
    return jnp.concatenate(out, axis=0)


def _unstack_pair(a, b, g):
    if g == 0:
        b = pltpu.roll(b, HEAD_DIM, axis=1)
    else:
        a = pltpu.roll(a, HEAD_DIM, axis=1)
    return jnp.where(_half_mask(a.shape, 1), b, a)


def _row_gate(sig, lanes, rows):
    return jnp.concatenate([jnp.broadcast_to(sig[:, c:c + 1], (rows, LANE)) for c in lanes], axis=0)


def _nsa_select(p3, pos, nb, n_sel):
    nq = p3.shape[1]
    imp = jnp.sum(p3, axis=0)
    n_i = lax.broadcasted_iota(jnp.int32, (nq, nb), 1)
    cur = pos // NSA_BLOCK
    score = jnp.where(n_i == 0, NSA_REP + 1.0, imp)
    score = jnp.where(n_i == cur, NSA_REP + 1.0, score)
    score = jnp.where(n_i == cur - 1, NSA_REP + 1.0, score)
    score = jnp.where(n_i > cur, -1.0, score)
    rank = jnp.zeros((nq, nb), F32)
    for m in range(nb):
        col = score[:, m:m + 1]
        tie = jnp.where(n_i > m, 1.0, 0.0)
        rank = rank + jnp.where(col > score, 1.0, jnp.where(col == score, tie, 0.0))
    return jnp.where(rank < n_sel, 1.0, 0.0)


def _nsa_select_t(imp, pos, n_sel):
    nb, nq = imp.shape
    n_i = lax.broadcasted_iota(jnp.int32, (nb, nq), 0)
    cur = pos // NSA_BLOCK
    score = jnp.where(n_i == 0, NSA_REP + 1.0, imp)
    score = jnp.where(n_i == cur, NSA_REP + 1.0, score)
    score = jnp.where(n_i == cur - 1, NSA_REP + 1.0, score)
    score = jnp.where(n_i > cur, -1.0, score)
    rank = jnp.zeros((nb, nq), F32)
    for m in range(nb):
        row = score[m:m + 1, :]
        tie = jnp.where(n_i > m, 1.0, 0.0)
        rank = rank + jnp.where(row > score, 1.0, jnp.where(row == score, tie, 0.0))
    return jnp.where(rank < n_sel, 1.0, 0.0)


def _lane_tile(x, n):
    return jnp.concatenate([x] * n, axis=1)


_V_ROWS = HEAD_DIM + 16
_NSA_QUERY_BLOCK = 256
_SEL_TILE = 512
_WIN_TILE = 256


def _flash_tile_t(kt, vt_aug, q_t, valid, m_ref, acc_ref, g):
    nk, nq = valid.shape
    rep = q_t.shape[1] // nq
    s = jnp.dot(kt, q_t, preferred_element_type=F32)
    s = jnp.concatenate([jnp.where(valid, s[:, r * nq:(r + 1) * nq], NEG) for r in range(rep)], axis=1)
    m_old = m_ref[g, 0:1, :]
    m_new = jnp.maximum(m_old, jnp.max(s, axis=0, keepdims=True))
    alpha = jnp.exp2(m_old - m_new)
    p = jnp.exp2(s - m_new)
    acc_ref[g] = alpha * acc_ref[g] + jnp.dot(vt_aug, p.astype(BF16), preferred_element_type=F32)
    m_ref[g] = jnp.broadcast_to(m_new, m_ref.shape[1:])


def _nsa_prompt_body(q_ref, small_ref, cmp_ref, ks_ref, vs_ref, kw_ref, vw_ref, o_ref, vst_ref, vwt_ref, m_ref, acc_ref,
                     *, nb, n_sel):
    qb = q_ref.shape[0]
    t = ks_ref.shape[0]
    i = pl.program_id(1)
    q0 = i * qb
    rep = NSA_REP
    hd = HEAD_DIM
    groups = range(NSA_KV_HEADS)

    @pl.when(i == 0)
    def _():
        ones = jnp.ones((_V_ROWS - hd, t), BF16)
        for g in groups:
            vst_ref[g, hd:, :] = ones
            vwt_ref[g, hd:, :] = ones

        def fill(j, carry):
            k0 = pl.multiple_of(j * qb, qb)
            for src, dst in ((vs_ref, vst_ref), (vw_ref, vwt_ref)):
                v_t = src[pl.ds(k0, qb), :].T.astype(BF16)
                for g in groups:
                    dst[g, :hd, pl.ds(k0, qb)] = v_t[g * hd:(g + 1) * hd, :]
            return carry

        lax.fori_loop(0, t // qb, fill, 0)

    pos_r = q0 + lax.broadcasted_iota(jnp.int32, (1, qb), 1)
    sig_t = jax.nn.sigmoid(small_ref[...]).T
    q = q_ref[...] * (hd ** -0.5)
    pairs_t = [q[:, c * LANE:(c + 1) * LANE].T for c in range(NSA_HEADS // 2)]
    head_t = lambda h: pairs_t[h // 2][(h % 2) * hd:(h % 2 + 1) * hd, :]
    zeros = jnp.zeros((hd, rep * qb), F32)
    q_ts, sels = [], []
    for g in groups:
        qg_t = jnp.concatenate([head_t(g * rep + r) for r in range(rep)], axis=1)
        stack = lambda x: jnp.concatenate([x, zeros] if g == 0 else [zeros, x], axis=0).astype(BF16)
        q_t = stack(qg_t)
        q_ts.append(stack(qg_t * math.log2(math.e)))
        gate_row = lambda j: jnp.concatenate(
            [sig_t[SSM_HEADS + j * NSA_HEADS + g * rep + r:SSM_HEADS + j * NSA_HEADS + g * rep + r + 1, :]
             for r in range(rep)], axis=1)
        kc = cmp_ref[:, :LANE].astype(BF16)
        vc_t = cmp_ref[:, LANE:].T[g * hd:(g + 1) * hd, :].astype(BF16)
        s = jnp.dot(kc, q_t, preferred_element_type=F32)
        n_i = lax.broadcasted_iota(jnp.int32, (nb, qb), 0)
        complete = _lane_tile(jnp.where((n_i + 1) * NSA_BLOCK <= pos_r + 1, 1.0, 0.0), rep)
        s = jnp.where(complete > 0.5, s, NEG)
        p = jnp.exp(s - jnp.max(s, axis=0, keepdims=True))
        p = p / jnp.sum(p, axis=0, keepdims=True)
        p = jnp.where(_lane_tile(pos_r, rep) >= NSA_BLOCK - 1, p, 0.0)
        o_cmp = jnp.dot(vc_t, p.astype(BF16), preferred_element_type=F32)
        acc_ref[2 + g, :hd, :] = gate_row(0) * o_cmp
        imp = p[:, :qb]
        for r in range(1, rep):
            imp = imp + p[:, r * qb:(r + 1) * qb]
        sels.append(_nsa_select_t(imp, pos_r, n_sel).astype(BF16))

    def flash_init():
        m_ref[...] = jnp.full(m_ref.shape, NEG, F32)
        acc_ref[0:2] = jnp.zeros((2,) + acc_ref.shape[1:], F32)

    def flash_out(g):
        acc = acc_ref[g]
        return acc[:hd, :] / acc[hd:hd + 1, :]

    def flash_pass(k_ref, vt_ref, kb, first_key, valid_fn):
        kpos_i = lax.broadcasted_iota(jnp.int32, (kb, qb), 0)
        qpos = q0 + lax.broadcasted_iota(jnp.int32, (kb, qb), 1)
        flash_init()

        def step(j, carry):
            k0 = pl.multiple_of(j * kb, kb)
            kt = k_ref[pl.ds(k0, kb), :].astype(BF16)
            valids = valid_fn(k0, k0 + kpos_i, qpos)
            for g in groups:
                _flash_tile_t(kt, vt_ref[g, :, pl.ds(k0, kb)], q_ts[g], valids[g], m_ref, acc_ref, g)
            return carry

        lax.fori_loop(first_key // kb, (q0 + qb - 1) // kb + 1, step, 0)

    tile = lambda keys: keys if t % keys == 0 else qb

    def sel_valid(k0, kpos, qpos):
        kb = kpos.shape[0]
        blk = lax.broadcasted_iota(jnp.int32, (kb, nb), 1)
        key = lax.broadcasted_iota(jnp.int32, (kb, nb), 0)
        expand = jnp.where(blk == (k0 + key) // NSA_BLOCK, 1.0, 0.0).astype(BF16)
        causal = kpos <= qpos
        return [jnp.where(causal, jnp.dot(expand, sels[g], preferred_element_type=F32), 0.0) > 0.5 for g in groups]

    flash_pass(ks_ref, vst_ref, tile(_SEL_TILE), 0, sel_valid)
    for g in groups:
        gate_row = jnp.concatenate(
            [sig_t[SSM_HEADS + NSA_HEADS + g * rep + r:SSM_HEADS + NSA_HEADS + g * rep + r + 1, :] for r in range(rep)],
            axis=1)
        acc_ref[2 + g, :hd, :] += gate_row * flash_out(g)
    def win_valid(k0, kpos, qpos):
        diff = qpos - kpos
        valid = jnp.where(diff >= 0, diff, NSA_WINDOW + 1) <= NSA_WINDOW
        return [valid for _ in groups]

    flash_pass(kw_ref, vwt_ref, tile(_WIN_TILE), jnp.maximum(q0 - NSA_WINDOW, 0), win_valid)
    for g in groups:
        gate_row = jnp.concatenate(
            [sig_t[SSM_HEADS + 2 * NSA_HEADS + g * rep + r:SSM_HEADS + 2 * NSA_HEADS + g * rep + r + 1, :]
             for r in range(rep)], axis=1)
        comb = acc_ref[2 + g, :hd, :] + gate_row * flash_out(g)
        for c in range(rep // 2):
            pair = jnp.concatenate([comb[:, (2 * c) * qb:(2 * c + 1) * qb], comb[:, (2 * c + 1) * qb:(2 * c + 2) * qb]],
                                   axis=0)
            col = (g * rep // 2 + c) * LANE
            o_ref[:, col:col + LANE] = pair.T


def _nsa_prompt_attn(q, small, cmp, rows, win, bsz, t):
    qb = math.gcd(t, _NSA_QUERY_BLOCK)
    nq = t // qb
    nb = cmp.shape[1]
    n_sel = min(NSA_TOPK, nb)
    tok = lambda n: pl.BlockSpec((qb, n), lambda b, i: (b * nq + i, 0))
    seq = lambda c: pl.BlockSpec((None, t, LANE), lambda b, i: (b, 0, c))
    v_t = pltpu.VMEM((NSA_KV_HEADS, _V_ROWS, t), BF16)
    return pl.pallas_call(
        functools.partial(_nsa_prompt_body, nb=nb, n_sel=n_sel),
        grid=(bsz, nq),
        in_specs=[tok(NSA_HEADS * HEAD_DIM), tok(LANE), pl.BlockSpec((None, nb, 2 * LANE), lambda b, i: (b, 0, 0)),
                  seq(2), seq(3), seq(0), seq(1)],
        out_specs=tok(NSA_HEADS * HEAD_DIM),
        out_shape=jax.ShapeDtypeStruct((bsz * t, NSA_HEADS * HEAD_DIM), F32),
        scratch_shapes=[v_t, v_t, pltpu.VMEM((NSA_KV_HEADS, 8, NSA_REP * qb), F32),
                        pltpu.VMEM((2 * NSA_KV_HEADS, _V_ROWS, NSA_REP * qb), F32)],
        compiler_params=_cparams("arbitrary", "arbitrary"),
        name="nsa_prompt_attn",
    )(q, small, cmp, rows, rows, win, win)


_TN = (((0,), (0,)), ((), ()))


def _silu(x):
    return x * jax.nn.sigmoid(x)


def _softplus(x):
    return jnp.maximum(x, 0.0) + jnp.log(1.0 + jnp.exp(-jnp.abs(x)))


def _conv_silu(x_ref, buf_ref, w_ref, b_ref, xp_ref, tail_ref, first):
    n = x_ref.shape[0]

    @pl.when(first)
    def _():
        xp_ref[8 - (CONV_K - 1):8, :] = buf_ref[...]

    xp_ref[8:8 + n, :] = x_ref[...]
    y = xp_ref[8:8 + n, :] * w_ref[CONV_K - 1:CONV_K, :]
    for k in range(CONV_K - 1):
        y = y + xp_ref[5 + k:5 + k + n, :] * w_ref[k:k + 1, :]
    if b_ref is not None:
        y = y + b_ref[...]
    tail = xp_ref[8 + n - (CONV_K - 1):8 + n, :]
    tail_ref[...] = tail
    xp_ref[8 - (CONV_K - 1):8, :] = tail
    return _silu(y)


def _bf16_terms(x):
    hi = x.astype(BF16)
    r = x - hi.astype(F32)
    mid = r.astype(BF16)
    return hi, mid, (r - mid.astype(F32)).astype(BF16)


def _cumsum_rows(x, seg=None):
    n = x.shape[0]
    i = lax.broadcasted_iota(jnp.int32, (n, n), 0)
    j = lax.broadcasted_iota(jnp.int32, (n, n), 1)
    keep = i >= j if seg in (None, n) else jnp.where(i // seg == j // seg, i - j, -1) >= 0
    tri = jnp.where(keep, 1.0, 0.0).astype(BF16)
    return sum(jnp.dot(tri, term, preferred_element_type=F32) for term in _bf16_terms(x))


def _expand_heads(x, width, lane0=0, n_out=D_MODEL):
    h_i = lax.broadcasted_iota(jnp.int32, (LANE, n_out), 0)
    c_i = lax.broadcasted_iota(jnp.int32, (LANE, n_out), 1)
    sel = jnp.where(c_i // width + lane0 == h_i, 1.0, 0.0).astype(BF16)
    return sum(jnp.dot(term, sel, preferred_element_type=F32) for term in _bf16_terms(x))


def _decay_matrix(col, row, strict=False):
    n = col.shape[0]
    i = lax.broadcasted_iota(jnp.int32, (n, n), 0)
    j = lax.broadcasted_iota(jnp.int32, (n, n), 1)
    keep = (i > j) if strict else (i >= j)
    return jnp.exp(jnp.where(keep, col - row, NEG))


def _ssd_body(xbc_ref, z_ref, small_ref, buf_ref, h0_ref, cw_ref, cb_ref, dtb_ref, alog_ref, dx_ref, nw_ref,
              y_ref, conv_ref, h_ref, xp_ref):
    c = pl.program_id(1)
    n = xbc_ref.shape[0]
    hp = SSM_HEAD_DIM
    rep = SSM_HEADS // SSM_GROUPS

    @pl.when(c == 0)
    def _():
        h_ref[...] = h0_ref[...]

    act = _conv_silu(xbc_ref, buf_ref, cw_ref, cb_ref, xp_ref, conv_ref, c == 0)
    xs = act[:, :SSM_INNER]
    head_lane = lax.broadcasted_iota(jnp.int32, (n, LANE), 1) < SSM_HEADS
    dt = jnp.where(head_lane, _softplus(small_ref[...] + dtb_ref[...]), 0.0)
    la = dt * (-jnp.exp(alog_ref[...]))
    acs = _cumsum_rows(la)
    acs_t = acs.T
    xd = xs * _expand_heads(dt, hp)
    e_acs = _expand_heads(jnp.exp(acs), hp)
    xdd = (xd * _expand_heads(jnp.exp(acs[n - 1:n, :] - acs), hp)).astype(BF16)
    xd = xd.astype(BF16)
    lane2 = _half_mask((n, LANE), 1)
    for g in range(SSM_GROUPS):
        bc = act[:, SSM_INNER + g * SSM_STATE:SSM_INNER + (g + 1) * SSM_STATE].astype(BF16)
        cc = act[:, SSM_INNER + (SSM_GROUPS + g) * SSM_STATE:SSM_INNER + (SSM_GROUPS + g + 1) * SSM_STATE].astype(BF16)
        cb = lax.dot_general(cc, bc, _NT, preferred_element_type=F32)
        h_prev = h_ref[g * rep:(g + 1) * rep].reshape(rep * hp, SSM_STATE)
        y_off = lax.dot_general(cc, h_prev.astype(BF16), _NT, preferred_element_type=F32)
        for pair in range(rep // 2):
            halves = []
            for k in range(2):
                h = g * rep + 2 * pair + k
                lm = _decay_matrix(acs[:, h:h + 1], acs_t[h:h + 1, :])
                halves.append(jnp.dot((cb * lm).astype(BF16), xd[:, (h // 2) * LANE:(h // 2 + 1) * LANE],
                                      preferred_element_type=F32))
            col = (g * rep + 2 * pair) * hp
            y_ref[:, col:col + LANE] = jnp.where(lane2, halves[1], halves[0]) + y_off[:, 2 * pair * hp:2 * pair * hp + LANE] * e_acs[:, col:col + LANE]
        st = lax.dot_general(xdd[:, g * rep * hp:(g + 1) * rep * hp], bc, _TN, preferred_element_type=F32)
        for r in range(rep):
            h = g * rep + r
            dec = jnp.exp(acs_t[h:h + 1, n - 1:n])
            h_ref[h] = h_ref[h] * dec + st[r * hp:(r + 1) * hp, :]
    y = y_ref[...] + dx_ref[...] * xs
    y_ref[...] = _rms(y * _silu(z_ref[...]), nw_ref[...])


def _ssd_mixer(xbc, z, small, conv_buf, h0, lp, bsz, t, chunk):
    nc = t // chunk
    tok = lambda n: pl.BlockSpec((chunk, n), lambda b, c: (b * nc + c, 0))
    per_b = lambda shape: pl.BlockSpec((None,) + shape, lambda b, c: (b,) + (0,) * len(shape))
    pad = lambda v: jnp.pad(v.astype(F32), (0, LANE - v.shape[0])).reshape(1, LANE)
    cdim = SSM_CONV_DIM
    state = (SSM_HEADS, SSM_HEAD_DIM, SSM_STATE)
    return pl.pallas_call(
        _ssd_body,
        grid=(bsz, nc),
        in_specs=[tok(cdim), tok(SSM_INNER), tok(LANE), per_b((CONV_K - 1, cdim)), per_b(state),
                  _const_spec((CONV_K, cdim)), _const_spec((1, cdim)), _const_spec((1, LANE)), _const_spec((1, LANE)),
                  _const_spec((1, SSM_INNER)), _const_spec((1, SSM_INNER))],
        out_specs=[tok(SSM_INNER), per_b((CONV_K - 1, cdim)), per_b(state)],
        out_shape=[jax.ShapeDtypeStruct((bsz * t, SSM_INNER), F32),
                   jax.ShapeDtypeStruct((bsz, CONV_K - 1, cdim), F32),
                   jax.ShapeDtypeStruct((bsz,) + state, F32)],
        scratch_shapes=[pltpu.VMEM((chunk + 8, cdim), F32)],
        compiler_params=_cparams("arbitrary", "arbitrary"),
        name="ssd_mixer",
    )(xbc, z, small, conv_buf, h0, lp['ssm_conv_w'], lp['ssm_conv_b'].reshape(1, cdim), pad(lp['ssm_dt_bias']),
      pad(lp['ssm_a_log']), jnp.repeat(lp['ssm_d'].astype(F32), SSM_HEAD_DIM).reshape(1, SSM_INNER),
      lp['ssm_norm_w'].reshape(1, SSM_INNER))


def _split_bf16(x):
    hi = x.astype(BF16)
    return hi, (x - hi.astype(F32)).astype(BF16)


def _bmm(a, b):
    return jnp.einsum('hmk,hkn->hmn', a, b, preferred_element_type=F32)


def _bmm_nt(a, b):
    return jnp.einsum('hmk,hnk->hmn', a, b, preferred_element_type=F32)


def _bmm_tn(a, b):
    return jnp.einsum('hkm,hkn->hmn', a, b, preferred_element_type=F32)


def _bmm_split(a, b):
    a_hi, a_lo = _split_bf16(a)
    b_hi, b_lo = _split_bf16(b)
    return _bmm(a_hi, b_hi) + _bmm(a_hi, b_lo) + _bmm(a_lo, b_hi)


def _unit_lower_inverse(a):
    n = a.shape[-1]
    eye = jnp.where(lax.broadcasted_iota(jnp.int32, (n, n), 0) == lax.broadcasted_iota(jnp.int32, (n, n), 1), 1.0, 0.0)
    p = -a
    t = eye + p
    span = 2
    while span < n:
        p = _bmm_split(p, p)
        t = t + _bmm_split(t, p)
        span *= 2
    return t


_GDN_CHUNKS_PER_STEP = 4


def _gdn_body(qkv_ref, z_ref, small_ref, buf_ref, s0_ref, cw_ref, alog_ref, dtb_ref, nw_ref,
              y_ref, conv_ref, s_ref, xp_ref, *, chunk):
    c = pl.program_id(1)
    n = qkv_ref.shape[0]
    dk = GDN_HEAD_DIM
    nh = GDN_HEADS
    n_ck = n // chunk

    @pl.when(c == 0)
    def _():
        s_ref[...] = s0_ref[...]

    act = _conv_silu(qkv_ref, buf_ref, cw_ref, None, xp_ref, conv_ref, c == 0)
    lane = lax.broadcasted_iota(jnp.int32, (n, LANE), 1)
    raw = small_ref[...]
    beta = jax.nn.sigmoid(raw)
    g = jnp.where((lane >= nh) & (lane < 2 * nh), -jnp.exp(alog_ref[...]) * _softplus(raw + dtb_ref[...]), 0.0)
    gc = _cumsum_rows(g, chunk)
    gc_t = gc.T
    gc_end = jnp.concatenate([jnp.broadcast_to(gc[(j + 1) * chunk - 1:(j + 1) * chunk, :], (chunk, LANE))
                              for j in range(n_ck)], axis=0)
    beta_x = _expand_heads(beta, dk)
    egc_x = _expand_heads(jnp.exp(gc), dk, nh)
    edec_x = _expand_heads(jnp.exp(gc_end - gc), dk, nh)
    ii = lax.broadcasted_iota(jnp.int32, (chunk, chunk), 0)
    jj = lax.broadcasted_iota(jnp.int32, (chunk, chunk), 1)
    heads = lambda x, base=0: jnp.stack([x[j * chunk:(j + 1) * chunk, base + h * dk:base + (h + 1) * dk]
                                         for j in range(n_ck) for h in range(nh)])
    q, k, v = heads(act), heads(act, nh * dk), heads(act, 2 * nh * dk)
    q = q * lax.rsqrt(jnp.sum(q * q, axis=-1, keepdims=True) + 1e-6) * (dk ** -0.5)
    k = k * lax.rsqrt(jnp.sum(k * k, axis=-1, keepdims=True) + 1e-6)
    beta3, egc, edec = heads(beta_x), heads(egc_x), heads(edec_x)
    kb = k * beta3
    decay = jnp.stack([_decay_matrix(gc[j * chunk:(j + 1) * chunk, nh + h:nh + h + 1],
                                     gc_t[nh + h:nh + h + 1, j * chunk:(j + 1) * chunk])
                       for j in range(n_ck) for h in range(nh)])
    k16 = k.astype(BF16)
    amat = jnp.where(ii > jj, _bmm_nt(kb.astype(BF16), k16) * decay, 0.0)
    tmat = _unit_lower_inverse(amat).astype(BF16)
    u = _bmm(tmat, (v * beta3).astype(BF16))
    w = _bmm(tmat, (kb * egc).astype(BF16)).astype(BF16)
    qk = (_bmm_nt(q.astype(BF16), k16) * decay).astype(BF16)
    qg = (q * egc).astype(BF16)
    kdec = (k * edec).astype(BF16)
    s = s_ref[...]
    for j in range(n_ck):
        ck = slice(j * nh, (j + 1) * nh)
        s16 = s.astype(BF16)
        v16 = (u[ck] - _bmm(w[ck], s16)).astype(BF16)
        o = _bmm(qg[ck], s16) + _bmm(qk[ck], v16)
        last = (j + 1) * chunk - 1
        gl = jnp.stack([jnp.exp(gc_t[nh + h:nh + h + 1, last:last + 1]) for h in range(nh)])
        s = s * gl + _bmm_tn(kdec[ck], v16)
        y = _rms(o, nw_ref[...])
        for h in range(nh):
            sl = slice(h * dk, (h + 1) * dk)
            y_ref[j * chunk:(j + 1) * chunk, sl] = y[h] * _silu(z_ref[j * chunk:(j + 1) * chunk, sl])
    s_ref[...] = s


def _gdn_mixer(qkv, z, small, conv_buf, s0, lp, bsz, t, chunk, dil_decode=None):
    per_step = _GDN_CHUNKS_PER_STEP if (t // chunk) % _GDN_CHUNKS_PER_STEP == 0 else 1
    rows = per_step * chunk
    nc = t // rows
    tok = lambda n: pl.BlockSpec((rows, n), lambda b, c: (b * nc + c, 0))
    per_b = lambda shape: pl.BlockSpec((None,) + shape, lambda b, c: (b,) + (0,) * len(shape))
    pad8 =lambda v: jnp.pad(v.astype(F32), (GDN_HEADS, LANE - 2 * GDN_HEADS)).reshape(1, LANE)
    cdim = 3 * GDN_WIDTH
    state = (GDN_HEADS, GDN_HEAD_DIM, GDN_HEAD_DIM)
    operands = [qkv, z, small, conv_buf, s0, lp['gdn_conv_w'], pad8(lp['gdn_a_log']), pad8(lp['gdn_dt_bias']),
                lp['gdn_norm_w'].reshape(1, GDN_HEAD_DIM)]
    in_specs = [tok(cdim), tok(GDN_WIDTH), tok(LANE), per_b((CONV_K - 1, cdim)), per_b(state),
                _const_spec((CONV_K, cdim)), _const_spec((1, LANE)), _const_spec((1, LANE)),
                _const_spec((1, GDN_HEAD_DIM))]
    out_specs = [tok(GDN_WIDTH), per_b((CONV_K - 1, cdim)), per_b(state)]
    out_shape = [jax.ShapeDtypeStruct((bsz * t, GDN_WIDTH), F32),
                 jax.ShapeDtypeStruct((bsz, CONV_K - 1, cdim), F32),
                 jax.ShapeDtypeStruct((bsz,) + state, F32)]
    body, name = functools.partial(_gdn_body, chunk=chunk), "gdn_mixer"
    if dil_decode is not None:
        assert nc == 1
        d_ops, d_in, d_out, d_shape, unview = _dil_decode_operands(*dil_decode, bsz, t)
        operands, in_specs = operands + d_ops, in_specs + d_in
        out_specs, out_shape = out_specs + d_out, out_shape + d_shape
        body, name = functools.partial(_gdn_dil_decode_body, chunk=chunk), "gdn_dil_decode"
    outs = pl.pallas_call(
        body,
        grid=(bsz, nc),
        in_specs=in_specs,
        out_specs=out_specs,
        out_shape=out_shape,
        scratch_shapes=[pltpu.VMEM((rows + 8, cdim), F32)],
        compiler_params=_cparams("arbitrary", "arbitrary"),
        name=name,
    )(*operands)
    if dil_decode is None:
        return outs
    return list(outs[:4]) + [[unview(v) for v in outs[4:]]]


def _softmax_pv(s, valid, v16, n_rep=1):
    rows, nk = s.shape
    s = jnp.where(valid[None], s.reshape(n_rep, rows // n_rep, nk), NEG).reshape(rows, nk)
    m = jnp.max(s, axis=-1, keepdims=True)
    p = jnp.exp(s - m)
    l = jnp.sum(p, axis=-1, keepdims=True)
    return jnp.dot((p / l).astype(BF16), v16, preferred_element_type=F32), m + jnp.log(l)


_DIL_PAIRS = DIL_HEADS_PER_GROUP // 2


def _dil_prompt_body(q_ref, k_ref, v_ref, o_ref, lse_ref, kprev_ref, vprev_ref, *, window, step):
    i, h = pl.program_id(1), pl.program_id(2)
    pp = q_ref.shape[0]
    qb = q_ref.shape[1] // step
    scale = HEAD_DIM ** -0.5
    qpos = i * qb + lax.broadcasted_iota(jnp.int32, (qb, 2 * qb), 0)
    kpos = (i - 1) * qb + lax.broadcasted_iota(jnp.int32, (qb, 2 * qb), 1)
    diff = jnp.where(kpos >= 0, qpos - kpos, -1)
    valid = jnp.where(diff >= 0, diff, window + 1) <= window
    lower = _half_mask((qb, LANE), 0)

    @pl.when(i == 0)
    def _():
        for pr in range(pp):
            kprev_ref[h * pp + pr] = jnp.zeros(kprev_ref.shape[1:], BF16)
            vprev_ref[h * pp + pr] = jnp.zeros(vprev_ref.shape[1:], BF16)

    def one_class(c, carry):
        rows = pl.ds(c, qb, stride=step)
        kept = pl.ds(pl.multiple_of(c * qb, qb), qb)
        for pr in range(pp):
            q = q_ref[pr, rows, :] * scale
            k_cur = k_ref[pr, rows, :].astype(BF16)
            v_cur = v_ref[pr, rows, :].astype(BF16)
            kk = jnp.concatenate([kprev_ref[h * pp + pr, kept, :], k_cur], axis=0)
            vv = jnp.concatenate([vprev_ref[h * pp + pr, kept, :], v_cur], axis=0)
            kprev_ref[h * pp + pr, kept, :] = k_cur
            vprev_ref[h * pp + pr, kept, :] = v_cur
            o2, l2 = [], []
            for k in range(2):
                qh = jnp.where(_half_mask((qb, LANE), k), q, 0.0).astype(BF16)
                s = lax.dot_general(qh, kk, _NT, preferred_element_type=F32)
                o, lse = _softmax_pv(s, valid, vv)
                o2.append(o)
                l2.append(jnp.broadcast_to(lse, (qb, LANE)))
            o_ref[pr, rows, :] = jnp.where(lower, o2[0], o2[1])
            lse_ref[pr, rows, :] = jnp.where(lower, l2[0], l2[1])
        return carry

    lax.fori_loop(0, step, one_class, 0)


def _dil_prompt_attn(dil, gi, bsz, t):
    win, step = DIL_PATTERNS[gi]
    assert win % step == 0 and t % win == 0
    tile = win
    nq = t // tile
    pp = _DIL_PAIRS if _DIL_PAIRS * tile * LANE * 4 <= (1 << 20) else 1
    blk = lambda part: pl.BlockSpec(
        (pp, tile, LANE), lambda b, i, h: ((part * DIL_GROUPS + gi) * (_DIL_PAIRS // pp) + h, b * nq + i, 0))
    out_spec = pl.BlockSpec((pp, tile, LANE), lambda b, i, h: (h, b * nq + i, 0))
    out_sds = jax.ShapeDtypeStruct((_DIL_PAIRS, bsz * t, LANE), F32)
    carry = pltpu.VMEM((_DIL_PAIRS, tile, LANE), BF16)
    return pl.pallas_call(
        functools.partial(_dil_prompt_body, window=win // step, step=step),
        grid=(bsz, nq, _DIL_PAIRS // pp),
        in_specs=[blk(0), blk(1), blk(2)],
        out_specs=[out_spec, out_spec],
        out_shape=[out_sds, out_sds],
        scratch_shapes=[carry, carry],
        compiler_params=_cparams("arbitrary", "arbitrary", "arbitrary"),
        name=f"dil_prompt_attn_{gi}",
    )(dil, dil, dil)


def _dil_combine_body(o0, o1, o2, l0, l1, l2, y_ref):
    for pr in range(_DIL_PAIRS):
        m = jnp.maximum(jnp.maximum(l0[pr], l1[pr]), l2[pr])
        e0, e1, e2 = jnp.exp(l0[pr] - m), jnp.exp(l1[pr] - m), jnp.exp(l2[pr] - m)
        den = e0 + e1 + e2
        y_ref[:, pr * LANE:(pr + 1) * LANE] = (e0 / den) * o0[pr] + (e1 / den) * o1[pr] + (e2 / den) * o2[pr]


def _dil_combine(outs, lses, tm):
    n_tok = outs[0].shape[1]
    spec = pl.BlockSpec((_DIL_PAIRS, tm, LANE), lambda i: (0, i, 0))
    return pl.pallas_call(
        _dil_combine_body,
        grid=(n_tok // tm,),
        in_specs=[spec] * 6,
        out_specs=pl.BlockSpec((tm, DIL_WIDTH), lambda i: (i, 0)),
        out_shape=jax.ShapeDtypeStruct((n_tok, DIL_WIDTH), F32),
        compiler_params=_cparams("arbitrary"),
        name="dil_combine",
    )(*outs, *lses)


def _attend_cached(qg, k_t, v_t, k_n, v_n, valid_c, valid_n, n_rep=1):
    rows = qg.shape[0]
    mask = lambda s, v: jnp.where(v[None], s.reshape(n_rep, rows // n_rep, s.shape[1]), NEG).reshape(rows, s.shape[1])
    s_c = mask(jnp.dot(qg, k_t, preferred_element_type=F32), valid_c)
    s_n = mask(lax.dot_general(qg, k_n, _NT, preferred_element_type=F32), valid_n)
    m = jnp.maximum(jnp.max(s_c, axis=-1, keepdims=True), jnp.max(s_n, axis=-1, keepdims=True))
    p_c = jnp.exp(s_c - m)
    p_n = jnp.exp(s_n - m)
    l = jnp.sum(p_c, axis=-1, keepdims=True) + jnp.sum(p_n, axis=-1, keepdims=True)
    o = lax.dot_general((p_c / l).astype(BF16), v_t, _NT, preferred_element_type=F32)
    o = o + jnp.dot((p_n / l).astype(BF16), v_n, preferred_element_type=F32)
    return o, m + jnp.log(l)


def _nsa_decode_body(pt_ref, q_ref, small_ref, rows_ref, win_ref, *refs, n_pages, lw, nb, nb_pad, n_sel):
    del pt_ref, nb
    pages = refs[:n_pages]
    (pastwin_ref, pek_ref, pev_ref, wk1_ref, wk2_ref, wv1_ref, wv2_ref, o_ref, neww_ref,
     kc_s, vc_s, comb_s) = refs[n_pages:]
    t = q_ref.shape[0]
    rep = NSA_REP
    past_len = n_pages * PAGE_SIZE
    for dst, c in ((kc_s, 0), (vc_s, 1)):
        for p in range(n_pages):
            dst[p * PAGE_SIZE:(p + 1) * PAGE_SIZE, :] = pages[p][c * LANE:(c + 1) * LANE, :].T
        dst[past_len:past_len + t, :] = rows_ref[:, c * LANE:(c + 1) * LANE]
        dst[past_len + t:, :] = jnp.zeros((dst.shape[0] - past_len - t, LANE), F32)
    kcmp = _compress_rows(kc_s, pek_ref, wk1_ref, wk2_ref, nb_pad).astype(BF16)
    vcmp = _compress_rows(vc_s, pev_ref, wv1_ref, wv2_ref, nb_pad).astype(BF16)
    page_rows = lambda c: jnp.concatenate([pages[p][c * LANE:(c + 1) * LANE, :] for p in range(n_pages)],
                                          axis=1).astype(BF16)
    ks_t, vs_t = page_rows(2), page_rows(3)
    ks_n, vs_n = rows_ref[:, 2 * LANE:3 * LANE].astype(BF16), rows_ref[:, 3 * LANE:].astype(BF16)
    kw_t, vw_t = pastwin_ref[:LANE, :].astype(BF16), pastwin_ref[LANE:, :].astype(BF16)
    kw_n, vw_n = win_ref[:, :LANE].astype(BF16), win_ref[:, LANE:].astype(BF16)
    neww_ref[...] = jnp.concatenate([pastwin_ref[:, t:], win_ref[...].T], axis=1)
    pos_c = past_len + lax.broadcasted_iota(jnp.int32, (t, 1), 0)
    new_i = lax.broadcasted_iota(jnp.int32, (t, t), 1)
    back = lax.broadcasted_iota(jnp.int32, (t, t), 0) - new_i
    sig = jax.nn.sigmoid(small_ref[...])
    q = q_ref[...] * (HEAD_DIM ** -0.5)
    for g in range(NSA_KV_HEADS):
        qg = _stack_heads(q, range(g * rep, (g + 1) * rep), g).astype(BF16)
        gate_lane = lambda j: [SSM_HEADS + j * NSA_HEADS + g * rep + r for r in range(rep)]
        s = lax.dot_general(qg, kcmp, _NT, preferred_element_type=F32).reshape(rep, t, nb_pad)
        n_i = lax.broadcasted_iota(jnp.int32, (t, nb_pad), 1)
        complete = (n_i + 1) * NSA_BLOCK <= pos_c + 1
        s = jnp.where(complete[None], s, NEG)
        p = jnp.exp(s - jnp.max(s, axis=-1, keepdims=True))
        p = p / jnp.sum(p, axis=-1, keepdims=True)
        p = jnp.where((pos_c >= NSA_BLOCK - 1)[None], p, 0.0)
        o_cmp = jnp.dot(p.reshape(rep * t, nb_pad).astype(BF16), vcmp, preferred_element_type=F32)
        comb = _row_gate(sig, gate_lane(0), t) * o_cmp
        sel = _nsa_select(p, pos_c, nb_pad, n_sel).astype(BF16)
        expand = lambda n_keys, first: jnp.where(
            lax.broadcasted_iota(jnp.int32, (nb_pad, n_keys), 0)
            == (first + lax.broadcasted_iota(jnp.int32, (nb_pad, n_keys), 1)) // NSA_BLOCK, 1.0, 0.0).astype(BF16)
        chosen_c = jnp.dot(sel, expand(past_len, 0), preferred_element_type=F32)
        chosen_n = jnp.dot(sel, expand(t, past_len), preferred_element_type=F32)
        kpos = lax.broadcasted_iota(jnp.int32, (t, past_len), 1)
        valid_c = jnp.where(kpos <= pos_c, chosen_c, 0.0) > 0.5
        valid_n = jnp.where(back >= 0, chosen_n, 0.0) > 0.5
        o_sel, _ = _attend_cached(qg, ks_t, vs_t, ks_n, vs_n, valid_c, valid_n, rep)
        comb = comb + _row_gate(sig, gate_lane(1), t) * o_sel
        diff = pos_c - (past_len - lw + lax.broadcasted_iota(jnp.int32, (t, lw), 1))
        valid_c = jnp.where(diff >= 0, diff, NSA_WINDOW + 1) <= NSA_WINDOW
        valid_n = jnp.where(back >= 0, back, NSA_WINDOW + 1) <= NSA_WINDOW
        o_win, _ = _attend_cached(qg, kw_t, vw_t, kw_n, vw_n, valid_c, valid_n, rep)
        comb_s[...] = comb + _row_gate(sig, gate_lane(2), t) * o_win
        for c in range(rep // 2):
            a = comb_s[(2 * c) * t:(2 * c + 1) * t, :]
            b = comb_s[(2 * c + 1) * t:(2 * c + 2) * t, :]
            col = (g * rep // 2 + c) * LANE
            o_ref[:, col:col + LANE] = _unstack_pair(a, b, g)


def _nsa_decode_attn(q, small, rows, win, cache_kv, cache_win, page_table, lp, bsz, t):
    n_pages = page_table.shape[1]
    past_len = n_pages * PAGE_SIZE
    lw = cache_win.shape[1]
    nb = -(-(past_len + t) // NSA_BLOCK)
    nb_pad = -(-nb // 8) * 8
    pages = jnp.transpose(cache_kv, (0, 2, 3, 4, 1)).reshape(cache_kv.shape[0], 4 * LANE, PAGE_SIZE)
    pastwin = jnp.transpose(cache_win, (0, 2, 3, 4, 1)).reshape(bsz, 2 * LANE, lw)
    w1 = lambda w: w.reshape(NSA_BLOCK, HEAD_DIM, NSA_CMP_HIDDEN).astype(BF16)
    tok = lambda n: pl.BlockSpec((t, n), lambda b, pt: (b, 0))
    const = lambda shape: pl.BlockSpec(shape, lambda b, pt: (0,) * len(shape), pipeline_mode=pl.Buffered(1))
    page_spec = lambda p: pl.BlockSpec((None, 4 * LANE, PAGE_SIZE), lambda b, pt: (pt[b * n_pages + p], 0, 0))
    win_spec = pl.BlockSpec((None, 2 * LANE, lw), lambda b, pt: (b, 0, 0))
    grid_spec = pltpu.PrefetchScalarGridSpec(
        num_scalar_prefetch=1,
        grid=(bsz,),
        in_specs=[tok(NSA_HEADS * HEAD_DIM), tok(LANE), tok(4 * LANE), tok(2 * LANE)]
        + [page_spec(p) for p in range(n_pages)]
        + [win_spec, const((NSA_BLOCK, LANE)), const((NSA_BLOCK, LANE)),
           const((NSA_BLOCK, HEAD_DIM, NSA_CMP_HIDDEN)), const((NSA_CMP_HIDDEN, HEAD_DIM)),
           const((NSA_BLOCK, HEAD_DIM, NSA_CMP_HIDDEN)), const((NSA_CMP_HIDDEN, HEAD_DIM))],
        out_specs=[tok(NSA_HEADS * HEAD_DIM), win_spec],
        scratch_shapes=[pltpu.VMEM((nb_pad * NSA_BLOCK, LANE), F32), pltpu.VMEM((nb_pad * NSA_BLOCK, LANE), F32),
                        pltpu.VMEM((NSA_REP * t, LANE), F32)],
    )
    y, new_win = pl.pallas_call(
        functools.partial(_nsa_decode_body, n_pages=n_pages, lw=lw, nb=nb, nb_pad=nb_pad, n_sel=min(NSA_TOPK, nb)),
        grid_spec=grid_spec,
        out_shape=[jax.ShapeDtypeStruct((bsz * t, NSA_HEADS * HEAD_DIM), F32),
                   jax.ShapeDtypeStruct((bsz, 2 * LANE, lw), F32)],
        compiler_params=_cparams("arbitrary"),
        name="nsa_decode_attn",
    )(page_table.reshape(-1), q, small, rows, win, *([pages] * n_pages), pastwin,
      _pe2(lp['nsa_pe_k']), _pe2(lp['nsa_pe_v']), w1(lp['nsa_ck_w1']), lp['nsa_ck_w2'].astype(BF16),
      w1(lp['nsa_cv_w1']), lp['nsa_cv_w2'].astype(BF16))
    new_win = jnp.transpose(new_win.reshape(bsz, 2, NSA_KV_HEADS, HEAD_DIM, lw), (0, 4, 1, 2, 3))
    return y, new_win


def _dil_decode_group(q, k_new, v_new, cache_ref, step):
    t = q.shape[0]
    lg = cache_ref.shape[1]
    pairs = DIL_HEADS_PER_GROUP // 2
    tok_c = lax.broadcasted_iota(jnp.int32, (2 * t, lg), 0) % t
    ahead = lax.broadcasted_iota(jnp.int32, (2 * t, lg), 1) - tok_c
    valid_c = jnp.where(ahead >= 0, ahead % step, 1) == 0
    tok_n = lax.broadcasted_iota(jnp.int32, (2 * t, t), 0) % t
    back = tok_n - lax.broadcasted_iota(jnp.int32, (2 * t, t), 1)
    valid_n = jnp.where(back >= 0, back % step, 1) == 0
    lower = _half_mask((t, LANE), 0)
    outs, lses = [], []
    for pr in range(pairs):
        sl = slice(pr * LANE, (pr + 1) * LANE)
        k_t = cache_ref[pr * LANE:(pr + 1) * LANE, :].astype(BF16)
        v_t = cache_ref[DIL_WIDTH + pr * LANE:DIL_WIDTH + (pr + 1) * LANE, :].astype(BF16)
        kn = k_new[:, sl].astype(BF16)
        vn = v_new[:, sl].astype(BF16)
        qp = q[:, sl]
        q2 = jnp.concatenate([jnp.where(lower, qp, 0.0), jnp.where(lower, 0.0, qp)], axis=0).astype(BF16)
        o, lse = _attend_cached(q2, k_t, v_t, kn, vn, valid_c, valid_n)
        lse = jnp.broadcast_to(lse, (2 * t, LANE))
        outs.append(jnp.where(lower, o[:t], o[t:]))
        lses.append(jnp.where(lower, lse[:t], lse[t:]))
    return jnp.concatenate(outs, axis=1), jnp.concatenate(lses, axis=1)


def _dil_decode_body(x_ref, c0_ref, c1_ref, c2_ref, y_ref, n0_ref, n1_ref, n2_ref):
    scale = HEAD_DIM ** -0.5
    t = x_ref.shape[1]
    res = []
    for gi, (cache_ref, new_ref) in enumerate(((c0_ref, n0_ref), (c1_ref, n1_ref), (c2_ref, n2_ref))):
        _, step = DIL_PATTERNS[gi]
        part = lambda p: jnp.concatenate(
            [x_ref[(p * DIL_GROUPS + gi) * _DIL_PAIRS + pr] for pr in range(_DIL_PAIRS)], axis=1)
        res.append(_dil_decode_group(part(0) * scale, part(1), part(2), cache_ref, step))
        new_t = jnp.concatenate([part(1).T, part(2).T], axis=0)
        new_ref[...] = jnp.concatenate([cache_ref[:, t:], new_t], axis=1)
    (o0, l0), (o1, l1), (o2, l2) = res
    m = jnp.maximum(jnp.maximum(l0, l1), l2)
    e0, e1, e2 = jnp.exp(l0 - m), jnp.exp(l1 - m), jnp.exp(l2 - m)
    den = e0 + e1 + e2
    y_ref[...] = (e0 / den) * o0 + (e1 / den) * o1 + (e2 / den) * o2


def _dil_decode_operands(dil, bufs, bsz, t):
    n_rows = 2 * DIL_WIDTH
    views = []
    for (win, _), buf in zip(DIL_PATTERNS, bufs):
        assert buf.shape[1] == win, "decode path needs a full window of cached rows"
        views.append(jnp.transpose(buf, (0, 2, 3, 4, 1)).reshape(bsz, n_rows, win))
    cache_spec = lambda v: pl.BlockSpec((None, n_rows, v.shape[2]), lambda b, *_: (b, 0, 0))
    in_specs = [pl.BlockSpec((DIL_IN // LANE, t, LANE), lambda b, *_: (0, b, 0))] + [cache_spec(v) for v in views]
    out_specs = [pl.BlockSpec((t, DIL_WIDTH), lambda b, *_: (b, 0))] + [cache_spec(v) for v in views]
    out_shape = [jax.ShapeDtypeStruct((bsz * t, DIL_WIDTH), F32)] + [jax.ShapeDtypeStruct(v.shape, F32) for v in views]
    unview = lambda v: jnp.transpose(v.reshape(bsz, 2, DIL_HEADS_PER_GROUP, HEAD_DIM, v.shape[2]), (0, 4, 1, 2, 3))
    return [dil] + views, in_specs, out_specs, out_shape, unview


def _gdn_dil_decode_body(*refs, chunk):
    n_gdn_in, n_dil_in, n_gdn_out, n_dil_out = 9, 4, 3, 4
    gdn_in, rest = refs[:n_gdn_in], refs[n_gdn_in:]
    dil_in, rest = rest[:n_dil_in], rest[n_dil_in:]
    gdn_out, rest = rest[:n_gdn_out], rest[n_gdn_out:]
    dil_out, scratch = rest[:n_dil_out], rest[n_dil_out:]
    _gdn_body(*gdn_in, *gdn_out, *scratch, chunk=chunk)
    _dil_decode_body(*dil_in, *dil_out)


_AB_CUTS = (0, SSM_INNER, SSM_INNER + SSM_CONV_DIM, SSM_INNER + SSM_CONV_DIM + SSM_HEADS)
_AB_Q0 = _AB_CUTS[3]
_AB_KV0 = _AB_Q0 + NSA_HEADS * HEAD_DIM
_AB_WIN0 = _AB_KV0 + 4 * NSA_KV_HEADS * HEAD_DIM
_AB_GATE0 = _AB_WIN0 + 2 * NSA_KV_HEADS * HEAD_DIM
_AB_END = _AB_GATE0 + 3 * NSA_HEADS


def _pad_cols(w, n):
    return jnp.pad(w, ((0, 0), (0, n - w.shape[1])))


def _ab_weight_segs(w_in):
    small = jnp.concatenate([w_in[:, _AB_CUTS[2]:_AB_CUTS[3]], w_in[:, _AB_GATE0:_AB_END]], axis=1)
    segs = [w_in[:, _AB_CUTS[0]:_AB_CUTS[1]], w_in[:, _AB_CUTS[1]:_AB_CUTS[2]], w_in[:, _AB_Q0:_AB_KV0],
            w_in[:, _AB_KV0:_AB_WIN0], w_in[:, _AB_WIN0:_AB_GATE0], _pad_cols(small, LANE)]
    return [s.astype(BF16) for s in segs]


_CD_Z0 = 3 * GDN_WIDTH
_CD_B0 = _CD_Z0 + GDN_WIDTH
_CD_DIL0 = _CD_B0 + 2 * GDN_HEADS
_CD_END = _CD_DIL0 + DIL_IN


def _cd_weight_segs(w_in):
    segs = [w_in[:, :_CD_Z0], w_in[:, _CD_Z0:_CD_B0], w_in[:, _CD_DIL0:_CD_END],
            _pad_cols(w_in[:, _CD_B0:_CD_DIL0], LANE)]
    return [s.astype(BF16) for s in segs]


def _layer_ab(y, norm_pre, segs, lp, past, page_table, is_prompt, tm):
    bsz, t, d = y.shape
    if is_prompt:
        z, xbc, q, rows, win, small, rows_t, win_t = _norm_proj(
            y.reshape(bsz * t, d), norm_pre, segs, tm, t_segs=(segs[3].T, segs[4].T), seq_len=t)
        to_rows = lambda x_t, kinds: jnp.transpose(
            x_t.reshape(bsz, kinds, NSA_KV_HEADS, HEAD_DIM, x_t.shape[-1]), (0, 4, 1, 2, 3))
        rows_new = to_rows(rows_t, 4)
        new_win = to_rows(win_t[:, :, t - min(NSA_WINDOW, t):], 2)
        conv_buf = jnp.zeros((bsz, CONV_K - 1, SSM_CONV_DIM), F32)
        h0 = jnp.zeros((bsz, SSM_HEADS, SSM_HEAD_DIM, SSM_STATE), F32)
        chunk = math.gcd(t, SSM_CHUNK)
        rows3 = rows.reshape(bsz, t, 4 * LANE)
        cmp = _nsa_compress_prompt(rows3, lp)
        y_nsa = _nsa_prompt_attn(q, small, cmp, rows3, win.reshape(bsz, t, 2 * LANE), bsz, t)
    else:
        z, xbc, q, rows, win, small = _norm_proj(y.reshape(bsz * t, d), norm_pre, segs, tm)
        rows_new = rows.reshape(bsz, t, 4, NSA_KV_HEADS, HEAD_DIM)
        conv_buf, h0, chunk = past['ssm_conv'], past['ssm'], t
        y_nsa, new_win = _nsa_decode_attn(q, small, rows, win, past['nsa_kv'], past['nsa_win'], page_table, lp,
                                          bsz, t)
    y_ssm, new_conv, new_ssm = _ssd_mixer(xbc, z, small, conv_buf, h0, lp, bsz, t, chunk)
    return y_ssm, y_nsa, (new_conv, new_ssm, rows_new, new_win)


def _layer_cd(y, norm_pre, segs, lp, past, is_prompt, tm):
    bsz, t, d = y.shape
    qkv, z, dil, small = _norm_proj(y.reshape(bsz * t, d), norm_pre, segs, tm, plane_segs=(2,))
    if is_prompt:
        conv_buf = jnp.zeros((bsz, CONV_K - 1, 3 * GDN_WIDTH), F32)
        s0 = jnp.zeros((bsz, GDN_HEADS, GDN_HEAD_DIM, GDN_HEAD_DIM), F32)
        chunk = math.gcd(t, GDN_CHUNK)
        parts = [_dil_prompt_attn(dil, gi, bsz, t) for gi in range(DIL_GROUPS)]
        y_dil = _dil_combine([p[0] for p in parts], [p[1] for p in parts], min(tm * 2, bsz * t))
        bufs = []
        for gi, (win, _) in enumerate(DIL_PATTERNS):
            w = min(win, t)
            dil4 = dil.reshape(DIL_IN // LANE, bsz, t, LANE)
            tail = lambda part: lax.slice(dil4, ((part * DIL_GROUPS + gi) * _DIL_PAIRS, 0, t - w, 0),
                                          ((part * DIL_GROUPS + gi + 1) * _DIL_PAIRS, bsz, t, LANE))
            kv = lax.optimization_barrier(jnp.stack([tail(1), tail(2)]))
            kv = kv.reshape(2, _DIL_PAIRS, bsz, w, 2, HEAD_DIM)
            bufs.append(jnp.transpose(kv, (2, 3, 0, 1, 4, 5)).reshape(bsz, w, 2, DIL_HEADS_PER_GROUP, HEAD_DIM))
        y_gdn, new_conv, new_gdn = _gdn_mixer(qkv, z, small, conv_buf, s0, lp, bsz, t, chunk)
    else:
        y_gdn, new_conv, new_gdn, y_dil, bufs = _gdn_mixer(qkv, z, small, past['gdn_conv'], past['gdn'], lp, bsz, t, t,
                                                            dil_decode=(dil, past['dil']))
    return y_gdn, y_dil, (new_conv, new_gdn, bufs[0], bufs[1], bufs[2])


def kernel(x_prompt, x_sample, cache_ssm_conv, state_ssm, cache_nsa_kv, cache_nsa_win_kv, cache_gdn_conv, state_gdn,
           cache_dil0_kv, cache_dil1_kv, cache_dil2_kv, page_table, norm_mix_pre, norm_mix_post, norm_mlp_pre,
           norm_mlp_post, mlp_w1, mlp_w2, ab_w_in, ab_w_out, ssm_conv_w, ssm_conv_b, ssm_dt_bias, ssm_a_log, ssm_d,
           ssm_norm_w, nsa_pe_k, nsa_pe_v, nsa_ck_w1, nsa_ck_w2, nsa_cv_w1, nsa_cv_w2, cd_w_in, cd_w_out, gdn_conv_w,
           gdn_dt_bias, gdn_a_log, gdn_norm_w):
    depth = norm_mix_pre.shape[0]
    yp, ys = x_prompt, x_sample
    ab_p, ab_s, cd_p, cd_s = [], [], [], []
    tm = TOKEN_TILE
    for l in range(depth):
        j = l // 2
        if l % 2 == 0:
            lp = {'ssm_conv_w': ssm_conv_w[j], 'ssm_conv_b': ssm_conv_b[j], 'ssm_dt_bias': ssm_dt_bias[j],
                  'ssm_a_log': ssm_a_log[j], 'ssm_d': ssm_d[j], 'ssm_norm_w': ssm_norm_w[j],
                  'nsa_pe_k': nsa_pe_k[j], 'nsa_pe_v': nsa_pe_v[j], 'nsa_ck_w1': nsa_ck_w1[j],
                  'nsa_ck_w2': nsa_ck_w2[j], 'nsa_cv_w1': nsa_cv_w1[j], 'nsa_cv_w2': nsa_cv_w2[j]}
            past = {'ssm_conv': cache_ssm_conv[j], 'ssm': state_ssm[j], 'nsa_kv': cache_nsa_kv[j],
                    'nsa_win': cache_nsa_win_kv[j]}
            segs = _ab_weight_segs(ab_w_in[j])
            w_out = ab_w_out[j].astype(BF16)
            ka = SSM_INNER
            ap, bp, stp = _layer_ab(yp, norm_mix_pre[l], segs, lp, None, None, True, tm)
            as_, bs, sts = _layer_ab(ys, norm_mix_pre[l], segs, lp, past, page_table, False, tm)
            ab_p.append(stp)
            ab_s.append(sts)
        else:
            lp = {'gdn_conv_w': gdn_conv_w[j], 'gdn_dt_bias': gdn_dt_bias[j], 'gdn_a_log': gdn_a_log[j],
                  'gdn_norm_w': gdn_norm_w[j]}
            past = {'gdn_conv': cache_gdn_conv[j], 'gdn': state_gdn[j],
                    'dil': (cache_dil0_kv[j], cache_dil1_kv[j], cache_dil2_kv[j])}
            segs = _cd_weight_segs(cd_w_in[j])
            w_out = cd_w_out[j].astype(BF16)
            ka = GDN_WIDTH
            ap, bp, stp = _layer_cd(yp, norm_mix_pre[l], segs, lp, None, True, tm)
            as_, bs, sts = _layer_cd(ys, norm_mix_pre[l], segs, lp, past, False, tm)
            cd_p.append(stp)
            cd_s.append(sts)
        w1 = mlp_w1[l].astype(BF16)
        w2 = mlp_w2[l].astype(BF16)
        post = functools.partial(_post_block, wo_a=w_out[:ka], wo_b=w_out[ka:], w1=w1, w2=w2, n_mix=norm_mix_post[l],
                                 n_pre=norm_mlp_pre[l], n_post=norm_mlp_post[l], tm=tm)
        yp = post(ap, bp, yp.reshape(-1, D_MODEL)).reshape(yp.shape)
        ys = post(as_, bs, ys.reshape(-1, D_MODEL)).reshape(ys.shape)
    stack = lambda states, i: jnp.stack([s[i] for s in states])
    return (yp, ys,
            stack(ab_p, 0), stack(ab_s, 0), stack(ab_p, 1), stack(ab_s, 1),
            stack(ab_p, 2), stack(ab_s, 2), stack(ab_p, 3), stack(ab_s, 3),
            stack(cd_p, 0), stack(cd_s, 0), stack(cd_p, 1), stack(cd_s, 1),
            stack(cd_p, 2), stack(cd_s, 2), stack(cd_p, 3), stack(cd_s, 3),
            stack(cd_p, 4), stack(cd_s, 4))
```
